```python
import math
import jax, jax.numpy as jnp
from jax import lax
import numpy as np

D_MODEL = 2048
BATCH = 2
SEQ = 8192
DEPTH = 1

S5_WIDTH = 1024
S5_GROUP = 16
S5_GROUPS = S5_WIDTH // S5_GROUP
S5_STATE = 64
DT_MIN = 1e-3
DT_MAX = 1e-1
CONV_WIDTH = 1024
CONV_K = 31
IN_COLS = S5_WIDTH + 2 * CONV_WIDTH + 2 * D_MODEL
N_MEM = 256
XATTN_HEADS = 4
XATTN_HEAD_DIM = D_MODEL // XATTN_HEADS
MOE_GROUPS = 8
EXPERTS_PER_GROUP = 8
N_EXPERTS = MOE_GROUPS * EXPERTS_PER_GROUP
TOP_K_FINE = 2
D_EXPERT = 512
MOE_BLOCK = 128

RMS_EPS = 1e-6
LN_EPS = 1e-5

kernel_name = "hybrid_s5_conformer_xattn_hmoe"


def rms_norm(x, g):
    xf = x.astype(jnp.float32)
    y = xf * lax.rsqrt(jnp.mean(xf * xf, axis=-1, keepdims=True) + RMS_EPS)
    return (y * g.astype(jnp.float32)).astype(x.dtype)


def _cmul(ar, ai, br, bi):
    return ar * br - ai * bi, ar * bi + ai * br


def s5_mixer(u, lam_re, lam_im, log_dt, b_re, b_im, c_re, c_im, d_skip):
    f32 = jnp.float32
    bsz, L = u.shape[0], u.shape[1]
    dt = jnp.exp(log_dt.astype(f32))[:, None]
    lr, li = lam_re.astype(f32), lam_im.astype(f32)
    mag = jnp.exp(lr * dt)
    ang = li * dt
    lb_re, lb_im = mag * jnp.cos(ang), mag * jnp.sin(ang)
    den = lr * lr + li * li
    nr, ni = lb_re - 1.0, lb_im
    coef_re = (nr * lr + ni * li) / den
    coef_im = (ni * lr - nr * li) / den
    bb_re, bb_im = _cmul(coef_re[..., None], coef_im[..., None],
                         b_re.astype(f32), b_im.astype(f32))
    ug = u.astype(f32).reshape(bsz, L, S5_GROUPS, S5_GROUP)
    bu_re = jnp.einsum('blgc,gnc->blgn', ug, bb_re)
    bu_im = jnp.einsum('blgc,gnc->blgn', ug, bb_im)
    a_re = jnp.broadcast_to(lb_re, (1, L, S5_GROUPS, S5_STATE))
    a_im = jnp.broadcast_to(lb_im, (1, L, S5_GROUPS, S5_STATE))

    def combine(e1, e2):
        a1r, a1i, b1r, b1i = e1
        a2r, a2i, b2r, b2i = e2
        ar, ai = _cmul(a2r, a2i, a1r, a1i)
        br, bi = _cmul(a2r, a2i, b1r, b1i)
        return (ar, ai, br + b2r, bi + b2i)

    _, _, xr, xi = lax.associative_scan(combine, (a_re, a_im, bu_re, bu_im), axis=1)
    y = (jnp.einsum('blgn,gcn->blgc', xr, c_re.astype(f32))
         - jnp.einsum('blgn,gcn->blgc', xi, c_im.astype(f32))
         + d_skip.astype(f32).reshape(S5_GROUPS, S5_GROUP) * ug)
    return y.reshape(bsz, L, S5_WIDTH).astype(u.dtype)


def conformer_conv(v, dw_w, dw_b, ln_g, ln_b, w_out):
    a, b = jnp.split(v, 2, axis=-1)
    z = a * jax.nn.sigmoid(b)
    z = lax.conv_general_dilated(
        z, dw_w[:, None, :], window_strides=(1,), padding=[(CONV_K - 1, 0)],
        dimension_numbers=('NWC', 'WIO', 'NWC'),
        feature_group_count=CONV_WIDTH) + dw_b
    zf = z.astype(jnp.float32)
    mu = jnp.mean(zf, axis=-1, keepdims=True)
    var = jnp.mean(jnp.square(zf - mu), axis=-1, keepdims=True)
    zf = (zf - mu) * lax.rsqrt(var + LN_EPS) * ln_g.astype(jnp.float32) + ln_b.astype(jnp.float32)
    z = jax.nn.silu(zf).astype(v.dtype)
    return z @ w_out


def cross_attn(h, m, wq, wk, wv, wo):
    bsz, L, _ = h.shape
    q = (h @ wq).reshape(bsz, L, XATTN_HEADS, XATTN_HEAD_DIM)
    k = (m @ wk).reshape(bsz, N_MEM, XATTN_HEADS, XATTN_HEAD_DIM)
    v = (m @ wv).reshape(bsz, N_MEM, XATTN_HEADS, XATTN_HEAD_DIM)
    s = jnp.einsum('blhd,bmhd->bhlm', q, k).astype(jnp.float32) / math.sqrt(XATTN_HEAD_DIM)
    p = jax.nn.softmax(s, axis=-1).astype(h.dtype)
    o = jnp.einsum('bhlm,bmhd->blhd', p, v).reshape(bsz, L, D_MODEL)
    return o @ wo


def hier_moe(h, w_rg, b_rg, w_re, b_re, w_gate, w_up, w_down):
    bsz, L, D = h.shape
    T = bsz * L
    TK = T * TOP_K_FINE
    hf = h.reshape(T, D)
    cl = (hf @ w_rg).astype(jnp.float32) + b_rg.astype(jnp.float32)
    pg = jax.nn.softmax(cl, axis=-1)
    g_idx = jnp.argmax(cl, axis=-1).astype(jnp.int32)
    p_sel = jnp.take_along_axis(pg, g_idx[:, None], axis=1)[:, 0]
    fl = jnp.einsum('td,gde->tge', hf, w_re).astype(jnp.float32) + b_re.astype(jnp.float32)
    fl_sel = jnp.take_along_axis(fl, g_idx[:, None, None], axis=1)[:, 0]
    top_v, top_i = lax.top_k(fl_sel, TOP_K_FINE)
    w = p_sel[:, None] * jax.nn.softmax(top_v, axis=-1)
    e_ids = (g_idx[:, None] * EXPERTS_PER_GROUP + top_i).reshape(-1).astype(jnp.int32)
    w_flat = w.reshape(-1)
    tok = jnp.arange(TK, dtype=jnp.int32) // TOP_K_FINE

    order = jnp.argsort(e_ids)
    se = e_ids[order]
    counts = jax.ops.segment_sum(jnp.ones((TK,), jnp.int32), e_ids, num_segments=N_EXPERTS)
    starts = jnp.cumsum(counts) - counts
    pcounts = ((counts + MOE_BLOCK - 1) // MOE_BLOCK) * MOE_BLOCK
    pends = jnp.cumsum(pcounts)
    pstarts = pends - pcounts
    dest = pstarts[se] + jnp.arange(TK, dtype=jnp.int32) - starts[se]
    n_blocks = -(-TK // MOE_BLOCK) + N_EXPERTS
    n_rows = n_blocks * MOE_BLOCK
    row_tok = jnp.zeros((n_rows,), jnp.int32).at[dest].set(tok[order])
    row_w = jnp.zeros((n_rows,), jnp.float32).at[dest].set(w_flat[order])
    block_start = jnp.arange(n_blocks, dtype=jnp.int32) * MOE_BLOCK
    block_e = jnp.minimum(jnp.searchsorted(pends, block_start, side='right'),
                          N_EXPERTS - 1).astype(jnp.int32)
    xs = hf[row_tok].reshape(n_blocks, MOE_BLOCK, D)

    def expert_block(args):
        xb, e = args
        return (jax.nn.silu(xb @ w_gate[e]) * (xb @ w_up[e])) @ w_down[e]

    ys = lax.map(expert_block, (xs, block_e)).reshape(n_rows, D)
    ys = ys * row_w[:, None].astype(ys.dtype)
    out = jnp.zeros((T, D), h.dtype).at[row_tok].add(ys)
    return out.reshape(bsz, L, D)


def setup_inputs(seed: int = 0) -> dict:
    key = jax.random.key(seed)
    ks = jax.random.split(key, 32)
    f32 = jnp.float32

    def nrm(k, shape, scale):
        return jax.random.normal(k, shape, f32) * scale

    def gain(k, shape):
        return 1.0 + 0.02 * jax.random.normal(k, shape, f32)

    Dm = D_MODEL
    n_idx = jnp.arange(S5_STATE, dtype=f32)
    lam_re = -0.5 + 0.01 * jax.random.normal(ks[4], (DEPTH, S5_GROUPS, S5_STATE), f32)
    lam_im = math.pi * n_idx + 0.01 * jax.random.normal(ks[5], (DEPTH, S5_GROUPS, S5_STATE), f32)
    log_dt = jax.random.uniform(ks[6], (DEPTH, S5_GROUPS), f32,
                                math.log(DT_MIN), math.log(DT_MAX))
    return {
        "x": nrm(ks[0], (BATCH, SEQ, Dm), 1.0),
        "mem": nrm(ks[1], (BATCH, N_MEM, Dm), 1.0),
        "norm_mix_g": gain(ks[2], (DEPTH, Dm)),
        "w_in": nrm(ks[3], (DEPTH, Dm, IN_COLS), Dm ** -0.5),
        "s5_lambda_re": lam_re,
        "s5_lambda_im": lam_im,
        "s5_log_dt": log_dt,
        "s5_b_re": nrm(ks[7], (DEPTH, S5_GROUPS, S5_STATE, S5_GROUP), (2 * S5_GROUP) ** -0.5),
        "s5_b_im": nrm(ks[8], (DEPTH, S5_GROUPS, S5_STATE, S5_GROUP), (2 * S5_GROUP) ** -0.5),
        "s5_c_re": nrm(ks[9], (DEPTH, S5_GROUPS, S5_GROUP, S5_STATE), S5_STATE ** -0.5),
        "s5_c_im": nrm(ks[10], (DEPTH, S5_GROUPS, S5_GROUP, S5_STATE), S5_STATE ** -0.5),
        "s5_d": nrm(ks[11], (DEPTH, S5_WIDTH), 1.0),
        "s5_w_glu": nrm(ks[12], (DEPTH, S5_WIDTH, 2 * Dm), S5_WIDTH ** -0.5),
        "conv_dw_w": nrm(ks[13], (DEPTH, CONV_K, CONV_WIDTH), CONV_K ** -0.5),
        "conv_dw_b": nrm(ks[14], (DEPTH, CONV_WIDTH), 0.02),
        "conv_ln_g": gain(ks[15], (DEPTH, CONV_WIDTH)),
        "conv_ln_b": nrm(ks[16], (DEPTH, CONV_WIDTH), 0.02),
        "conv_w_out": nrm(ks[17], (DEPTH, CONV_WIDTH, Dm), CONV_WIDTH ** -0.5),
        "w_out": nrm(ks[18], (DEPTH, Dm, Dm), Dm ** -0.5),
        "norm_xattn_g": gain(ks[19], (DEPTH, Dm)),
        "norm_mem_g": gain(ks[20], (DEPTH, Dm)),
        "xattn_wq": nrm(ks[21], (DEPTH, Dm, Dm), Dm ** -0.5),
        "xattn_wk": nrm(ks[22], (DEPTH, Dm, Dm), Dm ** -0.5),
        "xattn_wv": nrm(ks[23], (DEPTH, Dm, Dm), Dm ** -0.5),
        "xattn_wo": nrm(ks[24], (DEPTH, Dm, Dm), Dm ** -0.5),
        "norm_moe_g": gain(ks[25], (DEPTH, Dm)),
        "router_w_group": nrm(ks[26], (DEPTH, Dm, MOE_GROUPS), Dm ** -0.5),
        "router_b_group": nrm(ks[27], (DEPTH, MOE_GROUPS), 0.01),
        "router_w_expert": nrm(ks[28], (DEPTH, MOE_GROUPS, Dm, EXPERTS_PER_GROUP), Dm ** -0.5),
        "router_b_expert": nrm(ks[29], (DEPTH, MOE_GROUPS, EXPERTS_PER_GROUP), 0.01),
        "exp_w_gate": nrm(ks[30], (DEPTH, N_EXPERTS, Dm, D_EXPERT), Dm ** -0.5),
        "exp_w_up": nrm(ks[31], (DEPTH, N_EXPERTS, Dm, D_EXPERT), Dm ** -0.5),
        "exp_w_down": nrm(jax.random.fold_in(key, 101), (DEPTH, N_EXPERTS, D_EXPERT, Dm), D_EXPERT ** -0.5),
        "norm_final_g": gain(jax.random.fold_in(key, 102), (Dm,)),
    }


def reference(x, mem, norm_mix_g, w_in, s5_lambda_re, s5_lambda_im, s5_log_dt,
              s5_b_re, s5_b_im, s5_c_re, s5_c_im, s5_d, s5_w_glu,
              conv_dw_w, conv_dw_b, conv_ln_g, conv_ln_b, conv_w_out, w_out,
              norm_xattn_g, norm_mem_g, xattn_wq, xattn_wk, xattn_wv, xattn_wo,
              norm_moe_g, router_w_group, router_b_group, router_w_expert, router_b_expert,
              exp_w_gate, exp_w_up, exp_w_down, norm_final_g):
    for l in range(DEPTH):
        h = rms_norm(x, norm_mix_g[l])
        proj = h @ w_in[l]
        u_s5 = proj[..., :S5_WIDTH]
        v_conv = proj[..., S5_WIDTH:S5_WIDTH + 2 * CONV_WIDTH]
        gates = jax.nn.sigmoid(proj[..., S5_WIDTH + 2 * CONV_WIDTH:])
        gate_a, gate_b = jnp.split(gates, 2, axis=-1)

        y_s5 = s5_mixer(u_s5, s5_lambda_re[l], s5_lambda_im[l], s5_log_dt[l],
                        s5_b_re[l], s5_b_im[l], s5_c_re[l], s5_c_im[l], s5_d[l])
        z = jax.nn.gelu(y_s5)
        val, gt = jnp.split(z @ s5_w_glu[l], 2, axis=-1)
        y_a = val * jax.nn.sigmoid(gt)

        y_b = conformer_conv(v_conv, conv_dw_w[l], conv_dw_b[l], conv_ln_g[l],
                             conv_ln_b[l], conv_w_out[l])

        x = x + (gate_a * y_a + gate_b * y_b) @ w_out[l]

        h = rms_norm(x, norm_xattn_g[l])
        m = rms_norm(mem, norm_mem_g[l])
        x = x + cross_attn(h, m, xattn_wq[l], xattn_wk[l], xattn_wv[l], xattn_wo[l])

        h = rms_norm(x, norm_moe_g[l])
        x = x + hier_moe(h, router_w_group[l], router_b_group[l], router_w_expert[l],
                         router_b_expert[l], exp_w_gate[l], exp_w_up[l], exp_w_down[l])
    return rms_norm(x, norm_final_g)
```

```python
import functools
import math

import jax
import jax.numpy as jnp
from jax import lax
from jax.experimental import pallas as pl
from jax.experimental.pallas import tpu as pltpu

F32 = jnp.float32
BF16 = jnp.bfloat16

RMS_EPS = 1e-6
LN_EPS = 1e-5

S5_SLAB_CH = 256
CONV_HALO = 32
XATTN_HEADS = 4
MOE_BLOCK = 128
SUBLANES = 8
LANES = 128
VMEM_LIMIT = 56 * 1024 * 1024

TILES = dict(proj_tm=512, proj_tn=512, s5_m=64, s5_cw=512, glu_tm=512, glu_tn=512, conv_tm=256,
             comb_tm=256, q_tm=512, q_tn=512, kv_tn=512, xattn_tm=512, router_tm=512, final_tm=512)


def _cparams(sem):
    return pltpu.CompilerParams(dimension_semantics=sem, vmem_limit_bytes=VMEM_LIMIT)


def _rms(x, g):
    return x * lax.rsqrt(jnp.mean(x * x, axis=-1, keepdims=True) + RMS_EPS) * g


def _sigmoid(x):
    return 1.0 / (1.0 + jnp.exp(-x))


def _gelu_tanh(x):
    c = math.sqrt(2.0 / math.pi)
    return 0.5 * x * (1.0 + jnp.tanh(c * (x + 0.044715 * (x * x * x))))


def _norm_mm_kernel(x_ref, g_ref, w_ref, o_ref, h_ref, *, act_from):
    j = pl.program_id(1)

    @pl.when(j == 0)
    def _():
        h_ref[...] = _rms(x_ref[...], g_ref[...]).astype(BF16)

    y = jnp.dot(h_ref[...], w_ref[...], preferred_element_type=F32)
    if act_from is None:
        o_ref[...] = y.astype(o_ref.dtype)
    else:
        @pl.when(j < act_from)
        def _():
            o_ref[...] = y.astype(o_ref.dtype)

        @pl.when(j >= act_from)
        def _():
            o_ref[...] = _sigmoid(y).astype(o_ref.dtype)


def norm_mm(x, g, w, *, tm, tn, out_dtype=F32, act_from=None):
    m, k = x.shape
    n = w.shape[1]
    return pl.pallas_call(
        functools.partial(_norm_mm_kernel, act_from=act_from),
        grid=(m // tm, n // tn),
        in_specs=[pl.BlockSpec((tm, k), lambda i, j: (i, 0)),
                  pl.BlockSpec((1, k), lambda i, j: (0, 0)),
                  pl.BlockSpec((k, tn), lambda i, j: (0, j))],
        out_specs=pl.BlockSpec((tm, tn), lambda i, j: (i, j)),
        out_shape=jax.ShapeDtypeStruct((m, n), out_dtype),
        scratch_shapes=[pltpu.VMEM((tm, k), BF16)],
        compiler_params=_cparams(("parallel", "arbitrary")),
        name="norm_mm",
    )(x, g.reshape(1, k), w)


def _mm_glu_kernel(z_ref, wv_ref, wg_ref, o_ref):
    z = z_ref[...]
    val = jnp.dot(z, wv_ref[...], preferred_element_type=F32)
    gt = jnp.dot(z, wg_ref[...], preferred_element_type=F32)
    o_ref[...] = val * _sigmoid(gt)


def mm_glu(z, w, *, tm, tn):
    m, k = z.shape
    nh = w.shape[1] // 2
    nj = nh // tn
    return pl.pallas_call(
        _mm_glu_kernel,
        grid=(m // tm, nj),
        in_specs=[pl.BlockSpec((tm, k), lambda i, j: (i, 0)),
                  pl.BlockSpec((k, tn), lambda i, j: (0, j)),
                  pl.BlockSpec((k, tn), lambda i, j: (0, j + nj))],
        out_specs=pl.BlockSpec((tm, tn), lambda i, j: (i, j)),
        out_shape=jax.ShapeDtypeStruct((m, nh), F32),
        compiler_params=_cparams(("parallel", "arbitrary")),
        name="s5_glu_mm",
    )(z, w, w)


def _s5_disc_kernel(lr_ref, li_ref, ldt_ref, br_ref, bi_ref, lbr_ref, lbi_ref, bbr_ref, bbi_ref):
    lr, li = lr_ref[...], li_ref[...]
    dt = jnp.exp(ldt_ref[...])
    mag = jnp.exp(lr * dt)
    ang = li * dt
    lb_re, lb_im = mag * jnp.cos(ang), mag * jnp.sin(ang)
    den = lr * lr + li * li
    nr, ni = lb_re - 1.0, lb_im
    coef_re = (nr * lr + ni * li) / den
    coef_im = (ni * lr - nr * li) / den
    lbr_ref[...] = lb_re
    lbi_ref[...] = lb_im
    br, bi = br_ref[...], bi_ref[...]
    bbr_ref[...] = coef_re[None] * br - coef_im[None] * bi
    bbi_ref[...] = coef_re[None] * bi + coef_im[None] * br


def s5_discretise(lam_re, lam_im, log_dt, b_re, b_im):
    g, n = lam_re.shape
    c = b_re.shape[-1]
    b_re_t = jnp.transpose(b_re, (2, 0, 1))
    b_im_t = jnp.transpose(b_im, (2, 0, 1))
    gn = jax.ShapeDtypeStruct((g, n), F32)
    cgn = jax.ShapeDtypeStruct((c, g, n), F32)
    return pl.pallas_call(
        _s5_disc_kernel,
        out_shape=(gn, gn, cgn, cgn),
        name="s5_discretise",
    )(lam_re, lam_im, log_dt.reshape(g, 1), b_re_t, b_im_t)


def s5_block_diag(bb_re, bb_im, c_re, c_im):
    c, g, n = bb_re.shape
    gs = S5_SLAB_CH // c
    n_slab = g // gs
    eye = jnp.eye(gs, dtype=F32)

    def bmat(bb):
        b = jnp.transpose(bb, (1, 0, 2)).reshape(n_slab, gs, c, n)
        full = b[:, :, :, None, :] * eye[None, :, None, :, None]
        return full.reshape(n_slab, gs * c, gs * n).astype(BF16)

    def cmat(cc):
        cm = jnp.transpose(cc.reshape(n_slab, gs, c, n), (0, 1, 3, 2))
        full = cm[:, :, :, None, :] * eye[None, :, None, :, None]
        return full.reshape(n_slab, gs * n, gs * c).astype(BF16)

    return bmat(bb_re), bmat(bb_im), cmat(c_re), cmat(c_im)


def s5_row_perm(m):
    rows = SUBLANES * m
    r = jnp.arange(rows)
    src = (r % SUBLANES) * m + r // SUBLANES
    return (src[:, None] == jnp.arange(rows)[None, :]).astype(BF16)


def _s5_scan_kernel(u_ref, perm_ref, permt_ref, bre_ref, bim_ref, cre_ref, cim_ref, lre_ref, lim_ref,
                    d_ref, o_ref, xre_ref, xim_ref, pre_ref, pim_ref, car_re_ref, car_im_ref, *, m, cw):
    t = pl.program_id(1)
    n_state = lre_ref.shape[1]
    n_slab = bre_ref.shape[0]
    slab_states = n_state // n_slab

    @pl.when(t == 0)
    def _():
        car_re_ref[...] = jnp.zeros_like(car_re_ref)
        car_im_ref[...] = jnp.zeros_like(car_im_ref)
        lr, li = lre_ref[...], lim_ref[...]
        pre_ref[0:1, :] = lr
        pim_ref[0:1, :] = li

        def pw(tau, carry):
            pr, pi = carry
            nr = pr * lr - pi * li
            ni = pr * li + pi * lr
            pre_ref[pl.ds(tau, 1), :] = nr
            pim_ref[pl.ds(tau, 1), :] = ni
            return nr, ni

        lax.fori_loop(1, m, pw, (lr, li))

    u = u_ref[...]
    up = jnp.dot(perm_ref[...], u.astype(BF16), preferred_element_type=F32).astype(BF16)
    for s in range(n_slab):
        us = up[:, s * S5_SLAB_CH:(s + 1) * S5_SLAB_CH]
        cols = slice(s * slab_states, (s + 1) * slab_states)
        xre_ref[:, cols] = jnp.dot(us, bre_ref[s], preferred_element_type=F32)
        xim_ref[:, cols] = jnp.dot(us, bim_ref[s], preferred_element_type=F32)

    row_id = lax.broadcasted_iota(jnp.int32, (SUBLANES, cw), 0)
    for cb in range(n_state // cw):
        cols = slice(cb * cw, (cb + 1) * cw)
        lr = jnp.broadcast_to(lre_ref[:, cols], (SUBLANES, cw))
        li = jnp.broadcast_to(lim_ref[:, cols], (SUBLANES, cw))

        def sweep1(tau, carry, cols=cols, lr=lr, li=li):
            sr, si = carry
            r0 = pl.multiple_of(tau * SUBLANES, SUBLANES)
            nr = sr * lr - si * li + xre_ref[pl.ds(r0, SUBLANES), cols]
            ni = sr * li + si * lr + xim_ref[pl.ds(r0, SUBLANES), cols]
            xre_ref[pl.ds(r0, SUBLANES), cols] = nr
            xim_ref[pl.ds(r0, SUBLANES), cols] = ni
            return nr, ni

        zero = jnp.zeros((SUBLANES, cw), F32)
        er, ei = lax.fori_loop(0, m, sweep1, (zero, zero), unroll=4)

        lmr = jnp.broadcast_to(pre_ref[m - 1:m, cols], (SUBLANES, cw))
        lmi = jnp.broadcast_to(pim_ref[m - 1:m, cols], (SUBLANES, cw))
        ir = jnp.where(row_id == 0, jnp.broadcast_to(car_re_ref[:, cols], (SUBLANES, cw)), 0.0)
        ii = jnp.where(row_id == 0, jnp.broadcast_to(car_im_ref[:, cols], (SUBLANES, cw)), 0.0)
        nr, ni = ir, ii
        for k in range(1, SUBLANES + 1):
            nr = ir * lmr - ii * lmi + er
            ni = ir * lmi + ii * lmr + ei
            if k < SUBLANES:
                ir = jnp.where(row_id == k, pltpu.roll(nr, 1, axis=0), ir)
                ii = jnp.where(row_id == k, pltpu.roll(ni, 1, axis=0), ii)
        car_re_ref[:, cols] = nr[SUBLANES - 1:SUBLANES, :]
        car_im_ref[:, cols] = ni[SUBLANES - 1:SUBLANES, :]

        def sweep2(tau, carry, cols=cols, ir=ir, ii=ii):
            r0 = pl.multiple_of(tau * SUBLANES, SUBLANES)
            pr = jnp.broadcast_to(pre_ref[pl.ds(tau, 1), cols], (SUBLANES, cw))
            pi = jnp.broadcast_to(pim_ref[pl.ds(tau, 1), cols], (SUBLANES, cw))
            xre_ref[pl.ds(r0, SUBLANES), cols] += pr * ir - pi * ii
            xim_ref[pl.ds(r0, SUBLANES), cols] += pr * ii + pi * ir
            return carry

        lax.fori_loop(0, m, sweep2, 0, unroll=4)

    ys = []
    for s in range(n_slab):
        cols = slice(s * slab_states, (s + 1) * slab_states)
        ys.append(jnp.dot(xre_ref[:, cols].astype(BF16), cre_ref[s], preferred_element_type=F32)
                  - jnp.dot(xim_ref[:, cols].astype(BF16), cim_ref[s], preferred_element_type=F32))
    yp = jnp.concatenate(ys, axis=1)
    pt = permt_ref[...]
    y1 = yp.astype(BF16)
    r1 = yp - y1.astype(F32)
    y2 = r1.astype(BF16)
    y3 = (r1 - y2.astype(F32)).astype(BF16)
    y = (jnp.dot(pt, y1, preferred_element_type=F32) + jnp.dot(pt, y2, preferred_element_type=F32)
         + jnp.dot(pt, y3, preferred_element_type=F32))
    y = y + d_ref[...] * u
    o_ref[...] = _gelu_tanh(y).astype(o_ref.dtype)


def s5_scan(proj, perm, bre, bim, cre, cim, lre, lim, d_skip, *, batch, seq, width, m, cw):
    rows = SUBLANES * m
    n_t = seq // rows
    n_state = lre.shape[1]
    const3 = lambda b, t: (0, 0, 0)
    const2 = lambda b, t: (0, 0)
    return pl.pallas_call(
        functools.partial(_s5_scan_kernel, m=m, cw=cw),
        grid=(batch, n_t),
        in_specs=[pl.BlockSpec((rows, width), lambda b, t: (b * n_t + t, 0)),
                  pl.BlockSpec((rows, rows), const2),
                  pl.BlockSpec((rows, rows), const2),
                  pl.BlockSpec(bre.shape, const3),
                  pl.BlockSpec(bim.shape, const3),
                  pl.BlockSpec(cre.shape, const3),
                  pl.BlockSpec(cim.shape, const3),
                  pl.BlockSpec((1, n_state), const2),
                  pl.BlockSpec((1, n_state), const2),
                  pl.BlockSpec((1, width), const2)],
        out_specs=pl.BlockSpec((rows, width), lambda b, t: (b * n_t + t, 0)),
        out_shape=jax.ShapeDtypeStruct((batch * seq, width), BF16),
        scratch_shapes=[pltpu.VMEM((rows, n_state), F32), pltpu.VMEM((rows, n_state), F32),
                        pltpu.VMEM((m, n_state), F32), pltpu.VMEM((m, n_state), F32),
                        pltpu.VMEM((1, n_state), F32), pltpu.VMEM((1, n_state), F32)],
        compiler_params=_cparams(("parallel", "arbitrary")),
        name="s5_scan",
    )(proj, perm, perm.T, bre, bim, cre, cim, lre, lim, d_skip.reshape(1, width))


def _conv_kernel(a_ref, b_ref, ah_ref, bh_ref, dww_ref, dwb_ref, lng_ref, lnb_ref, w_ref, o_ref,
                 zs_ref, zsh_ref, yc_ref, *, tm, taps):
    i = pl.program_id(1)
    zh = ah_ref[...] * _sigmoid(bh_ref[...])
    zs_ref[0:CONV_HALO, :] = jnp.where(i == 0, 0.0, zh)
    zs_ref[CONV_HALO:, :] = a_ref[...] * _sigmoid(b_ref[...])
    off0 = CONV_HALO - (taps - 1)
    rc = 16
    width = a_ref.shape[1]
    n_sh = zsh_ref.shape[1]
    for s in range(1, SUBLANES):
        zsh_ref[s - 1] = zs_ref[s:s + n_sh, :]

    def chunk(c, carry):
        r0 = pl.multiple_of(c * rc, rc)
        acc = jnp.broadcast_to(dwb_ref[...], (rc, width))
        for j in range(taps):
            q, s = divmod(off0 + j, SUBLANES)
            rq = pl.multiple_of(r0 + q * SUBLANES, SUBLANES)
            zrow = zs_ref[pl.ds(rq, rc), :] if s == 0 else zsh_ref[s - 1, pl.ds(rq, rc), :]
            acc = acc + dww_ref[j:j + 1, :] * zrow
        yc_ref[pl.ds(r0, rc), :] = acc
        return carry

    lax.fori_loop(0, tm // rc, chunk, 0)
    y = yc_ref[...]
    mu = jnp.mean(y, axis=-1, keepdims=True)
    var = jnp.mean(jnp.square(y - mu), axis=-1, keepdims=True)
    y = (y - mu) * lax.rsqrt(var + LN_EPS) * lng_ref[...] + lnb_ref[...]
    y = y * _sigmoid(y)
    o_ref[...] = jnp.dot(y.astype(BF16), w_ref[...], preferred_element_type=F32)


def conv_branch(proj, dw_w, dw_b, ln_g, ln_b, w_out, *, batch, seq, width, col0, tm):
    n_t = seq // tm
    ca, cb = col0 // width, col0 // width + 1
    hb = tm // CONV_HALO
    taps = dw_w.shape[0]
    d_out = w_out.shape[1]
    const2 = lambda b, i: (0, 0)
    halo = lambda b, i: jnp.maximum((b * n_t + i) * hb - 1, 0)
    return pl.pallas_call(
        functools.partial(_conv_kernel, tm=tm, taps=taps),
        grid=(batch, n_t),
        in_specs=[pl.BlockSpec((tm, width), lambda b, i: (b * n_t + i, ca)),
                  pl.BlockSpec((tm, width), lambda b, i: (b * n_t + i, cb)),
                  pl.BlockSpec((CONV_HALO, width), lambda b, i: (halo(b, i), ca)),
                  pl.BlockSpec((CONV_HALO, width), lambda b, i: (halo(b, i), cb)),
                  pl.BlockSpec((taps, width), const2),
                  pl.BlockSpec((1, width), const2),
                  pl.BlockSpec((1, width), const2),
                  pl.BlockSpec((1, width), const2),
                  pl.BlockSpec((width, d_out), const2)],
        out_specs=pl.BlockSpec((tm, d_out), lambda b, i: (b * n_t + i, 0)),
        out_shape=jax.ShapeDtypeStruct((batch * seq, d_out), F32),
        scratch_shapes=[pltpu.VMEM((tm + CONV_HALO, width), F32),
                        pltpu.VMEM((SUBLANES - 1, tm + CONV_HALO - SUBLANES, width), F32),
                        pltpu.VMEM((tm, width), F32)],
        compiler_params=_cparams(("parallel", "arbitrary")),
        name="conv_branch",
    )(proj, proj, proj, proj, dw_w, dw_b.reshape(1, width), ln_g.reshape(1, width),
      ln_b.reshape(1, width), w_out)


def _combine_kernel(*refs, parts):
    x_ref, ya_ref, yb_ref, w_ref, o_ref = refs[0], refs[1 + 2 * parts], refs[2 + 2 * parts], refs[-2], refs[-1]
    gw = ya_ref.shape[1] // parts
    acc = x_ref[...]
    for p in range(parts):
        cols = slice(p * gw, (p + 1) * gw)
        mix = refs[1 + p][...] * ya_ref[:, cols] + refs[1 + parts + p][...] * yb_ref[:, cols]
        acc = acc + jnp.dot(mix.astype(BF16), w_ref[cols, :], preferred_element_type=F32)
    o_ref[...] = acc


def combine(x, proj, y_a, y_b, w_out, *, gate_col0, tm):
    m, d = x.shape
    gw = math.gcd(gate_col0, d)
    parts = d // gw
    row = lambda c: (lambda i: (i, c))
    gate_specs = [pl.BlockSpec((tm, gw), row((gate_col0 + s * d) // gw + p)) for s in range(2) for p in range(parts)]
    return pl.pallas_call(
        functools.partial(_combine_kernel, parts=parts),
        grid=(m // tm,),
        in_specs=[pl.BlockSpec((tm, d), row(0))] + gate_specs +
                 [pl.BlockSpec((tm, d), row(0)),
                  pl.BlockSpec((tm, d), row(0)),
                  pl.BlockSpec((d, d), lambda i: (0, 0))],
        out_specs=pl.BlockSpec((tm, d), row(0)),
        out_shape=jax.ShapeDtypeStruct((m, d), F32),
        compiler_params=_cparams(("parallel",)),
        name="gated_combine",
    )(x, *([proj] * (2 * parts)), y_a, y_b, w_out)


def _xattn_kernel(x_ref, q_ref, kv_ref, wo_ref, o_ref, *, heads, scale):
    d = x_ref.shape[1]
    hd = d // heads
    acc = x_ref[...]
    for h in range(heads):
        q = q_ref[:, h * hd:(h + 1) * hd]
        k = kv_ref[:, h * hd:(h + 1) * hd]
        v = kv_ref[:, d + h * hd:d + (h + 1) * hd]
        s = lax.dot_general(q, k, (((1,), (1,)), ((), ())), preferred_element_type=F32) * scale
        s = s - jnp.max(s, axis=-1, keepdims=True)
        p = jnp.exp(s)
        p = p / jnp.sum(p, axis=-1, keepdims=True)
        o = jnp.dot(p.astype(BF16), v, preferred_element_type=F32)
        acc = acc + jnp.dot(o.astype(BF16), wo_ref[h * hd:(h + 1) * hd, :], preferred_element_type=F32)
    o_ref[...] = acc


def xattn(x, q, kv, wo, *, batch, seq, n_mem, heads, tm):
    d = x.shape[1]
    n_t = seq // tm
    scale = 1.0 / math.sqrt(d // heads)
    return pl.pallas_call(
        functools.partial(_xattn_kernel, heads=heads, scale=scale),
        grid=(batch, n_t),
        in_specs=[pl.BlockSpec((tm, d), lambda b, i: (b * n_t + i, 0)),
                  pl.BlockSpec((tm, d), lambda b, i: (b * n_t + i, 0)),
                  pl.BlockSpec((n_mem, 2 * d), lambda b, i: (b, 0)),
                  pl.BlockSpec((d, d), lambda b, i: (0, 0))],
        out_specs=pl.BlockSpec((tm, d), lambda b, i: (b * n_t + i, 0)),
        out_shape=jax.ShapeDtypeStruct((batch * seq, d), F32),
        compiler_params=_cparams(("parallel", "parallel")),
        name="xattn",
    )(x, q, kv, wo)


def _router_kernel(x_ref, g_ref, w_ref, b_ref, tri_ref, h_ref, info_ref, cnt_ref, run_ref, *, n_groups, epg):
    i = pl.program_id(0)
    tm = x_ref.shape[0]

    @pl.when(i == 0)
    def _():
        run_ref[...] = jnp.zeros_like(run_ref)

    h = _rms(x_ref[...], g_ref[...])
    h_ref[...] = h
    logits = jnp.dot(h, w_ref[...], preferred_element_type=F32,
                     precision=lax.Precision.HIGHEST) + b_ref[...]
    lane = lax.broadcasted_iota(jnp.int32, (tm, LANES), 1).astype(F32)
    neg = jnp.float32(-jnp.inf)
    big = jnp.float32(LANES)
    is_grp = lane < n_groups
    cl = jnp.where(is_grp, logits, neg)
    cmax = jnp.max(cl, axis=-1, keepdims=True)
    g_idx = jnp.min(jnp.where(cl == cmax, lane, big), axis=-1, keepdims=True)
    p_sel = 1.0 / jnp.sum(jnp.where(is_grp, jnp.exp(cl - cmax), 0.0), axis=-1, keepdims=True)
    lo = n_groups + g_idx * epg
    in_grp = (lane >= lo) & (lane < lo + epg)
    fl = jnp.where(in_grp, logits, neg)
    v1 = jnp.max(fl, axis=-1, keepdims=True)
    i1 = jnp.min(jnp.where(fl == v1, lane, big), axis=-1, keepdims=True)
    fl2 = jnp.where(lane == i1, neg, fl)
    v2 = jnp.max(fl2, axis=-1, keepdims=True)
    i2 = jnp.min(jnp.where(fl2 == v2, lane, big), axis=-1, keepdims=True)
    e2x = jnp.exp(v2 - v1)
    w1 = p_sel / (1.0 + e2x)
    w2 = p_sel * e2x / (1.0 + e2x)
    e1 = i1 - n_groups
    e2 = i2 - n_groups
    oh1 = lane == e1
    oh2 = lane == e2
    both = jnp.where(oh1 | oh2, 1.0, 0.0)
    before = jnp.dot(tri_ref[...], both.astype(BF16), preferred_element_type=F32) + run_ref[...]
    r1 = jnp.sum(jnp.where(oh1, before, 0.0), axis=-1, keepdims=True)
    r2 = jnp.sum(jnp.where(oh2, before, 0.0), axis=-1, keepdims=True)
    run_ref[...] += jnp.sum(both, axis=0, keepdims=True)
    cnt_ref[...] = run_ref[...]
    info = jnp.where(lane == 0, e1, 0.0)
    info = jnp.where(lane == 1, e2, info)
    info = jnp.where(lane == 2, r1, info)
    info = jnp.where(lane == 3, r2, info)
    info = jnp.where(lane == 4, w1, info)
    info = jnp.where(lane == 5, w2, info)
    info_ref[...] = info


def router(x, g, w_pad, b_pad, *, n_groups, epg, tm):
    m, d = x.shape
    tri = (jnp.arange(tm)[:, None] > jnp.arange(tm)[None, :]).astype(BF16)
    return pl.pallas_call(
        functools.partial(_router_kernel, n_groups=n_groups, epg=epg),
        grid=(m // tm,),
        in_specs=[pl.BlockSpec((tm, d), lambda i: (i, 0)),
                  pl.BlockSpec((1, d), lambda i: (0, 0)),
                  pl.BlockSpec((d, LANES), lambda i: (0, 0)),
                  pl.BlockSpec((1, LANES), lambda i: (0, 0)),
                  pl.BlockSpec((tm, tm), lambda i: (0, 0))],
        out_specs=[pl.BlockSpec((tm, d), lambda i: (i, 0)),
                   pl.BlockSpec((tm, LANES), lambda i: (i, 0)),
                   pl.BlockSpec((1, LANES), lambda i: (0, 0))],
        out_shape=(jax.ShapeDtypeStruct((m, d), F32),
                   jax.ShapeDtypeStruct((m, LANES), F32),
                   jax.ShapeDtypeStruct((1, LANES), F32)),
        scratch_shapes=[pltpu.VMEM((1, LANES), F32)],
        compiler_params=_cparams(("arbitrary",)),
        name="moe_router",
    )(x, g.reshape(1, d), w_pad, b_pad, tri)


def moe_dispatch_plan(info, counts, n_tokens, n_experts):
    e = info[:, 0:2].astype(jnp.int32)
    rank = info[:, 2:4].astype(jnp.int32)
    w = info[:, 4:6]
    counts = counts[0, :n_experts].astype(jnp.int32)
    pcounts = ((counts + MOE_BLOCK - 1) // MOE_BLOCK) * MOE_BLOCK
    pends = jnp.cumsum(pcounts)
    pstarts = pends - pcounts
    dest = (pstarts[e] + rank).reshape(-1)
    tk = 2 * n_tokens
    n_blocks = -(-tk // MOE_BLOCK) + n_experts
    n_rows = n_blocks * MOE_BLOCK
    a = jnp.arange(tk, dtype=jnp.int32)
    row_tok = jnp.zeros((n_rows,), jnp.int32).at[dest].set(a // 2)
    row_dst = jnp.full((n_rows,), -1, jnp.int32).at[dest].set(a)
    row_w = jnp.zeros((n_rows,), F32).at[dest].set(w.reshape(-1))
    block_start = jnp.arange(n_blocks, dtype=jnp.int32) * MOE_BLOCK
    block_e = jnp.minimum(jnp.searchsorted(pends, block_start, side='right'), n_experts - 1).astype(jnp.int32)
    n_used = (pends[-1] // MOE_BLOCK).astype(jnp.int32).reshape(1)
    return block_e, n_used, row_tok, row_dst, row_w


def _experts_kernel(be_ref, nb_ref, tok_ref, tokn_ref, dst_ref, rw_ref, h_hbm, wg_ref, wu_ref, wd_ref,
                    o_hbm, xbuf, ybuf, wgb, wub, wdb, gsem, ssem, *, spare0):
    i = pl.program_id(0)
    nb = nb_ref[0]
    slot = lax.rem(i, 2)

    def gather_start(tok_smem, s):
        for r in range(MOE_BLOCK):
            pltpu.make_async_copy(h_hbm.at[pl.ds(tok_smem[0, 0, r], 1)], xbuf.at[s, pl.ds(r, 1)],
                                  gsem.at[s]).start()

    def gather_wait(s):
        pltpu.make_async_copy(h_hbm.at[pl.ds(0, MOE_BLOCK)], xbuf.at[s], gsem.at[s]).wait()

    def scatter_wait(s):
        pltpu.make_async_copy(ybuf.at[s], o_hbm.at[pl.ds(0, MOE_BLOCK)], ssem.at[s]).wait()

    @pl.when(i == 0)
    def _():
        ybuf[...] = jnp.zeros_like(ybuf)
        for s in range(2):
            fill = pltpu.make_async_copy(ybuf.at[s], o_hbm.at[pl.ds(spare0 + s * MOE_BLOCK, MOE_BLOCK)],
                                         ssem.at[s])
            fill.start()
            fill.wait()

    @pl.when((i == 0) & (nb > 0))
    def _():
        gather_start(tok_ref, 0)

    @pl.when(i < nb)
    def _():
        @pl.when(i + 1 < nb)
        def _():
            gather_start(tokn_ref, 1 - slot)

        @pl.when((i == 0) | (be_ref[i] != be_ref[jnp.maximum(i - 1, 0)]))
        def _():
            wgb[...] = wg_ref[0].astype(BF16)
            wub[...] = wu_ref[0].astype(BF16)
            wdb[...] = wd_ref[0].astype(BF16)

        gather_wait(slot)
        x = xbuf[slot].astype(BF16)
        gte = jnp.dot(x, wgb[...], preferred_element_type=F32)
        up = jnp.dot(x, wub[...], preferred_element_type=F32)
        act = (gte * _sigmoid(gte) * up).astype(BF16)
        y = jnp.dot(act, wdb[...], preferred_element_type=F32) * rw_ref[...]

        @pl.when(i >= 2)
        def _():
            scatter_wait(slot)

        ybuf[slot] = y
        for r in range(MOE_BLOCK):
            d = dst_ref[0, 0, r]
            d = jnp.where(d >= 0, d, spare0 + slot * MOE_BLOCK + r)
            pltpu.make_async_copy(ybuf.at[slot, pl.ds(r, 1)], o_hbm.at[pl.ds(d, 1)], ssem.at[slot]).start()

        @pl.when(i == nb - 1)
        def _():
            scatter_wait(slot)

            @pl.when(nb >= 2)
            def _():
                scatter_wait(1 - slot)


def experts(h, w_gate, w_up, w_down, block_e, n_used, row_tok, row_dst, row_w):
    t, d = h.shape
    n_blocks = block_e.shape[0]
    de = w_gate.shape[2]
    tok3 = row_tok.reshape(n_blocks, 1, MOE_BLOCK)
    dst3 = row_dst.reshape(n_blocks, 1, MOE_BLOCK)
    smem_blk = lambda f: pl.BlockSpec((1, 1, MOE_BLOCK), f, memory_space=pltpu.SMEM)
    grid_spec = pltpu.PrefetchScalarGridSpec(
        num_scalar_prefetch=2,
        grid=(n_blocks,),
        in_specs=[smem_blk(lambda i, be, nb: (i, 0, 0)),
                  smem_blk(lambda i, be, nb: (jnp.minimum(i + 1, n_blocks - 1), 0, 0)),
                  smem_blk(lambda i, be, nb: (i, 0, 0)),
                  pl.BlockSpec((MOE_BLOCK, 1), lambda i, be, nb: (i, 0)),
                  pl.BlockSpec(memory_space=pl.ANY),
                  pl.BlockSpec((1, d, de), lambda i, be, nb: (be[i], 0, 0)),
                  pl.BlockSpec((1, d, de), lambda i, be, nb: (be[i], 0, 0)),
                  pl.BlockSpec((1, de, d), lambda i, be, nb: (be[i], 0, 0))],
        out_specs=pl.BlockSpec(memory_space=pl.ANY),
        scratch_shapes=[pltpu.VMEM((2, MOE_BLOCK, d), F32), pltpu.VMEM((2, MOE_BLOCK, d), F32),
                        pltpu.VMEM((d, de), BF16), pltpu.VMEM((d, de), BF16), pltpu.VMEM((de, d), BF16),
                        pltpu.SemaphoreType.DMA((2,)), pltpu.SemaphoreType.DMA((2,))],
    )
    return pl.pallas_call(
        functools.partial(_experts_kernel, spare0=2 * t),
        grid_spec=grid_spec,
        out_shape=jax.ShapeDtypeStruct((2 * t + 2 * MOE_BLOCK, d), F32),
        compiler_params=_cparams(("arbitrary",)),
        name="moe_experts",
    )(block_e, n_used, tok3, tok3, dst3, row_w.reshape(-1, 1), h, w_gate, w_up, w_down)


def _moe_add_kernel(x_ref, y0_ref, y1_ref, g_ref, o_ref, *, norm):
    s = x_ref[...] + (y0_ref[...] + y1_ref[...])
    o_ref[...] = _rms(s, g_ref[...]) if norm else s


def moe_add(x, y2, g, *, norm, tm):
    m, d = x.shape
    yv = y2.reshape(y2.shape[0] // 2, 2 * d)
    return pl.pallas_call(
        functools.partial(_moe_add_kernel, norm=norm),
        grid=(m // tm,),
        in_specs=[pl.BlockSpec((tm, d), lambda i: (i, 0)),
                  pl.BlockSpec((tm, d), lambda i: (i, 0)),
                  pl.BlockSpec((tm, d), lambda i: (i, 1)),
                  pl.BlockSpec((1, d), lambda i: (0, 0))],
        out_specs=pl.BlockSpec((tm, d), lambda i: (i, 0)),
        out_shape=jax.ShapeDtypeStruct((m, d), F32),
        compiler_params=_cparams(("parallel",)),
        name="moe_add_norm",
    )(x, yv, yv, g.reshape(1, d))


def forward(x, mem, norm_mix_g, w_in, s5_lambda_re, s5_lambda_im, s5_log_dt, s5_b_re, s5_b_im, s5_c_re,
            s5_c_im, s5_d, s5_w_glu, conv_dw_w, conv_dw_b, conv_ln_g, conv_ln_b, conv_w_out, w_out,
            norm_xattn_g, norm_mem_g, xattn_wq, xattn_wk, xattn_wv, xattn_wo, norm_moe_g, router_w_group,
            router_b_group, router_w_expert, router_b_expert, exp_w_gate, exp_w_up, exp_w_down, norm_final_g,
            *, tiles):
    batch, seq, d = x.shape
    depth = w_in.shape[0]
    n_mem = mem.shape[1]
    s5_width = s5_d.shape[1]
    conv_width = conv_dw_b.shape[1]
    n_groups, epg = router_w_expert.shape[1], router_w_expert.shape[3]
    t = batch * seq
    xf = x.reshape(t, d)
    memf = mem.reshape(batch * n_mem, d)
    perm = s5_row_perm(tiles["s5_m"])
    gate_col0 = s5_width + 2 * conv_width
    for l in range(depth):
        proj = norm_mm(xf, norm_mix_g[l], w_in[l].astype(BF16), tm=tiles["proj_tm"], tn=tiles["proj_tn"],
                       act_from=gate_col0 // tiles["proj_tn"])
        lbr, lbi, bbr, bbi = s5_discretise(s5_lambda_re[l], s5_lambda_im[l], s5_log_dt[l],
                                           s5_b_re[l], s5_b_im[l])
        bre, bim, cre, cim = s5_block_diag(bbr, bbi, s5_c_re[l], s5_c_im[l])
        z = s5_scan(proj, perm, bre, bim, cre, cim, lbr.reshape(1, -1), lbi.reshape(1, -1), s5_d[l],
                    batch=batch, seq=seq, width=s5_width, m=tiles["s5_m"], cw=tiles["s5_cw"])
        y_a = mm_glu(z, s5_w_glu[l].astype(BF16), tm=tiles["glu_tm"], tn=tiles["glu_tn"])
        y_b = conv_branch(proj, conv_dw_w[l], conv_dw_b[l], conv_ln_g[l], conv_ln_b[l],
                          conv_w_out[l].astype(BF16), batch=batch, seq=seq, width=conv_width,
                          col0=s5_width, tm=tiles["conv_tm"])
        xf = combine(xf, proj, y_a, y_b, w_out[l].astype(BF16), gate_col0=gate_col0, tm=tiles["comb_tm"])

        q = norm_mm(xf, norm_xattn_g[l], xattn_wq[l].astype(BF16), tm=tiles["q_tm"], tn=tiles["q_tn"],
                    out_dtype=BF16)
        wkv = jnp.concatenate([xattn_wk[l], xattn_wv[l]], axis=1).astype(BF16)
        kv = norm_mm(memf, norm_mem_g[l], wkv, tm=batch * n_mem, tn=tiles["kv_tn"], out_dtype=BF16)
        xf = xattn(xf, q, kv, xattn_wo[l].astype(BF16), batch=batch, seq=seq, n_mem=n_mem,
                   heads=XATTN_HEADS, tm=tiles["xattn_tm"])

        w_r = jnp.concatenate([router_w_group[l],
                               jnp.transpose(router_w_expert[l], (1, 0, 2)).reshape(d, n_groups * epg)], axis=1)
        b_r = jnp.concatenate([router_b_group[l], router_b_expert[l].reshape(-1)])
        n_r = w_r.shape[1]
        w_pad = jnp.pad(w_r, ((0, 0), (0, LANES - n_r)))
        b_pad = jnp.pad(b_r, (0, LANES - n_r)).reshape(1, LANES)
        h, info, counts = router(xf, norm_moe_g[l], w_pad, b_pad, n_groups=n_groups, epg=epg,
                                 tm=tiles["router_tm"])
        plan = moe_dispatch_plan(info, counts, t, n_groups * epg)
        y2 = experts(h, exp_w_gate[l], exp_w_up[l], exp_w_down[l], *plan)
        last = l + 1 == depth
        xf = moe_add(xf, y2, norm_final_g if last else jnp.ones((d,), F32), norm=last, tm=tiles["final_tm"])
    return xf.reshape(batch, seq, d)


def kernel(x, mem, norm_mix_g, w_in, s5_lambda_re, s5_lambda_im, s5_log_dt, s5_b_re, s5_b_im, s5_c_re, s5_c_im, s5_d, s5_w_glu, conv_dw_w, conv_dw_b, conv_ln_g, conv_ln_b, conv_w_out, w_out, norm_xattn_g, norm_mem_g, xattn_wq, xattn_wk, xattn_wv, xattn_wo, norm_moe_g, router_w_group, router_b_group, router_w_expert, router_b_expert, exp_w_gate, exp_w_up, exp_w_down, norm_final_g):
    return forward(x, mem, norm_mix_g, w_in, s5_lambda_re, s5_lambda_im, s5_log_dt, s5_b_re, s5_b_im,
                   s5_c_re, s5_c_im, s5_d, s5_w_glu, conv_dw_w, conv_dw_b, conv_ln_g, conv_ln_b, conv_w_out,
                   w_out, norm_xattn_g, norm_mem_g, xattn_wq, xattn_wk, xattn_wv, xattn_wo, norm_moe_g,
                   router_w_group, router_b_group, router_w_expert, router_b_expert, exp_w_gate, exp_w_up,
                   exp_w_down, norm_final_g, tiles=TILES)
```

```python
import functools
import math

import jax
import jax.numpy as jnp
from jax import lax
from jax.experimental import pallas as pl
from jax.experimental.pallas import tpu as pltpu

F32 = jnp.float32
BF16 = jnp.bfloat16

RMS_EPS = 1e-6
LN_EPS = 1e-5

S5_SLAB_CH = 256
CONV_HALO = 32
XATTN_HEADS = 4
MOE_BLOCK = 128
SUBLANES = 8
LANES = 128
VMEM_LIMIT = 56 * 1024 * 1024

TILES = dict(proj_tm=512, proj_tn=512, s5_m=64, s5_cw=512, glu_tm=512, glu_tn=512, conv_tm=256,
             comb_tm=256, q_tm=512, q_tn=512, kv_tn=512, xattn_tm=512, router_tm=512, moe_tm=256)


def _cparams(sem):
    return pltpu.CompilerParams(dimension_semantics=sem, vmem_limit_bytes=VMEM_LIMIT)


def _rms(x, g):
    return x * lax.rsqrt(jnp.mean(x * x, axis=-1, keepdims=True) + RMS_EPS) * g


def _sigmoid(x):
    return 1.0 / (1.0 + jnp.exp(-x))


def _gelu_tanh(x):
    c = math.sqrt(2.0 / math.pi)
    return 0.5 * x * (1.0 + jnp.tanh(c * (x + 0.044715 * (x * x * x))))


def _norm_mm_kernel(x_ref, g_ref, w_ref, o_ref, h_ref, *, act_from):
    j = pl.program_id(1)

    @pl.when(j == 0)
    def _():
        h_ref[...] = _rms(x_ref[...], g_ref[...]).astype(BF16)

    y = jnp.dot(h_ref[...], w_ref[...], preferred_element_type=F32)
    if act_from is None:
        o_ref[...] = y.astype(o_ref.dtype)
    else:
        @pl.when(j < act_from)
        def _():
            o_ref[...] = y.astype(o_ref.dtype)

        @pl.when(j >= act_from)
        def _():
            o_ref[...] = _sigmoid(y).astype(o_ref.dtype)


def norm_mm(x, g, w, *, tm, tn, out_dtype=F32, act_from=None):
    m, k = x.shape
    n = w.shape[1]
    return pl.pallas_call(
        functools.partial(_norm_mm_kernel, act_from=act_from),
        grid=(m // tm, n // tn),
        in_specs=[pl.BlockSpec((tm, k), lambda i, j: (i, 0)),
                  pl.BlockSpec((1, k), lambda i, j: (0, 0)),
                  pl.BlockSpec((k, tn), lambda i, j: (0, j))],
        out_specs=pl.BlockSpec((tm, tn), lambda i, j: (i, j)),
        out_shape=jax.ShapeDtypeStruct((m, n), out_dtype),
        scratch_shapes=[pltpu.VMEM((tm, k), BF16)],
        compiler_params=_cparams(("parallel", "arbitrary")),
        name="norm_mm",
    )(x, g.reshape(1, k), w)


def _mm_glu_kernel(z_ref, w_ref, o_ref, *, tn):
    z = z_ref[...]
    nh = o_ref.shape[1]
    for j in range(nh // tn):
        val = jnp.dot(z, w_ref[:, j * tn:(j + 1) * tn], preferred_element_type=F32)
        gt = jnp.dot(z, w_ref[:, nh + j * tn:nh + (j + 1) * tn], preferred_element_type=F32)
        o_ref[:, j * tn:(j + 1) * tn] = val * _sigmoid(gt)


def mm_glu(z, w, *, tm, tn):
    m, k = z.shape
    n = w.shape[1]
    return pl.pallas_call(
        functools.partial(_mm_glu_kernel, tn=tn),
        grid=(m // tm,),
        in_specs=[pl.BlockSpec((tm, k), lambda i: (i, 0)),
                  pl.BlockSpec((k, n), lambda i: (0, 0))],
        out_specs=pl.BlockSpec((tm, n // 2), lambda i: (i, 0)),
        out_shape=jax.ShapeDtypeStruct((m, n // 2), F32),
        compiler_params=_cparams(("parallel",)),
        name="s5_glu_mm",
    )(z, w)


def _norm_mm_rows_kernel(x_ref, g_ref, w_ref, o_ref, *, tn):
    h = _rms(x_ref[...], g_ref[...]).astype(BF16)
    for j in range(o_ref.shape[1] // tn):
        cols = slice(j * tn, (j + 1) * tn)
        o_ref[:, cols] = jnp.dot(h, w_ref[:, cols], preferred_element_type=F32).astype(o_ref.dtype)


def norm_mm_rows(x, g, w, *, tm, tn, out_dtype):
    m, k = x.shape
    n = w.shape[1]
    return pl.pallas_call(
        functools.partial(_norm_mm_rows_kernel, tn=tn),
        grid=(m // tm,),
        in_specs=[pl.BlockSpec((tm, k), lambda i: (i, 0)),
                  pl.BlockSpec((1, k), lambda i: (0, 0)),
                  pl.BlockSpec((k, n), lambda i: (0, 0))],
        out_specs=pl.BlockSpec((tm, n), lambda i: (i, 0)),
        out_shape=jax.ShapeDtypeStruct((m, n), out_dtype),
        compiler_params=_cparams(("parallel",)),
        name="norm_mm_rows",
    )(x, g.reshape(1, k), w)


def _s5_disc_kernel(lr_ref, li_ref, ldt_ref, br_ref, bi_ref, lbr_ref, lbi_ref, bbr_ref, bbi_ref):
    lr, li = lr_ref[...], li_ref[...]
    dt = jnp.exp(ldt_ref[...])
    mag = jnp.exp(lr * dt)
    ang = li * dt
    lb_re, lb_im = mag * jnp.cos(ang), mag * jnp.sin(ang)
    den = lr * lr + li * li
    nr, ni = lb_re - 1.0, lb_im
    coef_re = (nr * lr + ni * li) / den
    coef_im = (ni * lr - nr * li) / den
    lbr_ref[...] = lb_re
    lbi_ref[...] = lb_im
    br, bi = br_ref[...], bi_ref[...]
    bbr_ref[...] = coef_re[None] * br - coef_im[None] * bi
    bbi_ref[...] = coef_re[None] * bi + coef_im[None] * br


def s5_discretise(lam_re, lam_im, log_dt, b_re, b_im):
    g, n = lam_re.shape
    c = b_re.shape[-1]
    b_re_t = jnp.transpose(b_re, (2, 0, 1))
    b_im_t = jnp.transpose(b_im, (2, 0, 1))
    gn = jax.ShapeDtypeStruct((g, n), F32)
    cgn = jax.ShapeDtypeStruct((c, g, n), F32)
    return pl.pallas_call(
        _s5_disc_kernel,
        out_shape=(gn, gn, cgn, cgn),
        name="s5_discretise",
    )(lam_re, lam_im, log_dt.reshape(g, 1), b_re_t, b_im_t)


def s5_block_diag(bb_re, bb_im, c_re, c_im):
    c, g, n = bb_re.shape
    gs = S5_SLAB_CH // c
    n_slab = g // gs
    eye = jnp.eye(gs, dtype=F32)

    def bmat(bb):
        b = jnp.transpose(bb, (1, 0, 2)).reshape(n_slab, gs, c, n)
        full = b[:, :, :, None, :] * eye[None, :, None, :, None]
        return full.reshape(n_slab, gs * c, gs * n).astype(BF16)

    def cmat(cc):
        cm = jnp.transpose(cc.reshape(n_slab, gs, c, n), (0, 1, 3, 2))
        full = cm[:, :, :, None, :] * eye[None, :, None, :, None]
        return full.reshape(n_slab, gs * n, gs * c).astype(BF16)

    return bmat(bb_re), bmat(bb_im), cmat(c_re), cmat(c_im)


def s5_row_perm(m):
    rows = SUBLANES * m
    r = jnp.arange(rows)
    src = (r % SUBLANES) * m + r // SUBLANES
    return (src[:, None] == jnp.arange(rows)[None, :]).astype(BF16)


def _s5_scan_kernel(u_ref, perm_ref, permt_ref, bre_ref, bim_ref, cre_ref, cim_ref, lre_ref, lim_ref,
                    d_ref, o_ref, xre_ref, xim_ref, pre_ref, pim_ref, car_re_ref, car_im_ref, *, m, cw):
    t = pl.program_id(1)
    n_state = lre_ref.shape[1]
    n_slab = bre_ref.shape[0]
    slab_states = n_state // n_slab

    @pl.when(t == 0)
    def _():
        car_re_ref[...] = jnp.zeros_like(car_re_ref)
        car_im_ref[...] = jnp.zeros_like(car_im_ref)
        lr, li = lre_ref[...], lim_ref[...]
        pre_ref[0:1, :] = lr
        pim_ref[0:1, :] = li

        def pw(tau, carry):
            pr, pi = carry
            nr = pr * lr - pi * li
            ni = pr * li + pi * lr
            pre_ref[pl.ds(tau, 1), :] = nr
            pim_ref[pl.ds(tau, 1), :] = ni
            return nr, ni

        lax.fori_loop(1, m, pw, (lr, li))

    u = u_ref[...]
    up = jnp.dot(perm_ref[...], u.astype(BF16), preferred_element_type=F32).astype(BF16)
    for s in range(n_slab):
        us = up[:, s * S5_SLAB_CH:(s + 1) * S5_SLAB_CH]
        cols = slice(s * slab_states, (s + 1) * slab_states)
        xre_ref[:, cols] = jnp.dot(us, bre_ref[s], preferred_element_type=F32)
        xim_ref[:, cols] = jnp.dot(us, bim_ref[s], preferred_element_type=F32)

    row_id = lax.broadcasted_iota(jnp.int32, (SUBLANES, cw), 0)
    for cb in range(n_state // cw):
        cols = slice(cb * cw, (cb + 1) * cw)
        lr = jnp.broadcast_to(lre_ref[:, cols], (SUBLANES, cw))
        li = jnp.broadcast_to(lim_ref[:, cols], (SUBLANES, cw))

        def sweep1(tau, carry, cols=cols, lr=lr, li=li):
            sr, si = carry
            r0 = pl.multiple_of(tau * SUBLANES, SUBLANES)
            nr = sr * lr - si * li + xre_ref[pl.ds(r0, SUBLANES), cols]
            ni = sr * li + si * lr + xim_ref[pl.ds(r0, SUBLANES), cols]
            xre_ref[pl.ds(r0, SUBLANES), cols] = nr
            xim_ref[pl.ds(r0, SUBLANES), cols] = ni
            return nr, ni

        zero = jnp.zeros((SUBLANES, cw), F32)
        er, ei = lax.fori_loop(0, m, sweep1, (zero, zero), unroll=4)

        lmr = jnp.broadcast_to(pre_ref[m - 1:m, cols], (SUBLANES, cw))
        lmi = jnp.broadcast_to(pim_ref[m - 1:m, cols], (SUBLANES, cw))
        ir = jnp.where(row_id == 0, jnp.broadcast_to(car_re_ref[:, cols], (SUBLANES, cw)), 0.0)
        ii = jnp.where(row_id == 0, jnp.broadcast_to(car_im_ref[:, cols], (SUBLANES, cw)), 0.0)
        nr, ni = ir, ii
        for k in range(1, SUBLANES + 1):
            nr = ir * lmr - ii * lmi + er
            ni = ir * lmi + ii * lmr + ei
            if k < SUBLANES:
                ir = jnp.where(row_id == k, pltpu.roll(nr, 1, axis=0), ir)
                ii = jnp.where(row_id == k, pltpu.roll(ni, 1, axis=0), ii)
        car_re_ref[:, cols] = nr[SUBLANES - 1:SUBLANES, :]
        car_im_ref[:, cols] = ni[SUBLANES - 1:SUBLANES, :]

        def sweep2(tau, carry, cols=cols, ir=ir, ii=ii):
            r0 = pl.multiple_of(tau * SUBLANES, SUBLANES)
            pr = jnp.broadcast_to(pre_ref[pl.ds(tau, 1), cols], (SUBLANES, cw))
            pi = jnp.broadcast_to(pim_ref[pl.ds(tau, 1), cols], (SUBLANES, cw))
            xre_ref[pl.ds(r0, SUBLANES), cols] += pr * ir - pi * ii
            xim_ref[pl.ds(r0, SUBLANES), cols] += pr * ii + pi * ir
            return carry

        lax.fori_loop(0, m, sweep2, 0, unroll=4)

    ys = []
    for s in range(n_slab):
        cols = slice(s * slab_states, (s + 1) * slab_states)
        ys.append(jnp.dot(xre_ref[:, cols].astype(BF16), cre_ref[s], preferred_element_type=F32)
                  - jnp.dot(xim_ref[:, cols].astype(BF16), cim_ref[s], preferred_element_type=F32))
    yp = jnp.concatenate(ys, axis=1)
    pt = permt_ref[...]
    y1 = yp.astype(BF16)
    r1 = yp - y1.astype(F32)
    y2 = r1.astype(BF16)
    y3 = (r1 - y2.astype(F32)).astype(BF16)
    y = (jnp.dot(pt, y1, preferred_element_type=F32) + jnp.dot(pt, y2, preferred_element_type=F32)
         + jnp.dot(pt, y3, preferred_element_type=F32))
    y = y + d_ref[...] * u
    o_ref[...] = _gelu_tanh(y).astype(o_ref.dtype)


def s5_scan(proj, perm, bre, bim, cre, cim, lre, lim, d_skip, *, batch, seq, width, m, cw):
    rows = SUBLANES * m
    n_t = seq // rows
    n_state = lre.shape[1]
    const3 = lambda b, t: (0, 0, 0)
    const2 = lambda b, t: (0, 0)
    return pl.pallas_call(
        functools.partial(_s5_scan_kernel, m=m, cw=cw),
        grid=(batch, n_t),
        in_specs=[pl.BlockSpec((rows, width), lambda b, t: (b * n_t + t, 0)),
                  pl.BlockSpec((rows, rows), const2),
                  pl.BlockSpec((rows, rows), const2),
                  pl.BlockSpec(bre.shape, const3),
                  pl.BlockSpec(bim.shape, const3),
                  pl.BlockSpec(cre.shape, const3),
                  pl.BlockSpec(cim.shape, const3),
                  pl.BlockSpec((1, n_state), const2),
                  pl.BlockSpec((1, n_state), const2),
                  pl.BlockSpec((1, width), const2)],
        out_specs=pl.BlockSpec((rows, width), lambda b, t: (b * n_t + t, 0)),
        out_shape=jax.ShapeDtypeStruct((batch * seq, width), BF16),
        scratch_shapes=[pltpu.VMEM((rows, n_state), F32), pltpu.VMEM((rows, n_state), F32),
                        pltpu.VMEM((m, n_state), F32), pltpu.VMEM((m, n_state), F32),
                        pltpu.VMEM((1, n_state), F32), pltpu.VMEM((1, n_state), F32)],
        compiler_params=_cparams(("parallel", "arbitrary")),
        name="s5_scan",
    )(proj, perm, perm.T, bre, bim, cre, cim, lre, lim, d_skip.reshape(1, width))


def _conv_kernel(a_ref, b_ref, ah_ref, bh_ref, dww_ref, dwb_ref, lng_ref, lnb_ref, w_ref, o_ref,
                 zs_ref, zsh_ref, yc_ref, *, tm, taps):
    i = pl.program_id(1)
    zh = ah_ref[...] * _sigmoid(bh_ref[...])
    zs_ref[0:CONV_HALO, :] = jnp.where(i == 0, 0.0, zh)
    zs_ref[CONV_HALO:, :] = a_ref[...] * _sigmoid(b_ref[...])
    off0 = CONV_HALO - (taps - 1)
    rc = 16
    width = a_ref.shape[1]
    n_sh = zsh_ref.shape[1]
    for s in range(1, SUBLANES):
        zsh_ref[s - 1] = zs_ref[s:s + n_sh, :]

    def chunk(c, carry):
        r0 = pl.multiple_of(c * rc, rc)
        acc = jnp.broadcast_to(dwb_ref[...], (rc, width))
        for j in range(taps):
            q, s = divmod(off0 + j, SUBLANES)
            rq = pl.multiple_of(r0 + q * SUBLANES, SUBLANES)
            zrow = zs_ref[pl.ds(rq, rc), :] if s == 0 else zsh_ref[s - 1, pl.ds(rq, rc), :]
            acc = acc + dww_ref[j:j + 1, :] * zrow
        yc_ref[pl.ds(r0, rc), :] = acc
        return carry

    lax.fori_loop(0, tm // rc, chunk, 0)
    y = yc_ref[...]
    mu = jnp.mean(y, axis=-1, keepdims=True)
    var = jnp.mean(jnp.square(y - mu), axis=-1, keepdims=True)
    y = (y - mu) * lax.rsqrt(var + LN_EPS) * lng_ref[...] + lnb_ref[...]
    y = y * _sigmoid(y)
    o_ref[...] = jnp.dot(y.astype(BF16), w_ref[...], preferred_element_type=F32)


def conv_branch(proj, dw_w, dw_b, ln_g, ln_b, w_out, *, batch, seq, width, col0, tm):
    n_t = seq // tm
    ca, cb = col0 // width, col0 // width + 1
    hb = tm // CONV_HALO
    taps = dw_w.shape[0]
    d_out = w_out.shape[1]
    const2 = lambda b, i: (0, 0)
    halo = lambda b, i: jnp.maximum((b * n_t + i) * hb - 1, 0)
    return pl.pallas_call(
        functools.partial(_conv_kernel, tm=tm, taps=taps),
        grid=(batch, n_t),
        in_specs=[pl.BlockSpec((tm, width), lambda b, i: (b * n_t + i, ca)),
                  pl.BlockSpec((tm, width), lambda b, i: (b * n_t + i, cb)),
                  pl.BlockSpec((CONV_HALO, width), lambda b, i: (halo(b, i), ca)),
                  pl.BlockSpec((CONV_HALO, width), lambda b, i: (halo(b, i), cb)),
                  pl.BlockSpec((taps, width), const2),
                  pl.BlockSpec((1, width), const2),
                  pl.BlockSpec((1, width), const2),
                  pl.BlockSpec((1, width), const2),
                  pl.BlockSpec((width, d_out), const2)],
        out_specs=pl.BlockSpec((tm, d_out), lambda b, i: (b * n_t + i, 0)),
        out_shape=jax.ShapeDtypeStruct((batch * seq, d_out), F32),
        scratch_shapes=[pltpu.VMEM((tm + CONV_HALO, width), F32),
                        pltpu.VMEM((SUBLANES - 1, tm + CONV_HALO - SUBLANES, width), F32),
                        pltpu.VMEM((tm, width), F32)],
        compiler_params=_cparams(("parallel", "arbitrary")),
        name="conv_branch",
    )(proj, proj, proj, proj, dw_w, dw_b.reshape(1, width), ln_g.reshape(1, width),
      ln_b.reshape(1, width), w_out)


def _combine_kernel(*refs, parts):
    x_ref, ya_ref, yb_ref, w_ref, o_ref = refs[0], refs[1 + 2 * parts], refs[2 + 2 * parts], refs[-2], refs[-1]
    gw = ya_ref.shape[1] // parts
    acc = x_ref[...]
    for p in range(parts):
        cols = slice(p * gw, (p + 1) * gw)
        mix = refs[1 + p][...] * ya_ref[:, cols] + refs[1 + parts + p][...] * yb_ref[:, cols]
        acc = acc + jnp.dot(mix.astype(BF16), w_ref[cols, :], preferred_element_type=F32)
    o_ref[...] = acc


def combine(x, proj, y_a, y_b, w_out, *, gate_col0, tm):
    m, d = x.shape
    gw = math.gcd(gate_col0, d)
    parts = d // gw
    row = lambda c: (lambda i: (i, c))
    gate_specs = [pl.BlockSpec((tm, gw), row((gate_col0 + s * d) // gw + p)) for s in range(2) for p in range(parts)]
    return pl.pallas_call(
        functools.partial(_combine_kernel, parts=parts),
        grid=(m // tm,),
        in_specs=[pl.BlockSpec((tm, d), row(0))] + gate_specs +
                 [pl.BlockSpec((tm, d), row(0)),
                  pl.BlockSpec((tm, d), row(0)),
                  pl.BlockSpec((d, d), lambda i: (0, 0))],
        out_specs=pl.BlockSpec((tm, d), row(0)),
        out_shape=jax.ShapeDtypeStruct((m, d), F32),
        compiler_params=_cparams(("parallel",)),
        name="gated_combine",
    )(x, *([proj] * (2 * parts)), y_a, y_b, w_out)


def _xattn_kernel(x_ref, q_ref, kv_ref, wo_ref, o_ref, *, heads, scale):
    d = x_ref.shape[1]
    hd = d // heads
    acc = x_ref[...]
    for h in range(heads):
        q = q_ref[:, h * hd:(h + 1) * hd]
        k = kv_ref[:, h * hd:(h + 1) * hd]
        v = kv_ref[:, d + h * hd:d + (h + 1) * hd]
        s = lax.dot_general(q, k, (((1,), (1,)), ((), ())), preferred_element_type=F32) * scale
        s = s - jnp.max(s, axis=-1, keepdims=True)
        p = jnp.exp(s)
        p = p / jnp.sum(p, axis=-1, keepdims=True)
        o = jnp.dot(p.astype(BF16), v, preferred_element_type=F32)
        acc = acc + jnp.dot(o.astype(BF16), wo_ref[h * hd:(h + 1) * hd, :], preferred_element_type=F32)
    o_ref[...] = acc


def xattn(x, q, kv, wo, *, batch, seq, n_mem, heads, tm):
    d = x.shape[1]
    n_t = seq // tm
    scale = 1.0 / math.sqrt(d // heads)
    return pl.pallas_call(
        functools.partial(_xattn_kernel, heads=heads, scale=scale),
        grid=(batch, n_t),
        in_specs=[pl.BlockSpec((tm, d), lambda b, i: (b * n_t + i, 0)),
                  pl.BlockSpec((tm, d), lambda b, i: (b * n_t + i, 0)),
                  pl.BlockSpec((n_mem, 2 * d), lambda b, i: (b, 0)),
                  pl.BlockSpec((d, d), lambda b, i: (0, 0))],
        out_specs=pl.BlockSpec((tm, d), lambda b, i: (b * n_t + i, 0)),
        out_shape=jax.ShapeDtypeStruct((batch * seq, d), F32),
        compiler_params=_cparams(("parallel", "parallel")),
        name="xattn",
    )(x, q, kv, wo)


def _router_kernel(x_ref, g_ref, w_ref, b_ref, tri_ref, h_ref, info_ref, infot_ref, cnt_ref, run_ref, *,
                   n_groups, epg):
    i = pl.program_id(0)
    tm = x_ref.shape[0]

    @pl.when(i == 0)
    def _():
        run_ref[...] = jnp.zeros_like(run_ref)

    h = _rms(x_ref[...], g_ref[...])
    h_ref[...] = h
    logits = jnp.dot(h, w_ref[...], preferred_element_type=F32,
                     precision=lax.Precision.HIGHEST) + b_ref[...]
    lane = lax.broadcasted_iota(jnp.int32, (tm, LANES), 1).astype(F32)
    neg = jnp.float32(-jnp.inf)
    big = jnp.float32(LANES)
    is_grp = lane < n_groups
    cl = jnp.where(is_grp, logits, neg)
    cmax = jnp.max(cl, axis=-1, keepdims=True)
    g_idx = jnp.min(jnp.where(cl == cmax, lane, big), axis=-1, keepdims=True)
    p_sel = 1.0 / jnp.sum(jnp.where(is_grp, jnp.exp(cl - cmax), 0.0), axis=-1, keepdims=True)
    lo = n_groups + g_idx * epg
    in_grp = (lane >= lo) & (lane < lo + epg)
    fl = jnp.where(in_grp, logits, neg)
    v1 = jnp.max(fl, axis=-1, keepdims=True)
    i1 = jnp.min(jnp.where(fl == v1, lane, big), axis=-1, keepdims=True)
    fl2 = jnp.where(lane == i1, neg, fl)
    v2 = jnp.max(fl2, axis=-1, keepdims=True)
    i2 = jnp.min(jnp.where(fl2 == v2, lane, big), axis=-1, keepdims=True)
    e2x = jnp.exp(v2 - v1)
    w1 = p_sel / (1.0 + e2x)
    w2 = p_sel * e2x / (1.0 + e2x)
    e1 = i1 - n_groups
    e2 = i2 - n_groups
    oh1 = lane == e1
    oh2 = lane == e2
    both = jnp.where(oh1 | oh2, 1.0, 0.0)
    before = jnp.dot(tri_ref[...], both.astype(BF16), preferred_element_type=F32) + run_ref[...]
    r1 = jnp.sum(jnp.where(oh1, before, 0.0), axis=-1, keepdims=True)
    r2 = jnp.sum(jnp.where(oh2, before, 0.0), axis=-1, keepdims=True)
    run_ref[...] += jnp.sum(both, axis=0, keepdims=True)
    cnt_ref[...] = run_ref[...]
    info = jnp.where(lane == 0, e1, 0.0)
    info = jnp.where(lane == 1, e2, info)
    info = jnp.where(lane == 2, r1, info)
    info = jnp.where(lane == 3, r2, info)
    info = jnp.where(lane == 4, w1, info)
    info = jnp.where(lane == 5, w2, info)
    info_ref[...] = info
    infot_ref[...] = jnp.transpose(info)[0:SUBLANES, :]


def router(x, g, w_pad, b_pad, *, n_groups, epg, tm):
    m, d = x.shape
    tri = (jnp.arange(tm)[:, None] > jnp.arange(tm)[None, :]).astype(BF16)
    return pl.pallas_call(
        functools.partial(_router_kernel, n_groups=n_groups, epg=epg),
        grid=(m // tm,),
        in_specs=[pl.BlockSpec((tm, d), lambda i: (i, 0)),
                  pl.BlockSpec((1, d), lambda i: (0, 0)),
                  pl.BlockSpec((d, LANES), lambda i: (0, 0)),
                  pl.BlockSpec((1, LANES), lambda i: (0, 0)),
                  pl.BlockSpec((tm, tm), lambda i: (0, 0))],
        out_specs=[pl.BlockSpec((tm, d), lambda i: (i, 0)),
                   pl.BlockSpec((tm, LANES), lambda i: (i, 0)),
                   pl.BlockSpec((SUBLANES, tm), lambda i: (0, i)),
                   pl.BlockSpec((1, LANES), lambda i: (0, 0))],
        out_shape=(jax.ShapeDtypeStruct((m, d), F32),
                   jax.ShapeDtypeStruct((m, LANES), F32),
                   jax.ShapeDtypeStruct((SUBLANES, m), F32),
                   jax.ShapeDtypeStruct((1, LANES), F32)),
        scratch_shapes=[pltpu.VMEM((1, LANES), F32)],
        compiler_params=_cparams(("arbitrary",)),
        name="moe_router",
    )(x, g.reshape(1, d), w_pad, b_pad, tri)


def moe_dispatch_plan(infot, counts, n_tokens, n_experts):
    e = infot[0:2].astype(jnp.int32)
    rank = infot[2:4].astype(jnp.int32)
    counts = counts[0, :n_experts].astype(jnp.int32)
    pcounts = ((counts + MOE_BLOCK - 1) // MOE_BLOCK) * MOE_BLOCK
    pends = jnp.cumsum(pcounts)
    pstarts = pends - pcounts
    ids = jnp.arange(n_experts, dtype=jnp.int32)
    dest = rank + jnp.sum(jnp.where(e[:, :, None] == ids, pstarts, 0), axis=-1)
    n_blocks = -(-2 * n_tokens // MOE_BLOCK) + n_experts
    block_start = jnp.arange(n_blocks, dtype=jnp.int32) * MOE_BLOCK
    block_e = jnp.minimum(jnp.sum((pends[None, :] <= block_start[:, None]).astype(jnp.int32), axis=1),
                          n_experts - 1)
    n_used = (pends[-1] // MOE_BLOCK).reshape(1)
    return dest, block_e, n_used, pcounts, pends


def _dispatch_kernel(pc_ref, pe_ref, dest_ref, h_ref, xs_hbm, zbuf, sem, tsem, *, n_experts):
    i = pl.program_id(0)
    tm = h_ref.shape[0]

    @pl.when(i == 0)
    def _():
        zbuf[...] = jnp.zeros_like(zbuf)
        for wait in (False, True):
            for e in range(n_experts):
                @pl.when(pc_ref[e] > 0)
                def _():
                    row0 = pl.multiple_of(pe_ref[e] - MOE_BLOCK, MOE_BLOCK)
                    fill = pltpu.make_async_copy(zbuf, xs_hbm.at[pl.ds(row0, MOE_BLOCK)], sem)
                    fill.wait() if wait else fill.start()

    n_blocks = xs_hbm.shape[0] // MOE_BLOCK
    n_used = pe_ref[n_experts - 1] // MOE_BLOCK

    def tail_fill(b, wait):
        row0 = pl.multiple_of(b * MOE_BLOCK, MOE_BLOCK)
        fill = pltpu.make_async_copy(zbuf, xs_hbm.at[pl.ds(row0, MOE_BLOCK)], tsem)
        fill.wait() if wait else fill.start()

    @pl.when(i == 0)
    def _():
        lax.fori_loop(n_used, n_blocks, lambda b, c: (tail_fill(b, False), c)[1], 0)

    for k in range(2):
        for r in range(tm):
            pltpu.make_async_copy(h_ref.at[pl.ds(r, 1)], xs_hbm.at[pl.ds(dest_ref[k, r], 1)], sem).start()
    for k in range(2):
        pltpu.make_async_copy(h_ref, xs_hbm.at[pl.ds(0, tm)], sem).wait()

    @pl.when(i == pl.num_programs(0) - 1)
    def _():
        lax.fori_loop(n_used, n_blocks, lambda b, c: (tail_fill(b, True), c)[1], 0)


def moe_dispatch(h, dest, pcounts, pends, *, n_rows, tm):
    t, d = h.shape
    grid_spec = pltpu.PrefetchScalarGridSpec(
        num_scalar_prefetch=2,
        grid=(t // tm,),
        in_specs=[pl.BlockSpec((2, tm), lambda i, pc, pe: (0, i), memory_space=pltpu.SMEM),
                  pl.BlockSpec((tm, d), lambda i, pc, pe: (i, 0))],
        out_specs=pl.BlockSpec(memory_space=pl.ANY),
        scratch_shapes=[pltpu.VMEM((MOE_BLOCK, d), F32), pltpu.SemaphoreType.DMA(()),
                        pltpu.SemaphoreType.DMA(())],
    )
    return pl.pallas_call(
        functools.partial(_dispatch_kernel, n_experts=pcounts.shape[0]),
        grid_spec=grid_spec,
        out_shape=jax.ShapeDtypeStruct((n_rows, d), F32),
        compiler_params=_cparams(("arbitrary",)),
        name="moe_dispatch",
    )(pcounts, pends, dest, h)


def _experts_kernel(be_ref, nb_ref, x_ref, wg_ref, wu_ref, wd_ref, o_ref, wgb, wub, wdb):
    i = pl.program_id(0)

    @pl.when(i < nb_ref[0])
    def _():
        @pl.when((i == 0) | (be_ref[i] != be_ref[jnp.maximum(i - 1, 0)]))
        def _():
            wgb[...] = wg_ref[0].astype(BF16)
            wub[...] = wu_ref[0].astype(BF16)
            wdb[...] = wd_ref[0].astype(BF16)

        x = x_ref[...].astype(BF16)
        gte = jnp.dot(x, wgb[...], preferred_element_type=F32)
        up = jnp.dot(x, wub[...], preferred_element_type=F32)
        act = (gte * _sigmoid(gte) * up).astype(BF16)
        o_ref[...] = jnp.dot(act, wdb[...], preferred_element_type=F32)

    @pl.when(i >= nb_ref[0])
    def _():
        o_ref[...] = jnp.zeros_like(o_ref)


def experts(xs, w_gate, w_up, w_down, block_e, n_used):
    n_rows, d = xs.shape
    n_blocks = block_e.shape[0]
    de = w_gate.shape[2]
    grid_spec = pltpu.PrefetchScalarGridSpec(
        num_scalar_prefetch=2,
        grid=(n_blocks,),
        in_specs=[pl.BlockSpec((MOE_BLOCK, d), lambda i, be, nb: (jnp.minimum(i, nb[0] - 1), 0)),
                  pl.BlockSpec((1, d, de), lambda i, be, nb: (be[i], 0, 0)),
                  pl.BlockSpec((1, d, de), lambda i, be, nb: (be[i], 0, 0)),
                  pl.BlockSpec((1, de, d), lambda i, be, nb: (be[i], 0, 0))],
        out_specs=pl.BlockSpec((MOE_BLOCK, d), lambda i, be, nb: (i, 0)),
        scratch_shapes=[pltpu.VMEM((d, de), BF16), pltpu.VMEM((d, de), BF16), pltpu.VMEM((de, d), BF16)],
    )
    return pl.pallas_call(
        _experts_kernel,
        grid_spec=grid_spec,
        out_shape=jax.ShapeDtypeStruct((n_rows, d), F32),
        compiler_params=_cparams(("arbitrary",)),
        name="moe_experts",
    )(block_e, n_used, xs, w_gate, w_up, w_down)


def _moe_combine_kernel(dest_ref, destn_ref, x_ref, info_ref, g_ref, ys_hbm, o_ref, ybuf, sem, *, norm):
    i = pl.program_id(0)
    n = pl.num_programs(0)
    tm = x_ref.shape[0]
    slot = lax.rem(i, 2)

    def gather_start(dref, s):
        for k in range(2):
            for r in range(tm):
                pltpu.make_async_copy(ys_hbm.at[pl.ds(dref[k, r], 1)], ybuf.at[s, k, pl.ds(r, 1)],
                                      sem.at[s]).start()

    @pl.when(i == 0)
    def _():
        gather_start(dest_ref, 0)

    @pl.when(i + 1 < n)
    def _():
        gather_start(destn_ref, 1 - slot)

    for k in range(2):
        pltpu.make_async_copy(ys_hbm.at[pl.ds(0, tm)], ybuf.at[slot, k], sem.at[slot]).wait()
    w1 = info_ref[:, 4:5]
    w2 = info_ref[:, 5:6]
    s = x_ref[...] + (w1 * ybuf[slot, 0] + w2 * ybuf[slot, 1])
    o_ref[...] = _rms(s, g_ref[...]) if norm else s


def moe_combine(x, ys, dest, info, g, *, norm, tm):
    t, d = x.shape
    n = t // tm
    return pl.pallas_call(
        functools.partial(_moe_combine_kernel, norm=norm),
        grid=(n,),
        in_specs=[pl.BlockSpec((2, tm), lambda i: (0, i), memory_space=pltpu.SMEM),
                  pl.BlockSpec((2, tm), lambda i: (0, jnp.minimum(i + 1, n - 1)), memory_space=pltpu.SMEM),
                  pl.BlockSpec((tm, d), lambda i: (i, 0)),
                  pl.BlockSpec((tm, LANES), lambda i: (i, 0)),
                  pl.BlockSpec((1, d), lambda i: (0, 0)),
                  pl.BlockSpec(memory_space=pl.ANY)],
        out_specs=pl.BlockSpec((tm, d), lambda i: (i, 0)),
        out_shape=jax.ShapeDtypeStruct((t, d), F32),
        scratch_shapes=[pltpu.VMEM((2, 2, tm, d), F32), pltpu.SemaphoreType.DMA((2,))],
        compiler_params=_cparams(("arbitrary",)),
        name="moe_combine",
    )(dest, dest, x, info, g.reshape(1, d), ys)


def forward(x, mem, norm_mix_g, w_in, s5_lambda_re, s5_lambda_im, s5_log_dt, s5_b_re, s5_b_im, s5_c_re,
            s5_c_im, s5_d, s5_w_glu, conv_dw_w, conv_dw_b, conv_ln_g, conv_ln_b, conv_w_out, w_out,
            norm_xattn_g, norm_mem_g, xattn_wq, xattn_wk, xattn_wv, xattn_wo, norm_moe_g, router_w_group,
            router_b_group, router_w_expert, router_b_expert, exp_w_gate, exp_w_up, exp_w_down, norm_final_g,
            *, tiles):
    batch, seq, d = x.shape
    depth = w_in.shape[0]
    n_mem = mem.shape[1]
    s5_width = s5_d.shape[1]
    conv_width = conv_dw_b.shape[1]
    n_groups, epg = router_w_expert.shape[1], router_w_expert.shape[3]
    t = batch * seq
    xf = x.reshape(t, d)
    memf = mem.reshape(batch * n_mem, d)
    perm = s5_row_perm(tiles["s5_m"])
    gate_col0 = s5_width + 2 * conv_width
    for l in range(depth):
        proj = norm_mm(xf, norm_mix_g[l], w_in[l].astype(BF16), tm=tiles["proj_tm"], tn=tiles["proj_tn"],
                       act_from=gate_col0 // tiles["proj_tn"])
        lbr, lbi, bbr, bbi = s5_discretise(s5_lambda_re[l], s5_lambda_im[l], s5_log_dt[l],
                                           s5_b_re[l], s5_b_im[l])
        bre, bim, cre, cim = s5_block_diag(bbr, bbi, s5_c_re[l], s5_c_im[l])
        z = s5_scan(proj, perm, bre, bim, cre, cim, lbr.reshape(1, -1), lbi.reshape(1, -1), s5_d[l],
                    batch=batch, seq=seq, width=s5_width, m=tiles["s5_m"], cw=tiles["s5_cw"])
        y_a = mm_glu(z, s5_w_glu[l].astype(BF16), tm=tiles["glu_tm"], tn=tiles["glu_tn"])
        y_b = conv_branch(proj, conv_dw_w[l], conv_dw_b[l], conv_ln_g[l], conv_ln_b[l],
                          conv_w_out[l].astype(BF16), batch=batch, seq=seq, width=conv_width,
                          col0=s5_width, tm=tiles["conv_tm"])
        xf = combine(xf, proj, y_a, y_b, w_out[l].astype(BF16), gate_col0=gate_col0, tm=tiles["comb_tm"])

        q = norm_mm_rows(xf, norm_xattn_g[l], xattn_wq[l].astype(BF16), tm=tiles["q_tm"], tn=tiles["q_tn"],
                         out_dtype=BF16)
        wkv = jnp.concatenate([xattn_wk[l], xattn_wv[l]], axis=1).astype(BF16)
        kv = norm_mm(memf, norm_mem_g[l], wkv, tm=batch * n_mem, tn=tiles["kv_tn"], out_dtype=BF16)
        xf = xattn(xf, q, kv, xattn_wo[l].astype(BF16), batch=batch, seq=seq, n_mem=n_mem,
                   heads=XATTN_HEADS, tm=tiles["xattn_tm"])

        w_r = jnp.concatenate([router_w_group[l],
                               jnp.transpose(router_w_expert[l], (1, 0, 2)).reshape(d, n_groups * epg)], axis=1)
        b_r = jnp.concatenate([router_b_group[l], router_b_expert[l].reshape(-1)])
        n_r = w_r.shape[1]
        w_pad = jnp.pad(w_r, ((0, 0), (0, LANES - n_r)))
        b_pad = jnp.pad(b_r, (0, LANES - n_r)).reshape(1, LANES)
        h, info, infot, counts = router(xf, norm_moe_g[l], w_pad, b_pad, n_groups=n_groups, epg=epg,
                                        tm=tiles["router_tm"])
        dest, block_e, n_used, pcounts, pends = moe_dispatch_plan(infot, counts, t, n_groups * epg)
        xs = moe_dispatch(h, dest, pcounts, pends, n_rows=block_e.shape[0] * MOE_BLOCK, tm=tiles["moe_tm"])
        ys = experts(xs, exp_w_gate[l], exp_w_up[l], exp_w_down[l], block_e, n_used)
        last = l + 1 == depth
        xf = moe_combine(xf, ys, dest, info, norm_final_g if last else jnp.ones((d,), F32), norm=last,
                         tm=tiles["moe_tm"])
    return xf.reshape(batch, seq, d)


def kernel(x, mem, norm_mix_g, w_in, s5_lambda_re, s5_lambda_im, s5_log_dt, s5_b_re, s5_b_im, s5_c_re, s5_c_im, s5_d, s5_w_glu, conv_dw_w, conv_dw_b, conv_ln_g, conv_ln_b, conv_w_out, w_out, norm_xattn_g, norm_mem_g, xattn_wq, xattn_wk, xattn_wv, xattn_wo, norm_moe_g, router_w_group, router_b_group, router_w_expert, router_b_expert, exp_w_gate, exp_w_up, exp_w_down, norm_final_g):
    return forward(x, mem, norm_mix_g, w_in, s5_lambda_re, s5_lambda_im, s5_log_dt, s5_b_re, s5_b_im,
                   s5_c_re, s5_c_im, s5_d, s5_w_glu, conv_dw_w, conv_dw_b, conv_ln_g, conv_ln_b, conv_w_out,
                   w_out, norm_xattn_g, norm_mem_g, xattn_wq, xattn_wk, xattn_wv, xattn_wo, norm_moe_g,
                   router_w_group, router_b_group, router_w_expert, router_b_expert, exp_w_gate, exp_w_up,
                   exp_w_down, norm_final_g, tiles=TILES)
```

```python
import functools
import math

import jax
import jax.numpy as jnp
from jax import lax
from jax.experimental import pallas as pl
from jax.experimental.pallas import tpu as pltpu

F32 = jnp.float32
BF16 = jnp.bfloat16

RMS_EPS = 1e-6
LN_EPS = 1e-5

S5_SLAB_CH = 256
CONV_HALO = 32
CONV_ROW_CHUNK = 32
CONV_LANE_BLOCK = 256
XATTN_HEADS = 4
MOE_BLOCK = 128
SUBLANES = 8
LANES = 128
VMEM_LIMIT = 56 * 1024 * 1024

TILES = dict(proj_tm=512, proj_tn=512, s5_m=64, s5_cw=512, glu_tm=512, glu_tn=512, conv_tm=256,
             comb_tm=256, q_tm=512, q_tn=512, kv_tn=512, xattn_tm=512, router_tm=512, moe_tm=256)


def _cparams(sem):
    return pltpu.CompilerParams(dimension_semantics=sem, vmem_limit_bytes=VMEM_LIMIT)


def _rms(x, g):
    return x * lax.rsqrt(jnp.mean(x * x, axis=-1, keepdims=True) + RMS_EPS) * g


def _sigmoid(x):
    return 1.0 / (1.0 + jnp.exp(-x))


def _gelu_tanh(x):
    c = math.sqrt(2.0 / math.pi)
    return 0.5 * x * (1.0 + jnp.tanh(c * (x + 0.044715 * (x * x * x))))


def _norm_mm_kernel(x_ref, g_ref, w_ref, o_ref, h_ref, *, act_from):
    j = pl.program_id(1)

    @pl.when(j == 0)
    def _():
        h_ref[...] = _rms(x_ref[...], g_ref[...]).astype(BF16)

    y = jnp.dot(h_ref[...], w_ref[...], preferred_element_type=F32)
    if act_from is None:
        o_ref[...] = y.astype(o_ref.dtype)
    else:
        @pl.when(j < act_from)
        def _():
            o_ref[...] = y.astype(o_ref.dtype)

        @pl.when(j >= act_from)
        def _():
            o_ref[...] = _sigmoid(y).astype(o_ref.dtype)


def norm_mm(x, g, w, *, tm, tn, out_dtype=F32, act_from=None):
    m, k = x.shape
    n = w.shape[1]
    return pl.pallas_call(
        functools.partial(_norm_mm_kernel, act_from=act_from),
        grid=(m // tm, n // tn),
        in_specs=[pl.BlockSpec((tm, k), lambda i, j: (i, 0)),
                  pl.BlockSpec((1, k), lambda i, j: (0, 0)),
                  pl.BlockSpec((k, tn), lambda i, j: (0, j))],
        out_specs=pl.BlockSpec((tm, tn), lambda i, j: (i, j)),
        out_shape=jax.ShapeDtypeStruct((m, n), out_dtype),
        scratch_shapes=[pltpu.VMEM((tm, k), BF16)],
        compiler_params=_cparams(("parallel", "arbitrary")),
        name="norm_mm",
    )(x, g.reshape(1, k), w)


def _mm_glu_kernel(z_ref, w_ref, o_ref, *, tn):
    z = z_ref[...]
    nh = o_ref.shape[1]
    for j in range(nh // tn):
        val = jnp.dot(z, w_ref[:, j * tn:(j + 1) * tn], preferred_element_type=F32)
        gt = jnp.dot(z, w_ref[:, nh + j * tn:nh + (j + 1) * tn], preferred_element_type=F32)
        o_ref[:, j * tn:(j + 1) * tn] = val * _sigmoid(gt)


def mm_glu(z, w, *, tm, tn):
    m, k = z.shape
    n = w.shape[1]
    return pl.pallas_call(
        functools.partial(_mm_glu_kernel, tn=tn),
        grid=(m // tm,),
        in_specs=[pl.BlockSpec((tm, k), lambda i: (i, 0)),
                  pl.BlockSpec((k, n), lambda i: (0, 0), pipeline_mode=pl.Buffered(1))],
        out_specs=pl.BlockSpec((tm, n // 2), lambda i: (i, 0)),
        out_shape=jax.ShapeDtypeStruct((m, n // 2), F32),
        compiler_params=_cparams(("parallel",)),
        name="s5_glu_mm",
    )(z, w)


def _norm_mm_rows_kernel(x_ref, g_ref, w_ref, o_ref, *, tn):
    h = _rms(x_ref[...], g_ref[...]).astype(BF16)
    for j in range(o_ref.shape[1] // tn):
        cols = slice(j * tn, (j + 1) * tn)
        o_ref[:, cols] = jnp.dot(h, w_ref[:, cols], preferred_element_type=F32).astype(o_ref.dtype)


def norm_mm_rows(x, g, w, *, tm, tn, out_dtype):
    m, k = x.shape
    n = w.shape[1]
    return pl.pallas_call(
        functools.partial(_norm_mm_rows_kernel, tn=tn),
        grid=(m // tm,),
        in_specs=[pl.BlockSpec((tm, k), lambda i: (i, 0)),
                  pl.BlockSpec((1, k), lambda i: (0, 0)),
                  pl.BlockSpec((k, n), lambda i: (0, 0), pipeline_mode=pl.Buffered(1))],
        out_specs=pl.BlockSpec((tm, n), lambda i: (i, 0)),
        out_shape=jax.ShapeDtypeStruct((m, n), out_dtype),
        compiler_params=_cparams(("parallel",)),
        name="norm_mm_rows",
    )(x, g.reshape(1, k), w)


def _s5_disc_kernel(lr_ref, li_ref, ldt_ref, br_ref, bi_ref, lbr_ref, lbi_ref, bbr_ref, bbi_ref):
    lr, li = lr_ref[...], li_ref[...]
    dt = jnp.exp(ldt_ref[...])
    mag = jnp.exp(lr * dt)
    ang = li * dt
    lb_re, lb_im = mag * jnp.cos(ang), mag * jnp.sin(ang)
    den = lr * lr + li * li
    nr, ni = lb_re - 1.0, lb_im
    coef_re = (nr * lr + ni * li) / den
    coef_im = (ni * lr - nr * li) / den
    lbr_ref[...] = lb_re
    lbi_ref[...] = lb_im
    br, bi = br_ref[...], bi_ref[...]
    bbr_ref[...] = coef_re[None] * br - coef_im[None] * bi
    bbi_ref[...] = coef_re[None] * bi + coef_im[None] * br


def s5_discretise(lam_re, lam_im, log_dt, b_re, b_im):
    g, n = lam_re.shape
    c = b_re.shape[-1]
    b_re_t = jnp.transpose(b_re, (2, 0, 1))
    b_im_t = jnp.transpose(b_im, (2, 0, 1))
    gn = jax.ShapeDtypeStruct((g, n), F32)
    cgn = jax.ShapeDtypeStruct((c, g, n), F32)
    return pl.pallas_call(
        _s5_disc_kernel,
        out_shape=(gn, gn, cgn, cgn),
        name="s5_discretise",
    )(lam_re, lam_im, log_dt.reshape(g, 1), b_re_t, b_im_t)


def s5_block_diag(bb_re, bb_im, c_re, c_im):
    c, g, n = bb_re.shape
    gs = S5_SLAB_CH // c
    n_slab = g // gs
    eye = jnp.eye(gs, dtype=F32)

    def bmat(bb):
        b = jnp.transpose(bb, (1, 0, 2)).reshape(n_slab, gs, c, n)
        full = b[:, :, :, None, :] * eye[None, :, None, :, None]
        return full.reshape(n_slab, gs * c, gs * n).astype(BF16)

    def cmat(cc):
        cm = jnp.transpose(cc.reshape(n_slab, gs, c, n), (0, 1, 3, 2))
        full = cm[:, :, :, None, :] * eye[None, :, None, :, None]
        return full.reshape(n_slab, gs * n, gs * c).astype(BF16)

    return bmat(bb_re), bmat(bb_im), cmat(c_re), cmat(c_im)


def s5_row_perm(m):
    rows = SUBLANES * m
    r = jnp.arange(rows)
    src = (r % SUBLANES) * m + r // SUBLANES
    return (src[:, None] == jnp.arange(rows)[None, :]).astype(BF16)


def _s5_scan_kernel(u_ref, perm_ref, permt_ref, bre_ref, bim_ref, cre_ref, cim_ref, lre_ref, lim_ref,
                    d_ref, o_ref, xre_ref, xim_ref, pre_ref, pim_ref, car_re_ref, car_im_ref, *, m, cw):
    t = pl.program_id(1)
    n_state = lre_ref.shape[1]
    n_slab = bre_ref.shape[0]
    slab_states = n_state // n_slab

    @pl.when(t == 0)
    def _():
        car_re_ref[...] = jnp.zeros_like(car_re_ref)
        car_im_ref[...] = jnp.zeros_like(car_im_ref)
        lr, li = lre_ref[...], lim_ref[...]

        def pw(_, carry):
            pr, pi = carry
            return pr * lr - pi * li, pr * li + pi * lr

        pr, pi = lax.fori_loop(1, m, pw, (lr, li))
        pre_ref[...] = pr
        pim_ref[...] = pi

    u = u_ref[...]
    up = jnp.dot(perm_ref[...], u.astype(BF16), preferred_element_type=F32).astype(BF16)
    for s in range(n_slab):
        us = up[:, s * S5_SLAB_CH:(s + 1) * S5_SLAB_CH]
        cols = slice(s * slab_states, (s + 1) * slab_states)
        xre_ref[:, cols] = jnp.dot(us, bre_ref[s], preferred_element_type=F32)
        xim_ref[:, cols] = jnp.dot(us, bim_ref[s], preferred_element_type=F32)

    row_id = lax.broadcasted_iota(jnp.int32, (SUBLANES, cw), 0)
    for cb in range(n_state // cw):
        cols = slice(cb * cw, (cb + 1) * cw)
        lr = jnp.broadcast_to(lre_ref[:, cols], (SUBLANES, cw))
        li = jnp.broadcast_to(lim_ref[:, cols], (SUBLANES, cw))

        def step(tau, carry, store, cols=cols, lr=lr, li=li):
            sr, si = carry
            r0 = pl.multiple_of(tau * SUBLANES, SUBLANES)
            nr = sr * lr - si * li + xre_ref[pl.ds(r0, SUBLANES), cols]
            ni = sr * li + si * lr + xim_ref[pl.ds(r0, SUBLANES), cols]
            if store:
                xre_ref[pl.ds(r0, SUBLANES), cols] = nr
                xim_ref[pl.ds(r0, SUBLANES), cols] = ni
            return nr, ni

        zero = jnp.zeros((SUBLANES, cw), F32)
        er, ei = lax.fori_loop(0, m, functools.partial(step, store=False), (zero, zero), unroll=8)

        lmr = jnp.broadcast_to(pre_ref[:, cols], (SUBLANES, cw))
        lmi = jnp.broadcast_to(pim_ref[:, cols], (SUBLANES, cw))
        ir = jnp.where(row_id == 0, jnp.broadcast_to(car_re_ref[:, cols], (SUBLANES, cw)), 0.0)
        ii = jnp.where(row_id == 0, jnp.broadcast_to(car_im_ref[:, cols], (SUBLANES, cw)), 0.0)
        for k in range(1, SUBLANES):
            nr = ir * lmr - ii * lmi + er
            ni = ir * lmi + ii * lmr + ei
            ir = jnp.where(row_id == k, pltpu.roll(nr, 1, axis=0), ir)
            ii = jnp.where(row_id == k, pltpu.roll(ni, 1, axis=0), ii)

        fr, fi = lax.fori_loop(0, m, functools.partial(step, store=True), (ir, ii), unroll=8)
        car_re_ref[:, cols] = fr[SUBLANES - 1:SUBLANES, :]
        car_im_ref[:, cols] = fi[SUBLANES - 1:SUBLANES, :]

    ys = []
    for s in range(n_slab):
        cols = slice(s * slab_states, (s + 1) * slab_states)
        ys.append(jnp.dot(xre_ref[:, cols].astype(BF16), cre_ref[s], preferred_element_type=F32)
                  - jnp.dot(xim_ref[:, cols].astype(BF16), cim_ref[s], preferred_element_type=F32))
    yp = jnp.concatenate(ys, axis=1)
    pt = permt_ref[...]
    y1 = yp.astype(BF16)
    r1 = yp - y1.astype(F32)
    y2 = r1.astype(BF16)
    y3 = (r1 - y2.astype(F32)).astype(BF16)
    y = (jnp.dot(pt, y1, preferred_element_type=F32) + jnp.dot(pt, y2, preferred_element_type=F32)
         + jnp.dot(pt, y3, preferred_element_type=F32))
    y = y + d_ref[...] * u
    o_ref[...] = _gelu_tanh(y).astype(o_ref.dtype)


def s5_scan(proj, perm, bre, bim, cre, cim, lre, lim, d_skip, *, batch, seq, width, m, cw):
    rows = SUBLANES * m
    n_t = seq // rows
    n_state = lre.shape[1]
    const3 = lambda b, t: (0, 0, 0)
    const2 = lambda b, t: (0, 0)
    return pl.pallas_call(
        functools.partial(_s5_scan_kernel, m=m, cw=cw),
        grid=(batch, n_t),
        in_specs=[pl.BlockSpec((rows, width), lambda b, t: (b * n_t + t, 0)),
                  pl.BlockSpec((rows, rows), const2),
                  pl.BlockSpec((rows, rows), const2),
                  pl.BlockSpec(bre.shape, const3),
                  pl.BlockSpec(bim.shape, const3),
                  pl.BlockSpec(cre.shape, const3),
                  pl.BlockSpec(cim.shape, const3),
                  pl.BlockSpec((1, n_state), const2),
                  pl.BlockSpec((1, n_state), const2),
                  pl.BlockSpec((1, width), const2)],
        out_specs=pl.BlockSpec((rows, width), lambda b, t: (b * n_t + t, 0)),
        out_shape=jax.ShapeDtypeStruct((batch * seq, width), BF16),
        scratch_shapes=[pltpu.VMEM((rows, n_state), F32), pltpu.VMEM((rows, n_state), F32),
                        pltpu.VMEM((1, n_state), F32), pltpu.VMEM((1, n_state), F32),
                        pltpu.VMEM((1, n_state), F32), pltpu.VMEM((1, n_state), F32)],
        compiler_params=_cparams(("parallel", "arbitrary")),
        name="s5_scan",
    )(proj, perm, perm.T, bre, bim, cre, cim, lre, lim, d_skip.reshape(1, width))


def _conv_kernel(a_ref, b_ref, ah_ref, bh_ref, dww_ref, dwb_ref, lng_ref, lnb_ref, w_ref, o_ref,
                 zs_ref, zsh_ref, wb_ref, yc_ref, *, tm, taps):
    i = pl.program_id(1)
    zh = ah_ref[...] * _sigmoid(bh_ref[...])
    zs_ref[0:CONV_HALO, :] = jnp.where(i == 0, 0.0, zh)
    zs_ref[CONV_HALO:, :] = a_ref[...] * _sigmoid(b_ref[...])
    off0 = CONV_HALO - (taps - 1)
    width = a_ref.shape[1]
    n_sh = zsh_ref.shape[1]
    for s in range(1, SUBLANES):
        zsh_ref[s - 1] = zs_ref[s:s + n_sh, :]

    @pl.when(i == 0)
    def _():
        for j in range(taps):
            wb_ref[j] = jnp.broadcast_to(dww_ref[j:j + 1, :], (SUBLANES, width))
        wb_ref[taps] = jnp.broadcast_to(dwb_ref[...], (SUBLANES, width))

    rg = CONV_ROW_CHUNK // SUBLANES

    def chunk(c, carry):
        r0 = pl.multiple_of(c * CONV_ROW_CHUNK, CONV_ROW_CHUNK)
        for lb in range(width // CONV_LANE_BLOCK):
            lanes = slice(lb * CONV_LANE_BLOCK, (lb + 1) * CONV_LANE_BLOCK)
            bias = wb_ref[taps, :, lanes]
            acc = [bias] * rg
            for j in range(taps):
                q, s = divmod(off0 + j, SUBLANES)
                w = wb_ref[j, :, lanes]
                for g in range(rg):
                    rq = pl.multiple_of(r0 + (q + g) * SUBLANES, SUBLANES)
                    zrow = (zs_ref[pl.ds(rq, SUBLANES), lanes] if s == 0
                            else zsh_ref[s - 1, pl.ds(rq, SUBLANES), lanes])
                    acc[g] = acc[g] + w * zrow
            for g in range(rg):
                yc_ref[pl.ds(pl.multiple_of(r0 + g * SUBLANES, SUBLANES), SUBLANES), lanes] = acc[g]
        return carry

    lax.fori_loop(0, tm // CONV_ROW_CHUNK, chunk, 0)
    y = yc_ref[...]
    mu = jnp.mean(y, axis=-1, keepdims=True)
    var = jnp.mean(jnp.square(y - mu), axis=-1, keepdims=True)
    y = (y - mu) * lax.rsqrt(var + LN_EPS) * lng_ref[...] + lnb_ref[...]
    y = y * _sigmoid(y)
    o_ref[...] = jnp.dot(y.astype(BF16), w_ref[...], preferred_element_type=F32)


def conv_branch(proj, dw_w, dw_b, ln_g, ln_b, w_out, *, batch, seq, width, col0, tm):
    n_t = seq // tm
    ca, cb = col0 // width, col0 // width + 1
    hb = tm // CONV_HALO
    taps = dw_w.shape[0]
    d_out = w_out.shape[1]
    const2 = lambda b, i: (0, 0)
    halo = lambda b, i: jnp.maximum((b * n_t + i) * hb - 1, 0)
    return pl.pallas_call(
        functools.partial(_conv_kernel, tm=tm, taps=taps),
        grid=(batch, n_t),
        in_specs=[pl.BlockSpec((tm, width), lambda b, i: (b * n_t + i, ca)),
                  pl.BlockSpec((tm, width), lambda b, i: (b * n_t + i, cb)),
                  pl.BlockSpec((CONV_HALO, width), lambda b, i: (halo(b, i), ca)),
                  pl.BlockSpec((CONV_HALO, width), lambda b, i: (halo(b, i), cb)),
                  pl.BlockSpec((taps, width), const2),
                  pl.BlockSpec((1, width), const2),
                  pl.BlockSpec((1, width), const2),
                  pl.BlockSpec((1, width), const2),
                  pl.BlockSpec((width, d_out), const2)],
        out_specs=pl.BlockSpec((tm, d_out), lambda b, i: (b * n_t + i, 0)),
        out_shape=jax.ShapeDtypeStruct((batch * seq, d_out), F32),
        scratch_shapes=[pltpu.VMEM((tm + CONV_HALO, width), F32),
                        pltpu.VMEM((SUBLANES - 1, tm + CONV_HALO - SUBLANES, width), F32),
                        pltpu.VMEM((taps + 1, SUBLANES, width), F32),
                        pltpu.VMEM((tm, width), F32)],
        compiler_params=_cparams(("parallel", "arbitrary")),
        name="conv_branch",
    )(proj, proj, proj, proj, dw_w, dw_b.reshape(1, width), ln_g.reshape(1, width),
      ln_b.reshape(1, width), w_out)


def _combine_kernel(*refs, parts):
    x_ref, g_ref = refs[0], refs[1]
    ya_ref, yb_ref, wo_ref, o_ref = refs[2 + 2 * parts], refs[3 + 2 * parts], refs[-2], refs[-1]
    gw = ya_ref.shape[1] // parts
    x = x_ref[...]
    h = _rms(x, g_ref[...]).astype(BF16)
    o_ref[...] = x
    for p in range(parts):
        cols = slice(p * gw, (p + 1) * gw)
        ga = _sigmoid(jnp.dot(h, refs[2 + p][...], preferred_element_type=F32))
        gb = _sigmoid(jnp.dot(h, refs[2 + parts + p][...], preferred_element_type=F32))
        mix = ga * ya_ref[:, cols] + gb * yb_ref[:, cols]
        o_ref[...] += jnp.dot(mix.astype(BF16), wo_ref[cols, :], preferred_element_type=F32)


def combine(x, g, w_in, y_a, y_b, w_out, *, gate_col0, tm):
    m, d = x.shape
    gw = math.gcd(gate_col0, d)
    parts = d // gw
    row = lambda i: (i, 0)
    once = pl.Buffered(1)
    gate_specs = [pl.BlockSpec((d, gw), functools.partial(lambda i, c: (0, c), c=(gate_col0 + s * d) // gw + p),
                               pipeline_mode=once) for s in range(2) for p in range(parts)]
    return pl.pallas_call(
        functools.partial(_combine_kernel, parts=parts),
        grid=(m // tm,),
        in_specs=[pl.BlockSpec((tm, d), row), pl.BlockSpec((1, d), lambda i: (0, 0))] + gate_specs +
                 [pl.BlockSpec((tm, d), row),
                  pl.BlockSpec((tm, d), row),
                  pl.BlockSpec((d, d), lambda i: (0, 0), pipeline_mode=once)],
        out_specs=pl.BlockSpec((tm, d), row),
        out_shape=jax.ShapeDtypeStruct((m, d), F32),
        compiler_params=_cparams(("parallel",)),
        name="gated_combine",
    )(x, g.reshape(1, d), *([w_in] * (2 * parts)), y_a, y_b, w_out)


def _xattn_kernel(x_ref, q_ref, kv_ref, wo_ref, o_ref, *, heads, scale):
    d = x_ref.shape[1]
    hd = d // heads
    acc = x_ref[...]
    for h in range(heads):
        q = q_ref[:, h * hd:(h + 1) * hd]
        k = kv_ref[:, h * hd:(h + 1) * hd]
        v = kv_ref[:, d + h * hd:d + (h + 1) * hd]
        s = lax.dot_general(q, k, (((1,), (1,)), ((), ())), preferred_element_type=F32) * scale
        s = s - jnp.max(s, axis=-1, keepdims=True)
        p = jnp.exp(s)
        p = p / jnp.sum(p, axis=-1, keepdims=True)
        o = jnp.dot(p.astype(BF16), v, preferred_element_type=F32)
        acc = acc + jnp.dot(o.astype(BF16), wo_ref[h * hd:(h + 1) * hd, :], preferred_element_type=F32)
    o_ref[...] = acc


def xattn(x, q, kv, wo, *, batch, seq, n_mem, heads, tm):
    d = x.shape[1]
    n_t = seq // tm
    scale = 1.0 / math.sqrt(d // heads)
    return pl.pallas_call(
        functools.partial(_xattn_kernel, heads=heads, scale=scale),
        grid=(batch, n_t),
        in_specs=[pl.BlockSpec((tm, d), lambda b, i: (b * n_t + i, 0)),
                  pl.BlockSpec((tm, d), lambda b, i: (b * n_t + i, 0)),
                  pl.BlockSpec((n_mem, 2 * d), lambda b, i: (b, 0)),
                  pl.BlockSpec((d, d), lambda b, i: (0, 0))],
        out_specs=pl.BlockSpec((tm, d), lambda b, i: (b * n_t + i, 0)),
        out_shape=jax.ShapeDtypeStruct((batch * seq, d), F32),
        compiler_params=_cparams(("parallel", "parallel")),
        name="xattn",
    )(x, q, kv, wo)


def _router_kernel(x_ref, g_ref, w_ref, b_ref, tri_ref, h_ref, info_ref, infot_ref, cnt_ref, run_ref, *,
                   n_groups, epg):
    i = pl.program_id(0)
    tm = x_ref.shape[0]

    @pl.when(i == 0)
    def _():
        run_ref[...] = jnp.zeros_like(run_ref)

    h = _rms(x_ref[...], g_ref[...])
    h_ref[...] = h
    logits = jnp.dot(h, w_ref[...], preferred_element_type=F32,
                     precision=lax.Precision.HIGHEST) + b_ref[...]
    lane = lax.broadcasted_iota(jnp.int32, (tm, LANES), 1).astype(F32)
    neg = jnp.float32(-jnp.inf)
    big = jnp.float32(LANES)
    is_grp = lane < n_groups
    cl = jnp.where(is_grp, logits, neg)
    cmax = jnp.max(cl, axis=-1, keepdims=True)
    g_idx = jnp.min(jnp.where(cl == cmax, lane, big), axis=-1, keepdims=True)
    p_sel = 1.0 / jnp.sum(jnp.where(is_grp, jnp.exp(cl - cmax), 0.0), axis=-1, keepdims=True)
    lo = n_groups + g_idx * epg
    in_grp = (lane >= lo) & (lane < lo + epg)
    fl = jnp.where(in_grp, logits, neg)
    v1 = jnp.max(fl, axis=-1, keepdims=True)
    i1 = jnp.min(jnp.where(fl == v1, lane, big), axis=-1, keepdims=True)
    fl2 = jnp.where(lane == i1, neg, fl)
    v2 = jnp.max(fl2, axis=-1, keepdims=True)
    i2 = jnp.min(jnp.where(fl2 == v2, lane, big), axis=-1, keepdims=True)
    e2x = jnp.exp(v2 - v1)
    w1 = p_sel / (1.0 + e2x)
    w2 = p_sel * e2x / (1.0 + e2x)
    e1 = i1 - n_groups
    e2 = i2 - n_groups
    oh1 = lane == e1
    oh2 = lane == e2
    both = jnp.where(oh1 | oh2, 1.0, 0.0)
    before = jnp.dot(tri_ref[...], both.astype(BF16), preferred_element_type=F32) + run_ref[...]
    r1 = jnp.sum(jnp.where(oh1, before, 0.0), axis=-1, keepdims=True)
    r2 = jnp.sum(jnp.where(oh2, before, 0.0), axis=-1, keepdims=True)
    run_ref[...] += jnp.sum(both, axis=0, keepdims=True)
    cnt_ref[...] = run_ref[...]
    info = jnp.where(lane == 0, e1, 0.0)
    info = jnp.where(lane == 1, e2, info)
    info = jnp.where(lane == 2, r1, info)
    info = jnp.where(lane == 3, r2, info)
    info = jnp.where(lane == 4, w1, info)
    info = jnp.where(lane == 5, w2, info)
    info_ref[...] = info
    infot_ref[...] = jnp.transpose(info)[0:SUBLANES, :]


def router(x, g, w_pad, b_pad, *, n_groups, epg, tm):
    m, d = x.shape
    tri = (jnp.arange(tm)[:, None] > jnp.arange(tm)[None, :]).astype(BF16)
    return pl.pallas_call(
        functools.partial(_router_kernel, n_groups=n_groups, epg=epg),
        grid=(m // tm,),
        in_specs=[pl.BlockSpec((tm, d), lambda i: (i, 0)),
                  pl.BlockSpec((1, d), lambda i: (0, 0)),
                  pl.BlockSpec((d, LANES), lambda i: (0, 0)),
                  pl.BlockSpec((1, LANES), lambda i: (0, 0)),
                  pl.BlockSpec((tm, tm), lambda i: (0, 0))],
        out_specs=[pl.BlockSpec((tm, d), lambda i: (i, 0)),
                   pl.BlockSpec((tm, LANES), lambda i: (i, 0)),
                   pl.BlockSpec((SUBLANES, tm), lambda i: (0, i)),
                   pl.BlockSpec((1, LANES), lambda i: (0, 0))],
        out_shape=(jax.ShapeDtypeStruct((m, d), F32),
                   jax.ShapeDtypeStruct((m, LANES), F32),
                   jax.ShapeDtypeStruct((SUBLANES, m), F32),
                   jax.ShapeDtypeStruct((1, LANES), F32)),
        scratch_shapes=[pltpu.VMEM((1, LANES), F32)],
        compiler_params=_cparams(("arbitrary",)),
        name="moe_router",
    )(x, g.reshape(1, d), w_pad, b_pad, tri)


def moe_dispatch_plan(infot, counts, n_tokens, n_experts):
    e = infot[0:2].astype(jnp.int32)
    rank = infot[2:4].astype(jnp.int32)
    counts = counts[0, :n_experts].astype(jnp.int32)
    pcounts = ((counts + MOE_BLOCK - 1) // MOE_BLOCK) * MOE_BLOCK
    pends = jnp.cumsum(pcounts)
    pstarts = pends - pcounts
    ids = jnp.arange(n_experts, dtype=jnp.int32)
    dest = rank + jnp.sum(jnp.where(e[:, :, None] == ids, pstarts, 0), axis=-1)
    n_rows = (-(-2 * n_tokens // MOE_BLOCK) + n_experts) * MOE_BLOCK
    n_used = (pends[-1] // MOE_BLOCK).reshape(1)
    return dest, n_rows, n_used, pcounts, pends


def _dispatch_kernel(pc_ref, pe_ref, dest_ref, h_ref, xs_hbm, zbuf, sem, tsem, *, n_experts):
    i = pl.program_id(0)
    tm = h_ref.shape[0]

    @pl.when(i == 0)
    def _():
        zbuf[...] = jnp.zeros_like(zbuf)
        for wait in (False, True):
            for e in range(n_experts):
                @pl.when(pc_ref[e] > 0)
                def _():
                    row0 = pl.multiple_of(pe_ref[e] - MOE_BLOCK, MOE_BLOCK)
                    fill = pltpu.make_async_copy(zbuf, xs_hbm.at[pl.ds(row0, MOE_BLOCK)], sem)
                    fill.wait() if wait else fill.start()

    n_blocks = xs_hbm.shape[0] // MOE_BLOCK
    n_used = pe_ref[n_experts - 1] // MOE_BLOCK

    def tail_fill(b, wait):
        row0 = pl.multiple_of(b * MOE_BLOCK, MOE_BLOCK)
        fill = pltpu.make_async_copy(zbuf, xs_hbm.at[pl.ds(row0, MOE_BLOCK)], tsem)
        fill.wait() if wait else fill.start()

    @pl.when(i == 0)
    def _():
        lax.fori_loop(n_used, n_blocks, lambda b, c: (tail_fill(b, False), c)[1], 0)

    for k in range(2):
        for r in range(tm):
            pltpu.make_async_copy(h_ref.at[pl.ds(r, 1)], xs_hbm.at[pl.ds(dest_ref[k, r], 1)], sem).start()
    for k in range(2):
        pltpu.make_async_copy(h_ref, xs_hbm.at[pl.ds(0, tm)], sem).wait()

    @pl.when(i == pl.num_programs(0) - 1)
    def _():
        lax.fori_loop(n_used, n_blocks, lambda b, c: (tail_fill(b, True), c)[1], 0)


def moe_dispatch(h, dest, pcounts, pends, *, n_rows, tm):
    t, d = h.shape
    grid_spec = pltpu.PrefetchScalarGridSpec(
        num_scalar_prefetch=2,
        grid=(t // tm,),
        in_specs=[pl.BlockSpec((2, tm), lambda i, pc, pe: (0, i), memory_space=pltpu.SMEM),
                  pl.BlockSpec((tm, d), lambda i, pc, pe: (i, 0))],
        out_specs=pl.BlockSpec(memory_space=pl.ANY),
        scratch_shapes=[pltpu.VMEM((MOE_BLOCK, d), F32), pltpu.SemaphoreType.DMA(()),
                        pltpu.SemaphoreType.DMA(())],
    )
    return pl.pallas_call(
        functools.partial(_dispatch_kernel, n_experts=pcounts.shape[0]),
        grid_spec=grid_spec,
        out_shape=jax.ShapeDtypeStruct((n_rows, d), F32),
        compiler_params=_cparams(("arbitrary",)),
        name="moe_dispatch",
    )(pcounts, pends, dest, h)


def _experts_kernel(first_ref, cnt_ref, nb_ref, xs_hbm, wg_ref, wu_ref, wd_ref, ys_hbm,
                    xbuf, ybuf, wgb, wub, wdb, xsem, ysem):
    e = pl.program_id(0)
    nb = nb_ref[0]
    first, cnt = first_ref[e], cnt_ref[e]
    n_blocks = xs_hbm.shape[0] // MOE_BLOCK

    def rows(g):
        return pl.ds(pl.multiple_of(g * MOE_BLOCK, MOE_BLOCK), MOE_BLOCK)

    def x_copy(g, s):
        return pltpu.make_async_copy(xs_hbm.at[rows(g)], xbuf.at[s], xsem.at[s])

    def y_copy(g, s):
        return pltpu.make_async_copy(ybuf.at[s], ys_hbm.at[rows(g)], ysem.at[s])

    @pl.when((e == 0) & (nb > 0))
    def _():
        x_copy(0, 0).start()

    @pl.when(cnt > 0)
    def _():
        wgb[...] = wg_ref[0].astype(BF16)
        wub[...] = wu_ref[0].astype(BF16)
        wdb[...] = wd_ref[0].astype(BF16)

    def block(b, carry):
        g = first + b
        s = lax.rem(g, 2)

        @pl.when(g + 1 < nb)
        def _():
            x_copy(g + 1, 1 - s).start()

        x_copy(g, s).wait()
        x = xbuf[s].astype(BF16)
        gte = jnp.dot(x, wgb[...], preferred_element_type=F32)
        up = jnp.dot(x, wub[...], preferred_element_type=F32)
        act = (gte * _sigmoid(gte) * up).astype(BF16)
        y = jnp.dot(act, wdb[...], preferred_element_type=F32)

        @pl.when(g >= 2)
        def _():
            y_copy(g - 2, s).wait()

        ybuf[s] = y
        y_copy(g, s).start()
        return carry

    lax.fori_loop(0, cnt, block, 0)

    @pl.when(e == pl.num_programs(0) - 1)
    def _():
        for back in (1, 2):
            @pl.when(nb >= back)
            def _():
                y_copy(nb - back, lax.rem(nb - back, 2)).wait()

        ybuf[0] = jnp.zeros(ybuf.shape[1:], F32)
        lax.fori_loop(nb, n_blocks, lambda g, c: (y_copy(g, 0).start(), c)[1], 0)
        lax.fori_loop(nb, n_blocks, lambda g, c: (y_copy(g, 0).wait(), c)[1], 0)


def experts(xs, w_gate, w_up, w_down, first_blk, n_blk, n_used):
    n_rows, d = xs.shape
    n_experts, _, de = w_gate.shape
    grid_spec = pltpu.PrefetchScalarGridSpec(
        num_scalar_prefetch=3,
        grid=(n_experts,),
        in_specs=[pl.BlockSpec(memory_space=pl.ANY),
                  pl.BlockSpec((1, d, de), lambda e, *_: (e, 0, 0)),
                  pl.BlockSpec((1, d, de), lambda e, *_: (e, 0, 0)),
                  pl.BlockSpec((1, de, d), lambda e, *_: (e, 0, 0))],
        out_specs=pl.BlockSpec(memory_space=pl.ANY),
        scratch_shapes=[pltpu.VMEM((2, MOE_BLOCK, d), F32), pltpu.VMEM((2, MOE_BLOCK, d), F32),
                        pltpu.VMEM((d, de), BF16), pltpu.VMEM((d, de), BF16), pltpu.VMEM((de, d), BF16),
                        pltpu.SemaphoreType.DMA((2,)), pltpu.SemaphoreType.DMA((2,))],
    )
    return pl.pallas_call(
        _experts_kernel,
        grid_spec=grid_spec,
        out_shape=jax.ShapeDtypeStruct((n_rows, d), F32),
        compiler_params=_cparams(("arbitrary",)),
        name="moe_experts",
    )(first_blk, n_blk, n_used, xs, w_gate, w_up, w_down)


def _moe_combine_kernel(dest_ref, destn_ref, x_ref, info_ref, g_ref, ys_hbm, o_ref, ybuf, sem, *, norm):
    i = pl.program_id(0)
    n = pl.num_programs(0)
    tm = x_ref.shape[0]
    slot = lax.rem(i, 2)

    def gather_start(dref, s):
        for k in range(2):
            for r in range(tm):
                pltpu.make_async_copy(ys_hbm.at[pl.ds(dref[k, r], 1)], ybuf.at[s, k, pl.ds(r, 1)],
                                      sem.at[s]).start()

    @pl.when(i == 0)
    def _():
        gather_start(dest_ref, 0)

    @pl.when(i + 1 < n)
    def _():
        gather_start(destn_ref, 1 - slot)

    for k in range(2):
        pltpu.make_async_copy(ys_hbm.at[pl.ds(0, tm)], ybuf.at[slot, k], sem.at[slot]).wait()
    w1 = info_ref[:, 4:5]
    w2 = info_ref[:, 5:6]
    s = x_ref[...] + (w1 * ybuf[slot, 0] + w2 * ybuf[slot, 1])
    o_ref[...] = _rms(s, g_ref[...]) if norm else s


def moe_combine(x, ys, dest, info, g, *, norm, tm):
    t, d = x.shape
    n = t // tm
    return pl.pallas_call(
        functools.partial(_moe_combine_kernel, norm=norm),
        grid=(n,),
        in_specs=[pl.BlockSpec((2, tm), lambda i: (0, i), memory_space=pltpu.SMEM),
                  pl.BlockSpec((2, tm), lambda i: (0, jnp.minimum(i + 1, n - 1)), memory_space=pltpu.SMEM),
                  pl.BlockSpec((tm, d), lambda i: (i, 0)),
                  pl.BlockSpec((tm, LANES), lambda i: (i, 0)),
                  pl.BlockSpec((1, d), lambda i: (0, 0)),
                  pl.BlockSpec(memory_space=pl.ANY)],
        out_specs=pl.BlockSpec((tm, d), lambda i: (i, 0)),
        out_shape=jax.ShapeDtypeStruct((t, d), F32),
        scratch_shapes=[pltpu.VMEM((2, 2, tm, d), F32), pltpu.SemaphoreType.DMA((2,))],
        compiler_params=_cparams(("arbitrary",)),
        name="moe_combine",
    )(dest, dest, x, info, g.reshape(1, d), ys)


def forward(x, mem, norm_mix_g, w_in, s5_lambda_re, s5_lambda_im, s5_log_dt, s5_b_re, s5_b_im, s5_c_re,
            s5_c_im, s5_d, s5_w_glu, conv_dw_w, conv_dw_b, conv_ln_g, conv_ln_b, conv_w_out, w_out,
            norm_xattn_g, norm_mem_g, xattn_wq, xattn_wk, xattn_wv, xattn_wo, norm_moe_g, router_w_group,
            router_b_group, router_w_expert, router_b_expert, exp_w_gate, exp_w_up, exp_w_down, norm_final_g,
            *, tiles):
    batch, seq, d = x.shape
    depth = w_in.shape[0]
    n_mem = mem.shape[1]
    s5_width = s5_d.shape[1]
    conv_width = conv_dw_b.shape[1]
    n_groups, epg = router_w_expert.shape[1], router_w_expert.shape[3]
    t = batch * seq
    xf = x.reshape(t, d)
    memf = mem.reshape(batch * n_mem, d)
    perm = s5_row_perm(tiles["s5_m"])
    gate_col0 = s5_width + 2 * conv_width
    for l in range(depth):
        w_in_b = w_in[l].astype(BF16)
        proj = norm_mm_rows(xf, norm_mix_g[l], w_in_b[:, :gate_col0], tm=tiles["proj_tm"], tn=tiles["proj_tn"],
                            out_dtype=F32)
        lbr, lbi, bbr, bbi = s5_discretise(s5_lambda_re[l], s5_lambda_im[l], s5_log_dt[l],
                                           s5_b_re[l], s5_b_im[l])
        bre, bim, cre, cim = s5_block_diag(bbr, bbi, s5_c_re[l], s5_c_im[l])
        z = s5_scan(proj, perm, bre, bim, cre, cim, lbr.reshape(1, -1), lbi.reshape(1, -1), s5_d[l],
                    batch=batch, seq=seq, width=s5_width, m=tiles["s5_m"], cw=tiles["s5_cw"])
        y_a = mm_glu(z, s5_w_glu[l].astype(BF16), tm=tiles["glu_tm"], tn=tiles["glu_tn"])
        y_b = conv_branch(proj, conv_dw_w[l], conv_dw_b[l], conv_ln_g[l], conv_ln_b[l],
                          conv_w_out[l].astype(BF16), batch=batch, seq=seq, width=conv_width,
                          col0=s5_width, tm=tiles["conv_tm"])
        xf = combine(xf, norm_mix_g[l], w_in_b, y_a, y_b, w_out[l].astype(BF16), gate_col0=gate_col0,
                     tm=tiles["comb_tm"])

        q = norm_mm_rows(xf, norm_xattn_g[l], xattn_wq[l].astype(BF16), tm=tiles["q_tm"], tn=tiles["q_tn"],
                         out_dtype=BF16)
        wkv = jnp.concatenate([xattn_wk[l], xattn_wv[l]], axis=1).astype(BF16)
        kv = norm_mm(memf, norm_mem_g[l], wkv, tm=batch * n_mem, tn=tiles["kv_tn"], out_dtype=BF16)
        xf = xattn(xf, q, kv, xattn_wo[l].astype(BF16), batch=batch, seq=seq, n_mem=n_mem,
                   heads=XATTN_HEADS, tm=tiles["xattn_tm"])

        w_r = jnp.concatenate([router_w_group[l],
                               jnp.transpose(router_w_expert[l], (1, 0, 2)).reshape(d, n_groups * epg)], axis=1)
        b_r = jnp.concatenate([router_b_group[l], router_b_expert[l].reshape(-1)])
        n_r = w_r.shape[1]
        w_pad = jnp.pad(w_r, ((0, 0), (0, LANES - n_r)))
        b_pad = jnp.pad(b_r, (0, LANES - n_r)).reshape(1, LANES)
        h, info, infot, counts = router(xf, norm_moe_g[l], w_pad, b_pad, n_groups=n_groups, epg=epg,
                                        tm=tiles["router_tm"])
        dest, n_rows, n_used, pcounts, pends = moe_dispatch_plan(infot, counts, t, n_groups * epg)
        xs = moe_dispatch(h, dest, pcounts, pends, n_rows=n_rows, tm=tiles["moe_tm"])
        ys = experts(xs, exp_w_gate[l], exp_w_up[l], exp_w_down[l], (pends - pcounts) // MOE_BLOCK,
                     pcounts // MOE_BLOCK, n_used)
        last = l + 1 == depth
        xf = moe_combine(xf, ys, dest, info, norm_final_g if last else jnp.ones((d,), F32), norm=last,
                         tm=tiles["moe_tm"])
    return xf.reshape(batch, seq, d)


def kernel(x, mem, norm_mix_g, w_in, s5_lambda_re, s5_lambda_im, s5_log_dt, s5_b_re, s5_b_im, s5_c_re, s5_c_im, s5_d, s5_w_glu, conv_dw_w, conv_dw_b, conv_ln_g, conv_ln_b, conv_w_out, w_out, norm_xattn_g, norm_mem_g, xattn_wq, xattn_wk, xattn_wv, xattn_wo, norm_moe_g, router_w_group, router_b_group, router_w_expert, router_b_expert, exp_w_gate, exp_w_up, exp_w_down, norm_final_g):
    return forward(x, mem, norm_mix_g, w_in, s5_lambda_re, s5_lambda_im, s5_log_dt, s5_b_re, s5_b_im,
                   s5_c_re, s5_c_im, s5_d, s5_w_glu, conv_dw_w, conv_dw_b, conv_ln_g, conv_ln_b, conv_w_out,
                   w_out, norm_xattn_g, norm_mem_g, xattn_wq, xattn_wk, xattn_wv, xattn_wo, norm_moe_g,
                   router_w_group, router_b_group, router_w_expert, router_b_expert, exp_w_gate, exp_w_up,
                   exp_w_down, norm_final_g, tiles=TILES)
```

```python
import functools
import math

import jax
import jax.numpy as jnp
from jax import lax
from jax.experimental import pallas as pl
from jax.experimental.pallas import tpu as pltpu

F32 = jnp.float32
BF16 = jnp.bfloat16

RMS_EPS = 1e-6
LN_EPS = 1e-5

S5_SLAB_CH = 256
CONV_HALO = 32
CONV_ROW_CHUNK = 32
CONV_LANE_BLOCK = 256
XATTN_HEADS = 4
MOE_BLOCK = 128
BLOCK_DMA_QUEUE = 1
SUBLANES = 8
LANES = 128
VMEM_LIMIT = 56 * 1024 * 1024

TILES = dict(proj_tm=512, proj_tn=512, s5_m=64, s5_cw=512, glu_tn=512, conv_tm=256,
             comb_tm=256, q_tm=512, q_tn=512, kv_tn=512, xattn_tm=512, router_tm=512, moe_tm=256)


def _cparams(sem):
    return pltpu.CompilerParams(dimension_semantics=sem, vmem_limit_bytes=VMEM_LIMIT)


def _rms(x, g):
    return x * lax.rsqrt(jnp.mean(x * x, axis=-1, keepdims=True) + RMS_EPS) * g


def _sigmoid(x):
    return 1.0 / (1.0 + jnp.exp(-x))


def _gelu_tanh(x):
    c = math.sqrt(2.0 / math.pi)
    return 0.5 * x * (1.0 + jnp.tanh(c * (x + 0.044715 * (x * x * x))))


def _norm_mm_kernel(x_ref, g_ref, w_ref, o_ref, h_ref, *, act_from):
    j = pl.program_id(1)

    @pl.when(j == 0)
    def _():
        h_ref[...] = _rms(x_ref[...], g_ref[...]).astype(BF16)

    y = jnp.dot(h_ref[...], w_ref[...], preferred_element_type=F32)
    if act_from is None:
        o_ref[...] = y.astype(o_ref.dtype)
    else:
        @pl.when(j < act_from)
        def _():
            o_ref[...] = y.astype(o_ref.dtype)

        @pl.when(j >= act_from)
        def _():
            o_ref[...] = _sigmoid(y).astype(o_ref.dtype)


def norm_mm(x, g, w, *, tm, tn, out_dtype=F32, act_from=None):
    m, k = x.shape
    n = w.shape[1]
    return pl.pallas_call(
        functools.partial(_norm_mm_kernel, act_from=act_from),
        grid=(m // tm, n // tn),
        in_specs=[pl.BlockSpec((tm, k), lambda i, j: (i, 0)),
                  pl.BlockSpec((1, k), lambda i, j: (0, 0)),
                  pl.BlockSpec((k, tn), lambda i, j: (0, j))],
        out_specs=pl.BlockSpec((tm, tn), lambda i, j: (i, j)),
        out_shape=jax.ShapeDtypeStruct((m, n), out_dtype),
        scratch_shapes=[pltpu.VMEM((tm, k), BF16)],
        compiler_params=_cparams(("parallel", "arbitrary")),
        name="norm_mm",
    )(x, g.reshape(1, k), w)


def _norm_mm_rows_kernel(x_ref, g_ref, w_ref, o_ref, *, tn):
    h = _rms(x_ref[...], g_ref[...]).astype(BF16)
    for j in range(o_ref.shape[1] // tn):
        cols = slice(j * tn, (j + 1) * tn)
        o_ref[:, cols] = jnp.dot(h, w_ref[:, cols], preferred_element_type=F32).astype(o_ref.dtype)


def norm_mm_rows(x, g, w, *, tm, tn, out_dtype):
    m, k = x.shape
    n = w.shape[1]
    return pl.pallas_call(
        functools.partial(_norm_mm_rows_kernel, tn=tn),
        grid=(m // tm,),
        in_specs=[pl.BlockSpec((tm, k), lambda i: (i, 0)),
                  pl.BlockSpec((1, k), lambda i: (0, 0)),
                  pl.BlockSpec((k, n), lambda i: (0, 0), pipeline_mode=pl.Buffered(1))],
        out_specs=pl.BlockSpec((tm, n), lambda i: (i, 0)),
        out_shape=jax.ShapeDtypeStruct((m, n), out_dtype),
        compiler_params=_cparams(("parallel",)),
        name="norm_mm_rows",
    )(x, g.reshape(1, k), w)


def _s5_disc_kernel(lr_ref, li_ref, ldt_ref, br_ref, bi_ref, lbr_ref, lbi_ref, bbr_ref, bbi_ref):
    lr, li = lr_ref[...], li_ref[...]
    dt = jnp.exp(ldt_ref[...])
    mag = jnp.exp(lr * dt)
    ang = li * dt
    lb_re, lb_im = mag * jnp.cos(ang), mag * jnp.sin(ang)
    den = lr * lr + li * li
    nr, ni = lb_re - 1.0, lb_im
    coef_re = (nr * lr + ni * li) / den
    coef_im = (ni * lr - nr * li) / den
    lbr_ref[...] = lb_re
    lbi_ref[...] = lb_im
    br, bi = br_ref[...], bi_ref[...]
    bbr_ref[...] = coef_re[None] * br - coef_im[None] * bi
    bbi_ref[...] = coef_re[None] * bi + coef_im[None] * br


def s5_discretise(lam_re, lam_im, log_dt, b_re, b_im):
    g, n = lam_re.shape
    c = b_re.shape[-1]
    b_re_t = jnp.transpose(b_re, (2, 0, 1))
    b_im_t = jnp.transpose(b_im, (2, 0, 1))
    gn = jax.ShapeDtypeStruct((g, n), F32)
    cgn = jax.ShapeDtypeStruct((c, g, n), F32)
    return pl.pallas_call(
        _s5_disc_kernel,
        out_shape=(gn, gn, cgn, cgn),
        name="s5_discretise",
    )(lam_re, lam_im, log_dt.reshape(g, 1), b_re_t, b_im_t)


def s5_block_diag(bb_re, bb_im, c_re, c_im):
    c, g, n = bb_re.shape
    gs = S5_SLAB_CH // c
    n_slab = g // gs
    eye = jnp.eye(gs, dtype=F32)

    def bmat(bb):
        b = jnp.transpose(bb, (1, 0, 2)).reshape(n_slab, gs, c, n)
        full = b[:, :, :, None, :] * eye[None, :, None, :, None]
        return full.reshape(n_slab, gs * c, gs * n).astype(BF16)

    def cmat(cc):
        cm = jnp.transpose(cc.reshape(n_slab, gs, c, n), (0, 1, 3, 2))
        full = cm[:, :, :, None, :] * eye[None, :, None, :, None]
        return full.reshape(n_slab, gs * n, gs * c).astype(BF16)

    return bmat(bb_re), bmat(bb_im), cmat(c_re), cmat(c_im)


def _s5_kernel(proj_hbm, bre_ref, bim_ref, cre_ref, cim_ref, lre_ref, lim_ref, d_ref, wglu_ref, ya_hbm,
               ubuf, yabuf, xre_ref, xim_ref, pre_ref, pim_ref, car_re_ref, car_im_ref, isem, osem, *, m, cw, tn):
    t = pl.program_id(1)
    n_t = pl.num_programs(1)
    step_id = pl.program_id(0) * n_t + t
    n_steps = pl.num_programs(0) * n_t
    slot = lax.rem(step_id, 2)
    rows = SUBLANES * m
    width = d_ref.shape[1]
    n_state = lre_ref.shape[1]
    n_slab = bre_ref.shape[0]
    slab_states = n_state // n_slab

    def in_copy(step, s, k):
        r0 = pl.multiple_of(step * rows + k * m, SUBLANES)
        return pltpu.make_async_copy(proj_hbm.at[pl.ds(r0, m), pl.ds(0, width)], ubuf.at[s, :, k, :], isem.at[s])

    def out_copy(step, s, k):
        r0 = pl.multiple_of(step * rows + k * m, SUBLANES)
        return pltpu.make_async_copy(yabuf.at[s, :, k, :], ya_hbm.at[pl.ds(r0, m)], osem.at[s])

    @pl.when(step_id == 0)
    def _():
        for k in range(SUBLANES):
            in_copy(0, 0, k).start()

    @pl.when(step_id + 1 < n_steps)
    def _():
        for k in range(SUBLANES):
            in_copy(step_id + 1, 1 - slot, k).start()

    @pl.when(t == 0)
    def _():
        car_re_ref[...] = jnp.zeros_like(car_re_ref)
        car_im_ref[...] = jnp.zeros_like(car_im_ref)
        lr, li = lre_ref[...], lim_ref[...]

        def pw(_, carry):
            pr, pi = carry
            return pr * lr - pi * li, pr * li + pi * lr

        pr, pi = lax.fori_loop(1, m, pw, (lr, li))
        pre_ref[...] = pr
        pim_ref[...] = pi

    for k in range(SUBLANES):
        in_copy(step_id, slot, k).wait()
    u = ubuf[slot].reshape(rows, width)
    up = u.astype(BF16)
    for s in range(n_slab):
        us = up[:, s * S5_SLAB_CH:(s + 1) * S5_SLAB_CH]
        cols = slice(s * slab_states, (s + 1) * slab_states)
        xre_ref[:, cols] = jnp.dot(us, bre_ref[s], preferred_element_type=F32)
        xim_ref[:, cols] = jnp.dot(us, bim_ref[s], preferred_element_type=F32)

    row_id = lax.broadcasted_iota(jnp.int32, (SUBLANES, cw), 0)
    for cb in range(n_state // cw):
        cols = slice(cb * cw, (cb + 1) * cw)
        lr = jnp.broadcast_to(lre_ref[:, cols], (SUBLANES, cw))
        li = jnp.broadcast_to(lim_ref[:, cols], (SUBLANES, cw))

        def step(tau, carry, store, cols=cols, lr=lr, li=li):
            sr, si = carry
            r0 = pl.multiple_of(tau * SUBLANES, SUBLANES)
            nr = sr * lr - si * li + xre_ref[pl.ds(r0, SUBLANES), cols]
            ni = sr * li + si * lr + xim_ref[pl.ds(r0, SUBLANES), cols]
            if store:
                xre_ref[pl.ds(r0, SUBLANES), cols] = nr
                xim_ref[pl.ds(r0, SUBLANES), cols] = ni
            return nr, ni

        zero = jnp.zeros((SUBLANES, cw), F32)
        er, ei = lax.fori_loop(0, m, functools.partial(step, store=False), (zero, zero), unroll=8)

        lmr = jnp.broadcast_to(pre_ref[:, cols], (SUBLANES, cw))
        lmi = jnp.broadcast_to(pim_ref[:, cols], (SUBLANES, cw))
        ir = jnp.where(row_id == 0, jnp.broadcast_to(car_re_ref[:, cols], (SUBLANES, cw)), 0.0)
        ii = jnp.where(row_id == 0, jnp.broadcast_to(car_im_ref[:, cols], (SUBLANES, cw)), 0.0)
        for k in range(1, SUBLANES):
            nr = ir * lmr - ii * lmi + er
            ni = ir * lmi + ii * lmr + ei
            ir = jnp.where(row_id == k, pltpu.roll(nr, 1, axis=0), ir)
            ii = jnp.where(row_id == k, pltpu.roll(ni, 1, axis=0), ii)

        fr, fi = lax.fori_loop(0, m, functools.partial(step, store=True), (ir, ii), unroll=8)
        car_re_ref[:, cols] = fr[SUBLANES - 1:SUBLANES, :]
        car_im_ref[:, cols] = fi[SUBLANES - 1:SUBLANES, :]

    ys = []
    for s in range(n_slab):
        cols = slice(s * slab_states, (s + 1) * slab_states)
        ys.append(jnp.dot(xre_ref[:, cols].astype(BF16), cre_ref[s], preferred_element_type=F32)
                  - jnp.dot(xim_ref[:, cols].astype(BF16), cim_ref[s], preferred_element_type=F32))
    y = jnp.concatenate(ys, axis=1) + d_ref[...] * u
    z = _gelu_tanh(y).astype(BF16)

    @pl.when(step_id >= 2)
    def _():
        for k in range(SUBLANES):
            out_copy(step_id - 2, slot, k).wait()

    nh = wglu_ref.shape[1] // 2
    for j in range(nh // tn):
        val = jnp.dot(z, wglu_ref[:, j * tn:(j + 1) * tn], preferred_element_type=F32)
        gt = jnp.dot(z, wglu_ref[:, nh + j * tn:nh + (j + 1) * tn], preferred_element_type=F32)
        yabuf[slot, :, :, j * tn:(j + 1) * tn] = (val * _sigmoid(gt)).reshape(m, SUBLANES, tn)
    for k in range(SUBLANES):
        out_copy(step_id, slot, k).start()

    @pl.when(step_id == n_steps - 1)
    def _():
        for k in range(SUBLANES):
            out_copy(step_id, slot, k).wait()
        if ya_hbm.shape[0] // rows >= 2:
            for k in range(SUBLANES):
                out_copy(step_id - 1, 1 - slot, k).wait()


def s5_branch(proj, bre, bim, cre, cim, lre, lim, d_skip, w_glu, *, batch, seq, width, m, cw, tn):
    rows = SUBLANES * m
    n_t = seq // rows
    n_state = lre.shape[1]
    nh = w_glu.shape[1] // 2
    once = pl.Buffered(1)
    const3 = lambda b, t: (0, 0, 0)
    const2 = lambda b, t: (0, 0)
    return pl.pallas_call(
        functools.partial(_s5_kernel, m=m, cw=cw, tn=tn),
        grid=(batch, n_t),
        in_specs=[pl.BlockSpec(memory_space=pl.ANY),
                  pl.BlockSpec(bre.shape, const3, pipeline_mode=once),
                  pl.BlockSpec(bim.shape, const3, pipeline_mode=once),
                  pl.BlockSpec(cre.shape, const3, pipeline_mode=once),
                  pl.BlockSpec(cim.shape, const3, pipeline_mode=once),
                  pl.BlockSpec((1, n_state), const2),
                  pl.BlockSpec((1, n_state), const2),
                  pl.BlockSpec((1, width), const2),
                  pl.BlockSpec(w_glu.shape, const2, pipeline_mode=once)],
        out_specs=pl.BlockSpec(memory_space=pl.ANY),
        out_shape=jax.ShapeDtypeStruct((batch * seq, nh), F32),
        scratch_shapes=[pltpu.VMEM((2, m, SUBLANES, width), F32), pltpu.VMEM((2, m, SUBLANES, nh), F32),
                        pltpu.VMEM((rows, n_state), F32), pltpu.VMEM((rows, n_state), F32),
                        pltpu.VMEM((1, n_state), F32), pltpu.VMEM((1, n_state), F32),
                        pltpu.VMEM((1, n_state), F32), pltpu.VMEM((1, n_state), F32),
                        pltpu.SemaphoreType.DMA((2,)), pltpu.SemaphoreType.DMA((2,))],
        compiler_params=_cparams(("arbitrary", "arbitrary")),
        name="s5_branch",
    )(proj, bre, bim, cre, cim, lre, lim, d_skip.reshape(1, width), w_glu)


def _conv_kernel(a_ref, b_ref, ah_ref, bh_ref, dww_ref, dwb_ref, lng_ref, lnb_ref, w_ref, o_ref,
                 zs_ref, zsh_ref, wb_ref, yc_ref, *, tm, taps):
    i = pl.program_id(1)
    zh = ah_ref[...] * _sigmoid(bh_ref[...])
    zs_ref[0:CONV_HALO, :] = jnp.where(i == 0, 0.0, zh)
    zs_ref[CONV_HALO:, :] = a_ref[...] * _sigmoid(b_ref[...])
    off0 = CONV_HALO - (taps - 1)
    width = a_ref.shape[1]
    n_sh = zsh_ref.shape[1]
    for s in range(1, SUBLANES):
        zsh_ref[s - 1] = zs_ref[s:s + n_sh, :]

    @pl.when(i == 0)
    def _():
        for j in range(taps):
            wb_ref[j] = jnp.broadcast_to(dww_ref[j:j + 1, :], (SUBLANES, width))
        wb_ref[taps] = jnp.broadcast_to(dwb_ref[...], (SUBLANES, width))

    rg = CONV_ROW_CHUNK // SUBLANES

    def chunk(c, carry):
        r0 = pl.multiple_of(c * CONV_ROW_CHUNK, CONV_ROW_CHUNK)
        for lb in range(width // CONV_LANE_BLOCK):
            lanes = slice(lb * CONV_LANE_BLOCK, (lb + 1) * CONV_LANE_BLOCK)
            bias = wb_ref[taps, :, lanes]
            acc = [bias] * rg
            for j in range(taps):
                q, s = divmod(off0 + j, SUBLANES)
                w = wb_ref[j, :, lanes]
                for g in range(rg):
                    rq = pl.multiple_of(r0 + (q + g) * SUBLANES, SUBLANES)
                    zrow = (zs_ref[pl.ds(rq, SUBLANES), lanes] if s == 0
                            else zsh_ref[s - 1, pl.ds(rq, SUBLANES), lanes])
                    acc[g] = acc[g] + w * zrow
            for g in range(rg):
                yc_ref[pl.ds(pl.multiple_of(r0 + g * SUBLANES, SUBLANES), SUBLANES), lanes] = acc[g]
        return carry

    lax.fori_loop(0, tm // CONV_ROW_CHUNK, chunk, 0)
    y = yc_ref[...]
    mu = jnp.mean(y, axis=-1, keepdims=True)
    var = jnp.mean(jnp.square(y - mu), axis=-1, keepdims=True)
    y = (y - mu) * lax.rsqrt(var + LN_EPS) * lng_ref[...] + lnb_ref[...]
    y = y * _sigmoid(y)
    o_ref[...] = jnp.dot(y.astype(BF16), w_ref[...], preferred_element_type=F32)


def conv_branch(proj, dw_w, dw_b, ln_g, ln_b, w_out, *, batch, seq, width, col0, tm):
    n_t = seq // tm
    ca, cb = col0 // width, col0 // width + 1
    hb = tm // CONV_HALO
    taps = dw_w.shape[0]
    d_out = w_out.shape[1]
    const2 = lambda b, i: (0, 0)
    halo = lambda b, i: jnp.maximum((b * n_t + i) * hb - 1, 0)
    return pl.pallas_call(
        functools.partial(_conv_kernel, tm=tm, taps=taps),
        grid=(batch, n_t),
        in_specs=[pl.BlockSpec((tm, width), lambda b, i: (b * n_t + i, ca)),
                  pl.BlockSpec((tm, width), lambda b, i: (b * n_t + i, cb)),
                  pl.BlockSpec((CONV_HALO, width), lambda b, i: (halo(b, i), ca)),
                  pl.BlockSpec((CONV_HALO, width), lambda b, i: (halo(b, i), cb)),
                  pl.BlockSpec((taps, width), const2),
                  pl.BlockSpec((1, width), const2),
                  pl.BlockSpec((1, width), const2),
                  pl.BlockSpec((1, width), const2),
                  pl.BlockSpec((width, d_out), const2)],
        out_specs=pl.BlockSpec((tm, d_out), lambda b, i: (b * n_t + i, 0)),
        out_shape=jax.ShapeDtypeStruct((batch * seq, d_out), F32),
        scratch_shapes=[pltpu.VMEM((tm + CONV_HALO, width), F32),
                        pltpu.VMEM((SUBLANES - 1, tm + CONV_HALO - SUBLANES, width), F32),
                        pltpu.VMEM((taps + 1, SUBLANES, width), F32),
                        pltpu.VMEM((tm, width), F32)],
        compiler_params=_cparams(("parallel", "arbitrary")),
        name="conv_branch",
    )(proj, proj, proj, proj, dw_w, dw_b.reshape(1, width), ln_g.reshape(1, width),
      ln_b.reshape(1, width), w_out)


def _combine_kernel(*refs, parts):
    x_ref, g_ref = refs[0], refs[1]
    ya_ref, yb_ref, wo_ref, o_ref = refs[2 + 2 * parts], refs[3 + 2 * parts], refs[-2], refs[-1]
    gw = ya_ref.shape[1] // parts
    x = x_ref[...]
    h = _rms(x, g_ref[...]).astype(BF16)
    o_ref[...] = x
    for p in range(parts):
        cols = slice(p * gw, (p + 1) * gw)
        ga = _sigmoid(jnp.dot(h, refs[2 + p][...], preferred_element_type=F32))
        gb = _sigmoid(jnp.dot(h, refs[2 + parts + p][...], preferred_element_type=F32))
        mix = ga * ya_ref[:, cols] + gb * yb_ref[:, cols]
        o_ref[...] += jnp.dot(mix.astype(BF16), wo_ref[cols, :], preferred_element_type=F32)


def combine(x, g, w_in, y_a, y_b, w_out, *, gate_col0, tm):
    m, d = x.shape
    gw = math.gcd(gate_col0, d)
    parts = d // gw
    row = lambda i: (i, 0)
    once = pl.Buffered(1)
    gate_specs = [pl.BlockSpec((d, gw), functools.partial(lambda i, c: (0, c), c=(gate_col0 + s * d) // gw + p),
                               pipeline_mode=once) for s in range(2) for p in range(parts)]
    return pl.pallas_call(
        functools.partial(_combine_kernel, parts=parts),
        grid=(m // tm,),
        in_specs=[pl.BlockSpec((tm, d), row), pl.BlockSpec((1, d), lambda i: (0, 0))] + gate_specs +
                 [pl.BlockSpec((tm, d), row),
                  pl.BlockSpec((tm, d), row),
                  pl.BlockSpec((d, d), lambda i: (0, 0), pipeline_mode=once)],
        out_specs=pl.BlockSpec((tm, d), row),
        out_shape=jax.ShapeDtypeStruct((m, d), F32),
        compiler_params=_cparams(("parallel",)),
        name="gated_combine",
    )(x, g.reshape(1, d), *([w_in] * (2 * parts)), y_a, y_b, w_out)


def _xattn_kernel(x_ref, q_ref, kv_ref, wo_ref, o_ref, *, heads, scale):
    d = x_ref.shape[1]
    hd = d // heads
    acc = x_ref[...]
    for h in range(heads):
        q = q_ref[:, h * hd:(h + 1) * hd]
        k = kv_ref[:, h * hd:(h + 1) * hd]
        v = kv_ref[:, d + h * hd:d + (h + 1) * hd]
        s = lax.dot_general(q, k, (((1,), (1,)), ((), ())), preferred_element_type=F32) * scale
        s = s - jnp.max(s, axis=-1, keepdims=True)
        p = jnp.exp(s)
        p = p / jnp.sum(p, axis=-1, keepdims=True)
        o = jnp.dot(p.astype(BF16), v, preferred_element_type=F32)
        acc = acc + jnp.dot(o.astype(BF16), wo_ref[h * hd:(h + 1) * hd, :], preferred_element_type=F32)
    o_ref[...] = acc


def xattn(x, q, kv, wo, *, batch, seq, n_mem, heads, tm):
    d = x.shape[1]
    n_t = seq // tm
    scale = 1.0 / math.sqrt(d // heads)
    return pl.pallas_call(
        functools.partial(_xattn_kernel, heads=heads, scale=scale),
        grid=(batch, n_t),
        in_specs=[pl.BlockSpec((tm, d), lambda b, i: (b * n_t + i, 0)),
                  pl.BlockSpec((tm, d), lambda b, i: (b * n_t + i, 0)),
                  pl.BlockSpec((n_mem, 2 * d), lambda b, i: (b, 0)),
                  pl.BlockSpec((d, d), lambda b, i: (0, 0))],
        out_specs=pl.BlockSpec((tm, d), lambda b, i: (b * n_t + i, 0)),
        out_shape=jax.ShapeDtypeStruct((batch * seq, d), F32),
        compiler_params=_cparams(("parallel", "parallel")),
        name="xattn",
    )(x, q, kv, wo)


def _router_kernel(x_ref, g_ref, w_ref, b_ref, tri_ref, h_ref, info_ref, infot_ref, cnt_ref, run_ref, *,
                   n_groups, epg):
    i = pl.program_id(0)
    tm = x_ref.shape[0]

    @pl.when(i == 0)
    def _():
        run_ref[...] = jnp.zeros_like(run_ref)

    h = _rms(x_ref[...], g_ref[...])
    h_ref[...] = h
    h_hi = h.astype(BF16)
    h_lo = (h - h_hi.astype(F32)).astype(BF16)
    p = jnp.dot(h_hi, w_ref[...], preferred_element_type=F32)
    logits = (p[:, :LANES] + p[:, LANES:] + jnp.dot(h_lo, w_ref[:, :LANES], preferred_element_type=F32)
              + b_ref[...])
    lane = lax.broadcasted_iota(jnp.int32, (tm, LANES), 1).astype(F32)
    neg = jnp.float32(-jnp.inf)
    big = jnp.float32(LANES)
    is_grp = lane < n_groups
    cl = jnp.where(is_grp, logits, neg)
    cmax = jnp.max(cl, axis=-1, keepdims=True)
    g_idx = jnp.min(jnp.where(cl == cmax, lane, big), axis=-1, keepdims=True)
    p_sel = 1.0 / jnp.sum(jnp.where(is_grp, jnp.exp(cl - cmax), 0.0), axis=-1, keepdims=True)
    lo = n_groups + g_idx * epg
    in_grp = (lane >= lo) & (lane < lo + epg)
    fl = jnp.where(in_grp, logits, neg)
    v1 = jnp.max(fl, axis=-1, keepdims=True)
    i1 = jnp.min(jnp.where(fl == v1, lane, big), axis=-1, keepdims=True)
    fl2 = jnp.where(lane == i1, neg, fl)
    v2 = jnp.max(fl2, axis=-1, keepdims=True)
    i2 = jnp.min(jnp.where(fl2 == v2, lane, big), axis=-1, keepdims=True)
    e2x = jnp.exp(v2 - v1)
    w1 = p_sel / (1.0 + e2x)
    w2 = p_sel * e2x / (1.0 + e2x)
    e1 = i1 - n_groups
    e2 = i2 - n_groups
    oh1 = lane == e1
    oh2 = lane == e2
    both = jnp.where(oh1 | oh2, 1.0, 0.0)
    before = jnp.dot(tri_ref[...], both.astype(BF16), preferred_element_type=F32) + run_ref[...]
    r1 = jnp.sum(jnp.where(oh1, before, 0.0), axis=-1, keepdims=True)
    r2 = jnp.sum(jnp.where(oh2, before, 0.0), axis=-1, keepdims=True)
    run_ref[...] += jnp.sum(both, axis=0, keepdims=True)
    cnt_ref[...] = run_ref[...]
    info = jnp.where(lane == 0, e1, 0.0)
    info = jnp.where(lane == 1, e2, info)
    info = jnp.where(lane == 2, r1, info)
    info = jnp.where(lane == 3, r2, info)
    info = jnp.where(lane == 4, w1, info)
    info = jnp.where(lane == 5, w2, info)
    info_ref[...] = info
    infot_ref[...] = jnp.transpose(info)[0:SUBLANES, :]


def router(x, g, w_pad, b_pad, *, n_groups, epg, tm):
    m, d = x.shape
    tri = (jnp.arange(tm)[:, None] > jnp.arange(tm)[None, :]).astype(BF16)
    return pl.pallas_call(
        functools.partial(_router_kernel, n_groups=n_groups, epg=epg),
        grid=(m // tm,),
        in_specs=[pl.BlockSpec((tm, d), lambda i: (i, 0)),
                  pl.BlockSpec((1, d), lambda i: (0, 0)),
                  pl.BlockSpec((d, 2 * LANES), lambda i: (0, 0)),
                  pl.BlockSpec((1, LANES), lambda i: (0, 0)),
                  pl.BlockSpec((tm, tm), lambda i: (0, 0))],
        out_specs=[pl.BlockSpec((tm, d), lambda i: (i, 0)),
                   pl.BlockSpec((tm, LANES), lambda i: (i, 0)),
                   pl.BlockSpec((SUBLANES, tm), lambda i: (0, i)),
                   pl.BlockSpec((1, LANES), lambda i: (0, 0))],
        out_shape=(jax.ShapeDtypeStruct((m, d), F32),
                   jax.ShapeDtypeStruct((m, LANES), F32),
                   jax.ShapeDtypeStruct((SUBLANES, m), F32),
                   jax.ShapeDtypeStruct((1, LANES), F32)),
        scratch_shapes=[pltpu.VMEM((1, LANES), F32)],
        compiler_params=_cparams(("arbitrary",)),
        name="moe_router",
    )(x, g.reshape(1, d), w_pad, b_pad, tri)


def moe_dispatch_plan(infot, counts, n_tokens, n_experts):
    e = infot[0:2].astype(jnp.int32)
    rank = infot[2:4].astype(jnp.int32)
    counts = counts[0, :n_experts].astype(jnp.int32)
    pcounts = ((counts + MOE_BLOCK - 1) // MOE_BLOCK) * MOE_BLOCK
    pends = jnp.cumsum(pcounts)
    pstarts = pends - pcounts
    ids = jnp.arange(n_experts, dtype=jnp.int32)
    dest = rank + jnp.sum(jnp.where(e[:, :, None] == ids, pstarts, 0), axis=-1)
    n_rows = (-(-2 * n_tokens // MOE_BLOCK) + n_experts) * MOE_BLOCK
    n_used = (pends[-1] // MOE_BLOCK).reshape(1)
    return dest, n_rows, n_used, pcounts, pends


def _dispatch_kernel(pc_ref, pe_ref, dest_ref, h_ref, xs_hbm, zbuf, sem, tsem, *, n_experts):
    i = pl.program_id(0)
    tm = h_ref.shape[0]

    @pl.when(i == 0)
    def _():
        zbuf[...] = jnp.zeros_like(zbuf)
        for wait in (False, True):
            for e in range(n_experts):
                @pl.when(pc_ref[e] > 0)
                def _():
                    row0 = pl.multiple_of(pe_ref[e] - MOE_BLOCK, MOE_BLOCK)
                    fill = pltpu.make_async_copy(zbuf, xs_hbm.at[pl.ds(row0, MOE_BLOCK)], sem)
                    fill.wait() if wait else fill.start()

    n_blocks = xs_hbm.shape[0] // MOE_BLOCK
    n_used = pe_ref[n_experts - 1] // MOE_BLOCK

    def tail_fill(b, wait):
        row0 = pl.multiple_of(b * MOE_BLOCK, MOE_BLOCK)
        fill = pltpu.make_async_copy(zbuf, xs_hbm.at[pl.ds(row0, MOE_BLOCK)], tsem)
        fill.wait() if wait else fill.start()

    @pl.when(i == 0)
    def _():
        lax.fori_loop(n_used, n_blocks, lambda b, c: (tail_fill(b, False), c)[1], 0)

    for k in range(2):
        for r in range(tm):
            pltpu.make_async_copy(h_ref.at[pl.ds(r, 1)], xs_hbm.at[pl.ds(dest_ref[k, r], 1)],
                                  sem).start(priority=r % 2)
    for k in range(2):
        pltpu.make_async_copy(h_ref, xs_hbm.at[pl.ds(0, tm)], sem).wait()

    @pl.when(i == pl.num_programs(0) - 1)
    def _():
        lax.fori_loop(n_used, n_blocks, lambda b, c: (tail_fill(b, True), c)[1], 0)


def moe_dispatch(h, dest, pcounts, pends, *, n_rows, tm):
    t, d = h.shape
    grid_spec = pltpu.PrefetchScalarGridSpec(
        num_scalar_prefetch=2,
        grid=(t // tm,),
        in_specs=[pl.BlockSpec((2, tm), lambda i, pc, pe: (0, i), memory_space=pltpu.SMEM),
                  pl.BlockSpec((tm, d), lambda i, pc, pe: (i, 0))],
        out_specs=pl.BlockSpec(memory_space=pl.ANY),
        scratch_shapes=[pltpu.VMEM((MOE_BLOCK, d), F32), pltpu.SemaphoreType.DMA(()),
                        pltpu.SemaphoreType.DMA(())],
    )
    return pl.pallas_call(
        functools.partial(_dispatch_kernel, n_experts=pcounts.shape[0]),
        grid_spec=grid_spec,
        out_shape=jax.ShapeDtypeStruct((n_rows, d), F32),
        compiler_params=_cparams(("arbitrary",)),
        name="moe_dispatch",
    )(pcounts, pends, dest, h)


def _experts_kernel(first_ref, cnt_ref, nb_ref, xs_hbm, wg_ref, wu_ref, wd_ref, ys_hbm,
                    xbuf, ybuf, wgb, wub, wdb, xsem, ysem):
    e = pl.program_id(0)
    nb = nb_ref[0]
    first, cnt = first_ref[e], cnt_ref[e]
    n_blocks = xs_hbm.shape[0] // MOE_BLOCK

    def rows(g):
        return pl.ds(pl.multiple_of(g * MOE_BLOCK, MOE_BLOCK), MOE_BLOCK)

    def x_copy(g, s):
        return pltpu.make_async_copy(xs_hbm.at[rows(g)], xbuf.at[s], xsem.at[s])

    def y_copy(g, s):
        return pltpu.make_async_copy(ybuf.at[s], ys_hbm.at[rows(g)], ysem.at[s])

    @pl.when((e == 0) & (nb > 0))
    def _():
        x_copy(0, 0).start(priority=BLOCK_DMA_QUEUE)

    @pl.when(cnt > 0)
    def _():
        wgb[...] = wg_ref[0].astype(BF16)
        wub[...] = wu_ref[0].astype(BF16)
        wdb[...] = wd_ref[0].astype(BF16)

    def block(b, carry):
        g = first + b
        s = lax.rem(g, 2)

        @pl.when(g + 1 < nb)
        def _():
            x_copy(g + 1, 1 - s).start(priority=BLOCK_DMA_QUEUE)

        x_copy(g, s).wait()
        x = xbuf[s].astype(BF16)
        gte = jnp.dot(x, wgb[...], preferred_element_type=F32)
        up = jnp.dot(x, wub[...], preferred_element_type=F32)
        act = (gte * _sigmoid(gte) * up).astype(BF16)
        y = jnp.dot(act, wdb[...], preferred_element_type=F32)

        @pl.when(g >= 2)
        def _():
            y_copy(g - 2, s).wait()

        ybuf[s] = y
        y_copy(g, s).start(priority=BLOCK_DMA_QUEUE)
        return carry

    lax.fori_loop(0, cnt, block, 0)

    @pl.when(e == pl.num_programs(0) - 1)
    def _():
        for back in (1, 2):
            @pl.when(nb >= back)
            def _():
                y_copy(nb - back, lax.rem(nb - back, 2)).wait()

        ybuf[0] = jnp.zeros(ybuf.shape[1:], F32)
        lax.fori_loop(nb, n_blocks, lambda g, c: (y_copy(g, 0).start(), c)[1], 0)
        lax.fori_loop(nb, n_blocks, lambda g, c: (y_copy(g, 0).wait(), c)[1], 0)


def experts(xs, w_gate, w_up, w_down, first_blk, n_blk, n_used):
    n_rows, d = xs.shape
    n_experts, _, de = w_gate.shape
    grid_spec = pltpu.PrefetchScalarGridSpec(
        num_scalar_prefetch=3,
        grid=(n_experts,),
        in_specs=[pl.BlockSpec(memory_space=pl.ANY),
                  pl.BlockSpec((1, d, de), lambda e, *_: (e, 0, 0)),
                  pl.BlockSpec((1, d, de), lambda e, *_: (e, 0, 0)),
                  pl.BlockSpec((1, de, d), lambda e, *_: (e, 0, 0))],
        out_specs=pl.BlockSpec(memory_space=pl.ANY),
        scratch_shapes=[pltpu.VMEM((2, MOE_BLOCK, d), F32), pltpu.VMEM((2, MOE_BLOCK, d), F32),
                        pltpu.VMEM((d, de), BF16), pltpu.VMEM((d, de), BF16), pltpu.VMEM((de, d), BF16),
                        pltpu.SemaphoreType.DMA((2,)), pltpu.SemaphoreType.DMA((2,))],
    )
    return pl.pallas_call(
        _experts_kernel,
        grid_spec=grid_spec,
        out_shape=jax.ShapeDtypeStruct((n_rows, d), F32),
        compiler_params=_cparams(("arbitrary",)),
        name="moe_experts",
    )(first_blk, n_blk, n_used, xs, w_gate, w_up, w_down)


def _moe_combine_kernel(dest_ref, destn_ref, x_ref, info_ref, g_ref, ys_hbm, o_ref, ybuf, sem, *, norm):
    i = pl.program_id(0)
    n = pl.num_programs(0)
    tm = x_ref.shape[0]
    slot = lax.rem(i, 2)

    def gather_start(dref, s):
        for k in range(2):
            for r in range(tm):
                pltpu.make_async_copy(ys_hbm.at[pl.ds(dref[k, r], 1)], ybuf.at[s, k, pl.ds(r, 1)],
                                      sem.at[s]).start(priority=r % 2)

    @pl.when(i == 0)
    def _():
        gather_start(dest_ref, 0)

    @pl.when(i + 1 < n)
    def _():
        gather_start(destn_ref, 1 - slot)

    for k in range(2):
        pltpu.make_async_copy(ys_hbm.at[pl.ds(0, tm)], ybuf.at[slot, k], sem.at[slot]).wait()
    w1 = info_ref[:, 4:5]
    w2 = info_ref[:, 5:6]
    s = x_ref[...] + (w1 * ybuf[slot, 0] + w2 * ybuf[slot, 1])
    o_ref[...] = _rms(s, g_ref[...]) if norm else s


def moe_combine(x, ys, dest, info, g, *, norm, tm):
    t, d = x.shape
    n = t // tm
    return pl.pallas_call(
        functools.partial(_moe_combine_kernel, norm=norm),
        grid=(n,),
        in_specs=[pl.BlockSpec((2, tm), lambda i: (0, i), memory_space=pltpu.SMEM),
                  pl.BlockSpec((2, tm), lambda i: (0, jnp.minimum(i + 1, n - 1)), memory_space=pltpu.SMEM),
                  pl.BlockSpec((tm, d), lambda i: (i, 0)),
                  pl.BlockSpec((tm, LANES), lambda i: (i, 0)),
                  pl.BlockSpec((1, d), lambda i: (0, 0)),
                  pl.BlockSpec(memory_space=pl.ANY)],
        out_specs=pl.BlockSpec((tm, d), lambda i: (i, 0)),
        out_shape=jax.ShapeDtypeStruct((t, d), F32),
        scratch_shapes=[pltpu.VMEM((2, 2, tm, d), F32), pltpu.SemaphoreType.DMA((2,))],
        compiler_params=_cparams(("arbitrary",)),
        name="moe_combine",
    )(dest, dest, x, info, g.reshape(1, d), ys)


def forward(x, mem, norm_mix_g, w_in, s5_lambda_re, s5_lambda_im, s5_log_dt, s5_b_re, s5_b_im, s5_c_re,
            s5_c_im, s5_d, s5_w_glu, conv_dw_w, conv_dw_b, conv_ln_g, conv_ln_b, conv_w_out, w_out,
            norm_xattn_g, norm_mem_g, xattn_wq, xattn_wk, xattn_wv, xattn_wo, norm_moe_g, router_w_group,
            router_b_group, router_w_expert, router_b_expert, exp_w_gate, exp_w_up, exp_w_down, norm_final_g,
            *, tiles):
    batch, seq, d = x.shape
    depth = w_in.shape[0]
    n_mem = mem.shape[1]
    s5_width = s5_d.shape[1]
    conv_width = conv_dw_b.shape[1]
    n_groups, epg = router_w_expert.shape[1], router_w_expert.shape[3]
    t = batch * seq
    xf = x.reshape(t, d)
    memf = mem.reshape(batch * n_mem, d)
    gate_col0 = s5_width + 2 * conv_width
    for l in range(depth):
        w_in_b = w_in[l].astype(BF16)
        proj = norm_mm_rows(xf, norm_mix_g[l], w_in_b[:, :gate_col0], tm=tiles["proj_tm"], tn=tiles["proj_tn"],
                            out_dtype=F32)
        lbr, lbi, bbr, bbi = s5_discretise(s5_lambda_re[l], s5_lambda_im[l], s5_log_dt[l],
                                           s5_b_re[l], s5_b_im[l])
        bre, bim, cre, cim = s5_block_diag(bbr, bbi, s5_c_re[l], s5_c_im[l])
        y_a = s5_branch(proj, bre, bim, cre, cim, lbr.reshape(1, -1), lbi.reshape(1, -1), s5_d[l],
                        s5_w_glu[l].astype(BF16), batch=batch, seq=seq, width=s5_width, m=tiles["s5_m"],
                        cw=tiles["s5_cw"], tn=tiles["glu_tn"])
        y_b = conv_branch(proj, conv_dw_w[l], conv_dw_b[l], conv_ln_g[l], conv_ln_b[l],
                          conv_w_out[l].astype(BF16), batch=batch, seq=seq, width=conv_width,
                          col0=s5_width, tm=tiles["conv_tm"])
        xf = combine(xf, norm_mix_g[l], w_in_b, y_a, y_b, w_out[l].astype(BF16), gate_col0=gate_col0,
                     tm=tiles["comb_tm"])

        q = norm_mm_rows(xf, norm_xattn_g[l], xattn_wq[l].astype(BF16), tm=tiles["q_tm"], tn=tiles["q_tn"],
                         out_dtype=BF16)
        wkv = jnp.concatenate([xattn_wk[l], xattn_wv[l]], axis=1).astype(BF16)
        kv = norm_mm(memf, norm_mem_g[l], wkv, tm=batch * n_mem, tn=tiles["kv_tn"], out_dtype=BF16)
        xf = xattn(xf, q, kv, xattn_wo[l].astype(BF16), batch=batch, seq=seq, n_mem=n_mem,
                   heads=XATTN_HEADS, tm=tiles["xattn_tm"])

        w_r = jnp.concatenate([router_w_group[l],
                               jnp.transpose(router_w_expert[l], (1, 0, 2)).reshape(d, n_groups * epg)], axis=1)
        b_r = jnp.concatenate([router_b_group[l], router_b_expert[l].reshape(-1)])
        n_r = w_r.shape[1]
        w_pad = jnp.pad(w_r, ((0, 0), (0, LANES - n_r)))
        w_hi = w_pad.astype(BF16)
        w_pad = jnp.concatenate([w_hi, (w_pad - w_hi.astype(F32)).astype(BF16)], axis=1)
        b_pad = jnp.pad(b_r, (0, LANES - n_r)).reshape(1, LANES)
        h, info, infot, counts = router(xf, norm_moe_g[l], w_pad, b_pad, n_groups=n_groups, epg=epg,
                                        tm=tiles["router_tm"])
        dest, n_rows, n_used, pcounts, pends = moe_dispatch_plan(infot, counts, t, n_groups * epg)
        xs = moe_dispatch(h, dest, pcounts, pends, n_rows=n_rows, tm=tiles["moe_tm"])
        ys = experts(xs, exp_w_gate[l], exp_w_up[l], exp_w_down[l], (pends - pcounts) // MOE_BLOCK,
                     pcounts // MOE_BLOCK, n_used)
        last = l + 1 == depth
        xf = moe_combine(xf, ys, dest, info, norm_final_g if last else jnp.ones((d,), F32), norm=last,
                         tm=tiles["moe_tm"])
    return xf.reshape(batch, seq, d)


def kernel(x, mem, norm_mix_g, w_in, s5_lambda_re, s5_lambda_im, s5_log_dt, s5_b_re, s5_b_im, s5_c_re, s5_c_im, s5_d, s5_w_glu, conv_dw_w, conv_dw_b, conv_ln_g, conv_ln_b, conv_w_out, w_out, norm_xattn_g, norm_mem_g, xattn_wq, xattn_wk, xattn_wv, xattn_wo, norm_moe_g, router_w_group, router_b_group, router_w_expert, router_b_expert, exp_w_gate, exp_w_up, exp_w_down, norm_final_g):
    return forward(x, mem, norm_mix_g, w_in, s5_lambda_re, s5_lambda_im, s5_log_dt, s5_b_re, s5_b_im,
                   s5_c_re, s5_c_im, s5_d, s5_w_glu, conv_dw_w, conv_dw_b, conv_ln_g, conv_ln_b, conv_w_out,
                   w_out, norm_xattn_g, norm_mem_g, xattn_wq, xattn_wk, xattn_wv, xattn_wo, norm_moe_g,
                   router_w_group, router_b_group, router_w_expert, router_b_expert, exp_w_gate, exp_w_up,
                   exp_w_down, norm_final_g, tiles=TILES)
```

```python
import functools
import math

import jax
import jax.numpy as jnp
from jax import lax
from jax.experimental import pallas as pl
from jax.experimental.pallas import tpu as pltpu

F32 = jnp.float32
BF16 = jnp.bfloat16

RMS_EPS = 1e-6
LN_EPS = 1e-5

S5_SLAB_CH = 256
CONV_HALO = 32
CONV_ROW_CHUNK = 64
XATTN_HEADS = 4
MOE_BLOCK = 128
BLOCK_DMA_QUEUE = 1
SUBLANES = 8
LANES = 128
VMEM_LIMIT = 56 * 1024 * 1024

TILES = dict(proj_tm=512, proj_tn=512, s5_m=64, s5_cw=512, glu_tn=512, conv_tm=256,
             comb_tm=256, q_tm=512, q_tn=512, kv_tn=512, xattn_tm=512, router_tm=512, moe_tm=256)


def _cparams(sem):
    return pltpu.CompilerParams(dimension_semantics=sem, vmem_limit_bytes=VMEM_LIMIT)


def _rms(x, g):
    return x * lax.rsqrt(jnp.mean(x * x, axis=-1, keepdims=True) + RMS_EPS) * g


def _sigmoid(x):
    return 0.5 * jnp.tanh(0.5 * x) + 0.5


def _gelu_tanh(x):
    c = math.sqrt(2.0 / math.pi)
    return 0.5 * x * (1.0 + jnp.tanh(c * (x + 0.044715 * (x * x * x))))


def _norm_mm_kernel(x_ref, g_ref, w_ref, o_ref, h_ref, *, act_from):
    j = pl.program_id(1)

    @pl.when(j == 0)
    def _():
        h_ref[...] = _rms(x_ref[...], g_ref[...]).astype(BF16)

    y = jnp.dot(h_ref[...], w_ref[...], preferred_element_type=F32)
    if act_from is None:
        o_ref[...] = y.astype(o_ref.dtype)
    else:
        @pl.when(j < act_from)
        def _():
            o_ref[...] = y.astype(o_ref.dtype)

        @pl.when(j >= act_from)
        def _():
            o_ref[...] = _sigmoid(y).astype(o_ref.dtype)


def norm_mm(x, g, w, *, tm, tn, out_dtype=F32, act_from=None):
    m, k = x.shape
    n = w.shape[1]
    return pl.pallas_call(
        functools.partial(_norm_mm_kernel, act_from=act_from),
        grid=(m // tm, n // tn),
        in_specs=[pl.BlockSpec((tm, k), lambda i, j: (i, 0)),
                  pl.BlockSpec((1, k), lambda i, j: (0, 0)),
                  pl.BlockSpec((k, tn), lambda i, j: (0, j))],
        out_specs=pl.BlockSpec((tm, tn), lambda i, j: (i, j)),
        out_shape=jax.ShapeDtypeStruct((m, n), out_dtype),
        scratch_shapes=[pltpu.VMEM((tm, k), BF16)],
        compiler_params=_cparams(("parallel", "arbitrary")),
        name="norm_mm",
    )(x, g.reshape(1, k), w)


def _norm_mm_rows_kernel(x_ref, g_ref, w_ref, o_ref, *, tn):
    h = _rms(x_ref[...], g_ref[...]).astype(BF16)
    for j in range(o_ref.shape[1] // tn):
        cols = slice(j * tn, (j + 1) * tn)
        o_ref[:, cols] = jnp.dot(h, w_ref[:, cols], preferred_element_type=F32).astype(o_ref.dtype)


def norm_mm_rows(x, g, w, *, tm, tn, out_dtype):
    m, k = x.shape
    n = w.shape[1]
    return pl.pallas_call(
        functools.partial(_norm_mm_rows_kernel, tn=tn),
        grid=(m // tm,),
        in_specs=[pl.BlockSpec((tm, k), lambda i: (i, 0)),
                  pl.BlockSpec((1, k), lambda i: (0, 0)),
                  pl.BlockSpec((k, n), lambda i: (0, 0), pipeline_mode=pl.Buffered(1))],
        out_specs=pl.BlockSpec((tm, n), lambda i: (i, 0)),
        out_shape=jax.ShapeDtypeStruct((m, n), out_dtype),
        compiler_params=_cparams(("parallel",)),
        name="norm_mm_rows",
    )(x, g.reshape(1, k), w)


def _s5_disc_kernel(lr_ref, li_ref, ldt_ref, br_ref, bi_ref, lbr_ref, lbi_ref, bbr_ref, bbi_ref):
    lr, li = lr_ref[...], li_ref[...]
    dt = jnp.exp(ldt_ref[...])
    mag = jnp.exp(lr * dt)
    ang = li * dt
    lb_re, lb_im = mag * jnp.cos(ang), mag * jnp.sin(ang)
    den = lr * lr + li * li
    nr, ni = lb_re - 1.0, lb_im
    coef_re = (nr * lr + ni * li) / den
    coef_im = (ni * lr - nr * li) / den
    lbr_ref[...] = lb_re
    lbi_ref[...] = lb_im
    br, bi = br_ref[...], bi_ref[...]
    bbr_ref[...] = coef_re[None] * br - coef_im[None] * bi
    bbi_ref[...] = coef_re[None] * bi + coef_im[None] * br


def s5_discretise(lam_re, lam_im, log_dt, b_re, b_im):
    g, n = lam_re.shape
    c = b_re.shape[-1]
    b_re_t = jnp.transpose(b_re, (2, 0, 1))
    b_im_t = jnp.transpose(b_im, (2, 0, 1))
    gn = jax.ShapeDtypeStruct((g, n), F32)
    cgn = jax.ShapeDtypeStruct((c, g, n), F32)
    return pl.pallas_call(
        _s5_disc_kernel,
        out_shape=(gn, gn, cgn, cgn),
        name="s5_discretise",
    )(lam_re, lam_im, log_dt.reshape(g, 1), b_re_t, b_im_t)


def s5_block_diag(bb_re, bb_im, c_re, c_im):
    c, g, n = bb_re.shape
    gs = S5_SLAB_CH // c
    n_slab = g // gs
    eye = jnp.eye(gs, dtype=F32)

    def bmat(bb):
        b = jnp.transpose(bb, (1, 0, 2)).reshape(n_slab, gs, c, n)
        full = b[:, :, :, None, :] * eye[None, :, None, :, None]
        return full.reshape(n_slab, gs * c, gs * n).astype(BF16)

    def cmat(cc):
        cm = jnp.transpose(cc.reshape(n_slab, gs, c, n), (0, 1, 3, 2))
        full = cm[:, :, :, None, :] * eye[None, :, None, :, None]
        return full.reshape(n_slab, gs * n, gs * c).astype(BF16)

    return bmat(bb_re), bmat(bb_im), cmat(c_re), cmat(c_im)


def _s5_kernel(proj_hbm, bre_ref, bim_ref, cre_ref, cim_ref, lre_ref, lim_ref, d_ref, wglu_ref, ya_hbm,
               ubuf, yabuf, xre_ref, xim_ref, pre_ref, pim_ref, car_re_ref, car_im_ref, isem, osem, *, m, cw, tn):
    t = pl.program_id(1)
    n_t = pl.num_programs(1)
    step_id = pl.program_id(0) * n_t + t
    n_steps = pl.num_programs(0) * n_t
    slot = lax.rem(step_id, 2)
    rows = SUBLANES * m
    width = d_ref.shape[1]
    n_state = lre_ref.shape[1]
    n_slab = bre_ref.shape[0]
    slab_states = n_state // n_slab

    def in_copy(step, s, k):
        r0 = pl.multiple_of(step * rows + k * m, SUBLANES)
        return pltpu.make_async_copy(proj_hbm.at[pl.ds(r0, m), pl.ds(0, width)], ubuf.at[s, :, k, :], isem.at[s])

    def out_copy(step, s, k):
        r0 = pl.multiple_of(step * rows + k * m, SUBLANES)
        return pltpu.make_async_copy(yabuf.at[s, :, k, :], ya_hbm.at[pl.ds(r0, m)], osem.at[s])

    @pl.when(step_id == 0)
    def _():
        for k in range(SUBLANES):
            in_copy(0, 0, k).start()

    @pl.when(step_id + 1 < n_steps)
    def _():
        for k in range(SUBLANES):
            in_copy(step_id + 1, 1 - slot, k).start()

    @pl.when(t == 0)
    def _():
        car_re_ref[...] = jnp.zeros_like(car_re_ref)
        car_im_ref[...] = jnp.zeros_like(car_im_ref)
        lr, li = lre_ref[...], lim_ref[...]

        def pw(_, carry):
            pr, pi = carry
            return pr * lr - pi * li, pr * li + pi * lr

        pr, pi = lax.fori_loop(1, m, pw, (lr, li))
        pre_ref[...] = pr
        pim_ref[...] = pi

    for k in range(SUBLANES):
        in_copy(step_id, slot, k).wait()
    u = ubuf[slot].reshape(rows, width)
    up = u.astype(BF16)
    for s in range(n_slab):
        us = up[:, s * S5_SLAB_CH:(s + 1) * S5_SLAB_CH]
        cols = slice(s * slab_states, (s + 1) * slab_states)
        xre_ref[:, cols] = jnp.dot(us, bre_ref[s], preferred_element_type=F32)
        xim_ref[:, cols] = jnp.dot(us, bim_ref[s], preferred_element_type=F32)

    row_id = lax.broadcasted_iota(jnp.int32, (SUBLANES, cw), 0)
    for cb in range(n_state // cw):
        cols = slice(cb * cw, (cb + 1) * cw)
        lr = jnp.broadcast_to(lre_ref[:, cols], (SUBLANES, cw))
        li = jnp.broadcast_to(lim_ref[:, cols], (SUBLANES, cw))

        def step(tau, carry, store, cols=cols, lr=lr, li=li):
            sr, si = carry
            r0 = pl.multiple_of(tau * SUBLANES, SUBLANES)
            nr = sr * lr - si * li + xre_ref[pl.ds(r0, SUBLANES), cols]
            ni = sr * li + si * lr + xim_ref[pl.ds(r0, SUBLANES), cols]
            if store:
                xre_ref[pl.ds(r0, SUBLANES), cols] = nr
                xim_ref[pl.ds(r0, SUBLANES), cols] = ni
            return nr, ni

        zero = jnp.zeros((SUBLANES, cw), F32)
        er, ei = lax.fori_loop(0, m, functools.partial(step, store=False), (zero, zero), unroll=True)

        lmr = jnp.broadcast_to(pre_ref[:, cols], (SUBLANES, cw))
        lmi = jnp.broadcast_to(pim_ref[:, cols], (SUBLANES, cw))
        ir = jnp.where(row_id == 0, jnp.broadcast_to(car_re_ref[:, cols], (SUBLANES, cw)), 0.0)
        ii = jnp.where(row_id == 0, jnp.broadcast_to(car_im_ref[:, cols], (SUBLANES, cw)), 0.0)
        for k in range(1, SUBLANES):
            nr = ir * lmr - ii * lmi + er
            ni = ir * lmi + ii * lmr + ei
            ir = jnp.where(row_id == k, pltpu.roll(nr, 1, axis=0), ir)
            ii = jnp.where(row_id == k, pltpu.roll(ni, 1, axis=0), ii)

        fr, fi = lax.fori_loop(0, m, functools.partial(step, store=True), (ir, ii), unroll=True)
        car_re_ref[:, cols] = fr[SUBLANES - 1:SUBLANES, :]
        car_im_ref[:, cols] = fi[SUBLANES - 1:SUBLANES, :]

    ys = []
    for s in range(n_slab):
        cols = slice(s * slab_states, (s + 1) * slab_states)
        ys.append(jnp.dot(xre_ref[:, cols].astype(BF16), cre_ref[s], preferred_element_type=F32)
                  - jnp.dot(xim_ref[:, cols].astype(BF16), cim_ref[s], preferred_element_type=F32))
    y = jnp.concatenate(ys, axis=1) + d_ref[...] * u
    z = _gelu_tanh(y).astype(BF16)

    @pl.when(step_id >= 2)
    def _():
        for k in range(SUBLANES):
            out_copy(step_id - 2, slot, k).wait()

    nh = wglu_ref.shape[1] // 2
    for j in range(nh // tn):
        val = jnp.dot(z, wglu_ref[:, j * tn:(j + 1) * tn], preferred_element_type=F32)
        gt = jnp.dot(z, wglu_ref[:, nh + j * tn:nh + (j + 1) * tn], preferred_element_type=F32)
        yabuf[slot, :, :, j * tn:(j + 1) * tn] = (val * _sigmoid(gt)).reshape(m, SUBLANES, tn)
    for k in range(SUBLANES):
        out_copy(step_id, slot, k).start()

    @pl.when(step_id == n_steps - 1)
    def _():
        for k in range(SUBLANES):
            out_copy(step_id, slot, k).wait()
        if ya_hbm.shape[0] // rows >= 2:
            for k in range(SUBLANES):
                out_copy(step_id - 1, 1 - slot, k).wait()


def s5_branch(proj, bre, bim, cre, cim, lre, lim, d_skip, w_glu, *, batch, seq, width, m, cw, tn):
    rows = SUBLANES * m
    n_t = seq // rows
    n_state = lre.shape[1]
    nh = w_glu.shape[1] // 2
    once = pl.Buffered(1)
    const3 = lambda b, t: (0, 0, 0)
    const2 = lambda b, t: (0, 0)
    return pl.pallas_call(
        functools.partial(_s5_kernel, m=m, cw=cw, tn=tn),
        grid=(batch, n_t),
        in_specs=[pl.BlockSpec(memory_space=pl.ANY),
                  pl.BlockSpec(bre.shape, const3, pipeline_mode=once),
                  pl.BlockSpec(bim.shape, const3, pipeline_mode=once),
                  pl.BlockSpec(cre.shape, const3, pipeline_mode=once),
                  pl.BlockSpec(cim.shape, const3, pipeline_mode=once),
                  pl.BlockSpec((1, n_state), const2),
                  pl.BlockSpec((1, n_state), const2),
                  pl.BlockSpec((1, width), const2),
                  pl.BlockSpec(w_glu.shape, const2, pipeline_mode=once)],
        out_specs=pl.BlockSpec(memory_space=pl.ANY),
        out_shape=jax.ShapeDtypeStruct((batch * seq, nh), F32),
        scratch_shapes=[pltpu.VMEM((2, m, SUBLANES, width), F32), pltpu.VMEM((2, m, SUBLANES, nh), F32),
                        pltpu.VMEM((rows, n_state), F32), pltpu.VMEM((rows, n_state), F32),
                        pltpu.VMEM((1, n_state), F32), pltpu.VMEM((1, n_state), F32),
                        pltpu.VMEM((1, n_state), F32), pltpu.VMEM((1, n_state), F32),
                        pltpu.SemaphoreType.DMA((2,)), pltpu.SemaphoreType.DMA((2,))],
        compiler_params=_cparams(("arbitrary", "arbitrary")),
        name="s5_branch",
    )(proj, bre, bim, cre, cim, lre, lim, d_skip.reshape(1, width), w_glu)


def _conv_kernel(a_ref, b_ref, ah_ref, bh_ref, dww_ref, dwb_ref, lng_ref, lnb_ref, w_ref, o_ref,
                 zs_ref, wb_ref, yc_ref, *, tm, taps):
    i = pl.program_id(1)
    off0 = CONV_HALO - (taps - 1)
    width = a_ref.shape[1]
    n_slab = width // LANES
    zh = jnp.where(i == 0, 0.0, ah_ref[...] * _sigmoid(bh_ref[...]))
    z = a_ref[...] * _sigmoid(b_ref[...])
    for s in range(n_slab):
        zs_ref[s, 0:CONV_HALO, :] = zh[:, s * LANES:(s + 1) * LANES]
        zs_ref[s, CONV_HALO:, :] = z[:, s * LANES:(s + 1) * LANES]

    @pl.when(i == 0)
    def _():
        for j in range(taps):
            wb_ref[j] = jnp.broadcast_to(dww_ref[j:j + 1, :], (SUBLANES, width))
        wb_ref[taps] = jnp.broadcast_to(dwb_ref[...], (SUBLANES, width))

    rg = CONV_ROW_CHUNK // SUBLANES

    def chunk(c, carry):
        r0 = pl.multiple_of(c * CONV_ROW_CHUNK, CONV_ROW_CHUNK)
        for s in range(n_slab):
            lanes = slice(s * LANES, (s + 1) * LANES)
            bias = wb_ref[taps, :, lanes]
            acc = [bias] * rg
            for j in range(taps):
                w = wb_ref[j, :, lanes]
                for g in range(rg):
                    acc[g] = acc[g] + w * zs_ref[s, pl.ds(r0 + g * SUBLANES + off0 + j, SUBLANES), :]
            for g in range(rg):
                yc_ref[pl.ds(pl.multiple_of(r0 + g * SUBLANES, SUBLANES), SUBLANES), lanes] = acc[g]
        return carry

    lax.fori_loop(0, tm // CONV_ROW_CHUNK, chunk, 0)
    y = yc_ref[...]
    mu = jnp.mean(y, axis=-1, keepdims=True)
    var = jnp.mean(jnp.square(y - mu), axis=-1, keepdims=True)
    y = (y - mu) * lax.rsqrt(var + LN_EPS) * lng_ref[...] + lnb_ref[...]
    y = y * _sigmoid(y)
    o_ref[...] = jnp.dot(y.astype(BF16), w_ref[...], preferred_element_type=F32)


def conv_branch(proj, dw_w, dw_b, ln_g, ln_b, w_out, *, batch, seq, width, col0, tm):
    n_t = seq // tm
    ca, cb = col0 // width, col0 // width + 1
    hb = tm // CONV_HALO
    taps = dw_w.shape[0]
    d_out = w_out.shape[1]
    const2 = lambda b, i: (0, 0)
    halo = lambda b, i: jnp.maximum((b * n_t + i) * hb - 1, 0)
    return pl.pallas_call(
        functools.partial(_conv_kernel, tm=tm, taps=taps),
        grid=(batch, n_t),
        in_specs=[pl.BlockSpec((tm, width), lambda b, i: (b * n_t + i, ca)),
                  pl.BlockSpec((tm, width), lambda b, i: (b * n_t + i, cb)),
                  pl.BlockSpec((CONV_HALO, width), lambda b, i: (halo(b, i), ca)),
                  pl.BlockSpec((CONV_HALO, width), lambda b, i: (halo(b, i), cb)),
                  pl.BlockSpec((taps, width), const2),
                  pl.BlockSpec((1, width), const2),
                  pl.BlockSpec((1, width), const2),
                  pl.BlockSpec((1, width), const2),
                  pl.BlockSpec((width, d_out), const2)],
        out_specs=pl.BlockSpec((tm, d_out), lambda b, i: (b * n_t + i, 0)),
        out_shape=jax.ShapeDtypeStruct((batch * seq, d_out), F32),
        scratch_shapes=[pltpu.VMEM((width // LANES, tm + CONV_HALO, LANES), F32),
                        pltpu.VMEM((taps + 1, SUBLANES, width), F32),
                        pltpu.VMEM((tm, width), F32)],
        compiler_params=_cparams(("parallel", "arbitrary")),
        name="conv_branch",
    )(proj, proj, proj, proj, dw_w, dw_b.reshape(1, width), ln_g.reshape(1, width),
      ln_b.reshape(1, width), w_out)


def _combine_kernel(*refs, parts):
    x_ref, g_ref = refs[0], refs[1]
    ya_ref, yb_ref, wo_ref, o_ref = refs[2 + 2 * parts], refs[3 + 2 * parts], refs[-2], refs[-1]
    gw = ya_ref.shape[1] // parts
    x = x_ref[...]
    h = _rms(x, g_ref[...]).astype(BF16)
    o_ref[...] = x
    for p in range(parts):
        cols = slice(p * gw, (p + 1) * gw)
        ga = _sigmoid(jnp.dot(h, refs[2 + p][...], preferred_element_type=F32))
        gb = _sigmoid(jnp.dot(h, refs[2 + parts + p][...], preferred_element_type=F32))
        mix = ga * ya_ref[:, cols] + gb * yb_ref[:, cols]
        o_ref[...] += jnp.dot(mix.astype(BF16), wo_ref[cols, :], preferred_element_type=F32)


def combine(x, g, w_in, y_a, y_b, w_out, *, gate_col0, tm):
    m, d = x.shape
    gw = math.gcd(gate_col0, d)
    parts = d // gw
    row = lambda i: (i, 0)
    once = pl.Buffered(1)
    gate_specs = [pl.BlockSpec((d, gw), functools.partial(lambda i, c: (0, c), c=(gate_col0 + s * d) // gw + p),
                               pipeline_mode=once) for s in range(2) for p in range(parts)]
    return pl.pallas_call(
        functools.partial(_combine_kernel, parts=parts),
        grid=(m // tm,),
        in_specs=[pl.BlockSpec((tm, d), row), pl.BlockSpec((1, d), lambda i: (0, 0))] + gate_specs +
                 [pl.BlockSpec((tm, d), row),
                  pl.BlockSpec((tm, d), row),
                  pl.BlockSpec((d, d), lambda i: (0, 0), pipeline_mode=once)],
        out_specs=pl.BlockSpec((tm, d), row),
        out_shape=jax.ShapeDtypeStruct((m, d), F32),
        compiler_params=_cparams(("parallel",)),
        name="gated_combine",
    )(x, g.reshape(1, d), *([w_in] * (2 * parts)), y_a, y_b, w_out)


def _xattn_kernel(x_ref, q_ref, kv_ref, wo_ref, o_ref, *, heads, scale):
    d = x_ref.shape[1]
    hd = d // heads
    acc = x_ref[...]
    for h in range(heads):
        q = q_ref[:, h * hd:(h + 1) * hd]
        k = kv_ref[:, h * hd:(h + 1) * hd]
        v = kv_ref[:, d + h * hd:d + (h + 1) * hd]
        s = lax.dot_general(q, k, (((1,), (1,)), ((), ())), preferred_element_type=F32) * scale
        s = s - jnp.max(s, axis=-1, keepdims=True)
        p = jnp.exp(s)
        p = p / jnp.sum(p, axis=-1, keepdims=True)
        o = jnp.dot(p.astype(BF16), v, preferred_element_type=F32)
        acc = acc + jnp.dot(o.astype(BF16), wo_ref[h * hd:(h + 1) * hd, :], preferred_element_type=F32)
    o_ref[...] = acc


def xattn(x, q, kv, wo, *, batch, seq, n_mem, heads, tm):
    d = x.shape[1]
    n_t = seq // tm
    scale = 1.0 / math.sqrt(d // heads)
    return pl.pallas_call(
        functools.partial(_xattn_kernel, heads=heads, scale=scale),
        grid=(batch, n_t),
        in_specs=[pl.BlockSpec((tm, d), lambda b, i: (b * n_t + i, 0)),
                  pl.BlockSpec((tm, d), lambda b, i: (b * n_t + i, 0)),
                  pl.BlockSpec((n_mem, 2 * d), lambda b, i: (b, 0)),
                  pl.BlockSpec((d, d), lambda b, i: (0, 0))],
        out_specs=pl.BlockSpec((tm, d), lambda b, i: (b * n_t + i, 0)),
        out_shape=jax.ShapeDtypeStruct((batch * seq, d), F32),
        compiler_params=_cparams(("parallel", "parallel")),
        name="xattn",
    )(x, q, kv, wo)


def _router_kernel(x_ref, g_ref, w_ref, b_ref, tri_ref, h_ref, info_ref, infot_ref, cnt_ref, run_ref, *,
                   n_groups, epg):
    i = pl.program_id(0)
    tm = x_ref.shape[0]

    @pl.when(i == 0)
    def _():
        run_ref[...] = jnp.zeros_like(run_ref)

    h = _rms(x_ref[...], g_ref[...])
    h_ref[...] = h
    h_hi = h.astype(BF16)
    h_lo = (h - h_hi.astype(F32)).astype(BF16)
    p = jnp.dot(h_hi, w_ref[...], preferred_element_type=F32)
    logits = (p[:, :LANES] + p[:, LANES:] + jnp.dot(h_lo, w_ref[:, :LANES], preferred_element_type=F32)
              + b_ref[...])
    lane = lax.broadcasted_iota(jnp.int32, (tm, LANES), 1).astype(F32)
    neg = jnp.float32(-jnp.inf)
    big = jnp.float32(LANES)
    is_grp = lane < n_groups
    cl = jnp.where(is_grp, logits, neg)
    cmax = jnp.max(cl, axis=-1, keepdims=True)
    g_idx = jnp.min(jnp.where(cl == cmax, lane, big), axis=-1, keepdims=True)
    p_sel = 1.0 / jnp.sum(jnp.where(is_grp, jnp.exp(cl - cmax), 0.0), axis=-1, keepdims=True)
    lo = n_groups + g_idx * epg
    in_grp = (lane >= lo) & (lane < lo + epg)
    fl = jnp.where(in_grp, logits, neg)
    v1 = jnp.max(fl, axis=-1, keepdims=True)
    i1 = jnp.min(jnp.where(fl == v1, lane, big), axis=-1, keepdims=True)
    fl2 = jnp.where(lane == i1, neg, fl)
    v2 = jnp.max(fl2, axis=-1, keepdims=True)
    i2 = jnp.min(jnp.where(fl2 == v2, lane, big), axis=-1, keepdims=True)
    e2x = jnp.exp(v2 - v1)
    w1 = p_sel / (1.0 + e2x)
    w2 = p_sel * e2x / (1.0 + e2x)
    e1 = i1 - n_groups
    e2 = i2 - n_groups
    oh1 = lane == e1
    oh2 = lane == e2
    both = jnp.where(oh1 | oh2, 1.0, 0.0)
    before = jnp.dot(tri_ref[...], both.astype(BF16), preferred_element_type=F32) + run_ref[...]
    r1 = jnp.sum(jnp.where(oh1, before, 0.0), axis=-1, keepdims=True)
    r2 = jnp.sum(jnp.where(oh2, before, 0.0), axis=-1, keepdims=True)
    run_ref[...] += jnp.sum(both, axis=0, keepdims=True)
    cnt_ref[...] = run_ref[...]
    info = jnp.where(lane == 0, e1, 0.0)
    info = jnp.where(lane == 1, e2, info)
    info = jnp.where(lane == 2, r1, info)
    info = jnp.where(lane == 3, r2, info)
    info = jnp.where(lane == 4, w1, info)
    info = jnp.where(lane == 5, w2, info)
    info_ref[...] = info
    infot_ref[...] = jnp.transpose(info)[0:SUBLANES, :]


def router(x, g, w_pad, b_pad, *, n_groups, epg, tm):
    m, d = x.shape
    tri = (jnp.arange(tm)[:, None] > jnp.arange(tm)[None, :]).astype(BF16)
    return pl.pallas_call(
        functools.partial(_router_kernel, n_groups=n_groups, epg=epg),
        grid=(m // tm,),
        in_specs=[pl.BlockSpec((tm, d), lambda i: (i, 0)),
                  pl.BlockSpec((1, d), lambda i: (0, 0)),
                  pl.BlockSpec((d, 2 * LANES), lambda i: (0, 0)),
                  pl.BlockSpec((1, LANES), lambda i: (0, 0)),
                  pl.BlockSpec((tm, tm), lambda i: (0, 0))],
        out_specs=[pl.BlockSpec((tm, d), lambda i: (i, 0)),
                   pl.BlockSpec((tm, LANES), lambda i: (i, 0)),
                   pl.BlockSpec((SUBLANES, tm), lambda i: (0, i)),
                   pl.BlockSpec((1, LANES), lambda i: (0, 0))],
        out_shape=(jax.ShapeDtypeStruct((m, d), F32),
                   jax.ShapeDtypeStruct((m, LANES), F32),
                   jax.ShapeDtypeStruct((SUBLANES, m), F32),
                   jax.ShapeDtypeStruct((1, LANES), F32)),
        scratch_shapes=[pltpu.VMEM((1, LANES), F32)],
        compiler_params=_cparams(("arbitrary",)),
        name="moe_router",
    )(x, g.reshape(1, d), w_pad, b_pad, tri)


def moe_dispatch_plan(infot, counts, n_tokens, n_experts):
    e = infot[0:2].astype(jnp.int32)
    rank = infot[2:4].astype(jnp.int32)
    counts = counts[0, :n_experts].astype(jnp.int32)
    pcounts = ((counts + MOE_BLOCK - 1) // MOE_BLOCK) * MOE_BLOCK
    pends = jnp.cumsum(pcounts)
    pstarts = pends - pcounts
    ids = jnp.arange(n_experts, dtype=jnp.int32)
    dest = rank + jnp.sum(jnp.where(e[:, :, None] == ids, pstarts, 0), axis=-1)
    n_rows = (-(-2 * n_tokens // MOE_BLOCK) + n_experts) * MOE_BLOCK
    n_used = (pends[-1] // MOE_BLOCK).reshape(1)
    return dest, n_rows, n_used, pcounts, pends


def _dispatch_kernel(pc_ref, pe_ref, dest_ref, h_ref, xs_hbm, zbuf, sem, tsem, *, n_experts):
    i = pl.program_id(0)
    tm = h_ref.shape[0]

    @pl.when(i == 0)
    def _():
        zbuf[...] = jnp.zeros_like(zbuf)
        for wait in (False, True):
            for e in range(n_experts):
                @pl.when(pc_ref[e] > 0)
                def _():
                    row0 = pl.multiple_of(pe_ref[e] - MOE_BLOCK, MOE_BLOCK)
                    fill = pltpu.make_async_copy(zbuf, xs_hbm.at[pl.ds(row0, MOE_BLOCK)], sem)
                    fill.wait() if wait else fill.start()

    n_blocks = xs_hbm.shape[0] // MOE_BLOCK
    n_used = pe_ref[n_experts - 1] // MOE_BLOCK

    def tail_fill(b, wait):
        row0 = pl.multiple_of(b * MOE_BLOCK, MOE_BLOCK)
        fill = pltpu.make_async_copy(zbuf, xs_hbm.at[pl.ds(row0, MOE_BLOCK)], tsem)
        fill.wait() if wait else fill.start()

    @pl.when(i == 0)
    def _():
        lax.fori_loop(n_used, n_blocks, lambda b, c: (tail_fill(b, False), c)[1], 0)

    for k in range(2):
        for r in range(tm):
            pltpu.make_async_copy(h_ref.at[pl.ds(r, 1)], xs_hbm.at[pl.ds(dest_ref[k, r], 1)],
                                  sem).start(priority=r % 2)
    for k in range(2):
        pltpu.make_async_copy(h_ref, xs_hbm.at[pl.ds(0, tm)], sem).wait()

    @pl.when(i == pl.num_programs(0) - 1)
    def _():
        lax.fori_loop(n_used, n_blocks, lambda b, c: (tail_fill(b, True), c)[1], 0)


def moe_dispatch(h, dest, pcounts, pends, *, n_rows, tm):
    t, d = h.shape
    grid_spec = pltpu.PrefetchScalarGridSpec(
        num_scalar_prefetch=2,
        grid=(t // tm,),
        in_specs=[pl.BlockSpec((2, tm), lambda i, pc, pe: (0, i), memory_space=pltpu.SMEM),
                  pl.BlockSpec((tm, d), lambda i, pc, pe: (i, 0))],
        out_specs=pl.BlockSpec(memory_space=pl.ANY),
        scratch_shapes=[pltpu.VMEM((MOE_BLOCK, d), F32), pltpu.SemaphoreType.DMA(()),
                        pltpu.SemaphoreType.DMA(())],
    )
    return pl.pallas_call(
        functools.partial(_dispatch_kernel, n_experts=pcounts.shape[0]),
        grid_spec=grid_spec,
        out_shape=jax.ShapeDtypeStruct((n_rows, d), F32),
        compiler_params=_cparams(("arbitrary",)),
        name="moe_dispatch",
    )(pcounts, pends, dest, h)


def _experts_kernel(first_ref, cnt_ref, nb_ref, xs_hbm, wg_hbm, wu_hbm, wd_hbm, ys_hbm,
                    xbuf, ybuf, wgf, wuf, wdf, wgb, wub, wdb, xsem, ysem, wsem):
    e = pl.program_id(0)
    n_e = pl.num_programs(0)
    nb = nb_ref[0]
    first, cnt = first_ref[e], cnt_ref[e]
    n_blocks = xs_hbm.shape[0] // MOE_BLOCK
    wslot = lax.rem(e, 2)

    def rows(g):
        return pl.ds(pl.multiple_of(g * MOE_BLOCK, MOE_BLOCK), MOE_BLOCK)

    def x_copy(g, s):
        return pltpu.make_async_copy(xs_hbm.at[rows(g)], xbuf.at[s], xsem.at[s])

    def y_copy(g, s):
        return pltpu.make_async_copy(ybuf.at[s], ys_hbm.at[rows(g)], ysem.at[s])

    def w_copies(ex, s):
        half = wd_hbm.shape[1] // 2
        lo, hi = pl.ds(0, half), pl.ds(half, half)
        return ((pltpu.make_async_copy(wg_hbm.at[ex], wgf.at[s], wsem.at[s]), 0),
                (pltpu.make_async_copy(wu_hbm.at[ex], wuf.at[s], wsem.at[s]), 1),
                (pltpu.make_async_copy(wd_hbm.at[ex, lo], wdf.at[s, lo], wsem.at[s]), 0),
                (pltpu.make_async_copy(wd_hbm.at[ex, hi], wdf.at[s, hi], wsem.at[s]), 1))

    @pl.when((e == 0) & (nb > 0))
    def _():
        x_copy(0, 0).start(priority=BLOCK_DMA_QUEUE)

    @pl.when((e == 0) & (cnt > 0))
    def _():
        for cp, queue in w_copies(0, 0):
            cp.start(priority=queue)

    e_next = jnp.minimum(e + 1, n_e - 1)

    @pl.when((e + 1 < n_e) & (cnt_ref[e_next] > 0))
    def _():
        for cp, queue in w_copies(e_next, 1 - wslot):
            cp.start(priority=queue)

    @pl.when(cnt > 0)
    def _():
        for cp, _ in w_copies(e, wslot):
            cp.wait()
        wgb[...] = wgf[wslot].astype(BF16)
        wub[...] = wuf[wslot].astype(BF16)
        wdb[...] = wdf[wslot].astype(BF16)

    def block(b, carry):
        g = first + b
        s = lax.rem(g, 2)

        @pl.when(g + 1 < nb)
        def _():
            x_copy(g + 1, 1 - s).start(priority=BLOCK_DMA_QUEUE)

        x_copy(g, s).wait()
        x = xbuf[s].astype(BF16)
        gte = jnp.dot(x, wgb[...], preferred_element_type=F32)
        up = jnp.dot(x, wub[...], preferred_element_type=F32)
        act = (gte * _sigmoid(gte) * up).astype(BF16)
        y = jnp.dot(act, wdb[...], preferred_element_type=F32)

        @pl.when(g >= 2)
        def _():
            y_copy(g - 2, s).wait()

        ybuf[s] = y
        y_copy(g, s).start()
        return carry

    lax.fori_loop(0, cnt, block, 0)

    @pl.when(e == pl.num_programs(0) - 1)
    def _():
        for back in (1, 2):
            @pl.when(nb >= back)
            def _():
                y_copy(nb - back, lax.rem(nb - back, 2)).wait()

        ybuf[0] = jnp.zeros(ybuf.shape[1:], F32)
        lax.fori_loop(nb, n_blocks, lambda g, c: (y_copy(g, 0).start(), c)[1], 0)
        lax.fori_loop(nb, n_blocks, lambda g, c: (y_copy(g, 0).wait(), c)[1], 0)


def experts(xs, w_gate, w_up, w_down, first_blk, n_blk, n_used):
    n_rows, d = xs.shape
    n_experts, _, de = w_gate.shape
    grid_spec = pltpu.PrefetchScalarGridSpec(
        num_scalar_prefetch=3,
        grid=(n_experts,),
        in_specs=[pl.BlockSpec(memory_space=pl.ANY)] * 4,
        out_specs=pl.BlockSpec(memory_space=pl.ANY),
        scratch_shapes=[pltpu.VMEM((2, MOE_BLOCK, d), F32), pltpu.VMEM((2, MOE_BLOCK, d), F32),
                        pltpu.VMEM((2, d, de), F32), pltpu.VMEM((2, d, de), F32), pltpu.VMEM((2, de, d), F32),
                        pltpu.VMEM((d, de), BF16), pltpu.VMEM((d, de), BF16), pltpu.VMEM((de, d), BF16),
                        pltpu.SemaphoreType.DMA((2,)), pltpu.SemaphoreType.DMA((2,)),
                        pltpu.SemaphoreType.DMA((2,))],
    )
    return pl.pallas_call(
        _experts_kernel,
        grid_spec=grid_spec,
        out_shape=jax.ShapeDtypeStruct((n_rows, d), F32),
        compiler_params=_cparams(("arbitrary",)),
        name="moe_experts",
    )(first_blk, n_blk, n_used, xs, w_gate, w_up, w_down)


def _moe_combine_kernel(dest_ref, destn_ref, x_ref, info_ref, g_ref, ys_hbm, o_ref, ybuf, sem, *, norm):
    i = pl.program_id(0)
    n = pl.num_programs(0)
    tm = x_ref.shape[0]
    slot = lax.rem(i, 2)

    def gather_start(dref, s):
        for k in range(2):
            for r in range(tm):
                pltpu.make_async_copy(ys_hbm.at[pl.ds(dref[k, r], 1)], ybuf.at[s, k, pl.ds(r, 1)],
                                      sem.at[s]).start(priority=r % 2)

    @pl.when(i == 0)
    def _():
        gather_start(dest_ref, 0)

    @pl.when(i + 1 < n)
    def _():
        gather_start(destn_ref, 1 - slot)

    for k in range(2):
        pltpu.make_async_copy(ys_hbm.at[pl.ds(0, tm)], ybuf.at[slot, k], sem.at[slot]).wait()
    w1 = info_ref[:, 4:5]
    w2 = info_ref[:, 5:6]
    s = x_ref[...] + (w1 * ybuf[slot, 0] + w2 * ybuf[slot, 1])
    o_ref[...] = _rms(s, g_ref[...]) if norm else s


def moe_combine(x, ys, dest, info, g, *, norm, tm):
    t, d = x.shape
    n = t // tm
    return pl.pallas_call(
        functools.partial(_moe_combine_kernel, norm=norm),
        grid=(n,),
        in_specs=[pl.BlockSpec((2, tm), lambda i: (0, i), memory_space=pltpu.SMEM),
                  pl.BlockSpec((2, tm), lambda i: (0, jnp.minimum(i + 1, n - 1)), memory_space=pltpu.SMEM),
                  pl.BlockSpec((tm, d), lambda i: (i, 0)),
                  pl.BlockSpec((tm, LANES), lambda i: (i, 0)),
                  pl.BlockSpec((1, d), lambda i: (0, 0)),
                  pl.BlockSpec(memory_space=pl.ANY)],
        out_specs=pl.BlockSpec((tm, d), lambda i: (i, 0)),
        out_shape=jax.ShapeDtypeStruct((t, d), F32),
        scratch_shapes=[pltpu.VMEM((2, 2, tm, d), F32), pltpu.SemaphoreType.DMA((2,))],
        compiler_params=_cparams(("arbitrary",)),
        name="moe_combine",
    )(dest, dest, x, info, g.reshape(1, d), ys)


def forward(x, mem, norm_mix_g, w_in, s5_lambda_re, s5_lambda_im, s5_log_dt, s5_b_re, s5_b_im, s5_c_re,
            s5_c_im, s5_d, s5_w_glu, conv_dw_w, conv_dw_b, conv_ln_g, conv_ln_b, conv_w_out, w_out,
            norm_xattn_g, norm_mem_g, xattn_wq, xattn_wk, xattn_wv, xattn_wo, norm_moe_g, router_w_group,
            router_b_group, router_w_expert, router_b_expert, exp_w_gate, exp_w_up, exp_w_down, norm_final_g,
            *, tiles):
    batch, seq, d = x.shape
    depth = w_in.shape[0]
    n_mem = mem.shape[1]
    s5_width = s5_d.shape[1]
    conv_width = conv_dw_b.shape[1]
    n_groups, epg = router_w_expert.shape[1], router_w_expert.shape[3]
    t = batch * seq
    xf = x.reshape(t, d)
    memf = mem.reshape(batch * n_mem, d)
    gate_col0 = s5_width + 2 * conv_width
    for l in range(depth):
        w_in_b = w_in[l].astype(BF16)
        proj = norm_mm_rows(xf, norm_mix_g[l], w_in_b[:, :gate_col0], tm=tiles["proj_tm"], tn=tiles["proj_tn"],
                            out_dtype=F32)
        lbr, lbi, bbr, bbi = s5_discretise(s5_lambda_re[l], s5_lambda_im[l], s5_log_dt[l],
                                           s5_b_re[l], s5_b_im[l])
        bre, bim, cre, cim = s5_block_diag(bbr, bbi, s5_c_re[l], s5_c_im[l])
        y_a = s5_branch(proj, bre, bim, cre, cim, lbr.reshape(1, -1), lbi.reshape(1, -1), s5_d[l],
                        s5_w_glu[l].astype(BF16), batch=batch, seq=seq, width=s5_width, m=tiles["s5_m"],
                        cw=tiles["s5_cw"], tn=tiles["glu_tn"])
        y_b = conv_branch(proj, conv_dw_w[l], conv_dw_b[l], conv_ln_g[l], conv_ln_b[l],
                          conv_w_out[l].astype(BF16), batch=batch, seq=seq, width=conv_width,
                          col0=s5_width, tm=tiles["conv_tm"])
        xf = combine(xf, norm_mix_g[l], w_in_b, y_a, y_b, w_out[l].astype(BF16), gate_col0=gate_col0,
                     tm=tiles["comb_tm"])

        q = norm_mm_rows(xf, norm_xattn_g[l], xattn_wq[l].astype(BF16), tm=tiles["q_tm"], tn=tiles["q_tn"],
                         out_dtype=BF16)
        wkv = jnp.concatenate([xattn_wk[l], xattn_wv[l]], axis=1).astype(BF16)
        kv = norm_mm(memf, norm_mem_g[l], wkv, tm=batch * n_mem, tn=tiles["kv_tn"], out_dtype=BF16)
        xf = xattn(xf, q, kv, xattn_wo[l].astype(BF16), batch=batch, seq=seq, n_mem=n_mem,
                   heads=XATTN_HEADS, tm=tiles["xattn_tm"])

        w_r = jnp.concatenate([router_w_group[l],
                               jnp.transpose(router_w_expert[l], (1, 0, 2)).reshape(d, n_groups * epg)], axis=1)
        b_r = jnp.concatenate([router_b_group[l], router_b_expert[l].reshape(-1)])
        n_r = w_r.shape[1]
        w_pad = jnp.pad(w_r, ((0, 0), (0, LANES - n_r)))
        w_hi = w_pad.astype(BF16)
        w_pad = jnp.concatenate([w_hi, (w_pad - w_hi.astype(F32)).astype(BF16)], axis=1)
        b_pad = jnp.pad(b_r, (0, LANES - n_r)).reshape(1, LANES)
        h, info, infot, counts = router(xf, norm_moe_g[l], w_pad, b_pad, n_groups=n_groups, epg=epg,
                                        tm=tiles["router_tm"])
        dest, n_rows, n_used, pcounts, pends = moe_dispatch_plan(infot, counts, t, n_groups * epg)
        xs = moe_dispatch(h, dest, pcounts, pends, n_rows=n_rows, tm=tiles["moe_tm"])
        ys = experts(xs, exp_w_gate[l], exp_w_up[l], exp_w_down[l], (pends - pcounts) // MOE_BLOCK,
                     pcounts // MOE_BLOCK, n_used)
        last = l + 1 == depth
        xf = moe_combine(xf, ys, dest, info, norm_final_g if last else jnp.ones((d,), F32), norm=last,
                         tm=tiles["moe_tm"])
    return xf.reshape(batch, seq, d)


def kernel(x, mem, norm_mix_g, w_in, s5_lambda_re, s5_lambda_im, s5_log_dt, s5_b_re, s5_b_im, s5_c_re, s5_c_im, s5_d, s5_w_glu, conv_dw_w, conv_dw_b, conv_ln_g, conv_ln_b, conv_w_out, w_out, norm_xattn_g, norm_mem_g, xattn_wq, xattn_wk, xattn_wv, xattn_wo, norm_moe_g, router_w_group, router_b_group, router_w_expert, router_b_expert, exp_w_gate, exp_w_up, exp_w_down, norm_final_g):
    return forward(x, mem, norm_mix_g, w_in, s5_lambda_re, s5_lambda_im, s5_log_dt, s5_b_re, s5_b_im,
                   s5_c_re, s5_c_im, s5_d, s5_w_glu, conv_dw_w, conv_dw_b, conv_ln_g, conv_ln_b, conv_w_out,
                   w_out, norm_xattn_g, norm_mem_g, xattn_wq, xattn_wk, xattn_wv, xattn_wo, norm_moe_g,
                   router_w_group, router_b_group, router_w_expert, router_b_expert, exp_w_gate, exp_w_up,
                   exp_w_down, norm_final_g, tiles=TILES)
```

```python
import functools
import math

import jax
import jax.numpy as jnp
from jax import lax
from jax.experimental import pallas as pl
from jax.experimental.pallas import tpu as pltpu

F32 = jnp.float32
BF16 = jnp.bfloat16

RMS_EPS = 1e-6
LN_EPS = 1e-5

S5_SLAB_CH = 256
CONV_HALO = 32
CONV_ROW_CHUNK = 64
XATTN_HEADS = 4
MOE_BLOCK = 128
BLOCK_DMA_QUEUE = 1
EXPERT_RING = 4
SUBLANES = 8
LANES = 128
VMEM_LIMIT = 56 * 1024 * 1024

TILES = dict(proj_tm=512, proj_tn=512, s5_m=64, s5_cw=512, glu_tn=512, conv_tm=256,
             comb_tm=256, q_tm=512, q_tn=512, kv_tn=512, xattn_tm=512, router_tm=512, moe_tm=256)


def _cparams(sem):
    return pltpu.CompilerParams(dimension_semantics=sem, vmem_limit_bytes=VMEM_LIMIT)


def _rms(x, g):
    return x * lax.rsqrt(jnp.mean(x * x, axis=-1, keepdims=True) + RMS_EPS) * g


def _sigmoid(x):
    return 0.5 * jnp.tanh(0.5 * x) + 0.5


def _gelu_tanh(x):
    c = math.sqrt(2.0 / math.pi)
    return 0.5 * x * (1.0 + jnp.tanh(c * (x + 0.044715 * (x * x * x))))


def _norm_mm_kernel(x_ref, g_ref, w_ref, o_ref, h_ref, *, act_from):
    j = pl.program_id(1)

    @pl.when(j == 0)
    def _():
        h_ref[...] = _rms(x_ref[...], g_ref[...]).astype(BF16)

    y = jnp.dot(h_ref[...], w_ref[...], preferred_element_type=F32)
    if act_from is None:
        o_ref[...] = y.astype(o_ref.dtype)
    else:
        @pl.when(j < act_from)
        def _():
            o_ref[...] = y.astype(o_ref.dtype)

        @pl.when(j >= act_from)
        def _():
            o_ref[...] = _sigmoid(y).astype(o_ref.dtype)


def norm_mm(x, g, w, *, tm, tn, out_dtype=F32, act_from=None):
    m, k = x.shape
    n = w.shape[1]
    return pl.pallas_call(
        functools.partial(_norm_mm_kernel, act_from=act_from),
        grid=(m // tm, n // tn),
        in_specs=[pl.BlockSpec((tm, k), lambda i, j: (i, 0)),
                  pl.BlockSpec((1, k), lambda i, j: (0, 0)),
                  pl.BlockSpec((k, tn), lambda i, j: (0, j))],
        out_specs=pl.BlockSpec((tm, tn), lambda i, j: (i, j)),
        out_shape=jax.ShapeDtypeStruct((m, n), out_dtype),
        scratch_shapes=[pltpu.VMEM((tm, k), BF16)],
        compiler_params=_cparams(("parallel", "arbitrary")),
        name="norm_mm",
    )(x, g.reshape(1, k), w)


def _norm_mm_rows_kernel(x_ref, g_ref, w_ref, o_ref, *, tn):
    h = _rms(x_ref[...], g_ref[...]).astype(BF16)
    for j in range(o_ref.shape[1] // tn):
        cols = slice(j * tn, (j + 1) * tn)
        o_ref[:, cols] = jnp.dot(h, w_ref[:, cols], preferred_element_type=F32).astype(o_ref.dtype)


def norm_mm_rows(x, g, w, *, tm, tn, out_dtype):
    m, k = x.shape
    n = w.shape[1]
    return pl.pallas_call(
        functools.partial(_norm_mm_rows_kernel, tn=tn),
        grid=(m // tm,),
        in_specs=[pl.BlockSpec((tm, k), lambda i: (i, 0)),
                  pl.BlockSpec((1, k), lambda i: (0, 0)),
                  pl.BlockSpec((k, n), lambda i: (0, 0), pipeline_mode=pl.Buffered(1))],
        out_specs=pl.BlockSpec((tm, n), lambda i: (i, 0)),
        out_shape=jax.ShapeDtypeStruct((m, n), out_dtype),
        compiler_params=_cparams(("parallel",)),
        name="norm_mm_rows",
    )(x, g.reshape(1, k), w)


def _s5_disc_kernel(lr_ref, li_ref, ldt_ref, br_ref, bi_ref, lbr_ref, lbi_ref, bbr_ref, bbi_ref):
    lr, li = lr_ref[...], li_ref[...]
    dt = jnp.exp(ldt_ref[...])
    mag = jnp.exp(lr * dt)
    ang = li * dt
    lb_re, lb_im = mag * jnp.cos(ang), mag * jnp.sin(ang)
    den = lr * lr + li * li
    nr, ni = lb_re - 1.0, lb_im
    coef_re = (nr * lr + ni * li) / den
    coef_im = (ni * lr - nr * li) / den
    lbr_ref[...] = lb_re
    lbi_ref[...] = lb_im
    br, bi = br_ref[...], bi_ref[...]
    bbr_ref[...] = coef_re[None] * br - coef_im[None] * bi
    bbi_ref[...] = coef_re[None] * bi + coef_im[None] * br


def s5_discretise(lam_re, lam_im, log_dt, b_re, b_im):
    g, n = lam_re.shape
    c = b_re.shape[-1]
    b_re_t = jnp.transpose(b_re, (2, 0, 1))
    b_im_t = jnp.transpose(b_im, (2, 0, 1))
    gn = jax.ShapeDtypeStruct((g, n), F32)
    cgn = jax.ShapeDtypeStruct((c, g, n), F32)
    return pl.pallas_call(
        _s5_disc_kernel,
        out_shape=(gn, gn, cgn, cgn),
        name="s5_discretise",
    )(lam_re, lam_im, log_dt.reshape(g, 1), b_re_t, b_im_t)


def s5_block_diag(bb_re, bb_im, c_re, c_im):
    c, g, n = bb_re.shape
    gs = S5_SLAB_CH // c
    n_slab = g // gs
    eye = jnp.eye(gs, dtype=F32)

    def bmat(bb):
        b = jnp.transpose(bb, (1, 0, 2)).reshape(n_slab, gs, c, n)
        full = b[:, :, :, None, :] * eye[None, :, None, :, None]
        return full.reshape(n_slab, gs * c, gs * n).astype(BF16)

    def cmat(cc):
        cm = jnp.transpose(cc.reshape(n_slab, gs, c, n), (0, 1, 3, 2))
        full = cm[:, :, :, None, :] * eye[None, :, None, :, None]
        return full.reshape(n_slab, gs * n, gs * c).astype(BF16)

    return bmat(bb_re), bmat(bb_im), cmat(c_re), cmat(c_im)


def _s5_kernel(proj_hbm, bre_ref, bim_ref, cre_ref, cim_ref, lre_ref, lim_ref, d_ref, wglu_ref, ya_hbm,
               ubuf, yabuf, xre_ref, xim_ref, pre_ref, pim_ref, car_re_ref, car_im_ref, isem, osem, *, m, cw, tn):
    t = pl.program_id(1)
    n_t = pl.num_programs(1)
    step_id = pl.program_id(0) * n_t + t
    n_steps = pl.num_programs(0) * n_t
    slot = lax.rem(step_id, 2)
    rows = SUBLANES * m
    width = d_ref.shape[1]
    n_state = lre_ref.shape[1]
    n_slab = bre_ref.shape[0]
    slab_states = n_state // n_slab

    def in_copy(step, s, k):
        r0 = pl.multiple_of(step * rows + k * m, SUBLANES)
        return pltpu.make_async_copy(proj_hbm.at[pl.ds(r0, m), pl.ds(0, width)], ubuf.at[s, :, k, :], isem.at[s])

    def out_copy(step, s, k):
        r0 = pl.multiple_of(step * rows + k * m, SUBLANES)
        return pltpu.make_async_copy(yabuf.at[s, :, k, :], ya_hbm.at[pl.ds(r0, m)], osem.at[s])

    @pl.when(step_id == 0)
    def _():
        for k in range(SUBLANES):
            in_copy(0, 0, k).start()

    @pl.when(step_id + 1 < n_steps)
    def _():
        for k in range(SUBLANES):
            in_copy(step_id + 1, 1 - slot, k).start()

    @pl.when(t == 0)
    def _():
        car_re_ref[...] = jnp.zeros_like(car_re_ref)
        car_im_ref[...] = jnp.zeros_like(car_im_ref)
        lr, li = lre_ref[...], lim_ref[...]

        def pw(_, carry):
            pr, pi = carry
            return pr * lr - pi * li, pr * li + pi * lr

        pr, pi = lax.fori_loop(1, m, pw, (lr, li))
        pre_ref[...] = pr
        pim_ref[...] = pi

    for k in range(SUBLANES):
        in_copy(step_id, slot, k).wait()
    u = ubuf[slot].reshape(rows, width)
    up = u.astype(BF16)
    for s in range(n_slab):
        us = up[:, s * S5_SLAB_CH:(s + 1) * S5_SLAB_CH]
        cols = slice(s * slab_states, (s + 1) * slab_states)
        xre_ref[:, cols] = jnp.dot(us, bre_ref[s], preferred_element_type=F32)
        xim_ref[:, cols] = jnp.dot(us, bim_ref[s], preferred_element_type=F32)

    row_id = lax.broadcasted_iota(jnp.int32, (SUBLANES, cw), 0)
    for cb in range(n_state // cw):
        cols = slice(cb * cw, (cb + 1) * cw)
        lr = jnp.broadcast_to(lre_ref[:, cols], (SUBLANES, cw))
        li = jnp.broadcast_to(lim_ref[:, cols], (SUBLANES, cw))

        def step(tau, carry, store, cols=cols, lr=lr, li=li):
            sr, si = carry
            r0 = pl.multiple_of(tau * SUBLANES, SUBLANES)
            nr = sr * lr - si * li + xre_ref[pl.ds(r0, SUBLANES), cols]
            ni = sr * li + si * lr + xim_ref[pl.ds(r0, SUBLANES), cols]
            if store:
                xre_ref[pl.ds(r0, SUBLANES), cols] = nr
                xim_ref[pl.ds(r0, SUBLANES), cols] = ni
            return nr, ni

        zero = jnp.zeros((SUBLANES, cw), F32)
        er, ei = lax.fori_loop(0, m, functools.partial(step, store=False), (zero, zero), unroll=True)

        lmr = jnp.broadcast_to(pre_ref[:, cols], (SUBLANES, cw))
        lmi = jnp.broadcast_to(pim_ref[:, cols], (SUBLANES, cw))
        ir = jnp.where(row_id == 0, jnp.broadcast_to(car_re_ref[:, cols], (SUBLANES, cw)), 0.0)
        ii = jnp.where(row_id == 0, jnp.broadcast_to(car_im_ref[:, cols], (SUBLANES, cw)), 0.0)
        for k in range(1, SUBLANES):
            nr = ir * lmr - ii * lmi + er
            ni = ir * lmi + ii * lmr + ei
            ir = jnp.where(row_id == k, pltpu.roll(nr, 1, axis=0), ir)
            ii = jnp.where(row_id == k, pltpu.roll(ni, 1, axis=0), ii)

        fr, fi = lax.fori_loop(0, m, functools.partial(step, store=True), (ir, ii), unroll=True)
        car_re_ref[:, cols] = fr[SUBLANES - 1:SUBLANES, :]
        car_im_ref[:, cols] = fi[SUBLANES - 1:SUBLANES, :]

    ys = []
    for s in range(n_slab):
        cols = slice(s * slab_states, (s + 1) * slab_states)
        ys.append(jnp.dot(xre_ref[:, cols].astype(BF16), cre_ref[s], preferred_element_type=F32)
                  - jnp.dot(xim_ref[:, cols].astype(BF16), cim_ref[s], preferred_element_type=F32))
    y = jnp.concatenate(ys, axis=1) + d_ref[...] * u
    z = _gelu_tanh(y).astype(BF16)

    @pl.when(step_id >= 2)
    def _():
        for k in range(SUBLANES):
            out_copy(step_id - 2, slot, k).wait()

    nh = wglu_ref.shape[1] // 2
    for j in range(nh // tn):
        val = jnp.dot(z, wglu_ref[:, j * tn:(j + 1) * tn], preferred_element_type=F32)
        gt = jnp.dot(z, wglu_ref[:, nh + j * tn:nh + (j + 1) * tn], preferred_element_type=F32)
        yabuf[slot, :, :, j * tn:(j + 1) * tn] = (val * _sigmoid(gt)).reshape(m, SUBLANES, tn)
    for k in range(SUBLANES):
        out_copy(step_id, slot, k).start()

    @pl.when(step_id == n_steps - 1)
    def _():
        for k in range(SUBLANES):
            out_copy(step_id, slot, k).wait()
        if ya_hbm.shape[0] // rows >= 2:
            for k in range(SUBLANES):
                out_copy(step_id - 1, 1 - slot, k).wait()


def s5_branch(proj, bre, bim, cre, cim, lre, lim, d_skip, w_glu, *, batch, seq, width, m, cw, tn):
    rows = SUBLANES * m
    n_t = seq // rows
    n_state = lre.shape[1]
    nh = w_glu.shape[1] // 2
    once = pl.Buffered(1)
    const3 = lambda b, t: (0, 0, 0)
    const2 = lambda b, t: (0, 0)
    return pl.pallas_call(
        functools.partial(_s5_kernel, m=m, cw=cw, tn=tn),
        grid=(batch, n_t),
        in_specs=[pl.BlockSpec(memory_space=pl.ANY),
                  pl.BlockSpec(bre.shape, const3, pipeline_mode=once),
                  pl.BlockSpec(bim.shape, const3, pipeline_mode=once),
                  pl.BlockSpec(cre.shape, const3, pipeline_mode=once),
                  pl.BlockSpec(cim.shape, const3, pipeline_mode=once),
                  pl.BlockSpec((1, n_state), const2),
                  pl.BlockSpec((1, n_state), const2),
                  pl.BlockSpec((1, width), const2),
                  pl.BlockSpec(w_glu.shape, const2, pipeline_mode=once)],
        out_specs=pl.BlockSpec(memory_space=pl.ANY),
        out_shape=jax.ShapeDtypeStruct((batch * seq, nh), F32),
        scratch_shapes=[pltpu.VMEM((2, m, SUBLANES, width), F32), pltpu.VMEM((2, m, SUBLANES, nh), F32),
                        pltpu.VMEM((rows, n_state), F32), pltpu.VMEM((rows, n_state), F32),
                        pltpu.VMEM((1, n_state), F32), pltpu.VMEM((1, n_state), F32),
                        pltpu.VMEM((1, n_state), F32), pltpu.VMEM((1, n_state), F32),
                        pltpu.SemaphoreType.DMA((2,)), pltpu.SemaphoreType.DMA((2,))],
        compiler_params=_cparams(("arbitrary", "arbitrary")),
        name="s5_branch",
    )(proj, bre, bim, cre, cim, lre, lim, d_skip.reshape(1, width), w_glu)


def _conv_kernel(a_ref, b_ref, ah_ref, bh_ref, dww_ref, dwb_ref, lng_ref, lnb_ref, w_ref, o_ref,
                 zs_ref, wb_ref, yc_ref, *, tm, taps):
    i = pl.program_id(1)
    off0 = CONV_HALO - (taps - 1)
    width = a_ref.shape[1]
    n_slab = width // LANES
    zh = jnp.where(i == 0, 0.0, ah_ref[...] * _sigmoid(bh_ref[...]))
    z = a_ref[...] * _sigmoid(b_ref[...])
    for s in range(n_slab):
        zs_ref[s, 0:CONV_HALO, :] = zh[:, s * LANES:(s + 1) * LANES]
        zs_ref[s, CONV_HALO:, :] = z[:, s * LANES:(s + 1) * LANES]

    @pl.when(i == 0)
    def _():
        for j in range(taps):
            wb_ref[j] = jnp.broadcast_to(dww_ref[j:j + 1, :], (SUBLANES, width))
        wb_ref[taps] = jnp.broadcast_to(dwb_ref[...], (SUBLANES, width))

    rg = CONV_ROW_CHUNK // SUBLANES

    def chunk(c, carry):
        r0 = pl.multiple_of(c * CONV_ROW_CHUNK, CONV_ROW_CHUNK)
        for s in range(n_slab):
            lanes = slice(s * LANES, (s + 1) * LANES)
            bias = wb_ref[taps, :, lanes]
            acc = [bias] * rg
            for j in range(taps):
                w = wb_ref[j, :, lanes]
                for g in range(rg):
                    acc[g] = acc[g] + w * zs_ref[s, pl.ds(r0 + g * SUBLANES + off0 + j, SUBLANES), :]
            for g in range(rg):
                yc_ref[pl.ds(pl.multiple_of(r0 + g * SUBLANES, SUBLANES), SUBLANES), lanes] = acc[g]
        return carry

    lax.fori_loop(0, tm // CONV_ROW_CHUNK, chunk, 0)
    y = yc_ref[...]
    mu = jnp.mean(y, axis=-1, keepdims=True)
    var = jnp.mean(jnp.square(y - mu), axis=-1, keepdims=True)
    y = (y - mu) * lax.rsqrt(var + LN_EPS) * lng_ref[...] + lnb_ref[...]
    y = y * _sigmoid(y)
    o_ref[...] = jnp.dot(y.astype(BF16), w_ref[...], preferred_element_type=F32)


def conv_branch(proj, dw_w, dw_b, ln_g, ln_b, w_out, *, batch, seq, width, col0, tm):
    n_t = seq // tm
    ca, cb = col0 // width, col0 // width + 1
    hb = tm // CONV_HALO
    taps = dw_w.shape[0]
    d_out = w_out.shape[1]
    const2 = lambda b, i: (0, 0)
    halo = lambda b, i: jnp.maximum((b * n_t + i) * hb - 1, 0)
    return pl.pallas_call(
        functools.partial(_conv_kernel, tm=tm, taps=taps),
        grid=(batch, n_t),
        in_specs=[pl.BlockSpec((tm, width), lambda b, i: (b * n_t + i, ca)),
                  pl.BlockSpec((tm, width), lambda b, i: (b * n_t + i, cb)),
                  pl.BlockSpec((CONV_HALO, width), lambda b, i: (halo(b, i), ca)),
                  pl.BlockSpec((CONV_HALO, width), lambda b, i: (halo(b, i), cb)),
                  pl.BlockSpec((taps, width), const2),
                  pl.BlockSpec((1, width), const2),
                  pl.BlockSpec((1, width), const2),
                  pl.BlockSpec((1, width), const2),
                  pl.BlockSpec((width, d_out), const2)],
        out_specs=pl.BlockSpec((tm, d_out), lambda b, i: (b * n_t + i, 0)),
        out_shape=jax.ShapeDtypeStruct((batch * seq, d_out), F32),
        scratch_shapes=[pltpu.VMEM((width // LANES, tm + CONV_HALO, LANES), F32),
                        pltpu.VMEM((taps + 1, SUBLANES, width), F32),
                        pltpu.VMEM((tm, width), F32)],
        compiler_params=_cparams(("parallel", "arbitrary")),
        name="conv_branch",
    )(proj, proj, proj, proj, dw_w, dw_b.reshape(1, width), ln_g.reshape(1, width),
      ln_b.reshape(1, width), w_out)


def _combine_kernel(*refs, parts):
    x_ref, g_ref = refs[0], refs[1]
    ya_ref, yb_ref, wo_ref, o_ref = refs[2 + 2 * parts], refs[3 + 2 * parts], refs[-2], refs[-1]
    gw = ya_ref.shape[1] // parts
    x = x_ref[...]
    h = _rms(x, g_ref[...]).astype(BF16)
    o_ref[...] = x
    for p in range(parts):
        cols = slice(p * gw, (p + 1) * gw)
        ga = _sigmoid(jnp.dot(h, refs[2 + p][...], preferred_element_type=F32))
        gb = _sigmoid(jnp.dot(h, refs[2 + parts + p][...], preferred_element_type=F32))
        mix = ga * ya_ref[:, cols] + gb * yb_ref[:, cols]
        o_ref[...] += jnp.dot(mix.astype(BF16), wo_ref[cols, :], preferred_element_type=F32)


def combine(x, g, w_in, y_a, y_b, w_out, *, gate_col0, tm):
    m, d = x.shape
    gw = math.gcd(gate_col0, d)
    parts = d // gw
    row = lambda i: (i, 0)
    once = pl.Buffered(1)
    gate_specs = [pl.BlockSpec((d, gw), functools.partial(lambda i, c: (0, c), c=(gate_col0 + s * d) // gw + p),
                               pipeline_mode=once) for s in range(2) for p in range(parts)]
    return pl.pallas_call(
        functools.partial(_combine_kernel, parts=parts),
        grid=(m // tm,),
        in_specs=[pl.BlockSpec((tm, d), row), pl.BlockSpec((1, d), lambda i: (0, 0))] + gate_specs +
                 [pl.BlockSpec((tm, d), row),
                  pl.BlockSpec((tm, d), row),
                  pl.BlockSpec((d, d), lambda i: (0, 0), pipeline_mode=once)],
        out_specs=pl.BlockSpec((tm, d), row),
        out_shape=jax.ShapeDtypeStruct((m, d), F32),
        compiler_params=_cparams(("parallel",)),
        name="gated_combine",
    )(x, g.reshape(1, d), *([w_in] * (2 * parts)), y_a, y_b, w_out)


def _xattn_kernel(x_ref, q_ref, kv_ref, wo_ref, o_ref, *, heads, scale):
    d = x_ref.shape[1]
    hd = d // heads
    acc = x_ref[...]
    for h in range(heads):
        q = q_ref[:, h * hd:(h + 1) * hd]
        k = kv_ref[:, h * hd:(h + 1) * hd]
        v = kv_ref[:, d + h * hd:d + (h + 1) * hd]
        s = lax.dot_general(q, k, (((1,), (1,)), ((), ())), preferred_element_type=F32) * scale
        s = s - jnp.max(s, axis=-1, keepdims=True)
        p = jnp.exp(s)
        p = p / jnp.sum(p, axis=-1, keepdims=True)
        o = jnp.dot(p.astype(BF16), v, preferred_element_type=F32)
        acc = acc + jnp.dot(o.astype(BF16), wo_ref[h * hd:(h + 1) * hd, :], preferred_element_type=F32)
    o_ref[...] = acc


def xattn(x, q, kv, wo, *, batch, seq, n_mem, heads, tm):
    d = x.shape[1]
    n_t = seq // tm
    scale = 1.0 / math.sqrt(d // heads)
    return pl.pallas_call(
        functools.partial(_xattn_kernel, heads=heads, scale=scale),
        grid=(batch, n_t),
        in_specs=[pl.BlockSpec((tm, d), lambda b, i: (b * n_t + i, 0)),
                  pl.BlockSpec((tm, d), lambda b, i: (b * n_t + i, 0)),
                  pl.BlockSpec((n_mem, 2 * d), lambda b, i: (b, 0)),
                  pl.BlockSpec((d, d), lambda b, i: (0, 0))],
        out_specs=pl.BlockSpec((tm, d), lambda b, i: (b * n_t + i, 0)),
        out_shape=jax.ShapeDtypeStruct((batch * seq, d), F32),
        compiler_params=_cparams(("parallel", "parallel")),
        name="xattn",
    )(x, q, kv, wo)


def _router_kernel(x_ref, g_ref, w_ref, b_ref, tri_ref, h_ref, info_ref, infot_ref, cnt_ref, run_ref, *,
                   n_groups, epg):
    i = pl.program_id(0)
    tm = x_ref.shape[0]

    @pl.when(i == 0)
    def _():
        run_ref[...] = jnp.zeros_like(run_ref)

    h = _rms(x_ref[...], g_ref[...])
    h_ref[...] = h
    h_hi = h.astype(BF16)
    h_lo = (h - h_hi.astype(F32)).astype(BF16)
    p = jnp.dot(h_hi, w_ref[...], preferred_element_type=F32)
    logits = (p[:, :LANES] + p[:, LANES:] + jnp.dot(h_lo, w_ref[:, :LANES], preferred_element_type=F32)
              + b_ref[...])
    lane = lax.broadcasted_iota(jnp.int32, (tm, LANES), 1).astype(F32)
    neg = jnp.float32(-jnp.inf)
    big = jnp.float32(LANES)
    is_grp = lane < n_groups
    cl = jnp.where(is_grp, logits, neg)
    cmax = jnp.max(cl, axis=-1, keepdims=True)
    g_idx = jnp.min(jnp.where(cl == cmax, lane, big), axis=-1, keepdims=True)
    p_sel = 1.0 / jnp.sum(jnp.where(is_grp, jnp.exp(cl - cmax), 0.0), axis=-1, keepdims=True)
    lo = n_groups + g_idx * epg
    in_grp = (lane >= lo) & (lane < lo + epg)
    fl = jnp.where(in_grp, logits, neg)
    v1 = jnp.max(fl, axis=-1, keepdims=True)
    i1 = jnp.min(jnp.where(fl == v1, lane, big), axis=-1, keepdims=True)
    fl2 = jnp.where(lane == i1, neg, fl)
    v2 = jnp.max(fl2, axis=-1, keepdims=True)
    i2 = jnp.min(jnp.where(fl2 == v2, lane, big), axis=-1, keepdims=True)
    e2x = jnp.exp(v2 - v1)
    w1 = p_sel / (1.0 + e2x)
    w2 = p_sel * e2x / (1.0 + e2x)
    e1 = i1 - n_groups
    e2 = i2 - n_groups
    oh1 = lane == e1
    oh2 = lane == e2
    both = jnp.where(oh1 | oh2, 1.0, 0.0)
    before = jnp.dot(tri_ref[...], both.astype(BF16), preferred_element_type=F32) + run_ref[...]
    r1 = jnp.sum(jnp.where(oh1, before, 0.0), axis=-1, keepdims=True)
    r2 = jnp.sum(jnp.where(oh2, before, 0.0), axis=-1, keepdims=True)
    run_ref[...] += jnp.sum(both, axis=0, keepdims=True)
    cnt_ref[...] = run_ref[...]
    info = jnp.where(lane == 0, e1, 0.0)
    info = jnp.where(lane == 1, e2, info)
    info = jnp.where(lane == 2, r1, info)
    info = jnp.where(lane == 3, r2, info)
    info = jnp.where(lane == 4, w1, info)
    info = jnp.where(lane == 5, w2, info)
    info_ref[...] = info
    infot_ref[...] = jnp.transpose(info)[0:SUBLANES, :]


def router(x, g, w_pad, b_pad, *, n_groups, epg, tm):
    m, d = x.shape
    tri = (jnp.arange(tm)[:, None] > jnp.arange(tm)[None, :]).astype(BF16)
    return pl.pallas_call(
        functools.partial(_router_kernel, n_groups=n_groups, epg=epg),
        grid=(m // tm,),
        in_specs=[pl.BlockSpec((tm, d), lambda i: (i, 0)),
                  pl.BlockSpec((1, d), lambda i: (0, 0)),
                  pl.BlockSpec((d, 2 * LANES), lambda i: (0, 0)),
                  pl.BlockSpec((1, LANES), lambda i: (0, 0)),
                  pl.BlockSpec((tm, tm), lambda i: (0, 0))],
        out_specs=[pl.BlockSpec((tm, d), lambda i: (i, 0)),
                   pl.BlockSpec((tm, LANES), lambda i: (i, 0)),
                   pl.BlockSpec((SUBLANES, tm), lambda i: (0, i)),
                   pl.BlockSpec((1, LANES), lambda i: (0, 0))],
        out_shape=(jax.ShapeDtypeStruct((m, d), F32),
                   jax.ShapeDtypeStruct((m, LANES), F32),
                   jax.ShapeDtypeStruct((SUBLANES, m), F32),
                   jax.ShapeDtypeStruct((1, LANES), F32)),
        scratch_shapes=[pltpu.VMEM((1, LANES), F32)],
        compiler_params=_cparams(("arbitrary",)),
        name="moe_router",
    )(x, g.reshape(1, d), w_pad, b_pad, tri)


def moe_dispatch_plan(infot, counts, n_tokens, n_experts):
    e = infot[0:2].astype(jnp.int32)
    rank = infot[2:4].astype(jnp.int32)
    counts = counts[0, :n_experts].astype(jnp.int32)
    pcounts = ((counts + MOE_BLOCK - 1) // MOE_BLOCK) * MOE_BLOCK
    pends = jnp.cumsum(pcounts)
    pstarts = pends - pcounts
    ids = jnp.arange(n_experts, dtype=jnp.int32)
    dest = rank + jnp.sum(jnp.where(e[:, :, None] == ids, pstarts, 0), axis=-1)
    n_rows = (-(-2 * n_tokens // MOE_BLOCK) + n_experts) * MOE_BLOCK
    n_used = (pends[-1] // MOE_BLOCK).reshape(1)
    return dest, n_rows, n_used, pcounts, pends


def _dispatch_kernel(pc_ref, pe_ref, dest_ref, h_ref, xs_hbm, zbuf, sem, tsem, *, n_experts):
    i = pl.program_id(0)
    tm = h_ref.shape[0]

    @pl.when(i == 0)
    def _():
        zbuf[...] = jnp.zeros_like(zbuf)
        for wait in (False, True):
            for e in range(n_experts):
                @pl.when(pc_ref[e] > 0)
                def _():
                    row0 = pl.multiple_of(pe_ref[e] - MOE_BLOCK, MOE_BLOCK)
                    fill = pltpu.make_async_copy(zbuf, xs_hbm.at[pl.ds(row0, MOE_BLOCK)], sem)
                    fill.wait() if wait else fill.start()

    n_blocks = xs_hbm.shape[0] // MOE_BLOCK
    n_used = pe_ref[n_experts - 1] // MOE_BLOCK

    def tail_fill(b, wait):
        row0 = pl.multiple_of(b * MOE_BLOCK, MOE_BLOCK)
        fill = pltpu.make_async_copy(zbuf, xs_hbm.at[pl.ds(row0, MOE_BLOCK)], tsem)
        fill.wait() if wait else fill.start()

    @pl.when(i == 0)
    def _():
        lax.fori_loop(n_used, n_blocks, lambda b, c: (tail_fill(b, False), c)[1], 0)

    for k in range(2):
        for r in range(tm):
            pltpu.make_async_copy(h_ref.at[pl.ds(r, 1)], xs_hbm.at[pl.ds(dest_ref[k, r], 1)],
                                  sem).start(priority=r % 2)
    for k in range(2):
        pltpu.make_async_copy(h_ref, xs_hbm.at[pl.ds(0, tm)], sem).wait()

    @pl.when(i == pl.num_programs(0) - 1)
    def _():
        lax.fori_loop(n_used, n_blocks, lambda b, c: (tail_fill(b, True), c)[1], 0)


def moe_dispatch(h, dest, pcounts, pends, *, n_rows, tm):
    t, d = h.shape
    grid_spec = pltpu.PrefetchScalarGridSpec(
        num_scalar_prefetch=2,
        grid=(t // tm,),
        in_specs=[pl.BlockSpec((2, tm), lambda i, pc, pe: (0, i), memory_space=pltpu.SMEM),
                  pl.BlockSpec((tm, d), lambda i, pc, pe: (i, 0))],
        out_specs=pl.BlockSpec(memory_space=pl.ANY),
        scratch_shapes=[pltpu.VMEM((MOE_BLOCK, d), F32), pltpu.SemaphoreType.DMA(()),
                        pltpu.SemaphoreType.DMA(())],
    )
    return pl.pallas_call(
        functools.partial(_dispatch_kernel, n_experts=pcounts.shape[0]),
        grid_spec=grid_spec,
        out_shape=jax.ShapeDtypeStruct((n_rows, d), F32),
        compiler_params=_cparams(("arbitrary",)),
        name="moe_dispatch",
    )(pcounts, pends, dest, h)


def _experts_kernel(first_ref, cnt_ref, nb_ref, xs_hbm, wg_hbm, wu_hbm, wd_hbm, ys_hbm,
                    xbuf, ybuf, wgf, wuf, wdf, wgb, wub, wdb, xsem, ysem, wsem):
    e = pl.program_id(0)
    n_e = pl.num_programs(0)
    nb = nb_ref[0]
    first, cnt = first_ref[e], cnt_ref[e]
    n_blocks = xs_hbm.shape[0] // MOE_BLOCK
    wslot = lax.rem(e, 2)

    def rows(g):
        return pl.ds(pl.multiple_of(g * MOE_BLOCK, MOE_BLOCK), MOE_BLOCK)

    def x_copy(g, s):
        return pltpu.make_async_copy(xs_hbm.at[rows(g)], xbuf.at[s], xsem.at[s])

    def y_copy(g, s):
        return pltpu.make_async_copy(ybuf.at[s], ys_hbm.at[rows(g)], ysem.at[s])

    def w_copies(ex, s):
        half = wd_hbm.shape[1] // 2
        lo, hi = pl.ds(0, half), pl.ds(half, half)
        return ((pltpu.make_async_copy(wg_hbm.at[ex], wgf.at[s], wsem.at[s]), 0),
                (pltpu.make_async_copy(wu_hbm.at[ex], wuf.at[s], wsem.at[s]), 1),
                (pltpu.make_async_copy(wd_hbm.at[ex, lo], wdf.at[s, lo], wsem.at[s]), 0),
                (pltpu.make_async_copy(wd_hbm.at[ex, hi], wdf.at[s, hi], wsem.at[s]), 1))

    @pl.when(e == 0)
    def _():
        for g0 in range(EXPERT_RING - 1):
            @pl.when(g0 < nb)
            def _():
                x_copy(g0, g0).start(priority=BLOCK_DMA_QUEUE)

    @pl.when((e == 0) & (cnt > 0))
    def _():
        for cp, queue in w_copies(0, 0):
            cp.start(priority=queue)

    e_next = jnp.minimum(e + 1, n_e - 1)

    @pl.when((e + 1 < n_e) & (cnt_ref[e_next] > 0))
    def _():
        for cp, queue in w_copies(e_next, 1 - wslot):
            cp.start(priority=queue)

    @pl.when(cnt > 0)
    def _():
        for cp, _ in w_copies(e, wslot):
            cp.wait()
        wgb[...] = wgf[wslot].astype(BF16)
        wub[...] = wuf[wslot].astype(BF16)
        wdb[...] = wdf[wslot].astype(BF16)

    def block(b, carry):
        g = first + b
        s = lax.rem(g, EXPERT_RING)

        @pl.when(g + EXPERT_RING - 1 < nb)
        def _():
            x_copy(g + EXPERT_RING - 1, lax.rem(g + EXPERT_RING - 1, EXPERT_RING)).start(priority=BLOCK_DMA_QUEUE)

        x_copy(g, s).wait()
        x = xbuf[s].astype(BF16)
        gte = jnp.dot(x, wgb[...], preferred_element_type=F32)
        up = jnp.dot(x, wub[...], preferred_element_type=F32)
        act = (gte * _sigmoid(gte) * up).astype(BF16)
        y = jnp.dot(act, wdb[...], preferred_element_type=F32)

        @pl.when(g >= EXPERT_RING)
        def _():
            y_copy(g - EXPERT_RING, s).wait()

        ybuf[s] = y
        y_copy(g, s).start()
        return carry

    lax.fori_loop(0, cnt, block, 0)

    @pl.when(e == pl.num_programs(0) - 1)
    def _():
        for back in range(1, EXPERT_RING + 1):
            @pl.when(nb >= back)
            def _():
                y_copy(nb - back, lax.rem(nb - back, EXPERT_RING)).wait()

        ybuf[0] = jnp.zeros(ybuf.shape[1:], F32)
        lax.fori_loop(nb, n_blocks, lambda g, c: (y_copy(g, 0).start(), c)[1], 0)
        lax.fori_loop(nb, n_blocks, lambda g, c: (y_copy(g, 0).wait(), c)[1], 0)


def experts(xs, w_gate, w_up, w_down, first_blk, n_blk, n_used):
    n_rows, d = xs.shape
    n_experts, _, de = w_gate.shape
    grid_spec = pltpu.PrefetchScalarGridSpec(
        num_scalar_prefetch=3,
        grid=(n_experts,),
        in_specs=[pl.BlockSpec(memory_space=pl.ANY)] * 4,
        out_specs=pl.BlockSpec(memory_space=pl.ANY),
        scratch_shapes=[pltpu.VMEM((EXPERT_RING, MOE_BLOCK, d), F32), pltpu.VMEM((EXPERT_RING, MOE_BLOCK, d), F32),
                        pltpu.VMEM((2, d, de), F32), pltpu.VMEM((2, d, de), F32), pltpu.VMEM((2, de, d), F32),
                        pltpu.VMEM((d, de), BF16), pltpu.VMEM((d, de), BF16), pltpu.VMEM((de, d), BF16),
                        pltpu.SemaphoreType.DMA((EXPERT_RING,)), pltpu.SemaphoreType.DMA((EXPERT_RING,)),
                        pltpu.SemaphoreType.DMA((2,))],
    )
    return pl.pallas_call(
        _experts_kernel,
        grid_spec=grid_spec,
        out_shape=jax.ShapeDtypeStruct((n_rows, d), F32),
        compiler_params=_cparams(("arbitrary",)),
        name="moe_experts",
    )(first_blk, n_blk, n_used, xs, w_gate, w_up, w_down)


def _moe_combine_kernel(dest_ref, destn_ref, x_ref, info_ref, g_ref, ys_hbm, o_ref, ybuf, sem, *, norm):
    i = pl.program_id(0)
    n = pl.num_programs(0)
    tm = x_ref.shape[0]
    slot = lax.rem(i, 2)

    def gather_start(dref, s):
        for k in range(2):
            for r in range(tm):
                pltpu.make_async_copy(ys_hbm.at[pl.ds(dref[k, r], 1)], ybuf.at[s, k, pl.ds(r, 1)],
                                      sem.at[s]).start(priority=r % 2)

    @pl.when(i == 0)
    def _():
        gather_start(dest_ref, 0)

    @pl.when(i + 1 < n)
    def _():
        gather_start(destn_ref, 1 - slot)

    for k in range(2):
        pltpu.make_async_copy(ys_hbm.at[pl.ds(0, tm)], ybuf.at[slot, k], sem.at[slot]).wait()
    w1 = info_ref[:, 4:5]
    w2 = info_ref[:, 5:6]
    s = x_ref[...] + (w1 * ybuf[slot, 0] + w2 * ybuf[slot, 1])
    o_ref[...] = _rms(s, g_ref[...]) if norm else s


def moe_combine(x, ys, dest, info, g, *, norm, tm):
    t, d = x.shape
    n = t // tm
    return pl.pallas_call(
        functools.partial(_moe_combine_kernel, norm=norm),
        grid=(n,),
        in_specs=[pl.BlockSpec((2, tm), lambda i: (0, i), memory_space=pltpu.SMEM),
                  pl.BlockSpec((2, tm), lambda i: (0, jnp.minimum(i + 1, n - 1)), memory_space=pltpu.SMEM),
                  pl.BlockSpec((tm, d), lambda i: (i, 0)),
                  pl.BlockSpec((tm, LANES), lambda i: (i, 0)),
                  pl.BlockSpec((1, d), lambda i: (0, 0)),
                  pl.BlockSpec(memory_space=pl.ANY)],
        out_specs=pl.BlockSpec((tm, d), lambda i: (i, 0)),
        out_shape=jax.ShapeDtypeStruct((t, d), F32),
        scratch_shapes=[pltpu.VMEM((2, 2, tm, d), F32), pltpu.SemaphoreType.DMA((2,))],
        compiler_params=_cparams(("arbitrary",)),
        name="moe_combine",
    )(dest, dest, x, info, g.reshape(1, d), ys)


def forward(x, mem, norm_mix_g, w_in, s5_lambda_re, s5_lambda_im, s5_log_dt, s5_b_re, s5_b_im, s5_c_re,
            s5_c_im, s5_d, s5_w_glu, conv_dw_w, conv_dw_b, conv_ln_g, conv_ln_b, conv_w_out, w_out,
            norm_xattn_g, norm_mem_g, xattn_wq, xattn_wk, xattn_wv, xattn_wo, norm_moe_g, router_w_group,
            router_b_group, router_w_expert, router_b_expert, exp_w_gate, exp_w_up, exp_w_down, norm_final_g,
            *, tiles):
    batch, seq, d = x.shape
    depth = w_in.shape[0]
    n_mem = mem.shape[1]
    s5_width = s5_d.shape[1]
    conv_width = conv_dw_b.shape[1]
    n_groups, epg = router_w_expert.shape[1], router_w_expert.shape[3]
    t = batch * seq
    xf = x.reshape(t, d)
    memf = mem.reshape(batch * n_mem, d)
    gate_col0 = s5_width + 2 * conv_width
    for l in range(depth):
        w_in_b = w_in[l].astype(BF16)
        proj = norm_mm_rows(xf, norm_mix_g[l], w_in_b[:, :gate_col0], tm=tiles["proj_tm"], tn=tiles["proj_tn"],
                            out_dtype=F32)
        lbr, lbi, bbr, bbi = s5_discretise(s5_lambda_re[l], s5_lambda_im[l], s5_log_dt[l],
                                           s5_b_re[l], s5_b_im[l])
        bre, bim, cre, cim = s5_block_diag(bbr, bbi, s5_c_re[l], s5_c_im[l])
        y_a = s5_branch(proj, bre, bim, cre, cim, lbr.reshape(1, -1), lbi.reshape(1, -1), s5_d[l],
                        s5_w_glu[l].astype(BF16), batch=batch, seq=seq, width=s5_width, m=tiles["s5_m"],
                        cw=tiles["s5_cw"], tn=tiles["glu_tn"])
        y_b = conv_branch(proj, conv_dw_w[l], conv_dw_b[l], conv_ln_g[l], conv_ln_b[l],
                          conv_w_out[l].astype(BF16), batch=batch, seq=seq, width=conv_width,
                          col0=s5_width, tm=tiles["conv_tm"])
        xf = combine(xf, norm_mix_g[l], w_in_b, y_a, y_b, w_out[l].astype(BF16), gate_col0=gate_col0,
                     tm=tiles["comb_tm"])

        q = norm_mm_rows(xf, norm_xattn_g[l], xattn_wq[l].astype(BF16), tm=tiles["q_tm"], tn=tiles["q_tn"],
                         out_dtype=BF16)
        wkv = jnp.concatenate([xattn_wk[l], xattn_wv[l]], axis=1).astype(BF16)
        kv = norm_mm(memf, norm_mem_g[l], wkv, tm=batch * n_mem, tn=tiles["kv_tn"], out_dtype=BF16)
        xf = xattn(xf, q, kv, xattn_wo[l].astype(BF16), batch=batch, seq=seq, n_mem=n_mem,
                   heads=XATTN_HEADS, tm=tiles["xattn_tm"])

        w_r = jnp.concatenate([router_w_group[l],
                               jnp.transpose(router_w_expert[l], (1, 0, 2)).reshape(d, n_groups * epg)], axis=1)
        b_r = jnp.concatenate([router_b_group[l], router_b_expert[l].reshape(-1)])
        n_r = w_r.shape[1]
        w_pad = jnp.pad(w_r, ((0, 0), (0, LANES - n_r)))
        w_hi = w_pad.astype(BF16)
        w_pad = jnp.concatenate([w_hi, (w_pad - w_hi.astype(F32)).astype(BF16)], axis=1)
        b_pad = jnp.pad(b_r, (0, LANES - n_r)).reshape(1, LANES)
        h, info, infot, counts = router(xf, norm_moe_g[l], w_pad, b_pad, n_groups=n_groups, epg=epg,
                                        tm=tiles["router_tm"])
        dest, n_rows, n_used, pcounts, pends = moe_dispatch_plan(infot, counts, t, n_groups * epg)
        xs = moe_dispatch(h, dest, pcounts, pends, n_rows=n_rows, tm=tiles["moe_tm"])
        ys = experts(xs, exp_w_gate[l], exp_w_up[l], exp_w_down[l], (pends - pcounts) // MOE_BLOCK,
                     pcounts // MOE_BLOCK, n_used)
        last = l + 1 == depth
        xf = moe_combine(xf, ys, dest, info, norm_final_g if last else jnp.ones((d,), F32), norm=last,
                         tm=tiles["moe_tm"])
    return xf.reshape(batch, seq, d)


def kernel(x, mem, norm_mix_g, w_in, s5_lambda_re, s5_lambda_im, s5_log_dt, s5_b_re, s5_b_im, s5_c_re, s5_c_im, s5_d, s5_w_glu, conv_dw_w, conv_dw_b, conv_ln_g, conv_ln_b, conv_w_out, w_out, norm_xattn_g, norm_mem_g, xattn_wq, xattn_wk, xattn_wv, xattn_wo, norm_moe_g, router_w_group, router_b_group, router_w_expert, router_b_expert, exp_w_gate, exp_w_up, exp_w_down, norm_final_g):
    return forward(x, mem, norm_mix_g, w_in, s5_lambda_re, s5_lambda_im, s5_log_dt, s5_b_re, s5_b_im,
                   s5_c_re, s5_c_im, s5_d, s5_w_glu, conv_dw_w, conv_dw_b, conv_ln_g, conv_ln_b, conv_w_out,
                   w_out, norm_xattn_g, norm_mem_g, xattn_wq, xattn_wk, xattn_wv, xattn_wo, norm_moe_g,
                   router_w_group, router_b_group, router_w_expert, router_b_expert, exp_w_gate, exp_w_up,
                   exp_w_down, norm_final_g, tiles=TILES)
```

```python
import functools
import math

import jax
import jax.numpy as jnp
from jax import lax
from jax.experimental import pallas as pl
from jax.experimental.pallas import tpu as pltpu

F32 = jnp.float32
BF16 = jnp.bfloat16

RMS_EPS = 1e-6
LN_EPS = 1e-5

S5_SLAB_CH = 256
CONV_HALO = 32
CONV_ROW_CHUNK = 64
XATTN_HEADS = 4
MOE_BLOCK = 128
BLOCK_DMA_QUEUE = 1
EXPERT_RING = 4
SUBLANES = 8
LANES = 128
VMEM_LIMIT = 56 * 1024 * 1024

TILES = dict(proj_tm=512, proj_tn=512, s5_m=64, s5_cw=512, glu_tn=512, conv_tm=256,
             comb_tm=256, q_tm=512, q_tn=512, kv_tn=512, xattn_tm=512, router_tm=512, moe_tm=256)


def _cparams(sem):
    return pltpu.CompilerParams(dimension_semantics=sem, vmem_limit_bytes=VMEM_LIMIT)


def _rms(x, g):
    return x * lax.rsqrt(jnp.mean(x * x, axis=-1, keepdims=True) + RMS_EPS) * g


def _sigmoid(x):
    return 0.5 * jnp.tanh(0.5 * x) + 0.5


def _gelu_tanh(x):
    c = math.sqrt(2.0 / math.pi)
    return 0.5 * x * (1.0 + jnp.tanh(c * (x + 0.044715 * (x * x * x))))


def _norm_mm_kernel(x_ref, g_ref, w_ref, o_ref, h_ref, *, act_from):
    j = pl.program_id(1)

    @pl.when(j == 0)
    def _():
        h_ref[...] = _rms(x_ref[...], g_ref[...]).astype(BF16)

    y = jnp.dot(h_ref[...], w_ref[...], preferred_element_type=F32)
    if act_from is None:
        o_ref[...] = y.astype(o_ref.dtype)
    else:
        @pl.when(j < act_from)
        def _():
            o_ref[...] = y.astype(o_ref.dtype)

        @pl.when(j >= act_from)
        def _():
            o_ref[...] = _sigmoid(y).astype(o_ref.dtype)


def norm_mm(x, g, w, *, tm, tn, out_dtype=F32, act_from=None):
    m, k = x.shape
    n = w.shape[1]
    return pl.pallas_call(
        functools.partial(_norm_mm_kernel, act_from=act_from),
        grid=(m // tm, n // tn),
        in_specs=[pl.BlockSpec((tm, k), lambda i, j: (i, 0)),
                  pl.BlockSpec((1, k), lambda i, j: (0, 0)),
                  pl.BlockSpec((k, tn), lambda i, j: (0, j))],
        out_specs=pl.BlockSpec((tm, tn), lambda i, j: (i, j)),
        out_shape=jax.ShapeDtypeStruct((m, n), out_dtype),
        scratch_shapes=[pltpu.VMEM((tm, k), BF16)],
        compiler_params=_cparams(("parallel", "arbitrary")),
        name="norm_mm",
    )(x, g.reshape(1, k), w)


def _norm_mm_rows_kernel(x_ref, g_ref, w_ref, o_ref, *, tn):
    h = _rms(x_ref[...], g_ref[...]).astype(BF16)
    for j in range(o_ref.shape[1] // tn):
        cols = slice(j * tn, (j + 1) * tn)
        o_ref[:, cols] = jnp.dot(h, w_ref[:, cols], preferred_element_type=F32).astype(o_ref.dtype)


def norm_mm_rows(x, g, w, *, tm, tn, out_dtype):
    m, k = x.shape
    n = w.shape[1]
    return pl.pallas_call(
        functools.partial(_norm_mm_rows_kernel, tn=tn),
        grid=(m // tm,),
        in_specs=[pl.BlockSpec((tm, k), lambda i: (i, 0)),
                  pl.BlockSpec((1, k), lambda i: (0, 0)),
                  pl.BlockSpec((k, n), lambda i: (0, 0), pipeline_mode=pl.Buffered(1))],
        out_specs=pl.BlockSpec((tm, n), lambda i: (i, 0)),
        out_shape=jax.ShapeDtypeStruct((m, n), out_dtype),
        compiler_params=_cparams(("parallel",)),
        name="norm_mm_rows",
    )(x, g.reshape(1, k), w)


def _s5_disc_kernel(lr_ref, li_ref, ldt_ref, br_ref, bi_ref, lbr_ref, lbi_ref, bbr_ref, bbi_ref):
    lr, li = lr_ref[...], li_ref[...]
    dt = jnp.exp(ldt_ref[...])
    mag = jnp.exp(lr * dt)
    ang = li * dt
    lb_re, lb_im = mag * jnp.cos(ang), mag * jnp.sin(ang)
    den = lr * lr + li * li
    nr, ni = lb_re - 1.0, lb_im
    coef_re = (nr * lr + ni * li) / den
    coef_im = (ni * lr - nr * li) / den
    lbr_ref[...] = lb_re
    lbi_ref[...] = lb_im
    br, bi = br_ref[...], bi_ref[...]
    bbr_ref[...] = coef_re[None] * br - coef_im[None] * bi
    bbi_ref[...] = coef_re[None] * bi + coef_im[None] * br


def s5_discretise(lam_re, lam_im, log_dt, b_re, b_im):
    g, n = lam_re.shape
    c = b_re.shape[-1]
    b_re_t = jnp.transpose(b_re, (2, 0, 1))
    b_im_t = jnp.transpose(b_im, (2, 0, 1))
    gn = jax.ShapeDtypeStruct((g, n), F32)
    cgn = jax.ShapeDtypeStruct((c, g, n), F32)
    return pl.pallas_call(
        _s5_disc_kernel,
        out_shape=(gn, gn, cgn, cgn),
        name="s5_discretise",
    )(lam_re, lam_im, log_dt.reshape(g, 1), b_re_t, b_im_t)


def s5_block_diag(bb_re, bb_im, c_re, c_im):
    c, g, n = bb_re.shape
    gs = S5_SLAB_CH // c
    n_slab = g // gs
    eye = jnp.eye(gs, dtype=F32)

    def bmat(bb):
        b = jnp.transpose(bb, (1, 0, 2)).reshape(n_slab, gs, c, n)
        full = b[:, :, :, None, :] * eye[None, :, None, :, None]
        return full.reshape(n_slab, gs * c, gs * n).astype(BF16)

    def cmat(cc):
        cm = jnp.transpose(cc.reshape(n_slab, gs, c, n), (0, 1, 3, 2))
        full = cm[:, :, :, None, :] * eye[None, :, None, :, None]
        return full.reshape(n_slab, gs * n, gs * c).astype(BF16)

    return bmat(bb_re), bmat(bb_im), cmat(c_re), cmat(c_im)


def _s5_kernel(proj_hbm, bre_ref, bim_ref, cre_ref, cim_ref, lre_ref, lim_ref, d_ref, wglu_ref, ya_hbm,
               ubuf, yabuf, zbuf, xre_ref, xim_ref, pre_ref, pim_ref, car_re_ref, car_im_ref, isem, osem,
               *, m, cw, tn):
    t = pl.program_id(1)
    n_t = pl.num_programs(1)
    step_id = pl.program_id(0) * n_t + t
    n_steps = pl.num_programs(0) * n_t
    slot = lax.rem(step_id, 2)
    rows = SUBLANES * m
    width = d_ref.shape[1]
    n_state = lre_ref.shape[1]
    n_slab = bre_ref.shape[0]
    slab_states = n_state // n_slab

    def in_copy(step, s, k):
        r0 = pl.multiple_of(step * rows + k * m, SUBLANES)
        return pltpu.make_async_copy(proj_hbm.at[pl.ds(r0, m), pl.ds(0, width)], ubuf.at[s, :, k, :], isem.at[s])

    def out_copy(step, s, k):
        r0 = pl.multiple_of(step * rows + k * m, SUBLANES)
        return pltpu.make_async_copy(yabuf.at[s, :, k, :], ya_hbm.at[pl.ds(r0, m)], osem.at[s])

    @pl.when(step_id == 0)
    def _():
        for k in range(SUBLANES):
            in_copy(0, 0, k).start()

    @pl.when(step_id + 1 < n_steps)
    def _():
        for k in range(SUBLANES):
            in_copy(step_id + 1, 1 - slot, k).start()

    @pl.when(t == 0)
    def _():
        car_re_ref[...] = jnp.zeros_like(car_re_ref)
        car_im_ref[...] = jnp.zeros_like(car_im_ref)
        lr, li = lre_ref[...], lim_ref[...]

        def pw(_, carry):
            pr, pi = carry
            return pr * lr - pi * li, pr * li + pi * lr

        pr, pi = lax.fori_loop(1, m, pw, (lr, li))
        pre_ref[...] = pr
        pim_ref[...] = pi

    @pl.when(step_id == 0)
    def _():
        zbuf[...] = jnp.zeros_like(zbuf)

    @pl.when(step_id >= 3)
    def _():
        for k in range(SUBLANES):
            out_copy(step_id - 3, 1 - slot, k).wait()

    nh = wglu_ref.shape[1] // 2

    def glu_pieces(z, s):
        def piece(j):
            val = jnp.dot(z, wglu_ref[:, j * tn:(j + 1) * tn], preferred_element_type=F32)
            gt = jnp.dot(z, wglu_ref[:, nh + j * tn:nh + (j + 1) * tn], preferred_element_type=F32)
            yabuf[s, :, :, j * tn:(j + 1) * tn] = (val * _sigmoid(gt)).reshape(m, SUBLANES, tn)
        return [functools.partial(piece, j) for j in range(nh // tn)]

    for k in range(SUBLANES):
        in_copy(step_id, slot, k).wait()
    u = ubuf[slot].reshape(rows, width)
    up = u.astype(BF16)

    def b_project(s):
        us = up[:, s * S5_SLAB_CH:(s + 1) * S5_SLAB_CH]
        cols = slice(s * slab_states, (s + 1) * slab_states)
        xre_ref[:, cols] = jnp.dot(us, bre_ref[s], preferred_element_type=F32)
        xim_ref[:, cols] = jnp.dot(us, bim_ref[s], preferred_element_type=F32)

    def c_project(s):
        cols = slice(s * slab_states, (s + 1) * slab_states)
        return (jnp.dot(xre_ref[:, cols].astype(BF16), cre_ref[s], preferred_element_type=F32)
                - jnp.dot(xim_ref[:, cols].astype(BF16), cim_ref[s], preferred_element_type=F32))

    n_cb = n_state // cw
    cb_per_slab = slab_states // cw
    glu_prev = glu_pieces(zbuf[...], 1 - slot)
    glu_at = {(i * n_cb) // len(glu_prev): [] for i in range(len(glu_prev))}
    for i, piece in enumerate(glu_prev):
        glu_at[(i * n_cb) // len(glu_prev)].append(piece)
    ys = []
    b_project(0)
    row_id = lax.broadcasted_iota(jnp.int32, (SUBLANES, cw), 0)
    for cb in range(n_cb):
        cols = slice(cb * cw, (cb + 1) * cw)
        slab = cb // cb_per_slab
        if cb % cb_per_slab == 0:
            if slab + 1 < n_slab:
                b_project(slab + 1)
            if slab >= 1:
                ys.append(c_project(slab - 1))
        for piece in glu_at.get(cb, []):
            piece()
        lr = jnp.broadcast_to(lre_ref[:, cols], (SUBLANES, cw))
        li = jnp.broadcast_to(lim_ref[:, cols], (SUBLANES, cw))

        def step(tau, carry, store, cols=cols, lr=lr, li=li):
            sr, si = carry
            r0 = pl.multiple_of(tau * SUBLANES, SUBLANES)
            nr = sr * lr - si * li + xre_ref[pl.ds(r0, SUBLANES), cols]
            ni = sr * li + si * lr + xim_ref[pl.ds(r0, SUBLANES), cols]
            if store:
                xre_ref[pl.ds(r0, SUBLANES), cols] = nr
                xim_ref[pl.ds(r0, SUBLANES), cols] = ni
            return nr, ni

        zero = jnp.zeros((SUBLANES, cw), F32)
        er, ei = lax.fori_loop(0, m, functools.partial(step, store=False), (zero, zero), unroll=True)

        lmr = jnp.broadcast_to(pre_ref[:, cols], (SUBLANES, cw))
        lmi = jnp.broadcast_to(pim_ref[:, cols], (SUBLANES, cw))
        ir = jnp.where(row_id == 0, jnp.broadcast_to(car_re_ref[:, cols], (SUBLANES, cw)), 0.0)
        ii = jnp.where(row_id == 0, jnp.broadcast_to(car_im_ref[:, cols], (SUBLANES, cw)), 0.0)
        for k in range(1, SUBLANES):
            nr = ir * lmr - ii * lmi + er
            ni = ir * lmi + ii * lmr + ei
            ir = jnp.where(row_id == k, pltpu.roll(nr, 1, axis=0), ir)
            ii = jnp.where(row_id == k, pltpu.roll(ni, 1, axis=0), ii)

        fr, fi = lax.fori_loop(0, m, functools.partial(step, store=True), (ir, ii), unroll=True)
        car_re_ref[:, cols] = fr[SUBLANES - 1:SUBLANES, :]
        car_im_ref[:, cols] = fi[SUBLANES - 1:SUBLANES, :]

    ys.append(c_project(n_slab - 1))
    y = jnp.concatenate(ys, axis=1) + d_ref[...] * u
    zbuf[...] = _gelu_tanh(y).astype(BF16)

    @pl.when(step_id >= 1)
    def _():
        for k in range(SUBLANES):
            out_copy(step_id - 1, 1 - slot, k).start()

    @pl.when(step_id == n_steps - 1)
    def _():
        @pl.when(step_id >= 2)
        def _():
            for k in range(SUBLANES):
                out_copy(step_id - 2, slot, k).wait()
        for piece in glu_pieces(zbuf[...], slot):
            piece()
        for k in range(SUBLANES):
            out_copy(step_id, slot, k).start()
        for k in range(SUBLANES):
            out_copy(step_id, slot, k).wait()

        @pl.when(step_id >= 1)
        def _():
            for k in range(SUBLANES):
                out_copy(step_id - 1, 1 - slot, k).wait()


def s5_branch(proj, bre, bim, cre, cim, lre, lim, d_skip, w_glu, *, batch, seq, width, m, cw, tn):
    rows = SUBLANES * m
    n_t = seq // rows
    n_state = lre.shape[1]
    nh = w_glu.shape[1] // 2
    once = pl.Buffered(1)
    const3 = lambda b, t: (0, 0, 0)
    const2 = lambda b, t: (0, 0)
    return pl.pallas_call(
        functools.partial(_s5_kernel, m=m, cw=cw, tn=tn),
        grid=(batch, n_t),
        in_specs=[pl.BlockSpec(memory_space=pl.ANY),
                  pl.BlockSpec(bre.shape, const3, pipeline_mode=once),
                  pl.BlockSpec(bim.shape, const3, pipeline_mode=once),
                  pl.BlockSpec(cre.shape, const3, pipeline_mode=once),
                  pl.BlockSpec(cim.shape, const3, pipeline_mode=once),
                  pl.BlockSpec((1, n_state), const2),
                  pl.BlockSpec((1, n_state), const2),
                  pl.BlockSpec((1, width), const2),
                  pl.BlockSpec(w_glu.shape, const2, pipeline_mode=once)],
        out_specs=pl.BlockSpec(memory_space=pl.ANY),
        out_shape=jax.ShapeDtypeStruct((batch * seq, nh), F32),
        scratch_shapes=[pltpu.VMEM((2, m, SUBLANES, width), F32), pltpu.VMEM((2, m, SUBLANES, nh), F32),
                        pltpu.VMEM((rows, width), BF16),
                        pltpu.VMEM((rows, n_state), F32), pltpu.VMEM((rows, n_state), F32),
                        pltpu.VMEM((1, n_state), F32), pltpu.VMEM((1, n_state), F32),
                        pltpu.VMEM((1, n_state), F32), pltpu.VMEM((1, n_state), F32),
                        pltpu.SemaphoreType.DMA((2,)), pltpu.SemaphoreType.DMA((2,))],
        compiler_params=_cparams(("arbitrary", "arbitrary")),
        name="s5_branch",
    )(proj, bre, bim, cre, cim, lre, lim, d_skip.reshape(1, width), w_glu)


def _conv_kernel(a_ref, b_ref, ah_ref, bh_ref, dww_ref, dwb_ref, lng_ref, lnb_ref, w_ref, o_ref,
                 zs_ref, wb_ref, yc_ref, *, tm, taps):
    i = pl.program_id(1)
    off0 = CONV_HALO - (taps - 1)
    width = a_ref.shape[1]
    n_slab = width // LANES
    zh = jnp.where(i == 0, 0.0, ah_ref[...] * _sigmoid(bh_ref[...]))
    z = a_ref[...] * _sigmoid(b_ref[...])
    for s in range(n_slab):
        zs_ref[s, 0:CONV_HALO, :] = zh[:, s * LANES:(s + 1) * LANES]
        zs_ref[s, CONV_HALO:, :] = z[:, s * LANES:(s + 1) * LANES]

    @pl.when(i == 0)
    def _():
        for j in range(taps):
            wb_ref[j] = jnp.broadcast_to(dww_ref[j:j + 1, :], (SUBLANES, width))
        wb_ref[taps] = jnp.broadcast_to(dwb_ref[...], (SUBLANES, width))

    rg = CONV_ROW_CHUNK // SUBLANES

    def chunk(c, carry):
        r0 = pl.multiple_of(c * CONV_ROW_CHUNK, CONV_ROW_CHUNK)
        for s in range(n_slab):
            lanes = slice(s * LANES, (s + 1) * LANES)
            bias = wb_ref[taps, :, lanes]
            acc = [bias] * rg
            for j in range(taps):
                w = wb_ref[j, :, lanes]
                for g in range(rg):
                    acc[g] = acc[g] + w * zs_ref[s, pl.ds(r0 + g * SUBLANES + off0 + j, SUBLANES), :]
            for g in range(rg):
                yc_ref[pl.ds(pl.multiple_of(r0 + g * SUBLANES, SUBLANES), SUBLANES), lanes] = acc[g]
        return carry

    lax.fori_loop(0, tm // CONV_ROW_CHUNK, chunk, 0)
    y = yc_ref[...]
    mu = jnp.mean(y, axis=-1, keepdims=True)
    var = jnp.mean(jnp.square(y - mu), axis=-1, keepdims=True)
    y = (y - mu) * lax.rsqrt(var + LN_EPS) * lng_ref[...] + lnb_ref[...]
    y = y * _sigmoid(y)
    o_ref[...] = jnp.dot(y.astype(BF16), w_ref[...], preferred_element_type=F32)


def conv_branch(proj, dw_w, dw_b, ln_g, ln_b, w_out, *, batch, seq, width, col0, tm):
    n_t = seq // tm
    ca, cb = col0 // width, col0 // width + 1
    hb = tm // CONV_HALO
    taps = dw_w.shape[0]
    d_out = w_out.shape[1]
    const2 = lambda b, i: (0, 0)
    halo = lambda b, i: jnp.maximum((b * n_t + i) * hb - 1, 0)
    return pl.pallas_call(
        functools.partial(_conv_kernel, tm=tm, taps=taps),
        grid=(batch, n_t),
        in_specs=[pl.BlockSpec((tm, width), lambda b, i: (b * n_t + i, ca)),
                  pl.BlockSpec((tm, width), lambda b, i: (b * n_t + i, cb)),
                  pl.BlockSpec((CONV_HALO, width), lambda b, i: (halo(b, i), ca)),
                  pl.BlockSpec((CONV_HALO, width), lambda b, i: (halo(b, i), cb)),
                  pl.BlockSpec((taps, width), const2),
                  pl.BlockSpec((1, width), const2),
                  pl.BlockSpec((1, width), const2),
                  pl.BlockSpec((1, width), const2),
                  pl.BlockSpec((width, d_out), const2)],
        out_specs=pl.BlockSpec((tm, d_out), lambda b, i: (b * n_t + i, 0)),
        out_shape=jax.ShapeDtypeStruct((batch * seq, d_out), F32),
        scratch_shapes=[pltpu.VMEM((width // LANES, tm + CONV_HALO, LANES), F32),
                        pltpu.VMEM((taps + 1, SUBLANES, width), F32),
                        pltpu.VMEM((tm, width), F32)],
        compiler_params=_cparams(("parallel", "arbitrary")),
        name="conv_branch",
    )(proj, proj, proj, proj, dw_w, dw_b.reshape(1, width), ln_g.reshape(1, width),
      ln_b.reshape(1, width), w_out)


def _combine_kernel(*refs, parts):
    x_ref, g_ref = refs[0], refs[1]
    ya_ref, yb_ref, wo_ref, o_ref = refs[2 + 2 * parts], refs[3 + 2 * parts], refs[-2], refs[-1]
    gw = ya_ref.shape[1] // parts
    x = x_ref[...]
    h = _rms(x, g_ref[...]).astype(BF16)
    o_ref[...] = x
    for p in range(parts):
        cols = slice(p * gw, (p + 1) * gw)
        ga = _sigmoid(jnp.dot(h, refs[2 + p][...], preferred_element_type=F32))
        gb = _sigmoid(jnp.dot(h, refs[2 + parts + p][...], preferred_element_type=F32))
        mix = ga * ya_ref[:, cols] + gb * yb_ref[:, cols]
        o_ref[...] += jnp.dot(mix.astype(BF16), wo_ref[cols, :], preferred_element_type=F32)


def combine(x, g, w_in, y_a, y_b, w_out, *, gate_col0, tm):
    m, d = x.shape
    gw = math.gcd(gate_col0, d)
    parts = d // gw
    row = lambda i: (i, 0)
    once = pl.Buffered(1)
    gate_specs = [pl.BlockSpec((d, gw), functools.partial(lambda i, c: (0, c), c=(gate_col0 + s * d) // gw + p),
                               pipeline_mode=once) for s in range(2) for p in range(parts)]
    return pl.pallas_call(
        functools.partial(_combine_kernel, parts=parts),
        grid=(m // tm,),
        in_specs=[pl.BlockSpec((tm, d), row), pl.BlockSpec((1, d), lambda i: (0, 0))] + gate_specs +
                 [pl.BlockSpec((tm, d), row),
                  pl.BlockSpec((tm, d), row),
                  pl.BlockSpec((d, d), lambda i: (0, 0), pipeline_mode=once)],
        out_specs=pl.BlockSpec((tm, d), row),
        out_shape=jax.ShapeDtypeStruct((m, d), F32),
        compiler_params=_cparams(("parallel",)),
        name="gated_combine",
    )(x, g.reshape(1, d), *([w_in] * (2 * parts)), y_a, y_b, w_out)


def _xattn_kernel(x_ref, q_ref, kv_ref, wo_ref, o_ref, *, heads, scale):
    d = x_ref.shape[1]
    hd = d // heads
    acc = x_ref[...]
    for h in range(heads):
        q = q_ref[:, h * hd:(h + 1) * hd]
        k = kv_ref[:, h * hd:(h + 1) * hd]
        v = kv_ref[:, d + h * hd:d + (h + 1) * hd]
        s = lax.dot_general(q, k, (((1,), (1,)), ((), ())), preferred_element_type=F32) * scale
        s = s - jnp.max(s, axis=-1, keepdims=True)
        p = jnp.exp(s)
        p = p / jnp.sum(p, axis=-1, keepdims=True)
        o = jnp.dot(p.astype(BF16), v, preferred_element_type=F32)
        acc = acc + jnp.dot(o.astype(BF16), wo_ref[h * hd:(h + 1) * hd, :], preferred_element_type=F32)
    o_ref[...] = acc


def xattn(x, q, kv, wo, *, batch, seq, n_mem, heads, tm):
    d = x.shape[1]
    n_t = seq // tm
    scale = 1.0 / math.sqrt(d // heads)
    return pl.pallas_call(
        functools.partial(_xattn_kernel, heads=heads, scale=scale),
        grid=(batch, n_t),
        in_specs=[pl.BlockSpec((tm, d), lambda b, i: (b * n_t + i, 0)),
                  pl.BlockSpec((tm, d), lambda b, i: (b * n_t + i, 0)),
                  pl.BlockSpec((n_mem, 2 * d), lambda b, i: (b, 0)),
                  pl.BlockSpec((d, d), lambda b, i: (0, 0))],
        out_specs=pl.BlockSpec((tm, d), lambda b, i: (b * n_t + i, 0)),
        out_shape=jax.ShapeDtypeStruct((batch * seq, d), F32),
        compiler_params=_cparams(("parallel", "parallel")),
        name="xattn",
    )(x, q, kv, wo)


def _router_kernel(x_ref, g_ref, w_ref, b_ref, tri_ref, h_ref, info_ref, infot_ref, cnt_ref, run_ref, *,
                   n_groups, epg):
    i = pl.program_id(0)
    tm = x_ref.shape[0]

    @pl.when(i == 0)
    def _():
        run_ref[...] = jnp.zeros_like(run_ref)

    h = _rms(x_ref[...], g_ref[...])
    h_ref[...] = h
    h_hi = h.astype(BF16)
    h_lo = (h - h_hi.astype(F32)).astype(BF16)
    p = jnp.dot(h_hi, w_ref[...], preferred_element_type=F32)
    logits = (p[:, :LANES] + p[:, LANES:] + jnp.dot(h_lo, w_ref[:, :LANES], preferred_element_type=F32)
              + b_ref[...])
    lane = lax.broadcasted_iota(jnp.int32, (tm, LANES), 1).astype(F32)
    neg = jnp.float32(-jnp.inf)
    big = jnp.float32(LANES)
    is_grp = lane < n_groups
    cl = jnp.where(is_grp, logits, neg)
    cmax = jnp.max(cl, axis=-1, keepdims=True)
    g_idx = jnp.min(jnp.where(cl == cmax, lane, big), axis=-1, keepdims=True)
    p_sel = 1.0 / jnp.sum(jnp.where(is_grp, jnp.exp(cl - cmax), 0.0), axis=-1, keepdims=True)
    lo = n_groups + g_idx * epg
    in_grp = (lane >= lo) & (lane < lo + epg)
    fl = jnp.where(in_grp, logits, neg)
    v1 = jnp.max(fl, axis=-1, keepdims=True)
    i1 = jnp.min(jnp.where(fl == v1, lane, big), axis=-1, keepdims=True)
    fl2 = jnp.where(lane == i1, neg, fl)
    v2 = jnp.max(fl2, axis=-1, keepdims=True)
    i2 = jnp.min(jnp.where(fl2 == v2, lane, big), axis=-1, keepdims=True)
    e2x = jnp.exp(v2 - v1)
    w1 = p_sel / (1.0 + e2x)
    w2 = p_sel * e2x / (1.0 + e2x)
    e1 = i1 - n_groups
    e2 = i2 - n_groups
    oh1 = lane == e1
    oh2 = lane == e2
    both = jnp.where(oh1 | oh2, 1.0, 0.0)
    before = jnp.dot(tri_ref[...], both.astype(BF16), preferred_element_type=F32) + run_ref[...]
    r1 = jnp.sum(jnp.where(oh1, before, 0.0), axis=-1, keepdims=True)
    r2 = jnp.sum(jnp.where(oh2, before, 0.0), axis=-1, keepdims=True)
    run_ref[...] += jnp.sum(both, axis=0, keepdims=True)
    cnt_ref[...] = run_ref[...]
    info = jnp.where(lane == 0, e1, 0.0)
    info = jnp.where(lane == 1, e2, info)
    info = jnp.where(lane == 2, r1, info)
    info = jnp.where(lane == 3, r2, info)
    info = jnp.where(lane == 4, w1, info)
    info = jnp.where(lane == 5, w2, info)
    info_ref[...] = info
    infot_ref[...] = jnp.transpose(info)[0:SUBLANES, :]


def router(x, g, w_pad, b_pad, *, n_groups, epg, tm):
    m, d = x.shape
    tri = (jnp.arange(tm)[:, None] > jnp.arange(tm)[None, :]).astype(BF16)
    return pl.pallas_call(
        functools.partial(_router_kernel, n_groups=n_groups, epg=epg),
        grid=(m // tm,),
        in_specs=[pl.BlockSpec((tm, d), lambda i: (i, 0)),
                  pl.BlockSpec((1, d), lambda i: (0, 0)),
                  pl.BlockSpec((d, 2 * LANES), lambda i: (0, 0)),
                  pl.BlockSpec((1, LANES), lambda i: (0, 0)),
                  pl.BlockSpec((tm, tm), lambda i: (0, 0))],
        out_specs=[pl.BlockSpec((tm, d), lambda i: (i, 0)),
                   pl.BlockSpec((tm, LANES), lambda i: (i, 0)),
                   pl.BlockSpec((SUBLANES, tm), lambda i: (0, i)),
                   pl.BlockSpec((1, LANES), lambda i: (0, 0))],
        out_shape=(jax.ShapeDtypeStruct((m, d), F32),
                   jax.ShapeDtypeStruct((m, LANES), F32),
                   jax.ShapeDtypeStruct((SUBLANES, m), F32),
                   jax.ShapeDtypeStruct((1, LANES), F32)),
        scratch_shapes=[pltpu.VMEM((1, LANES), F32)],
        compiler_params=_cparams(("arbitrary",)),
        name="moe_router",
    )(x, g.reshape(1, d), w_pad, b_pad, tri)


def moe_dispatch_plan(infot, counts, n_tokens, n_experts):
    e = infot[0:2].astype(jnp.int32)
    rank = infot[2:4].astype(jnp.int32)
    counts = counts[0, :n_experts].astype(jnp.int32)
    pcounts = ((counts + MOE_BLOCK - 1) // MOE_BLOCK) * MOE_BLOCK
    pends = jnp.cumsum(pcounts)
    pstarts = pends - pcounts
    ids = jnp.arange(n_experts, dtype=jnp.int32)
    dest = rank + jnp.sum(jnp.where(e[:, :, None] == ids, pstarts, 0), axis=-1)
    n_rows = (-(-2 * n_tokens // MOE_BLOCK) + n_experts) * MOE_BLOCK
    n_used = (pends[-1] // MOE_BLOCK).reshape(1)
    return dest, n_rows, n_used, pcounts, pends


def _dispatch_kernel(pc_ref, pe_ref, dest_ref, h_ref, xs_hbm, zbuf, sem, tsem, *, n_experts):
    i = pl.program_id(0)
    tm = h_ref.shape[0]

    @pl.when(i == 0)
    def _():
        zbuf[...] = jnp.zeros_like(zbuf)
        for wait in (False, True):
            for e in range(n_experts):
                @pl.when(pc_ref[e] > 0)
                def _():
                    row0 = pl.multiple_of(pe_ref[e] - MOE_BLOCK, MOE_BLOCK)
                    fill = pltpu.make_async_copy(zbuf, xs_hbm.at[pl.ds(row0, MOE_BLOCK)], sem)
                    fill.wait() if wait else fill.start()

    n_blocks = xs_hbm.shape[0] // MOE_BLOCK
    n_used = pe_ref[n_experts - 1] // MOE_BLOCK

    def tail_fill(b, wait):
        row0 = pl.multiple_of(b * MOE_BLOCK, MOE_BLOCK)
        fill = pltpu.make_async_copy(zbuf, xs_hbm.at[pl.ds(row0, MOE_BLOCK)], tsem)
        fill.wait() if wait else fill.start()

    @pl.when(i == 0)
    def _():
        lax.fori_loop(n_used, n_blocks, lambda b, c: (tail_fill(b, False), c)[1], 0)

    for k in range(2):
        for r in range(tm):
            pltpu.make_async_copy(h_ref.at[pl.ds(r, 1)], xs_hbm.at[pl.ds(dest_ref[k, r], 1)],
                                  sem).start(priority=r % 2)
    for k in range(2):
        pltpu.make_async_copy(h_ref, xs_hbm.at[pl.ds(0, tm)], sem).wait()

    @pl.when(i == pl.num_programs(0) - 1)
    def _():
        lax.fori_loop(n_used, n_blocks, lambda b, c: (tail_fill(b, True), c)[1], 0)


def moe_dispatch(h, dest, pcounts, pends, *, n_rows, tm):
    t, d = h.shape
    grid_spec = pltpu.PrefetchScalarGridSpec(
        num_scalar_prefetch=2,
        grid=(t // tm,),
        in_specs=[pl.BlockSpec((2, tm), lambda i, pc, pe: (0, i), memory_space=pltpu.SMEM),
                  pl.BlockSpec((tm, d), lambda i, pc, pe: (i, 0))],
        out_specs=pl.BlockSpec(memory_space=pl.ANY),
        scratch_shapes=[pltpu.VMEM((MOE_BLOCK, d), F32), pltpu.SemaphoreType.DMA(()),
                        pltpu.SemaphoreType.DMA(())],
    )
    return pl.pallas_call(
        functools.partial(_dispatch_kernel, n_experts=pcounts.shape[0]),
        grid_spec=grid_spec,
        out_shape=jax.ShapeDtypeStruct((n_rows, d), F32),
        compiler_params=_cparams(("arbitrary",)),
        name="moe_dispatch",
    )(pcounts, pends, dest, h)


def _experts_kernel(first_ref, cnt_ref, nb_ref, xs_hbm, wg_hbm, wu_hbm, wd_hbm, ys_hbm,
                    xbuf, ybuf, wgf, wuf, wdf, wgb, wub, wdb, xsem, ysem, wsem):
    e = pl.program_id(0)
    n_e = pl.num_programs(0)
    nb = nb_ref[0]
    first, cnt = first_ref[e], cnt_ref[e]
    n_blocks = xs_hbm.shape[0] // MOE_BLOCK
    wslot = lax.rem(e, 2)

    def rows(g):
        return pl.ds(pl.multiple_of(g * MOE_BLOCK, MOE_BLOCK), MOE_BLOCK)

    def x_copy(g, s):
        return pltpu.make_async_copy(xs_hbm.at[rows(g)], xbuf.at[s], xsem.at[s])

    def y_copy(g, s):
        return pltpu.make_async_copy(ybuf.at[s], ys_hbm.at[rows(g)], ysem.at[s])

    def w_copies(ex, s):
        half = wd_hbm.shape[1] // 2
        lo, hi = pl.ds(0, half), pl.ds(half, half)
        return ((pltpu.make_async_copy(wg_hbm.at[ex], wgf.at[s], wsem.at[s]), 0),
                (pltpu.make_async_copy(wu_hbm.at[ex], wuf.at[s], wsem.at[s]), 1),
                (pltpu.make_async_copy(wd_hbm.at[ex, lo], wdf.at[s, lo], wsem.at[s]), 0),
                (pltpu.make_async_copy(wd_hbm.at[ex, hi], wdf.at[s, hi], wsem.at[s]), 1))

    @pl.when(e == 0)
    def _():
        for g0 in range(EXPERT_RING - 1):
            @pl.when(g0 < nb)
            def _():
                x_copy(g0, g0).start(priority=BLOCK_DMA_QUEUE)

    @pl.when((e == 0) & (cnt > 0))
    def _():
        for cp, queue in w_copies(0, 0):
            cp.start(priority=queue)

    e_next = jnp.minimum(e + 1, n_e - 1)

    @pl.when((e + 1 < n_e) & (cnt_ref[e_next] > 0))
    def _():
        for cp, queue in w_copies(e_next, 1 - wslot):
            cp.start(priority=queue)

    @pl.when(cnt > 0)
    def _():
        for cp, _ in w_copies(e, wslot):
            cp.wait()
        wgb[...] = wgf[wslot].astype(BF16)
        wub[...] = wuf[wslot].astype(BF16)
        wdb[...] = wdf[wslot].astype(BF16)

    def block(b, carry):
        g = first + b
        s = lax.rem(g, EXPERT_RING)

        @pl.when(g + EXPERT_RING - 1 < nb)
        def _():
            x_copy(g + EXPERT_RING - 1, lax.rem(g + EXPERT_RING - 1, EXPERT_RING)).start(priority=BLOCK_DMA_QUEUE)

        x_copy(g, s).wait()
        x = xbuf[s].astype(BF16)
        gte = jnp.dot(x, wgb[...], preferred_element_type=F32)
        up = jnp.dot(x, wub[...], preferred_element_type=F32)
        act = (gte * _sigmoid(gte) * up).astype(BF16)
        y = jnp.dot(act, wdb[...], preferred_element_type=F32)

        @pl.when(g >= EXPERT_RING)
        def _():
            y_copy(g - EXPERT_RING, s).wait()

        ybuf[s] = y
        y_copy(g, s).start()
        return carry

    lax.fori_loop(0, cnt, block, 0)

    @pl.when(e == pl.num_programs(0) - 1)
    def _():
        for back in range(1, EXPERT_RING + 1):
            @pl.when(nb >= back)
            def _():
                y_copy(nb - back, lax.rem(nb - back, EXPERT_RING)).wait()

        ybuf[0] = jnp.zeros(ybuf.shape[1:], F32)
        lax.fori_loop(nb, n_blocks, lambda g, c: (y_copy(g, 0).start(), c)[1], 0)
        lax.fori_loop(nb, n_blocks, lambda g, c: (y_copy(g, 0).wait(), c)[1], 0)


def experts(xs, w_gate, w_up, w_down, first_blk, n_blk, n_used):
    n_rows, d = xs.shape
    n_experts, _, de = w_gate.shape
    grid_spec = pltpu.PrefetchScalarGridSpec(
        num_scalar_prefetch=3,
        grid=(n_experts,),
        in_specs=[pl.BlockSpec(memory_space=pl.ANY)] * 4,
        out_specs=pl.BlockSpec(memory_space=pl.ANY),
        scratch_shapes=[pltpu.VMEM((EXPERT_RING, MOE_BLOCK, d), F32), pltpu.VMEM((EXPERT_RING, MOE_BLOCK, d), F32),
                        pltpu.VMEM((2, d, de), F32), pltpu.VMEM((2, d, de), F32), pltpu.VMEM((2, de, d), F32),
                        pltpu.VMEM((d, de), BF16), pltpu.VMEM((d, de), BF16), pltpu.VMEM((de, d), BF16),
                        pltpu.SemaphoreType.DMA((EXPERT_RING,)), pltpu.SemaphoreType.DMA((EXPERT_RING,)),
                        pltpu.SemaphoreType.DMA((2,))],
    )
    return pl.pallas_call(
        _experts_kernel,
        grid_spec=grid_spec,
        out_shape=jax.ShapeDtypeStruct((n_rows, d), F32),
        compiler_params=_cparams(("arbitrary",)),
        name="moe_experts",
    )(first_blk, n_blk, n_used, xs, w_gate, w_up, w_down)


def _moe_combine_kernel(dest_ref, destn_ref, x_ref, info_ref, g_ref, ys_hbm, o_ref, ybuf, sem, *, norm):
    i = pl.program_id(0)
    n = pl.num_programs(0)
    tm = x_ref.shape[0]
    slot = lax.rem(i, 2)

    def gather_start(dref, s):
        for k in range(2):
            for r in range(tm):
                pltpu.make_async_copy(ys_hbm.at[pl.ds(dref[k, r], 1)], ybuf.at[s, k, pl.ds(r, 1)],
                                      sem.at[s]).start(priority=r % 2)

    @pl.when(i == 0)
    def _():
        gather_start(dest_ref, 0)

    @pl.when(i + 1 < n)
    def _():
        gather_start(destn_ref, 1 - slot)

    for k in range(2):
        pltpu.make_async_copy(ys_hbm.at[pl.ds(0, tm)], ybuf.at[slot, k], sem.at[slot]).wait()
    w1 = info_ref[:, 4:5]
    w2 = info_ref[:, 5:6]
    s = x_ref[...] + (w1 * ybuf[slot, 0] + w2 * ybuf[slot, 1])
    o_ref[...] = _rms(s, g_ref[...]) if norm else s


def moe_combine(x, ys, dest, info, g, *, norm, tm):
    t, d = x.shape
    n = t // tm
    return pl.pallas_call(
        functools.partial(_moe_combine_kernel, norm=norm),
        grid=(n,),
        in_specs=[pl.BlockSpec((2, tm), lambda i: (0, i), memory_space=pltpu.SMEM),
                  pl.BlockSpec((2, tm), lambda i: (0, jnp.minimum(i + 1, n - 1)), memory_space=pltpu.SMEM),
                  pl.BlockSpec((tm, d), lambda i: (i, 0)),
                  pl.BlockSpec((tm, LANES), lambda i: (i, 0)),
                  pl.BlockSpec((1, d), lambda i: (0, 0)),
                  pl.BlockSpec(memory_space=pl.ANY)],
        out_specs=pl.BlockSpec((tm, d), lambda i: (i, 0)),
        out_shape=jax.ShapeDtypeStruct((t, d), F32),
        scratch_shapes=[pltpu.VMEM((2, 2, tm, d), F32), pltpu.SemaphoreType.DMA((2,))],
        compiler_params=_cparams(("arbitrary",)),
        name="moe_combine",
    )(dest, dest, x, info, g.reshape(1, d), ys)


def forward(x, mem, norm_mix_g, w_in, s5_lambda_re, s5_lambda_im, s5_log_dt, s5_b_re, s5_b_im, s5_c_re,
            s5_c_im, s5_d, s5_w_glu, conv_dw_w, conv_dw_b, conv_ln_g, conv_ln_b, conv_w_out, w_out,
            norm_xattn_g, norm_mem_g, xattn_wq, xattn_wk, xattn_wv, xattn_wo, norm_moe_g, router_w_group,
            router_b_group, router_w_expert, router_b_expert, exp_w_gate, exp_w_up, exp_w_down, norm_final_g,
            *, tiles):
    batch, seq, d = x.shape
    depth = w_in.shape[0]
    n_mem = mem.shape[1]
    s5_width = s5_d.shape[1]
    conv_width = conv_dw_b.shape[1]
    n_groups, epg = router_w_expert.shape[1], router_w_expert.shape[3]
    t = batch * seq
    xf = x.reshape(t, d)
    memf = mem.reshape(batch * n_mem, d)
    gate_col0 = s5_width + 2 * conv_width
    for l in range(depth):
        w_in_b = w_in[l].astype(BF16)
        proj = norm_mm_rows(xf, norm_mix_g[l], w_in_b[:, :gate_col0], tm=tiles["proj_tm"], tn=tiles["proj_tn"],
                            out_dtype=F32)
        lbr, lbi, bbr, bbi = s5_discretise(s5_lambda_re[l], s5_lambda_im[l], s5_log_dt[l],
                                           s5_b_re[l], s5_b_im[l])
        bre, bim, cre, cim = s5_block_diag(bbr, bbi, s5_c_re[l], s5_c_im[l])
        y_a = s5_branch(proj, bre, bim, cre, cim, lbr.reshape(1, -1), lbi.reshape(1, -1), s5_d[l],
                        s5_w_glu[l].astype(BF16), batch=batch, seq=seq, width=s5_width, m=tiles["s5_m"],
                        cw=tiles["s5_cw"], tn=tiles["glu_tn"])
        y_b = conv_branch(proj, conv_dw_w[l], conv_dw_b[l], conv_ln_g[l], conv_ln_b[l],
                          conv_w_out[l].astype(BF16), batch=batch, seq=seq, width=conv_width,
                          col0=s5_width, tm=tiles["conv_tm"])
        xf = combine(xf, norm_mix_g[l], w_in_b, y_a, y_b, w_out[l].astype(BF16), gate_col0=gate_col0,
                     tm=tiles["comb_tm"])

        q = norm_mm_rows(xf, norm_xattn_g[l], xattn_wq[l].astype(BF16), tm=tiles["q_tm"], tn=tiles["q_tn"],
                         out_dtype=BF16)
        wkv = jnp.concatenate([xattn_wk[l], xattn_wv[l]], axis=1).astype(BF16)
        kv = norm_mm(memf, norm_mem_g[l], wkv, tm=batch * n_mem, tn=tiles["kv_tn"], out_dtype=BF16)
        xf = xattn(xf, q, kv, xattn_wo[l].astype(BF16), batch=batch, seq=seq, n_mem=n_mem,
                   heads=XATTN_HEADS, tm=tiles["xattn_tm"])

        w_r = jnp.concatenate([router_w_group[l],
                               jnp.transpose(router_w_expert[l], (1, 0, 2)).reshape(d, n_groups * epg)], axis=1)
        b_r = jnp.concatenate([router_b_group[l], router_b_expert[l].reshape(-1)])
        n_r = w_r.shape[1]
        w_pad = jnp.pad(w_r, ((0, 0), (0, LANES - n_r)))
        w_hi = w_pad.astype(BF16)
        w_pad = jnp.concatenate([w_hi, (w_pad - w_hi.astype(F32)).astype(BF16)], axis=1)
        b_pad = jnp.pad(b_r, (0, LANES - n_r)).reshape(1, LANES)
        h, info, infot, counts = router(xf, norm_moe_g[l], w_pad, b_pad, n_groups=n_groups, epg=epg,
                                        tm=tiles["router_tm"])
        dest, n_rows, n_used, pcounts, pends = moe_dispatch_plan(infot, counts, t, n_groups * epg)
        xs = moe_dispatch(h, dest, pcounts, pends, n_rows=n_rows, tm=tiles["moe_tm"])
        ys = experts(xs, exp_w_gate[l], exp_w_up[l], exp_w_down[l], (pends - pcounts) // MOE_BLOCK,
                     pcounts // MOE_BLOCK, n_used)
        last = l + 1 == depth
        xf = moe_combine(xf, ys, dest, info, norm_final_g if last else jnp.ones((d,), F32), norm=last,
                         tm=tiles["moe_tm"])
    return xf.reshape(batch, seq, d)


def kernel(x, mem, norm_mix_g, w_in, s5_lambda_re, s5_lambda_im, s5_log_dt, s5_b_re, s5_b_im, s5_c_re, s5_c_im, s5_d, s5_w_glu, conv_dw_w, conv_dw_b, conv_ln_g, conv_ln_b, conv_w_out, w_out, norm_xattn_g, norm_mem_g, xattn_wq, xattn_wk, xattn_wv, xattn_wo, norm_moe_g, router_w_group, router_b_group, router_w_expert, router_b_expert, exp_w_gate, exp_w_up, exp_w_down, norm_final_g):
    return forward(x, mem, norm_mix_g, w_in, s5_lambda_re, s5_lambda_im, s5_log_dt, s5_b_re, s5_b_im,
                   s5_c_re, s5_c_im, s5_d, s5_w_glu, conv_dw_w, conv_dw_b, conv_ln_g, conv_ln_b, conv_w_out,
                   w_out, norm_xattn_g, norm_mem_g, xattn_wq, xattn_wk, xattn_wv, xattn_wo, norm_moe_g,
                   router_w_group, router_b_group, router_w_expert, router_b_expert, exp_w_gate, exp_w_up,
                   exp_w_down, norm_final_g, tiles=TILES)
```

```python
import functools
import math

import jax
import jax.numpy as jnp
from jax import lax
from jax.experimental import pallas as pl
from jax.experimental.pallas import tpu as pltpu

F32 = jnp.float32
BF16 = jnp.bfloat16

RMS_EPS = 1e-6
LN_EPS = 1e-5

S5_SLAB_CH = 256
CONV_HALO = 32
CONV_ROW_CHUNK = 64
XATTN_HEADS = 4
MOE_BLOCK = 128
BLOCK_DMA_QUEUE = 1
EXPERT_RING = 4
SUBLANES = 8
LANES = 128
VMEM_LIMIT = 56 * 1024 * 1024

TILES = dict(proj_tm=512, proj_tn=512, s5_m=64, s5_cw=512, glu_tn=512, conv_tm=256,
             comb_tm=256, q_tm=512, q_tn=512, kv_tn=512, xattn_tm=512, moe_tm=256)


def _cparams(sem):
    return pltpu.CompilerParams(dimension_semantics=sem, vmem_limit_bytes=VMEM_LIMIT)


def _rms(x, g):
    return x * lax.rsqrt(jnp.mean(x * x, axis=-1, keepdims=True) + RMS_EPS) * g


def _sigmoid(x):
    return 0.5 * jnp.tanh(0.5 * x) + 0.5


def _gelu_tanh(x):
    c = math.sqrt(2.0 / math.pi)
    return 0.5 * x * (1.0 + jnp.tanh(c * (x + 0.044715 * (x * x * x))))


def _norm_mm_kernel(x_ref, g_ref, w_ref, o_ref, h_ref, *, act_from):
    j = pl.program_id(1)

    @pl.when(j == 0)
    def _():
        h_ref[...] = _rms(x_ref[...], g_ref[...]).astype(BF16)

    y = jnp.dot(h_ref[...], w_ref[...], preferred_element_type=F32)
    if act_from is None:
        o_ref[...] = y.astype(o_ref.dtype)
    else:
        @pl.when(j < act_from)
        def _():
            o_ref[...] = y.astype(o_ref.dtype)

        @pl.when(j >= act_from)
        def _():
            o_ref[...] = _sigmoid(y).astype(o_ref.dtype)


def norm_mm(x, g, w, *, tm, tn, out_dtype=F32, act_from=None):
    m, k = x.shape
    n = w.shape[1]
    return pl.pallas_call(
        functools.partial(_norm_mm_kernel, act_from=act_from),
        grid=(m // tm, n // tn),
        in_specs=[pl.BlockSpec((tm, k), lambda i, j: (i, 0)),
                  pl.BlockSpec((1, k), lambda i, j: (0, 0)),
                  pl.BlockSpec((k, tn), lambda i, j: (0, j))],
        out_specs=pl.BlockSpec((tm, tn), lambda i, j: (i, j)),
        out_shape=jax.ShapeDtypeStruct((m, n), out_dtype),
        scratch_shapes=[pltpu.VMEM((tm, k), BF16)],
        compiler_params=_cparams(("parallel", "arbitrary")),
        name="norm_mm",
    )(x, g.reshape(1, k), w)


def _norm_mm_rows_kernel(x_ref, g_ref, w_ref, o_ref, *, tn):
    h = _rms(x_ref[...], g_ref[...]).astype(BF16)
    for j in range(o_ref.shape[1] // tn):
        cols = slice(j * tn, (j + 1) * tn)
        o_ref[:, cols] = jnp.dot(h, w_ref[:, cols], preferred_element_type=F32).astype(o_ref.dtype)


def norm_mm_rows(x, g, w, *, tm, tn, out_dtype, n_cols=None):
    m, k = x.shape
    n = w.shape[1] if n_cols is None else n_cols
    return pl.pallas_call(
        functools.partial(_norm_mm_rows_kernel, tn=tn),
        grid=(m // tm,),
        in_specs=[pl.BlockSpec((tm, k), lambda i: (i, 0)),
                  pl.BlockSpec((1, k), lambda i: (0, 0)),
                  pl.BlockSpec((k, n), lambda i: (0, 0), pipeline_mode=pl.Buffered(1))],
        out_specs=pl.BlockSpec((tm, n), lambda i: (i, 0)),
        out_shape=jax.ShapeDtypeStruct((m, n), out_dtype),
        compiler_params=_cparams(("parallel",)),
        name="norm_mm_rows",
    )(x, g.reshape(1, k), w)


def _s5_disc_kernel(lr_ref, li_ref, ldt_ref, br_ref, bi_ref, lbr_ref, lbi_ref, bbr_ref, bbi_ref):
    lr, li = lr_ref[...], li_ref[...]
    dt = jnp.exp(ldt_ref[...])
    mag = jnp.exp(lr * dt)
    ang = li * dt
    lb_re, lb_im = mag * jnp.cos(ang), mag * jnp.sin(ang)
    den = lr * lr + li * li
    nr, ni = lb_re - 1.0, lb_im
    coef_re = (nr * lr + ni * li) / den
    coef_im = (ni * lr - nr * li) / den
    lbr_ref[...] = lb_re
    lbi_ref[...] = lb_im
    br, bi = br_ref[...], bi_ref[...]
    bbr_ref[...] = coef_re[None] * br - coef_im[None] * bi
    bbi_ref[...] = coef_re[None] * bi + coef_im[None] * br


def s5_discretise(lam_re, lam_im, log_dt, b_re, b_im):
    g, n = lam_re.shape
    c = b_re.shape[-1]
    b_re_t = jnp.transpose(b_re, (2, 0, 1))
    b_im_t = jnp.transpose(b_im, (2, 0, 1))
    gn = jax.ShapeDtypeStruct((g, n), F32)
    cgn = jax.ShapeDtypeStruct((c, g, n), F32)
    return pl.pallas_call(
        _s5_disc_kernel,
        out_shape=(gn, gn, cgn, cgn),
        name="s5_discretise",
    )(lam_re, lam_im, log_dt.reshape(g, 1), b_re_t, b_im_t)


def s5_block_diag(bb_re, bb_im, c_re, c_im):
    c, g, n = bb_re.shape
    gs = S5_SLAB_CH // c
    n_slab = g // gs
    eye = jnp.eye(gs, dtype=F32)

    def bmat(bb):
        b = jnp.transpose(bb, (1, 0, 2)).reshape(n_slab, gs, c, n)
        full = b[:, :, :, None, :] * eye[None, :, None, :, None]
        return full.reshape(n_slab, gs * c, gs * n).astype(BF16)

    def cmat(cc):
        cm = jnp.transpose(cc.reshape(n_slab, gs, c, n), (0, 1, 3, 2))
        full = cm[:, :, :, None, :] * eye[None, :, None, :, None]
        return full.reshape(n_slab, gs * n, gs * c).astype(BF16)

    return bmat(bb_re), bmat(bb_im), cmat(c_re), cmat(c_im)


def _s5_kernel(proj_hbm, bre_ref, bim_ref, cre_ref, cim_ref, lre_ref, lim_ref, d_ref, wglu_ref, ya_hbm,
               ubuf, yabuf, zbuf, xre_ref, xim_ref, pre_ref, pim_ref, car_re_ref, car_im_ref, isem, osem,
               *, m, cw, tn):
    t = pl.program_id(1)
    n_t = pl.num_programs(1)
    step_id = pl.program_id(0) * n_t + t
    n_steps = pl.num_programs(0) * n_t
    slot = lax.rem(step_id, 2)
    rows = SUBLANES * m
    width = d_ref.shape[1]
    n_state = lre_ref.shape[1]
    n_slab = bre_ref.shape[0]
    slab_states = n_state // n_slab

    def in_copy(step, s, k):
        r0 = pl.multiple_of(step * rows + k * m, SUBLANES)
        return pltpu.make_async_copy(proj_hbm.at[pl.ds(r0, m), pl.ds(0, width)], ubuf.at[s, :, k, :], isem.at[s])

    def out_copy(step, s, k):
        r0 = pl.multiple_of(step * rows + k * m, SUBLANES)
        return pltpu.make_async_copy(yabuf.at[s, :, k, :], ya_hbm.at[pl.ds(r0, m)], osem.at[s])

    @pl.when(step_id == 0)
    def _():
        for k in range(SUBLANES):
            in_copy(0, 0, k).start()

    @pl.when(step_id + 1 < n_steps)
    def _():
        for k in range(SUBLANES):
            in_copy(step_id + 1, 1 - slot, k).start()

    @pl.when(t == 0)
    def _():
        car_re_ref[...] = jnp.zeros_like(car_re_ref)
        car_im_ref[...] = jnp.zeros_like(car_im_ref)
        lr, li = lre_ref[...], lim_ref[...]

        def pw(_, carry):
            pr, pi = carry
            return pr * lr - pi * li, pr * li + pi * lr

        pr, pi = lax.fori_loop(1, m, pw, (lr, li))
        pre_ref[...] = pr
        pim_ref[...] = pi

    @pl.when(step_id == 0)
    def _():
        zbuf[...] = jnp.zeros_like(zbuf)

    @pl.when(step_id >= 3)
    def _():
        for k in range(SUBLANES):
            out_copy(step_id - 3, 1 - slot, k).wait()

    nh = wglu_ref.shape[1] // 2

    def glu_pieces(z, s):
        def piece(j):
            val = jnp.dot(z, wglu_ref[:, j * tn:(j + 1) * tn], preferred_element_type=F32)
            gt = jnp.dot(z, wglu_ref[:, nh + j * tn:nh + (j + 1) * tn], preferred_element_type=F32)
            yabuf[s, :, :, j * tn:(j + 1) * tn] = (val * _sigmoid(gt)).reshape(m, SUBLANES, tn)
        return [functools.partial(piece, j) for j in range(nh // tn)]

    for k in range(SUBLANES):
        in_copy(step_id, slot, k).wait()
    u = ubuf[slot].reshape(rows, width)
    up = u.astype(BF16)

    def b_project(s):
        us = up[:, s * S5_SLAB_CH:(s + 1) * S5_SLAB_CH]
        cols = slice(s * slab_states, (s + 1) * slab_states)
        xre_ref[:, cols] = jnp.dot(us, bre_ref[s], preferred_element_type=F32)
        xim_ref[:, cols] = jnp.dot(us, bim_ref[s], preferred_element_type=F32)

    def c_project(s):
        cols = slice(s * slab_states, (s + 1) * slab_states)
        return (jnp.dot(xre_ref[:, cols].astype(BF16), cre_ref[s], preferred_element_type=F32)
                - jnp.dot(xim_ref[:, cols].astype(BF16), cim_ref[s], preferred_element_type=F32))

    n_cb = n_state // cw
    cb_per_slab = slab_states // cw
    glu_prev = glu_pieces(zbuf[...], 1 - slot)
    glu_at = {(i * n_cb) // len(glu_prev): [] for i in range(len(glu_prev))}
    for i, piece in enumerate(glu_prev):
        glu_at[(i * n_cb) // len(glu_prev)].append(piece)
    ys = []
    b_project(0)
    row_id = lax.broadcasted_iota(jnp.int32, (SUBLANES, cw), 0)
    for cb in range(n_cb):
        cols = slice(cb * cw, (cb + 1) * cw)
        slab = cb // cb_per_slab
        if cb % cb_per_slab == 0:
            if slab + 1 < n_slab:
                b_project(slab + 1)
            if slab >= 1:
                ys.append(c_project(slab - 1))
        for piece in glu_at.get(cb, []):
            piece()
        lr = jnp.broadcast_to(lre_ref[:, cols], (SUBLANES, cw))
        li = jnp.broadcast_to(lim_ref[:, cols], (SUBLANES, cw))

        def step(tau, carry, store, cols=cols, lr=lr, li=li):
            sr, si = carry
            r0 = pl.multiple_of(tau * SUBLANES, SUBLANES)
            nr = sr * lr - si * li + xre_ref[pl.ds(r0, SUBLANES), cols]
            ni = sr * li + si * lr + xim_ref[pl.ds(r0, SUBLANES), cols]
            if store:
                xre_ref[pl.ds(r0, SUBLANES), cols] = nr
                xim_ref[pl.ds(r0, SUBLANES), cols] = ni
            return nr, ni

        zero = jnp.zeros((SUBLANES, cw), F32)
        er, ei = lax.fori_loop(0, m, functools.partial(step, store=False), (zero, zero), unroll=True)

        lmr = jnp.broadcast_to(pre_ref[:, cols], (SUBLANES, cw))
        lmi = jnp.broadcast_to(pim_ref[:, cols], (SUBLANES, cw))
        ir = jnp.where(row_id == 0, jnp.broadcast_to(car_re_ref[:, cols], (SUBLANES, cw)), 0.0)
        ii = jnp.where(row_id == 0, jnp.broadcast_to(car_im_ref[:, cols], (SUBLANES, cw)), 0.0)
        for k in range(1, SUBLANES):
            nr = ir * lmr - ii * lmi + er
            ni = ir * lmi + ii * lmr + ei
            ir = jnp.where(row_id == k, pltpu.roll(nr, 1, axis=0), ir)
            ii = jnp.where(row_id == k, pltpu.roll(ni, 1, axis=0), ii)

        fr, fi = lax.fori_loop(0, m, functools.partial(step, store=True), (ir, ii), unroll=True)
        car_re_ref[:, cols] = fr[SUBLANES - 1:SUBLANES, :]
        car_im_ref[:, cols] = fi[SUBLANES - 1:SUBLANES, :]

    ys.append(c_project(n_slab - 1))
    y = jnp.concatenate(ys, axis=1) + d_ref[...] * u
    zbuf[...] = _gelu_tanh(y).astype(BF16)

    @pl.when(step_id >= 1)
    def _():
        for k in range(SUBLANES):
            out_copy(step_id - 1, 1 - slot, k).start()

    @pl.when(step_id == n_steps - 1)
    def _():
        @pl.when(step_id >= 2)
        def _():
            for k in range(SUBLANES):
                out_copy(step_id - 2, slot, k).wait()
        for piece in glu_pieces(zbuf[...], slot):
            piece()
        for k in range(SUBLANES):
            out_copy(step_id, slot, k).start()
        for k in range(SUBLANES):
            out_copy(step_id, slot, k).wait()

        @pl.when(step_id >= 1)
        def _():
            for k in range(SUBLANES):
                out_copy(step_id - 1, 1 - slot, k).wait()


def s5_branch(proj, bre, bim, cre, cim, lre, lim, d_skip, w_glu, *, batch, seq, width, m, cw, tn):
    rows = SUBLANES * m
    n_t = seq // rows
    n_state = lre.shape[1]
    nh = w_glu.shape[1] // 2
    once = pl.Buffered(1)
    const3 = lambda b, t: (0, 0, 0)
    const2 = lambda b, t: (0, 0)
    return pl.pallas_call(
        functools.partial(_s5_kernel, m=m, cw=cw, tn=tn),
        grid=(batch, n_t),
        in_specs=[pl.BlockSpec(memory_space=pl.ANY),
                  pl.BlockSpec(bre.shape, const3, pipeline_mode=once),
                  pl.BlockSpec(bim.shape, const3, pipeline_mode=once),
                  pl.BlockSpec(cre.shape, const3, pipeline_mode=once),
                  pl.BlockSpec(cim.shape, const3, pipeline_mode=once),
                  pl.BlockSpec((1, n_state), const2),
                  pl.BlockSpec((1, n_state), const2),
                  pl.BlockSpec((1, width), const2),
                  pl.BlockSpec(w_glu.shape, const2, pipeline_mode=once)],
        out_specs=pl.BlockSpec(memory_space=pl.ANY),
        out_shape=jax.ShapeDtypeStruct((batch * seq, nh), F32),
        scratch_shapes=[pltpu.VMEM((2, m, SUBLANES, width), F32), pltpu.VMEM((2, m, SUBLANES, nh), F32),
                        pltpu.VMEM((rows, width), BF16),
                        pltpu.VMEM((rows, n_state), F32), pltpu.VMEM((rows, n_state), F32),
                        pltpu.VMEM((1, n_state), F32), pltpu.VMEM((1, n_state), F32),
                        pltpu.VMEM((1, n_state), F32), pltpu.VMEM((1, n_state), F32),
                        pltpu.SemaphoreType.DMA((2,)), pltpu.SemaphoreType.DMA((2,))],
        compiler_params=_cparams(("arbitrary", "arbitrary")),
        name="s5_branch",
    )(proj, bre, bim, cre, cim, lre, lim, d_skip.reshape(1, width), w_glu)


def _conv_kernel(a_ref, b_ref, ah_ref, bh_ref, dww_ref, dwb_ref, lng_ref, lnb_ref, w_ref, o_ref,
                 zs_ref, wb_ref, yc_ref, *, tm, taps):
    i = pl.program_id(1)
    off0 = CONV_HALO - (taps - 1)
    width = a_ref.shape[1]
    n_slab = width // LANES
    zh = jnp.where(i == 0, 0.0, ah_ref[...] * _sigmoid(bh_ref[...]))
    z = a_ref[...] * _sigmoid(b_ref[...])
    for s in range(n_slab):
        zs_ref[s, 0:CONV_HALO, :] = zh[:, s * LANES:(s + 1) * LANES]
        zs_ref[s, CONV_HALO:, :] = z[:, s * LANES:(s + 1) * LANES]

    @pl.when(i == 0)
    def _():
        for j in range(taps):
            wb_ref[j] = jnp.broadcast_to(dww_ref[j:j + 1, :], (SUBLANES, width))
        wb_ref[taps] = jnp.broadcast_to(dwb_ref[...], (SUBLANES, width))

    rg = CONV_ROW_CHUNK // SUBLANES

    def chunk(c, carry):
        r0 = pl.multiple_of(c * CONV_ROW_CHUNK, CONV_ROW_CHUNK)
        for s in range(n_slab):
            lanes = slice(s * LANES, (s + 1) * LANES)
            bias = wb_ref[taps, :, lanes]
            acc = [bias] * rg
            for j in range(taps):
                w = wb_ref[j, :, lanes]
                for g in range(rg):
                    acc[g] = acc[g] + w * zs_ref[s, pl.ds(r0 + g * SUBLANES + off0 + j, SUBLANES), :]
            for g in range(rg):
                yc_ref[pl.ds(pl.multiple_of(r0 + g * SUBLANES, SUBLANES), SUBLANES), lanes] = acc[g]
        return carry

    lax.fori_loop(0, tm // CONV_ROW_CHUNK, chunk, 0)
    y = yc_ref[...]
    mu = jnp.mean(y, axis=-1, keepdims=True)
    var = jnp.mean(jnp.square(y - mu), axis=-1, keepdims=True)
    y = (y - mu) * lax.rsqrt(var + LN_EPS) * lng_ref[...] + lnb_ref[...]
    y = y * _sigmoid(y)
    o_ref[...] = jnp.dot(y.astype(BF16), w_ref[...], preferred_element_type=F32)


def conv_branch(proj, dw_w, dw_b, ln_g, ln_b, w_out, *, batch, seq, width, col0, tm):
    n_t = seq // tm
    ca, cb = col0 // width, col0 // width + 1
    hb = tm // CONV_HALO
    taps = dw_w.shape[0]
    d_out = w_out.shape[1]
    const2 = lambda b, i: (0, 0)
    halo = lambda b, i: jnp.maximum((b * n_t + i) * hb - 1, 0)
    return pl.pallas_call(
        functools.partial(_conv_kernel, tm=tm, taps=taps),
        grid=(batch, n_t),
        in_specs=[pl.BlockSpec((tm, width), lambda b, i: (b * n_t + i, ca)),
                  pl.BlockSpec((tm, width), lambda b, i: (b * n_t + i, cb)),
                  pl.BlockSpec((CONV_HALO, width), lambda b, i: (halo(b, i), ca)),
                  pl.BlockSpec((CONV_HALO, width), lambda b, i: (halo(b, i), cb)),
                  pl.BlockSpec((taps, width), const2),
                  pl.BlockSpec((1, width), const2),
                  pl.BlockSpec((1, width), const2),
                  pl.BlockSpec((1, width), const2),
                  pl.BlockSpec((width, d_out), const2)],
        out_specs=pl.BlockSpec((tm, d_out), lambda b, i: (b * n_t + i, 0)),
        out_shape=jax.ShapeDtypeStruct((batch * seq, d_out), F32),
        scratch_shapes=[pltpu.VMEM((width // LANES, tm + CONV_HALO, LANES), F32),
                        pltpu.VMEM((taps + 1, SUBLANES, width), F32),
                        pltpu.VMEM((tm, width), F32)],
        compiler_params=_cparams(("parallel", "arbitrary")),
        name="conv_branch",
    )(proj, proj, proj, proj, dw_w, dw_b.reshape(1, width), ln_g.reshape(1, width),
      ln_b.reshape(1, width), w_out)


def _combine_kernel(*refs, parts):
    x_ref, g_ref = refs[0], refs[1]
    ya_ref, yb_ref, wo_ref, o_ref = refs[2 + 2 * parts], refs[3 + 2 * parts], refs[-2], refs[-1]
    gw = ya_ref.shape[1] // parts
    x = x_ref[...]
    h = _rms(x, g_ref[...]).astype(BF16)
    o_ref[...] = x
    for p in range(parts):
        cols = slice(p * gw, (p + 1) * gw)
        ga = _sigmoid(jnp.dot(h, refs[2 + p][...], preferred_element_type=F32))
        gb = _sigmoid(jnp.dot(h, refs[2 + parts + p][...], preferred_element_type=F32))
        mix = ga * ya_ref[:, cols] + gb * yb_ref[:, cols]
        o_ref[...] += jnp.dot(mix.astype(BF16), wo_ref[cols, :], preferred_element_type=F32)


def combine(x, g, w_in, y_a, y_b, w_out, *, gate_col0, tm):
    m, d = x.shape
    gw = math.gcd(gate_col0, d)
    parts = d // gw
    row = lambda i: (i, 0)
    once = pl.Buffered(1)
    gate_specs = [pl.BlockSpec((d, gw), functools.partial(lambda i, c: (0, c), c=(gate_col0 + s * d) // gw + p),
                               pipeline_mode=once) for s in range(2) for p in range(parts)]
    return pl.pallas_call(
        functools.partial(_combine_kernel, parts=parts),
        grid=(m // tm,),
        in_specs=[pl.BlockSpec((tm, d), row), pl.BlockSpec((1, d), lambda i: (0, 0))] + gate_specs +
                 [pl.BlockSpec((tm, d), row),
                  pl.BlockSpec((tm, d), row),
                  pl.BlockSpec((d, d), lambda i: (0, 0), pipeline_mode=once)],
        out_specs=pl.BlockSpec((tm, d), row),
        out_shape=jax.ShapeDtypeStruct((m, d), F32),
        compiler_params=_cparams(("parallel",)),
        name="gated_combine",
    )(x, g.reshape(1, d), *([w_in] * (2 * parts)), y_a, y_b, w_out)


def _route_tile(x, g_ref, w_ref, b_ref, tri_ref, h_ref, info_ref, infot_ref, cnt_ref, run_ref, *,
                n_groups, epg):
    tm = x.shape[0]
    h = _rms(x, g_ref[...])
    h_ref[...] = h
    h_hi = h.astype(BF16)
    h_lo = (h - h_hi.astype(F32)).astype(BF16)
    p = jnp.dot(h_hi, w_ref[...], preferred_element_type=F32)
    logits = (p[:, :LANES] + p[:, LANES:] + jnp.dot(h_lo, w_ref[:, :LANES], preferred_element_type=F32)
              + b_ref[...])
    lane = lax.broadcasted_iota(jnp.int32, (tm, LANES), 1).astype(F32)
    neg = jnp.float32(-jnp.inf)
    big = jnp.float32(LANES)
    is_grp = lane < n_groups
    cl = jnp.where(is_grp, logits, neg)
    cmax = jnp.max(cl, axis=-1, keepdims=True)
    g_idx = jnp.min(jnp.where(cl == cmax, lane, big), axis=-1, keepdims=True)
    p_sel = 1.0 / jnp.sum(jnp.where(is_grp, jnp.exp(cl - cmax), 0.0), axis=-1, keepdims=True)
    lo = n_groups + g_idx * epg
    in_grp = (lane >= lo) & (lane < lo + epg)
    fl = jnp.where(in_grp, logits, neg)
    v1 = jnp.max(fl, axis=-1, keepdims=True)
    i1 = jnp.min(jnp.where(fl == v1, lane, big), axis=-1, keepdims=True)
    fl2 = jnp.where(lane == i1, neg, fl)
    v2 = jnp.max(fl2, axis=-1, keepdims=True)
    i2 = jnp.min(jnp.where(fl2 == v2, lane, big), axis=-1, keepdims=True)
    e2x = jnp.exp(v2 - v1)
    w1 = p_sel / (1.0 + e2x)
    w2 = p_sel * e2x / (1.0 + e2x)
    e1 = i1 - n_groups
    e2 = i2 - n_groups
    oh1 = lane == e1
    oh2 = lane == e2
    both = jnp.where(oh1 | oh2, 1.0, 0.0)
    before = jnp.dot(tri_ref[...], both.astype(BF16), preferred_element_type=F32) + run_ref[...]
    r1 = jnp.sum(jnp.where(oh1, before, 0.0), axis=-1, keepdims=True)
    r2 = jnp.sum(jnp.where(oh2, before, 0.0), axis=-1, keepdims=True)
    run_ref[...] += jnp.sum(both, axis=0, keepdims=True)
    cnt_ref[...] = run_ref[...]
    info = jnp.where(lane == 0, e1, 0.0)
    info = jnp.where(lane == 1, e2, info)
    info = jnp.where(lane == 2, r1, info)
    info = jnp.where(lane == 3, r2, info)
    info = jnp.where(lane == 4, w1, info)
    info = jnp.where(lane == 5, w2, info)
    info_ref[...] = info
    infot_ref[...] = jnp.transpose(info)[0:SUBLANES, :]


def _xattn_router_kernel(x_ref, q_ref, kv_ref, wo_ref, g_ref, w_ref, b_ref, tri_ref,
                         o_ref, h_ref, info_ref, infot_ref, cnt_ref, run_ref, *, heads, scale, n_groups, epg):
    @pl.when((pl.program_id(0) == 0) & (pl.program_id(1) == 0))
    def _():
        run_ref[...] = jnp.zeros_like(run_ref)

    d = x_ref.shape[1]
    hd = d // heads
    acc = x_ref[...]
    for h in range(heads):
        q = q_ref[:, h * hd:(h + 1) * hd]
        k = kv_ref[:, h * hd:(h + 1) * hd]
        v = kv_ref[:, d + h * hd:d + (h + 1) * hd]
        s = lax.dot_general(q, k, (((1,), (1,)), ((), ())), preferred_element_type=F32) * scale
        s = s - jnp.max(s, axis=-1, keepdims=True)
        p = jnp.exp(s)
        p = p / jnp.sum(p, axis=-1, keepdims=True)
        o = jnp.dot(p.astype(BF16), v, preferred_element_type=F32)
        acc = acc + jnp.dot(o.astype(BF16), wo_ref[h * hd:(h + 1) * hd, :], preferred_element_type=F32)
    o_ref[...] = acc
    _route_tile(acc, g_ref, w_ref, b_ref, tri_ref, h_ref, info_ref, infot_ref, cnt_ref, run_ref,
                n_groups=n_groups, epg=epg)


def xattn_router(x, q, kv, wo, g, w_pad, b_pad, *, batch, seq, n_mem, heads, n_groups, epg, tm):
    d = x.shape[1]
    m = batch * seq
    n_t = seq // tm
    scale = 1.0 / math.sqrt(d // heads)
    tri = (jnp.arange(tm)[:, None] > jnp.arange(tm)[None, :]).astype(BF16)
    row = lambda b, i: (b * n_t + i, 0)
    const = lambda b, i: (0, 0)
    once = pl.Buffered(1)
    return pl.pallas_call(
        functools.partial(_xattn_router_kernel, heads=heads, scale=scale, n_groups=n_groups, epg=epg),
        grid=(batch, n_t),
        in_specs=[pl.BlockSpec((tm, d), row),
                  pl.BlockSpec((tm, d), row),
                  pl.BlockSpec((n_mem, 2 * d), lambda b, i: (b, 0)),
                  pl.BlockSpec((d, d), const, pipeline_mode=once),
                  pl.BlockSpec((1, d), const),
                  pl.BlockSpec((d, 2 * LANES), const, pipeline_mode=once),
                  pl.BlockSpec((1, LANES), const),
                  pl.BlockSpec((tm, tm), const, pipeline_mode=once)],
        out_specs=[pl.BlockSpec((tm, d), row),
                   pl.BlockSpec((tm, d), row),
                   pl.BlockSpec((tm, LANES), row),
                   pl.BlockSpec((SUBLANES, tm), lambda b, i: (0, b * n_t + i)),
                   pl.BlockSpec((1, LANES), const)],
        out_shape=(jax.ShapeDtypeStruct((m, d), F32),
                   jax.ShapeDtypeStruct((m, d), F32),
                   jax.ShapeDtypeStruct((m, LANES), F32),
                   jax.ShapeDtypeStruct((SUBLANES, m), F32),
                   jax.ShapeDtypeStruct((1, LANES), F32)),
        scratch_shapes=[pltpu.VMEM((1, LANES), F32)],
        compiler_params=_cparams(("arbitrary", "arbitrary")),
        name="xattn_router",
    )(x, q, kv, wo, g.reshape(1, d), w_pad, b_pad, tri)


def moe_dispatch_plan(infot, counts, n_tokens, n_experts):
    e = infot[0:2].astype(jnp.int32)
    rank = infot[2:4].astype(jnp.int32)
    counts = counts[0, :n_experts].astype(jnp.int32)
    pcounts = ((counts + MOE_BLOCK - 1) // MOE_BLOCK) * MOE_BLOCK
    pends = jnp.cumsum(pcounts)
    pstarts = pends - pcounts
    ids = jnp.arange(n_experts, dtype=jnp.int32)
    dest = rank + jnp.sum(jnp.where(e[:, :, None] == ids, pstarts, 0), axis=-1)
    n_rows = (-(-2 * n_tokens // MOE_BLOCK) + n_experts) * MOE_BLOCK
    n_used = (pends[-1] // MOE_BLOCK).reshape(1)
    return dest, n_rows, n_used, pcounts, pends


def _dispatch_kernel(pc_ref, pe_ref, dest_ref, h_ref, xs_hbm, zbuf, sem, tsem, *, n_experts):
    i = pl.program_id(0)
    tm = h_ref.shape[0]

    @pl.when(i == 0)
    def _():
        zbuf[...] = jnp.zeros_like(zbuf)
        for wait in (False, True):
            for e in range(n_experts):
                @pl.when(pc_ref[e] > 0)
                def _():
                    row0 = pl.multiple_of(pe_ref[e] - MOE_BLOCK, MOE_BLOCK)
                    fill = pltpu.make_async_copy(zbuf, xs_hbm.at[pl.ds(row0, MOE_BLOCK)], sem)
                    fill.wait() if wait else fill.start()

    n_blocks = xs_hbm.shape[0] // MOE_BLOCK
    n_used = pe_ref[n_experts - 1] // MOE_BLOCK

    def tail_fill(b, wait):
        row0 = pl.multiple_of(b * MOE_BLOCK, MOE_BLOCK)
        fill = pltpu.make_async_copy(zbuf, xs_hbm.at[pl.ds(row0, MOE_BLOCK)], tsem)
        fill.wait() if wait else fill.start()

    @pl.when(i == 0)
    def _():
        lax.fori_loop(n_used, n_blocks, lambda b, c: (tail_fill(b, False), c)[1], 0)

    for k in range(2):
        for r in range(tm):
            pltpu.make_async_copy(h_ref.at[pl.ds(r, 1)], xs_hbm.at[pl.ds(dest_ref[k, r], 1)],
                                  sem).start(priority=r % 2)
    for k in range(2):
        pltpu.make_async_copy(h_ref, xs_hbm.at[pl.ds(0, tm)], sem).wait()

    @pl.when(i == pl.num_programs(0) - 1)
    def _():
        lax.fori_loop(n_used, n_blocks, lambda b, c: (tail_fill(b, True), c)[1], 0)


def moe_dispatch(h, dest, pcounts, pends, *, n_rows, tm):
    t, d = h.shape
    grid_spec = pltpu.PrefetchScalarGridSpec(
        num_scalar_prefetch=2,
        grid=(t // tm,),
        in_specs=[pl.BlockSpec((2, tm), lambda i, pc, pe: (0, i), memory_space=pltpu.SMEM),
                  pl.BlockSpec((tm, d), lambda i, pc, pe: (i, 0))],
        out_specs=pl.BlockSpec(memory_space=pl.ANY),
        scratch_shapes=[pltpu.VMEM((MOE_BLOCK, d), F32), pltpu.SemaphoreType.DMA(()),
                        pltpu.SemaphoreType.DMA(())],
    )
    return pl.pallas_call(
        functools.partial(_dispatch_kernel, n_experts=pcounts.shape[0]),
        grid_spec=grid_spec,
        out_shape=jax.ShapeDtypeStruct((n_rows, d), F32),
        compiler_params=_cparams(("arbitrary",)),
        name="moe_dispatch",
    )(pcounts, pends, dest, h)


def _experts_kernel(first_ref, cnt_ref, nb_ref, xs_hbm, wg_hbm, wu_hbm, wd_hbm, ys_hbm,
                    xbuf, ybuf, wgf, wuf, wdf, wgb, wub, wdb, xsem, ysem, wsem):
    e = pl.program_id(0)
    n_e = pl.num_programs(0)
    nb = nb_ref[0]
    first, cnt = first_ref[e], cnt_ref[e]
    n_blocks = xs_hbm.shape[0] // MOE_BLOCK
    wslot = lax.rem(e, 2)

    def rows(g):
        return pl.ds(pl.multiple_of(g * MOE_BLOCK, MOE_BLOCK), MOE_BLOCK)

    def x_copy(g, s):
        return pltpu.make_async_copy(xs_hbm.at[rows(g)], xbuf.at[s], xsem.at[s])

    def y_copy(g, s):
        return pltpu.make_async_copy(ybuf.at[s], ys_hbm.at[rows(g)], ysem.at[s])

    def w_copies(ex, s):
        half = wd_hbm.shape[1] // 2
        lo, hi = pl.ds(0, half), pl.ds(half, half)
        return ((pltpu.make_async_copy(wg_hbm.at[ex], wgf.at[s], wsem.at[s]), 0),
                (pltpu.make_async_copy(wu_hbm.at[ex], wuf.at[s], wsem.at[s]), 1),
                (pltpu.make_async_copy(wd_hbm.at[ex, lo], wdf.at[s, lo], wsem.at[s]), 0),
                (pltpu.make_async_copy(wd_hbm.at[ex, hi], wdf.at[s, hi], wsem.at[s]), 1))

    @pl.when(e == 0)
    def _():
        for g0 in range(EXPERT_RING - 1):
            @pl.when(g0 < nb)
            def _():
                x_copy(g0, g0).start(priority=BLOCK_DMA_QUEUE)

    @pl.when((e == 0) & (cnt > 0))
    def _():
        for cp, queue in w_copies(0, 0):
            cp.start(priority=queue)

    e_next = jnp.minimum(e + 1, n_e - 1)

    @pl.when((e + 1 < n_e) & (cnt_ref[e_next] > 0))
    def _():
        for cp, queue in w_copies(e_next, 1 - wslot):
            cp.start(priority=queue)

    @pl.when(cnt > 0)
    def _():
        for cp, _ in w_copies(e, wslot):
            cp.wait()
        wgb[...] = wgf[wslot].astype(BF16)
        wub[...] = wuf[wslot].astype(BF16)
        wdb[...] = wdf[wslot].astype(BF16)

    def block(b, carry):
        g = first + b
        s = lax.rem(g, EXPERT_RING)

        @pl.when(g + EXPERT_RING - 1 < nb)
        def _():
            x_copy(g + EXPERT_RING - 1, lax.rem(g + EXPERT_RING - 1, EXPERT_RING)).start(priority=BLOCK_DMA_QUEUE)

        x_copy(g, s).wait()
        x = xbuf[s].astype(BF16)
        gte = jnp.dot(x, wgb[...], preferred_element_type=F32)
        up = jnp.dot(x, wub[...], preferred_element_type=F32)
        act = (gte * _sigmoid(gte) * up).astype(BF16)
        y = jnp.dot(act, wdb[...], preferred_element_type=F32)

        @pl.when(g >= EXPERT_RING)
        def _():
            y_copy(g - EXPERT_RING, s).wait()

        ybuf[s] = y
        y_copy(g, s).start()
        return carry

    lax.fori_loop(0, cnt, block, 0)

    @pl.when(e == pl.num_programs(0) - 1)
    def _():
        for back in range(1, EXPERT_RING + 1):
            @pl.when(nb >= back)
            def _():
                y_copy(nb - back, lax.rem(nb - back, EXPERT_RING)).wait()

        ybuf[0] = jnp.zeros(ybuf.shape[1:], F32)
        lax.fori_loop(nb, n_blocks, lambda g, c: (y_copy(g, 0).start(), c)[1], 0)
        lax.fori_loop(nb, n_blocks, lambda g, c: (y_copy(g, 0).wait(), c)[1], 0)


def experts(xs, w_gate, w_up, w_down, first_blk, n_blk, n_used):
    n_rows, d = xs.shape
    n_experts, _, de = w_gate.shape
    grid_spec = pltpu.PrefetchScalarGridSpec(
        num_scalar_prefetch=3,
        grid=(n_experts,),
        in_specs=[pl.BlockSpec(memory_space=pl.ANY)] * 4,
        out_specs=pl.BlockSpec(memory_space=pl.ANY),
        scratch_shapes=[pltpu.VMEM((EXPERT_RING, MOE_BLOCK, d), F32), pltpu.VMEM((EXPERT_RING, MOE_BLOCK, d), F32),
                        pltpu.VMEM((2, d, de), F32), pltpu.VMEM((2, d, de), F32), pltpu.VMEM((2, de, d), F32),
                        pltpu.VMEM((d, de), BF16), pltpu.VMEM((d, de), BF16), pltpu.VMEM((de, d), BF16),
                        pltpu.SemaphoreType.DMA((EXPERT_RING,)), pltpu.SemaphoreType.DMA((EXPERT_RING,)),
                        pltpu.SemaphoreType.DMA((2,))],
    )
    return pl.pallas_call(
        _experts_kernel,
        grid_spec=grid_spec,
        out_shape=jax.ShapeDtypeStruct((n_rows, d), F32),
        compiler_params=_cparams(("arbitrary",)),
        name="moe_experts",
    )(first_blk, n_blk, n_used, xs, w_gate, w_up, w_down)


def _moe_combine_kernel(dest_ref, destn_ref, x_ref, info_ref, g_ref, ys_hbm, o_ref, ybuf, sem, *, norm):
    i = pl.program_id(0)
    n = pl.num_programs(0)
    tm = x_ref.shape[0]
    slot = lax.rem(i, 2)

    def gather_start(dref, s):
        for k in range(2):
            for r in range(tm):
                pltpu.make_async_copy(ys_hbm.at[pl.ds(dref[k, r], 1)], ybuf.at[s, k, pl.ds(r, 1)],
                                      sem.at[s]).start(priority=r % 2)

    @pl.when(i == 0)
    def _():
        gather_start(dest_ref, 0)

    @pl.when(i + 1 < n)
    def _():
        gather_start(destn_ref, 1 - slot)

    for k in range(2):
        pltpu.make_async_copy(ys_hbm.at[pl.ds(0, tm)], ybuf.at[slot, k], sem.at[slot]).wait()
    w1 = info_ref[:, 4:5]
    w2 = info_ref[:, 5:6]
    s = x_ref[...] + (w1 * ybuf[slot, 0] + w2 * ybuf[slot, 1])
    o_ref[...] = _rms(s, g_ref[...]) if norm else s


def moe_combine(x, ys, dest, info, g, *, norm, tm):
    t, d = x.shape
    n = t // tm
    return pl.pallas_call(
        functools.partial(_moe_combine_kernel, norm=norm),
        grid=(n,),
        in_specs=[pl.BlockSpec((2, tm), lambda i: (0, i), memory_space=pltpu.SMEM),
                  pl.BlockSpec((2, tm), lambda i: (0, jnp.minimum(i + 1, n - 1)), memory_space=pltpu.SMEM),
                  pl.BlockSpec((tm, d), lambda i: (i, 0)),
                  pl.BlockSpec((tm, LANES), lambda i: (i, 0)),
                  pl.BlockSpec((1, d), lambda i: (0, 0)),
                  pl.BlockSpec(memory_space=pl.ANY)],
        out_specs=pl.BlockSpec((tm, d), lambda i: (i, 0)),
        out_shape=jax.ShapeDtypeStruct((t, d), F32),
        scratch_shapes=[pltpu.VMEM((2, 2, tm, d), F32), pltpu.SemaphoreType.DMA((2,))],
        compiler_params=_cparams(("arbitrary",)),
        name="moe_combine",
    )(dest, dest, x, info, g.reshape(1, d), ys)


def forward(x, mem, norm_mix_g, w_in, s5_lambda_re, s5_lambda_im, s5_log_dt, s5_b_re, s5_b_im, s5_c_re,
            s5_c_im, s5_d, s5_w_glu, conv_dw_w, conv_dw_b, conv_ln_g, conv_ln_b, conv_w_out, w_out,
            norm_xattn_g, norm_mem_g, xattn_wq, xattn_wk, xattn_wv, xattn_wo, norm_moe_g, router_w_group,
            router_b_group, router_w_expert, router_b_expert, exp_w_gate, exp_w_up, exp_w_down, norm_final_g,
            *, tiles):
    batch, seq, d = x.shape
    depth = w_in.shape[0]
    n_mem = mem.shape[1]
    s5_width = s5_d.shape[1]
    conv_width = conv_dw_b.shape[1]
    n_groups, epg = router_w_expert.shape[1], router_w_expert.shape[3]
    t = batch * seq
    xf = x.reshape(t, d)
    memf = mem.reshape(batch * n_mem, d)
    gate_col0 = s5_width + 2 * conv_width
    for l in range(depth):
        w_in_b = w_in[l].astype(BF16)
        proj = norm_mm_rows(xf, norm_mix_g[l], w_in_b, tm=tiles["proj_tm"], tn=tiles["proj_tn"],
                            out_dtype=F32, n_cols=gate_col0)
        lbr, lbi, bbr, bbi = s5_discretise(s5_lambda_re[l], s5_lambda_im[l], s5_log_dt[l],
                                           s5_b_re[l], s5_b_im[l])
        bre, bim, cre, cim = s5_block_diag(bbr, bbi, s5_c_re[l], s5_c_im[l])
        y_a = s5_branch(proj, bre, bim, cre, cim, lbr.reshape(1, -1), lbi.reshape(1, -1), s5_d[l],
                        s5_w_glu[l].astype(BF16), batch=batch, seq=seq, width=s5_width, m=tiles["s5_m"],
                        cw=tiles["s5_cw"], tn=tiles["glu_tn"])
        y_b = conv_branch(proj, conv_dw_w[l], conv_dw_b[l], conv_ln_g[l], conv_ln_b[l],
                          conv_w_out[l].astype(BF16), batch=batch, seq=seq, width=conv_width,
                          col0=s5_width, tm=tiles["conv_tm"])
        xf = combine(xf, norm_mix_g[l], w_in_b, y_a, y_b, w_out[l].astype(BF16), gate_col0=gate_col0,
                     tm=tiles["comb_tm"])

        q = norm_mm_rows(xf, norm_xattn_g[l], xattn_wq[l].astype(BF16), tm=tiles["q_tm"], tn=tiles["q_tn"],
                         out_dtype=BF16)
        wkv = jnp.concatenate([xattn_wk[l], xattn_wv[l]], axis=1).astype(BF16)
        kv = norm_mm(memf, norm_mem_g[l], wkv, tm=batch * n_mem, tn=tiles["kv_tn"], out_dtype=BF16)
        w_r = jnp.concatenate([router_w_group[l],
                               jnp.transpose(router_w_expert[l], (1, 0, 2)).reshape(d, n_groups * epg)], axis=1)
        b_r = jnp.concatenate([router_b_group[l], router_b_expert[l].reshape(-1)])
        n_r = w_r.shape[1]
        w_pad = jnp.pad(w_r, ((0, 0), (0, LANES - n_r)))
        w_hi = w_pad.astype(BF16)
        w_pad = jnp.concatenate([w_hi, (w_pad - w_hi.astype(F32)).astype(BF16)], axis=1)
        b_pad = jnp.pad(b_r, (0, LANES - n_r)).reshape(1, LANES)
        xf, h, info, infot, counts = xattn_router(
            xf, q, kv, xattn_wo[l].astype(BF16), norm_moe_g[l], w_pad, b_pad, batch=batch, seq=seq,
            n_mem=n_mem, heads=XATTN_HEADS, n_groups=n_groups, epg=epg, tm=tiles["xattn_tm"])
        dest, n_rows, n_used, pcounts, pends = moe_dispatch_plan(infot, counts, t, n_groups * epg)
        xs = moe_dispatch(h, dest, pcounts, pends, n_rows=n_rows, tm=tiles["moe_tm"])
        ys = experts(xs, exp_w_gate[l], exp_w_up[l], exp_w_down[l], (pends - pcounts) // MOE_BLOCK,
                     pcounts // MOE_BLOCK, n_used)
        last = l + 1 == depth
        xf = moe_combine(xf, ys, dest, info, norm_final_g if last else jnp.ones((d,), F32), norm=last,
                         tm=tiles["moe_tm"])
    return xf.reshape(batch, seq, d)


def kernel(x, mem, norm_mix_g, w_in, s5_lambda_re, s5_lambda_im, s5_log_dt, s5_b_re, s5_b_im, s5_c_re, s5_c_im, s5_d, s5_w_glu, conv_dw_w, conv_dw_b, conv_ln_g, conv_ln_b, conv_w_out, w_out, norm_xattn_g, norm_mem_g, xattn_wq, xattn_wk, xattn_wv, xattn_wo, norm_moe_g, router_w_group, router_b_group, router_w_expert, router_b_expert, exp_w_gate, exp_w_up, exp_w_down, norm_final_g):
    return forward(x, mem, norm_mix_g, w_in, s5_lambda_re, s5_lambda_im, s5_log_dt, s5_b_re, s5_b_im,
                   s5_c_re, s5_c_im, s5_d, s5_w_glu, conv_dw_w, conv_dw_b, conv_ln_g, conv_ln_b, conv_w_out,
                   w_out, norm_xattn_g, norm_mem_g, xattn_wq, xattn_wk, xattn_wv, xattn_wo, norm_moe_g,
                   router_w_group, router_b_group, router_w_expert, router_b_expert, exp_w_gate, exp_w_up,
                   exp_w_down, norm_final_g, tiles=TILES)
```

```python
import functools
import math

import jax
import jax.numpy as jnp
from jax import lax
from jax.experimental import pallas as pl
from jax.experimental.pallas import tpu as pltpu

F32 = jnp.float32
BF16 = jnp.bfloat16

RMS_EPS = 1e-6
LN_EPS = 1e-5

S5_SLAB_CH = 256
CONV_HALO = 32
CONV_ROW_CHUNK = 64
XATTN_HEADS = 4
MOE_BLOCK = 128
BLOCK_DMA_QUEUE = 1
EXPERT_X_RING = 8
EXPERT_Y_RING = 4
SUBLANES = 8
LANES = 128
VMEM_LIMIT = 56 * 1024 * 1024

TILES = dict(proj_tm=512, proj_tn=512, s5_m=64, s5_cw=512, glu_tn=512, conv_tm=256,
             comb_tm=256, q_tm=512, q_tn=512, kv_tn=512, xattn_tm=512, moe_tm=256)


def _cparams(sem):
    return pltpu.CompilerParams(dimension_semantics=sem, vmem_limit_bytes=VMEM_LIMIT)


def _rms(x, g):
    return x * lax.rsqrt(jnp.mean(x * x, axis=-1, keepdims=True) + RMS_EPS) * g


def _sigmoid(x):
    return 0.5 * jnp.tanh(0.5 * x) + 0.5


def _gelu_tanh(x):
    c = math.sqrt(2.0 / math.pi)
    return 0.5 * x * (1.0 + jnp.tanh(c * (x + 0.044715 * (x * x * x))))


def _norm_mm_kernel(x_ref, g_ref, w_ref, o_ref, h_ref, *, act_from):
    j = pl.program_id(1)

    @pl.when(j == 0)
    def _():
        h_ref[...] = _rms(x_ref[...], g_ref[...]).astype(BF16)

    y = jnp.dot(h_ref[...], w_ref[...], preferred_element_type=F32)
    if act_from is None:
        o_ref[...] = y.astype(o_ref.dtype)
    else:
        @pl.when(j < act_from)
        def _():
            o_ref[...] = y.astype(o_ref.dtype)

        @pl.when(j >= act_from)
        def _():
            o_ref[...] = _sigmoid(y).astype(o_ref.dtype)


def norm_mm(x, g, w, *, tm, tn, out_dtype=F32, act_from=None):
    m, k = x.shape
    n = w.shape[1]
    return pl.pallas_call(
        functools.partial(_norm_mm_kernel, act_from=act_from),
        grid=(m // tm, n // tn),
        in_specs=[pl.BlockSpec((tm, k), lambda i, j: (i, 0)),
                  pl.BlockSpec((1, k), lambda i, j: (0, 0)),
                  pl.BlockSpec((k, tn), lambda i, j: (0, j))],
        out_specs=pl.BlockSpec((tm, tn), lambda i, j: (i, j)),
        out_shape=jax.ShapeDtypeStruct((m, n), out_dtype),
        scratch_shapes=[pltpu.VMEM((tm, k), BF16)],
        compiler_params=_cparams(("parallel", "arbitrary")),
        name="norm_mm",
    )(x, g.reshape(1, k), w)


def _norm_mm_rows_kernel(x_ref, g_ref, w_ref, o_ref, *, tn):
    h = _rms(x_ref[...], g_ref[...]).astype(BF16)
    for j in range(o_ref.shape[1] // tn):
        cols = slice(j * tn, (j + 1) * tn)
        o_ref[:, cols] = jnp.dot(h, w_ref[:, cols], preferred_element_type=F32).astype(o_ref.dtype)


def norm_mm_rows(x, g, w, *, tm, tn, out_dtype, n_cols=None):
    m, k = x.shape
    n = w.shape[1] if n_cols is None else n_cols
    return pl.pallas_call(
        functools.partial(_norm_mm_rows_kernel, tn=tn),
        grid=(m // tm,),
        in_specs=[pl.BlockSpec((tm, k), lambda i: (i, 0)),
                  pl.BlockSpec((1, k), lambda i: (0, 0)),
                  pl.BlockSpec((k, n), lambda i: (0, 0), pipeline_mode=pl.Buffered(1))],
        out_specs=pl.BlockSpec((tm, n), lambda i: (i, 0)),
        out_shape=jax.ShapeDtypeStruct((m, n), out_dtype),
        compiler_params=_cparams(("parallel",)),
        name="norm_mm_rows",
    )(x, g.reshape(1, k), w)


def _s5_disc_kernel(lr_ref, li_ref, ldt_ref, br_ref, bi_ref, lbr_ref, lbi_ref, bbr_ref, bbi_ref):
    lr, li = lr_ref[...], li_ref[...]
    dt = jnp.exp(ldt_ref[...])
    mag = jnp.exp(lr * dt)
    ang = li * dt
    lb_re, lb_im = mag * jnp.cos(ang), mag * jnp.sin(ang)
    den = lr * lr + li * li
    nr, ni = lb_re - 1.0, lb_im
    coef_re = (nr * lr + ni * li) / den
    coef_im = (ni * lr - nr * li) / den
    lbr_ref[...] = lb_re
    lbi_ref[...] = lb_im
    br, bi = br_ref[...], bi_ref[...]
    bbr_ref[...] = coef_re[None] * br - coef_im[None] * bi
    bbi_ref[...] = coef_re[None] * bi + coef_im[None] * br


def s5_discretise(lam_re, lam_im, log_dt, b_re, b_im):
    g, n = lam_re.shape
    c = b_re.shape[-1]
    b_re_t = jnp.transpose(b_re, (2, 0, 1))
    b_im_t = jnp.transpose(b_im, (2, 0, 1))
    gn = jax.ShapeDtypeStruct((g, n), F32)
    cgn = jax.ShapeDtypeStruct((c, g, n), F32)
    return pl.pallas_call(
        _s5_disc_kernel,
        out_shape=(gn, gn, cgn, cgn),
        name="s5_discretise",
    )(lam_re, lam_im, log_dt.reshape(g, 1), b_re_t, b_im_t)


def s5_block_diag(bb_re, bb_im, c_re, c_im):
    c, g, n = bb_re.shape
    gs = S5_SLAB_CH // c
    n_slab = g // gs
    eye = jnp.eye(gs, dtype=F32)

    def bmat(bb):
        b = jnp.transpose(bb, (1, 0, 2)).reshape(n_slab, gs, c, n)
        full = b[:, :, :, None, :] * eye[None, :, None, :, None]
        return full.reshape(n_slab, gs * c, gs * n).astype(BF16)

    def cmat(cc):
        cm = jnp.transpose(cc.reshape(n_slab, gs, c, n), (0, 1, 3, 2))
        full = cm[:, :, :, None, :] * eye[None, :, None, :, None]
        return full.reshape(n_slab, gs * n, gs * c).astype(BF16)

    return bmat(bb_re), bmat(bb_im), cmat(c_re), cmat(c_im)


def _s5_kernel(proj_hbm, bre_ref, bim_ref, cre_ref, cim_ref, lre_ref, lim_ref, d_ref, wglu_ref, ya_hbm,
               ubuf, yabuf, zbuf, xre_ref, xim_ref, pre_ref, pim_ref, car_re_ref, car_im_ref, isem, osem,
               *, m, cw, tn):
    t = pl.program_id(1)
    n_t = pl.num_programs(1)
    step_id = pl.program_id(0) * n_t + t
    n_steps = pl.num_programs(0) * n_t
    slot = lax.rem(step_id, 2)
    rows = SUBLANES * m
    width = d_ref.shape[1]
    n_state = lre_ref.shape[1]
    n_slab = bre_ref.shape[0]
    slab_states = n_state // n_slab

    def in_copy(step, s, k):
        r0 = pl.multiple_of(step * rows + k * m, SUBLANES)
        return pltpu.make_async_copy(proj_hbm.at[pl.ds(r0, m), pl.ds(0, width)], ubuf.at[s, :, k, :], isem.at[s])

    def out_copy(step, s, k):
        r0 = pl.multiple_of(step * rows + k * m, SUBLANES)
        return pltpu.make_async_copy(yabuf.at[s, :, k, :], ya_hbm.at[pl.ds(r0, m)], osem.at[s])

    @pl.when(step_id == 0)
    def _():
        for k in range(SUBLANES):
            in_copy(0, 0, k).start()

    @pl.when(step_id + 1 < n_steps)
    def _():
        for k in range(SUBLANES):
            in_copy(step_id + 1, 1 - slot, k).start()

    @pl.when(t == 0)
    def _():
        car_re_ref[...] = jnp.zeros_like(car_re_ref)
        car_im_ref[...] = jnp.zeros_like(car_im_ref)
        lr, li = lre_ref[...], lim_ref[...]

        def pw(_, carry):
            pr, pi = carry
            return pr * lr - pi * li, pr * li + pi * lr

        pr, pi = lax.fori_loop(1, m, pw, (lr, li))
        pre_ref[...] = pr
        pim_ref[...] = pi

    @pl.when(step_id == 0)
    def _():
        zbuf[...] = jnp.zeros_like(zbuf)

    @pl.when(step_id >= 3)
    def _():
        for k in range(SUBLANES):
            out_copy(step_id - 3, 1 - slot, k).wait()

    nh = wglu_ref.shape[1] // 2

    def glu_pieces(z, s):
        def piece(j):
            val = jnp.dot(z, wglu_ref[:, j * tn:(j + 1) * tn], preferred_element_type=F32)
            gt = jnp.dot(z, wglu_ref[:, nh + j * tn:nh + (j + 1) * tn], preferred_element_type=F32)
            yabuf[s, :, :, j * tn:(j + 1) * tn] = (val * _sigmoid(gt)).reshape(m, SUBLANES, tn)
        return [functools.partial(piece, j) for j in range(nh // tn)]

    for k in range(SUBLANES):
        in_copy(step_id, slot, k).wait()
    u = ubuf[slot].reshape(rows, width)
    up = u.astype(BF16)

    def b_project(s):
        us = up[:, s * S5_SLAB_CH:(s + 1) * S5_SLAB_CH]
        cols = slice(s * slab_states, (s + 1) * slab_states)
        xre_ref[:, cols] = jnp.dot(us, bre_ref[s], preferred_element_type=F32)
        xim_ref[:, cols] = jnp.dot(us, bim_ref[s], preferred_element_type=F32)

    def c_project(s):
        cols = slice(s * slab_states, (s + 1) * slab_states)
        return (jnp.dot(xre_ref[:, cols].astype(BF16), cre_ref[s], preferred_element_type=F32)
                - jnp.dot(xim_ref[:, cols].astype(BF16), cim_ref[s], preferred_element_type=F32))

    n_cb = n_state // cw
    cb_per_slab = slab_states // cw
    glu_prev = glu_pieces(zbuf[...], 1 - slot)
    glu_at = {(i * n_cb) // len(glu_prev): [] for i in range(len(glu_prev))}
    for i, piece in enumerate(glu_prev):
        glu_at[(i * n_cb) // len(glu_prev)].append(piece)
    ys = []
    b_project(0)
    row_id = lax.broadcasted_iota(jnp.int32, (SUBLANES, cw), 0)
    for cb in range(n_cb):
        cols = slice(cb * cw, (cb + 1) * cw)
        slab = cb // cb_per_slab
        if cb % cb_per_slab == 0:
            if slab + 1 < n_slab:
                b_project(slab + 1)
            if slab >= 1:
                ys.append(c_project(slab - 1))
        for piece in glu_at.get(cb, []):
            piece()
        lr = jnp.broadcast_to(lre_ref[:, cols], (SUBLANES, cw))
        li = jnp.broadcast_to(lim_ref[:, cols], (SUBLANES, cw))

        def step(tau, carry, store, cols=cols, lr=lr, li=li):
            sr, si = carry
            r0 = pl.multiple_of(tau * SUBLANES, SUBLANES)
            nr = sr * lr - si * li + xre_ref[pl.ds(r0, SUBLANES), cols]
            ni = sr * li + si * lr + xim_ref[pl.ds(r0, SUBLANES), cols]
            if store:
                xre_ref[pl.ds(r0, SUBLANES), cols] = nr
                xim_ref[pl.ds(r0, SUBLANES), cols] = ni
            return nr, ni

        zero = jnp.zeros((SUBLANES, cw), F32)
        er, ei = lax.fori_loop(0, m, functools.partial(step, store=False), (zero, zero), unroll=True)

        lmr = jnp.broadcast_to(pre_ref[:, cols], (SUBLANES, cw))
        lmi = jnp.broadcast_to(pim_ref[:, cols], (SUBLANES, cw))
        ir = jnp.where(row_id == 0, jnp.broadcast_to(car_re_ref[:, cols], (SUBLANES, cw)), 0.0)
        ii = jnp.where(row_id == 0, jnp.broadcast_to(car_im_ref[:, cols], (SUBLANES, cw)), 0.0)
        for k in range(1, SUBLANES):
            nr = ir * lmr - ii * lmi + er
            ni = ir * lmi + ii * lmr + ei
            ir = jnp.where(row_id == k, pltpu.roll(nr, 1, axis=0), ir)
            ii = jnp.where(row_id == k, pltpu.roll(ni, 1, axis=0), ii)

        fr, fi = lax.fori_loop(0, m, functools.partial(step, store=True), (ir, ii), unroll=True)
        car_re_ref[:, cols] = fr[SUBLANES - 1:SUBLANES, :]
        car_im_ref[:, cols] = fi[SUBLANES - 1:SUBLANES, :]

    ys.append(c_project(n_slab - 1))
    y = jnp.concatenate(ys, axis=1) + d_ref[...] * u
    zbuf[...] = _gelu_tanh(y).astype(BF16)

    @pl.when(step_id >= 1)
    def _():
        for k in range(SUBLANES):
            out_copy(step_id - 1, 1 - slot, k).start()

    @pl.when(step_id == n_steps - 1)
    def _():
        @pl.when(step_id >= 2)
        def _():
            for k in range(SUBLANES):
                out_copy(step_id - 2, slot, k).wait()
        for piece in glu_pieces(zbuf[...], slot):
            piece()
        for k in range(SUBLANES):
            out_copy(step_id, slot, k).start()
        for k in range(SUBLANES):
            out_copy(step_id, slot, k).wait()

        @pl.when(step_id >= 1)
        def _():
            for k in range(SUBLANES):
                out_copy(step_id - 1, 1 - slot, k).wait()


def s5_branch(proj, bre, bim, cre, cim, lre, lim, d_skip, w_glu, *, batch, seq, width, m, cw, tn):
    rows = SUBLANES * m
    n_t = seq // rows
    n_state = lre.shape[1]
    nh = w_glu.shape[1] // 2
    once = pl.Buffered(1)
    const3 = lambda b, t: (0, 0, 0)
    const2 = lambda b, t: (0, 0)
    return pl.pallas_call(
        functools.partial(_s5_kernel, m=m, cw=cw, tn=tn),
        grid=(batch, n_t),
        in_specs=[pl.BlockSpec(memory_space=pl.ANY),
                  pl.BlockSpec(bre.shape, const3, pipeline_mode=once),
                  pl.BlockSpec(bim.shape, const3, pipeline_mode=once),
                  pl.BlockSpec(cre.shape, const3, pipeline_mode=once),
                  pl.BlockSpec(cim.shape, const3, pipeline_mode=once),
                  pl.BlockSpec((1, n_state), const2),
                  pl.BlockSpec((1, n_state), const2),
                  pl.BlockSpec((1, width), const2),
                  pl.BlockSpec(w_glu.shape, const2, pipeline_mode=once)],
        out_specs=pl.BlockSpec(memory_space=pl.ANY),
        out_shape=jax.ShapeDtypeStruct((batch * seq, nh), F32),
        scratch_shapes=[pltpu.VMEM((2, m, SUBLANES, width), F32), pltpu.VMEM((2, m, SUBLANES, nh), F32),
                        pltpu.VMEM((rows, width), BF16),
                        pltpu.VMEM((rows, n_state), F32), pltpu.VMEM((rows, n_state), F32),
                        pltpu.VMEM((1, n_state), F32), pltpu.VMEM((1, n_state), F32),
                        pltpu.VMEM((1, n_state), F32), pltpu.VMEM((1, n_state), F32),
                        pltpu.SemaphoreType.DMA((2,)), pltpu.SemaphoreType.DMA((2,))],
        compiler_params=_cparams(("arbitrary", "arbitrary")),
        name="s5_branch",
    )(proj, bre, bim, cre, cim, lre, lim, d_skip.reshape(1, width), w_glu)


def _conv_kernel(a_ref, b_ref, ah_ref, bh_ref, dww_ref, dwb_ref, lng_ref, lnb_ref, w_ref, o_ref,
                 zs_ref, wb_ref, yc_ref, *, tm, taps):
    i = pl.program_id(1)
    off0 = CONV_HALO - (taps - 1)
    width = a_ref.shape[1]
    n_slab = width // LANES
    zh = jnp.where(i == 0, 0.0, ah_ref[...] * _sigmoid(bh_ref[...]))
    z = a_ref[...] * _sigmoid(b_ref[...])
    for s in range(n_slab):
        zs_ref[s, 0:CONV_HALO, :] = zh[:, s * LANES:(s + 1) * LANES]
        zs_ref[s, CONV_HALO:, :] = z[:, s * LANES:(s + 1) * LANES]

    @pl.when(i == 0)
    def _():
        for j in range(taps):
            wb_ref[j] = jnp.broadcast_to(dww_ref[j:j + 1, :], (SUBLANES, width))
        wb_ref[taps] = jnp.broadcast_to(dwb_ref[...], (SUBLANES, width))

    rg = CONV_ROW_CHUNK // SUBLANES

    def chunk(c, carry):
        r0 = pl.multiple_of(c * CONV_ROW_CHUNK, CONV_ROW_CHUNK)
        for s in range(n_slab):
            lanes = slice(s * LANES, (s + 1) * LANES)
            bias = wb_ref[taps, :, lanes]
            acc = [bias] * rg
            for j in range(taps):
                w = wb_ref[j, :, lanes]
                for g in range(rg):
                    acc[g] = acc[g] + w * zs_ref[s, pl.ds(r0 + g * SUBLANES + off0 + j, SUBLANES), :]
            for g in range(rg):
                yc_ref[pl.ds(pl.multiple_of(r0 + g * SUBLANES, SUBLANES), SUBLANES), lanes] = acc[g]
        return carry

    lax.fori_loop(0, tm // CONV_ROW_CHUNK, chunk, 0)
    y = yc_ref[...]
    mu = jnp.mean(y, axis=-1, keepdims=True)
    var = jnp.mean(jnp.square(y - mu), axis=-1, keepdims=True)
    y = (y - mu) * lax.rsqrt(var + LN_EPS) * lng_ref[...] + lnb_ref[...]
    y = y * _sigmoid(y)
    o_ref[...] = jnp.dot(y.astype(BF16), w_ref[...], preferred_element_type=F32)


def conv_branch(proj, dw_w, dw_b, ln_g, ln_b, w_out, *, batch, seq, width, col0, tm):
    n_t = seq // tm
    ca, cb = col0 // width, col0 // width + 1
    hb = tm // CONV_HALO
    taps = dw_w.shape[0]
    d_out = w_out.shape[1]
    const2 = lambda b, i: (0, 0)
    halo = lambda b, i: jnp.maximum((b * n_t + i) * hb - 1, 0)
    return pl.pallas_call(
        functools.partial(_conv_kernel, tm=tm, taps=taps),
        grid=(batch, n_t),
        in_specs=[pl.BlockSpec((tm, width), lambda b, i: (b * n_t + i, ca)),
                  pl.BlockSpec((tm, width), lambda b, i: (b * n_t + i, cb)),
                  pl.BlockSpec((CONV_HALO, width), lambda b, i: (halo(b, i), ca)),
                  pl.BlockSpec((CONV_HALO, width), lambda b, i: (halo(b, i), cb)),
                  pl.BlockSpec((taps, width), const2),
                  pl.BlockSpec((1, width), const2),
                  pl.BlockSpec((1, width), const2),
                  pl.BlockSpec((1, width), const2),
                  pl.BlockSpec((width, d_out), const2)],
        out_specs=pl.BlockSpec((tm, d_out), lambda b, i: (b * n_t + i, 0)),
        out_shape=jax.ShapeDtypeStruct((batch * seq, d_out), F32),
        scratch_shapes=[pltpu.VMEM((width // LANES, tm + CONV_HALO, LANES), F32),
                        pltpu.VMEM((taps + 1, SUBLANES, width), F32),
                        pltpu.VMEM((tm, width), F32)],
        compiler_params=_cparams(("parallel", "arbitrary")),
        name="conv_branch",
    )(proj, proj, proj, proj, dw_w, dw_b.reshape(1, width), ln_g.reshape(1, width),
      ln_b.reshape(1, width), w_out)


def _combine_kernel(*refs, parts):
    x_ref, g_ref = refs[0], refs[1]
    ya_ref, yb_ref, wo_ref, o_ref = refs[2 + 2 * parts], refs[3 + 2 * parts], refs[-2], refs[-1]
    gw = ya_ref.shape[1] // parts
    x = x_ref[...]
    h = _rms(x, g_ref[...]).astype(BF16)
    o_ref[...] = x
    for p in range(parts):
        cols = slice(p * gw, (p + 1) * gw)
        ga = _sigmoid(jnp.dot(h, refs[2 + p][...], preferred_element_type=F32))
        gb = _sigmoid(jnp.dot(h, refs[2 + parts + p][...], preferred_element_type=F32))
        mix = ga * ya_ref[:, cols] + gb * yb_ref[:, cols]
        o_ref[...] += jnp.dot(mix.astype(BF16), wo_ref[cols, :], preferred_element_type=F32)


def combine(x, g, w_in, y_a, y_b, w_out, *, gate_col0, tm):
    m, d = x.shape
    gw = math.gcd(gate_col0, d)
    parts = d // gw
    row = lambda i: (i, 0)
    once = pl.Buffered(1)
    gate_specs = [pl.BlockSpec((d, gw), functools.partial(lambda i, c: (0, c), c=(gate_col0 + s * d) // gw + p),
                               pipeline_mode=once) for s in range(2) for p in range(parts)]
    return pl.pallas_call(
        functools.partial(_combine_kernel, parts=parts),
        grid=(m // tm,),
        in_specs=[pl.BlockSpec((tm, d), row), pl.BlockSpec((1, d), lambda i: (0, 0))] + gate_specs +
                 [pl.BlockSpec((tm, d), row),
                  pl.BlockSpec((tm, d), row),
                  pl.BlockSpec((d, d), lambda i: (0, 0), pipeline_mode=once)],
        out_specs=pl.BlockSpec((tm, d), row),
        out_shape=jax.ShapeDtypeStruct((m, d), F32),
        compiler_params=_cparams(("parallel",)),
        name="gated_combine",
    )(x, g.reshape(1, d), *([w_in] * (2 * parts)), y_a, y_b, w_out)


def _route_tile(x, g_ref, w_ref, b_ref, tri_ref, h_ref, info_ref, infot_ref, cnt_ref, run_ref, *,
                n_groups, epg):
    tm = x.shape[0]
    h = _rms(x, g_ref[...])
    half = h.shape[1] // 2
    bits = pltpu.bitcast(h.astype(BF16).astype(F32), jnp.uint32)
    h_ref[...] = (bits[:, :half] >> 16) | (bits[:, half:] & jnp.uint32(0xFFFF0000))
    h_hi = h.astype(BF16)
    h_lo = (h - h_hi.astype(F32)).astype(BF16)
    p = jnp.dot(h_hi, w_ref[...], preferred_element_type=F32)
    logits = (p[:, :LANES] + p[:, LANES:] + jnp.dot(h_lo, w_ref[:, :LANES], preferred_element_type=F32)
              + b_ref[...])
    lane = lax.broadcasted_iota(jnp.int32, (tm, LANES), 1).astype(F32)
    neg = jnp.float32(-jnp.inf)
    big = jnp.float32(LANES)
    is_grp = lane < n_groups
    cl = jnp.where(is_grp, logits, neg)
    cmax = jnp.max(cl, axis=-1, keepdims=True)
    g_idx = jnp.min(jnp.where(cl == cmax, lane, big), axis=-1, keepdims=True)
    p_sel = 1.0 / jnp.sum(jnp.where(is_grp, jnp.exp(cl - cmax), 0.0), axis=-1, keepdims=True)
    lo = n_groups + g_idx * epg
    in_grp = (lane >= lo) & (lane < lo + epg)
    fl = jnp.where(in_grp, logits, neg)
    v1 = jnp.max(fl, axis=-1, keepdims=True)
    i1 = jnp.min(jnp.where(fl == v1, lane, big), axis=-1, keepdims=True)
    fl2 = jnp.where(lane == i1, neg, fl)
    v2 = jnp.max(fl2, axis=-1, keepdims=True)
    i2 = jnp.min(jnp.where(fl2 == v2, lane, big), axis=-1, keepdims=True)
    e2x = jnp.exp(v2 - v1)
    w1 = p_sel / (1.0 + e2x)
    w2 = p_sel * e2x / (1.0 + e2x)
    e1 = i1 - n_groups
    e2 = i2 - n_groups
    oh1 = lane == e1
    oh2 = lane == e2
    both = jnp.where(oh1 | oh2, 1.0, 0.0)
    before = jnp.dot(tri_ref[...], both.astype(BF16), preferred_element_type=F32) + run_ref[...]
    r1 = jnp.sum(jnp.where(oh1, before, 0.0), axis=-1, keepdims=True)
    r2 = jnp.sum(jnp.where(oh2, before, 0.0), axis=-1, keepdims=True)
    run_ref[...] += jnp.sum(both, axis=0, keepdims=True)
    cnt_ref[...] = run_ref[...]
    info = jnp.where(lane == 0, e1, 0.0)
    info = jnp.where(lane == 1, e2, info)
    info = jnp.where(lane == 2, r1, info)
    info = jnp.where(lane == 3, r2, info)
    info = jnp.where(lane == 4, w1, info)
    info = jnp.where(lane == 5, w2, info)
    info_ref[...] = info
    infot_ref[...] = jnp.transpose(info)[0:SUBLANES, :]


def _xattn_router_kernel(x_ref, q_ref, kv_ref, wo_ref, g_ref, w_ref, b_ref, tri_ref,
                         o_ref, h_ref, info_ref, infot_ref, cnt_ref, run_ref, *, heads, scale, n_groups, epg):
    @pl.when((pl.program_id(0) == 0) & (pl.program_id(1) == 0))
    def _():
        run_ref[...] = jnp.zeros_like(run_ref)

    d = x_ref.shape[1]
    hd = d // heads
    acc = x_ref[...]
    for h in range(heads):
        q = q_ref[:, h * hd:(h + 1) * hd]
        k = kv_ref[:, h * hd:(h + 1) * hd]
        v = kv_ref[:, d + h * hd:d + (h + 1) * hd]
        s = lax.dot_general(q, k, (((1,), (1,)), ((), ())), preferred_element_type=F32) * scale
        s = s - jnp.max(s, axis=-1, keepdims=True)
        p = jnp.exp(s)
        p = p / jnp.sum(p, axis=-1, keepdims=True)
        o = jnp.dot(p.astype(BF16), v, preferred_element_type=F32)
        acc = acc + jnp.dot(o.astype(BF16), wo_ref[h * hd:(h + 1) * hd, :], preferred_element_type=F32)
    o_ref[...] = acc
    _route_tile(acc, g_ref, w_ref, b_ref, tri_ref, h_ref, info_ref, infot_ref, cnt_ref, run_ref,
                n_groups=n_groups, epg=epg)


def xattn_router(x, q, kv, wo, g, w_pad, b_pad, *, batch, seq, n_mem, heads, n_groups, epg, tm):
    d = x.shape[1]
    m = batch * seq
    n_t = seq // tm
    scale = 1.0 / math.sqrt(d // heads)
    tri = (jnp.arange(tm)[:, None] > jnp.arange(tm)[None, :]).astype(BF16)
    row = lambda b, i: (b * n_t + i, 0)
    const = lambda b, i: (0, 0)
    once = pl.Buffered(1)
    return pl.pallas_call(
        functools.partial(_xattn_router_kernel, heads=heads, scale=scale, n_groups=n_groups, epg=epg),
        grid=(batch, n_t),
        in_specs=[pl.BlockSpec((tm, d), row),
                  pl.BlockSpec((tm, d), row),
                  pl.BlockSpec((n_mem, 2 * d), lambda b, i: (b, 0)),
                  pl.BlockSpec((d, d), const, pipeline_mode=once),
                  pl.BlockSpec((1, d), const),
                  pl.BlockSpec((d, 2 * LANES), const, pipeline_mode=once),
                  pl.BlockSpec((1, LANES), const),
                  pl.BlockSpec((tm, tm), const, pipeline_mode=once)],
        out_specs=[pl.BlockSpec((tm, d), row),
                   pl.BlockSpec((tm, d // 2), row),
                   pl.BlockSpec((tm, LANES), row),
                   pl.BlockSpec((SUBLANES, tm), lambda b, i: (0, b * n_t + i)),
                   pl.BlockSpec((1, LANES), const)],
        out_shape=(jax.ShapeDtypeStruct((m, d), F32),
                   jax.ShapeDtypeStruct((m, d // 2), jnp.uint32),
                   jax.ShapeDtypeStruct((m, LANES), F32),
                   jax.ShapeDtypeStruct((SUBLANES, m), F32),
                   jax.ShapeDtypeStruct((1, LANES), F32)),
        scratch_shapes=[pltpu.VMEM((1, LANES), F32)],
        compiler_params=_cparams(("arbitrary", "arbitrary")),
        name="xattn_router",
    )(x, q, kv, wo, g.reshape(1, d), w_pad, b_pad, tri)


def moe_dispatch_plan(infot, counts, n_tokens, n_experts):
    e = infot[0:2].astype(jnp.int32)
    rank = infot[2:4].astype(jnp.int32)
    counts = counts[0, :n_experts].astype(jnp.int32)
    pcounts = ((counts + MOE_BLOCK - 1) // MOE_BLOCK) * MOE_BLOCK
    pends = jnp.cumsum(pcounts)
    pstarts = pends - pcounts
    ids = jnp.arange(n_experts, dtype=jnp.int32)
    dest = rank + jnp.sum(jnp.where(e[:, :, None] == ids, pstarts, 0), axis=-1)
    n_rows = (-(-2 * n_tokens // MOE_BLOCK) + n_experts) * MOE_BLOCK
    n_used = (pends[-1] // MOE_BLOCK).reshape(1)
    return dest, n_rows, n_used, pcounts, pends


def _dispatch_kernel(pc_ref, pe_ref, dest_ref, h_ref, xs_hbm, zbuf, sem, tsem, *, n_experts):
    i = pl.program_id(0)
    tm = h_ref.shape[0]

    @pl.when(i == 0)
    def _():
        zbuf[...] = jnp.zeros_like(zbuf)
        for wait in (False, True):
            for e in range(n_experts):
                @pl.when(pc_ref[e] > 0)
                def _():
                    row0 = pl.multiple_of(pe_ref[e] - MOE_BLOCK, MOE_BLOCK)
                    fill = pltpu.make_async_copy(zbuf, xs_hbm.at[pl.ds(row0, MOE_BLOCK)], sem)
                    fill.wait() if wait else fill.start()

    n_blocks = xs_hbm.shape[0] // MOE_BLOCK
    n_used = pe_ref[n_experts - 1] // MOE_BLOCK

    def tail_fill(b, wait):
        row0 = pl.multiple_of(b * MOE_BLOCK, MOE_BLOCK)
        fill = pltpu.make_async_copy(zbuf, xs_hbm.at[pl.ds(row0, MOE_BLOCK)], tsem)
        fill.wait() if wait else fill.start()

    @pl.when(i == 0)
    def _():
        lax.fori_loop(n_used, n_blocks, lambda b, c: (tail_fill(b, False), c)[1], 0)

    for k in range(2):
        for r in range(tm):
            pltpu.make_async_copy(h_ref.at[pl.ds(r, 1)], xs_hbm.at[pl.ds(dest_ref[k, r], 1)],
                                  sem).start(priority=r % 2)
    for k in range(2):
        pltpu.make_async_copy(h_ref, xs_hbm.at[pl.ds(0, tm)], sem).wait()

    @pl.when(i == pl.num_programs(0) - 1)
    def _():
        lax.fori_loop(n_used, n_blocks, lambda b, c: (tail_fill(b, True), c)[1], 0)


def moe_dispatch(h, dest, pcounts, pends, *, n_rows, tm):
    t, d = h.shape
    grid_spec = pltpu.PrefetchScalarGridSpec(
        num_scalar_prefetch=2,
        grid=(t // tm,),
        in_specs=[pl.BlockSpec((2, tm), lambda i, pc, pe: (0, i), memory_space=pltpu.SMEM),
                  pl.BlockSpec((tm, d), lambda i, pc, pe: (i, 0))],
        out_specs=pl.BlockSpec(memory_space=pl.ANY),
        scratch_shapes=[pltpu.VMEM((MOE_BLOCK, d), h.dtype), pltpu.SemaphoreType.DMA(()),
                        pltpu.SemaphoreType.DMA(())],
    )
    return pl.pallas_call(
        functools.partial(_dispatch_kernel, n_experts=pcounts.shape[0]),
        grid_spec=grid_spec,
        out_shape=jax.ShapeDtypeStruct((n_rows, d), h.dtype),
        compiler_params=_cparams(("arbitrary",)),
        name="moe_dispatch",
    )(pcounts, pends, dest, h)


def _experts_kernel(first_ref, cnt_ref, nb_ref, xs_hbm, wg_hbm, wu_hbm, wd_hbm, ys_hbm,
                    xbuf, ybuf, wgf, wuf, wdf, wgb, wub, wdb, xsem, ysem, wsem):
    e = pl.program_id(0)
    n_e = pl.num_programs(0)
    nb = nb_ref[0]
    first, cnt = first_ref[e], cnt_ref[e]
    n_blocks = xs_hbm.shape[0] // MOE_BLOCK
    wslot = lax.rem(e, 2)

    def rows(g):
        return pl.ds(pl.multiple_of(g * MOE_BLOCK, MOE_BLOCK), MOE_BLOCK)

    def x_copy(g, s):
        return pltpu.make_async_copy(xs_hbm.at[rows(g)], xbuf.at[s], xsem.at[s])

    def y_copy(g, s):
        return pltpu.make_async_copy(ybuf.at[s], ys_hbm.at[rows(g)], ysem.at[s])

    def w_copies(ex, s):
        half = wd_hbm.shape[1] // 2
        lo, hi = pl.ds(0, half), pl.ds(half, half)
        return ((pltpu.make_async_copy(wg_hbm.at[ex], wgf.at[s], wsem.at[s]), 0),
                (pltpu.make_async_copy(wu_hbm.at[ex], wuf.at[s], wsem.at[s]), 1),
                (pltpu.make_async_copy(wd_hbm.at[ex, lo], wdf.at[s, lo], wsem.at[s]), 0),
                (pltpu.make_async_copy(wd_hbm.at[ex, hi], wdf.at[s, hi], wsem.at[s]), 1))

    @pl.when(e == 0)
    def _():
        for g0 in range(EXPERT_X_RING - 1):
            @pl.when(g0 < nb)
            def _():
                x_copy(g0, g0).start(priority=BLOCK_DMA_QUEUE)

    @pl.when((e == 0) & (cnt > 0))
    def _():
        for cp, queue in w_copies(0, 0):
            cp.start(priority=queue)

    e_next = jnp.minimum(e + 1, n_e - 1)

    @pl.when((e + 1 < n_e) & (cnt_ref[e_next] > 0))
    def _():
        for cp, queue in w_copies(e_next, 1 - wslot):
            cp.start(priority=queue)

    @pl.when(cnt > 0)
    def _():
        for cp, _ in w_copies(e, wslot):
            cp.wait()
        wgb[...] = wgf[wslot].astype(BF16)
        wub[...] = wuf[wslot].astype(BF16)
        wdb[...] = wdf[wslot].astype(BF16)

    def block(b, carry):
        g = first + b
        s = lax.rem(g, EXPERT_X_RING)
        sy = lax.rem(g, EXPERT_Y_RING)
        ahead = g + EXPERT_X_RING - 1

        @pl.when(ahead < nb)
        def _():
            x_copy(ahead, lax.rem(ahead, EXPERT_X_RING)).start(priority=BLOCK_DMA_QUEUE)

        x_copy(g, s).wait()
        packed = xbuf[s]
        half = packed.shape[1]
        x_lo = pltpu.bitcast(packed << 16, F32).astype(BF16)
        x_hi = pltpu.bitcast(packed & jnp.uint32(0xFFFF0000), F32).astype(BF16)
        gte = (jnp.dot(x_lo, wgb[:half, :], preferred_element_type=F32)
               + jnp.dot(x_hi, wgb[half:, :], preferred_element_type=F32))
        up = (jnp.dot(x_lo, wub[:half, :], preferred_element_type=F32)
              + jnp.dot(x_hi, wub[half:, :], preferred_element_type=F32))
        act = (gte * _sigmoid(gte) * up).astype(BF16)
        y = jnp.dot(act, wdb[...], preferred_element_type=F32)

        @pl.when(g >= EXPERT_Y_RING)
        def _():
            y_copy(g - EXPERT_Y_RING, sy).wait()

        ybuf[sy] = y
        y_copy(g, sy).start()
        return carry

    lax.fori_loop(0, cnt, block, 0)

    @pl.when(e == pl.num_programs(0) - 1)
    def _():
        for back in range(1, EXPERT_Y_RING + 1):
            @pl.when(nb >= back)
            def _():
                y_copy(nb - back, lax.rem(nb - back, EXPERT_Y_RING)).wait()

        ybuf[0] = jnp.zeros(ybuf.shape[1:], F32)
        lax.fori_loop(nb, n_blocks, lambda g, c: (y_copy(g, 0).start(), c)[1], 0)
        lax.fori_loop(nb, n_blocks, lambda g, c: (y_copy(g, 0).wait(), c)[1], 0)


def experts(xs, w_gate, w_up, w_down, first_blk, n_blk, n_used):
    n_rows, d_packed = xs.shape
    n_experts, d, de = w_gate.shape
    grid_spec = pltpu.PrefetchScalarGridSpec(
        num_scalar_prefetch=3,
        grid=(n_experts,),
        in_specs=[pl.BlockSpec(memory_space=pl.ANY)] * 4,
        out_specs=pl.BlockSpec(memory_space=pl.ANY),
        scratch_shapes=[pltpu.VMEM((EXPERT_X_RING, MOE_BLOCK, d_packed), xs.dtype),
                        pltpu.VMEM((EXPERT_Y_RING, MOE_BLOCK, d), F32),
                        pltpu.VMEM((2, d, de), F32), pltpu.VMEM((2, d, de), F32), pltpu.VMEM((2, de, d), F32),
                        pltpu.VMEM((d, de), BF16), pltpu.VMEM((d, de), BF16), pltpu.VMEM((de, d), BF16),
                        pltpu.SemaphoreType.DMA((EXPERT_X_RING,)), pltpu.SemaphoreType.DMA((EXPERT_Y_RING,)),
                        pltpu.SemaphoreType.DMA((2,))],
    )
    return pl.pallas_call(
        _experts_kernel,
        grid_spec=grid_spec,
        out_shape=jax.ShapeDtypeStruct((n_rows, d), F32),
        compiler_params=_cparams(("arbitrary",)),
        name="moe_experts",
    )(first_blk, n_blk, n_used, xs, w_gate, w_up, w_down)


def _moe_combine_kernel(dest_ref, destn_ref, x_ref, info_ref, g_ref, ys_hbm, o_ref, ybuf, sem, *, norm):
    i = pl.program_id(0)
    n = pl.num_programs(0)
    tm = x_ref.shape[0]
    slot = lax.rem(i, 2)

    def gather_start(dref, s):
        for k in range(2):
            for r in range(tm):
                pltpu.make_async_copy(ys_hbm.at[pl.ds(dref[k, r], 1)], ybuf.at[s, k, pl.ds(r, 1)],
                                      sem.at[s]).start(priority=r % 2)

    @pl.when(i == 0)
    def _():
        gather_start(dest_ref, 0)

    @pl.when(i + 1 < n)
    def _():
        gather_start(destn_ref, 1 - slot)

    for k in range(2):
        pltpu.make_async_copy(ys_hbm.at[pl.ds(0, tm)], ybuf.at[slot, k], sem.at[slot]).wait()
    w1 = info_ref[:, 4:5]
    w2 = info_ref[:, 5:6]
    s = x_ref[...] + (w1 * ybuf[slot, 0] + w2 * ybuf[slot, 1])
    o_ref[...] = _rms(s, g_ref[...]) if norm else s


def moe_combine(x, ys, dest, info, g, *, norm, tm):
    t, d = x.shape
    n = t // tm
    return pl.pallas_call(
        functools.partial(_moe_combine_kernel, norm=norm),
        grid=(n,),
        in_specs=[pl.BlockSpec((2, tm), lambda i: (0, i), memory_space=pltpu.SMEM),
                  pl.BlockSpec((2, tm), lambda i: (0, jnp.minimum(i + 1, n - 1)), memory_space=pltpu.SMEM),
                  pl.BlockSpec((tm, d), lambda i: (i, 0)),
                  pl.BlockSpec((tm, LANES), lambda i: (i, 0)),
                  pl.BlockSpec((1, d), lambda i: (0, 0)),
                  pl.BlockSpec(memory_space=pl.ANY)],
        out_specs=pl.BlockSpec((tm, d), lambda i: (i, 0)),
        out_shape=jax.ShapeDtypeStruct((t, d), F32),
        scratch_shapes=[pltpu.VMEM((2, 2, tm, d), F32), pltpu.SemaphoreType.DMA((2,))],
        compiler_params=_cparams(("arbitrary",)),
        name="moe_combine",
    )(dest, dest, x, info, g.reshape(1, d), ys)


def forward(x, mem, norm_mix_g, w_in, s5_lambda_re, s5_lambda_im, s5_log_dt, s5_b_re, s5_b_im, s5_c_re,
            s5_c_im, s5_d, s5_w_glu, conv_dw_w, conv_dw_b, conv_ln_g, conv_ln_b, conv_w_out, w_out,
            norm_xattn_g, norm_mem_g, xattn_wq, xattn_wk, xattn_wv, xattn_wo, norm_moe_g, router_w_group,
            router_b_group, router_w_expert, router_b_expert, exp_w_gate, exp_w_up, exp_w_down, norm_final_g,
            *, tiles):
    batch, seq, d = x.shape
    depth = w_in.shape[0]
    n_mem = mem.shape[1]
    s5_width = s5_d.shape[1]
    conv_width = conv_dw_b.shape[1]
    n_groups, epg = router_w_expert.shape[1], router_w_expert.shape[3]
    t = batch * seq
    xf = x.reshape(t, d)
    memf = mem.reshape(batch * n_mem, d)
    gate_col0 = s5_width + 2 * conv_width
    for l in range(depth):
        w_in_b = w_in[l].astype(BF16)
        proj = norm_mm_rows(xf, norm_mix_g[l], w_in_b, tm=tiles["proj_tm"], tn=tiles["proj_tn"],
                            out_dtype=F32, n_cols=gate_col0)
        lbr, lbi, bbr, bbi = s5_discretise(s5_lambda_re[l], s5_lambda_im[l], s5_log_dt[l],
                                           s5_b_re[l], s5_b_im[l])
        bre, bim, cre, cim = s5_block_diag(bbr, bbi, s5_c_re[l], s5_c_im[l])
        y_a = s5_branch(proj, bre, bim, cre, cim, lbr.reshape(1, -1), lbi.reshape(1, -1), s5_d[l],
                        s5_w_glu[l].astype(BF16), batch=batch, seq=seq, width=s5_width, m=tiles["s5_m"],
                        cw=tiles["s5_cw"], tn=tiles["glu_tn"])
        y_b = conv_branch(proj, conv_dw_w[l], conv_dw_b[l], conv_ln_g[l], conv_ln_b[l],
                          conv_w_out[l].astype(BF16), batch=batch, seq=seq, width=conv_width,
                          col0=s5_width, tm=tiles["conv_tm"])
        xf = combine(xf, norm_mix_g[l], w_in_b, y_a, y_b, w_out[l].astype(BF16), gate_col0=gate_col0,
                     tm=tiles["comb_tm"])

        q = norm_mm_rows(xf, norm_xattn_g[l], xattn_wq[l].astype(BF16), tm=tiles["q_tm"], tn=tiles["q_tn"],
                         out_dtype=BF16)
        wkv = jnp.concatenate([xattn_wk[l], xattn_wv[l]], axis=1).astype(BF16)
        kv = norm_mm(memf, norm_mem_g[l], wkv, tm=batch * n_mem, tn=tiles["kv_tn"], out_dtype=BF16)
        w_r = jnp.concatenate([router_w_group[l],
                               jnp.transpose(router_w_expert[l], (1, 0, 2)).reshape(d, n_groups * epg)], axis=1)
        b_r = jnp.concatenate([router_b_group[l], router_b_expert[l].reshape(-1)])
        n_r = w_r.shape[1]
        w_pad = jnp.pad(w_r, ((0, 0), (0, LANES - n_r)))
        w_hi = w_pad.astype(BF16)
        w_pad = jnp.concatenate([w_hi, (w_pad - w_hi.astype(F32)).astype(BF16)], axis=1)
        b_pad = jnp.pad(b_r, (0, LANES - n_r)).reshape(1, LANES)
        xf, h, info, infot, counts = xattn_router(
            xf, q, kv, xattn_wo[l].astype(BF16), norm_moe_g[l], w_pad, b_pad, batch=batch, seq=seq,
            n_mem=n_mem, heads=XATTN_HEADS, n_groups=n_groups, epg=epg, tm=tiles["xattn_tm"])
        dest, n_rows, n_used, pcounts, pends = moe_dispatch_plan(infot, counts, t, n_groups * epg)
        xs = moe_dispatch(h, dest, pcounts, pends, n_rows=n_rows, tm=tiles["moe_tm"])
        ys = experts(xs, exp_w_gate[l], exp_w_up[l], exp_w_down[l], (pends - pcounts) // MOE_BLOCK,
                     pcounts // MOE_BLOCK, n_used)
        last = l + 1 == depth
        xf = moe_combine(xf, ys, dest, info, norm_final_g if last else jnp.ones((d,), F32), norm=last,
                         tm=tiles["moe_tm"])
    return xf.reshape(batch, seq, d)


def kernel(x, mem, norm_mix_g, w_in, s5_lambda_re, s5_lambda_im, s5_log_dt, s5_b_re, s5_b_im, s5_c_re, s5_c_im, s5_d, s5_w_glu, conv_dw_w, conv_dw_b, conv_ln_g, conv_ln_b, conv_w_out, w_out, norm_xattn_g, norm_mem_g, xattn_wq, xattn_wk, xattn_wv, xattn_wo, norm_moe_g, router_w_group, router_b_group, router_w_expert, router_b_expert, exp_w_gate, exp_w_up, exp_w_down, norm_final_g):
    return forward(x, mem, norm_mix_g, w_in, s5_lambda_re, s5_lambda_im, s5_log_dt, s5_b_re, s5_b_im,
                   s5_c_re, s5_c_im, s5_d, s5_w_glu, conv_dw_w, conv_dw_b, conv_ln_g, conv_ln_b, conv_w_out,
                   w_out, norm_xattn_g, norm_mem_g, xattn_wq, xattn_wk, xattn_wv, xattn_wo, norm_moe_g,
                   router_w_group, router_b_group, router_w_expert, router_b_expert, exp_w_gate, exp_w_up,
                   exp_w_down, norm_final_g, tiles=TILES)
```

```python
import functools
import math

import jax
import jax.numpy as jnp
from jax import lax
from jax.experimental import pallas as pl
from jax.experimental.pallas import tpu as pltpu

F32 = jnp.float32
BF16 = jnp.bfloat16

RMS_EPS = 1e-6
LN_EPS = 1e-5

S5_SLAB_CH = 256
CONV_HALO = 32
CONV_ROW_CHUNK = 64
XATTN_HEADS = 4
MOE_BLOCK = 128
BLOCK_DMA_QUEUE = 1
EXPERT_X_RING = 8
EXPERT_Y_RING = 4
SUBLANES = 8
LANES = 128
VMEM_LIMIT = 56 * 1024 * 1024

TILES = dict(proj_tm=512, proj_tn=512, s5_m=64, s5_cw=512, glu_tn=512, conv_tm=256,
             comb_tm=256, q_tm=512, q_tn=512, kv_tn=512, xattn_tm=512, moe_tm=256)


def _cparams(sem):
    return pltpu.CompilerParams(dimension_semantics=sem, vmem_limit_bytes=VMEM_LIMIT)


def _rms(x, g):
    return x * lax.rsqrt(jnp.mean(x * x, axis=-1, keepdims=True) + RMS_EPS) * g


def _sigmoid(x):
    return 0.5 * jnp.tanh(0.5 * x) + 0.5


def _gelu_tanh(x):
    c = math.sqrt(2.0 / math.pi)
    return 0.5 * x * (1.0 + jnp.tanh(c * (x + 0.044715 * (x * x * x))))


def _norm_mm_kernel(x_ref, g_ref, w_ref, o_ref, h_ref, *, act_from):
    j = pl.program_id(1)

    @pl.when(j == 0)
    def _():
        h_ref[...] = _rms(x_ref[...], g_ref[...]).astype(BF16)

    y = jnp.dot(h_ref[...], w_ref[...], preferred_element_type=F32)
    if act_from is None:
        o_ref[...] = y.astype(o_ref.dtype)
    else:
        @pl.when(j < act_from)
        def _():
            o_ref[...] = y.astype(o_ref.dtype)

        @pl.when(j >= act_from)
        def _():
            o_ref[...] = _sigmoid(y).astype(o_ref.dtype)


def norm_mm(x, g, w, *, tm, tn, out_dtype=F32, act_from=None):
    m, k = x.shape
    n = w.shape[1]
    return pl.pallas_call(
        functools.partial(_norm_mm_kernel, act_from=act_from),
        grid=(m // tm, n // tn),
        in_specs=[pl.BlockSpec((tm, k), lambda i, j: (i, 0)),
                  pl.BlockSpec((1, k), lambda i, j: (0, 0)),
                  pl.BlockSpec((k, tn), lambda i, j: (0, j))],
        out_specs=pl.BlockSpec((tm, tn), lambda i, j: (i, j)),
        out_shape=jax.ShapeDtypeStruct((m, n), out_dtype),
        scratch_shapes=[pltpu.VMEM((tm, k), BF16)],
        compiler_params=_cparams(("parallel", "arbitrary")),
        name="norm_mm",
    )(x, g.reshape(1, k), w)


def _norm_mm_rows_kernel(x_ref, g_ref, w_ref, o_ref, *, tn):
    h = _rms(x_ref[...], g_ref[...]).astype(BF16)
    for j in range(o_ref.shape[1] // tn):
        cols = slice(j * tn, (j + 1) * tn)
        o_ref[:, cols] = jnp.dot(h, w_ref[:, cols], preferred_element_type=F32).astype(o_ref.dtype)


def norm_mm_rows(x, g, w, *, tm, tn, out_dtype, n_cols=None):
    m, k = x.shape
    n = w.shape[1] if n_cols is None else n_cols
    return pl.pallas_call(
        functools.partial(_norm_mm_rows_kernel, tn=tn),
        grid=(m // tm,),
        in_specs=[pl.BlockSpec((tm, k), lambda i: (i, 0)),
                  pl.BlockSpec((1, k), lambda i: (0, 0)),
                  pl.BlockSpec((k, n), lambda i: (0, 0), pipeline_mode=pl.Buffered(1))],
        out_specs=pl.BlockSpec((tm, n), lambda i: (i, 0)),
        out_shape=jax.ShapeDtypeStruct((m, n), out_dtype),
        compiler_params=_cparams(("parallel",)),
        name="norm_mm_rows",
    )(x, g.reshape(1, k), w)


def _s5_disc_kernel(lr_ref, li_ref, ldt_ref, br_ref, bi_ref, lbr_ref, lbi_ref, bbr_ref, bbi_ref):
    lr, li = lr_ref[...], li_ref[...]
    dt = jnp.exp(ldt_ref[...])
    mag = jnp.exp(lr * dt)
    ang = li * dt
    lb_re, lb_im = mag * jnp.cos(ang), mag * jnp.sin(ang)
    den = lr * lr + li * li
    nr, ni = lb_re - 1.0, lb_im
    coef_re = (nr * lr + ni * li) / den
    coef_im = (ni * lr - nr * li) / den
    lbr_ref[...] = lb_re
    lbi_ref[...] = lb_im
    br, bi = br_ref[...], bi_ref[...]
    bbr_ref[...] = coef_re[None] * br - coef_im[None] * bi
    bbi_ref[...] = coef_re[None] * bi + coef_im[None] * br


def s5_discretise(lam_re, lam_im, log_dt, b_re, b_im):
    g, n = lam_re.shape
    c = b_re.shape[-1]
    b_re_t = jnp.transpose(b_re, (2, 0, 1))
    b_im_t = jnp.transpose(b_im, (2, 0, 1))
    gn = jax.ShapeDtypeStruct((g, n), F32)
    cgn = jax.ShapeDtypeStruct((c, g, n), F32)
    return pl.pallas_call(
        _s5_disc_kernel,
        out_shape=(gn, gn, cgn, cgn),
        name="s5_discretise",
    )(lam_re, lam_im, log_dt.reshape(g, 1), b_re_t, b_im_t)


def s5_block_diag(bb_re, bb_im, c_re, c_im):
    c, g, n = bb_re.shape
    gs = S5_SLAB_CH // c
    n_slab = g // gs
    eye = jnp.eye(gs, dtype=F32)

    def bmat(bb):
        b = jnp.transpose(bb, (1, 0, 2)).reshape(n_slab, gs, c, n)
        full = b[:, :, :, None, :] * eye[None, :, None, :, None]
        return full.reshape(n_slab, gs * c, gs * n).astype(BF16)

    def cmat(cc):
        cm = jnp.transpose(cc.reshape(n_slab, gs, c, n), (0, 1, 3, 2))
        full = cm[:, :, :, None, :] * eye[None, :, None, :, None]
        return full.reshape(n_slab, gs * n, gs * c).astype(BF16)

    return bmat(bb_re), bmat(bb_im), cmat(c_re), cmat(c_im)


def _s5_kernel(proj_hbm, bre_ref, bim_ref, cre_ref, cim_ref, lre_ref, lim_ref, d_ref, wglu_ref, ya_hbm,
               ubuf, yabuf, zbuf, xre_ref, xim_ref, pre_ref, pim_ref, car_re_ref, car_im_ref, isem, osem,
               *, m, cw, tn):
    t = pl.program_id(1)
    n_t = pl.num_programs(1)
    step_id = pl.program_id(0) * n_t + t
    n_steps = pl.num_programs(0) * n_t
    slot = lax.rem(step_id, 2)
    rows = SUBLANES * m
    width = d_ref.shape[1]
    n_state = lre_ref.shape[1]
    n_slab = bre_ref.shape[0]
    slab_states = n_state // n_slab

    def in_copy(step, s, k):
        r0 = pl.multiple_of(step * rows + k * m, SUBLANES)
        return pltpu.make_async_copy(proj_hbm.at[pl.ds(r0, m), pl.ds(0, width)], ubuf.at[s, :, k, :], isem.at[s])

    def out_copy(step, s, k):
        r0 = pl.multiple_of(step * rows + k * m, SUBLANES)
        return pltpu.make_async_copy(yabuf.at[s, :, k, :], ya_hbm.at[pl.ds(r0, m)], osem.at[s])

    @pl.when(step_id == 0)
    def _():
        for k in range(SUBLANES):
            in_copy(0, 0, k).start()

    @pl.when(step_id + 1 < n_steps)
    def _():
        for k in range(SUBLANES):
            in_copy(step_id + 1, 1 - slot, k).start()

    @pl.when(t == 0)
    def _():
        car_re_ref[...] = jnp.zeros_like(car_re_ref)
        car_im_ref[...] = jnp.zeros_like(car_im_ref)
        lr, li = lre_ref[...], lim_ref[...]

        def pw(_, carry):
            pr, pi = carry
            return pr * lr - pi * li, pr * li + pi * lr

        pr, pi = lax.fori_loop(1, m, pw, (lr, li))
        pre_ref[...] = pr
        pim_ref[...] = pi

    @pl.when(step_id == 0)
    def _():
        zbuf[...] = jnp.zeros_like(zbuf)

    @pl.when(step_id >= 3)
    def _():
        for k in range(SUBLANES):
            out_copy(step_id - 3, 1 - slot, k).wait()

    nh = wglu_ref.shape[1] // 2

    def glu_pieces(z, s):
        def piece(j):
            val = jnp.dot(z, wglu_ref[:, j * tn:(j + 1) * tn], preferred_element_type=F32)
            gt = jnp.dot(z, wglu_ref[:, nh + j * tn:nh + (j + 1) * tn], preferred_element_type=F32)
            yabuf[s, :, :, j * tn:(j + 1) * tn] = (val * _sigmoid(gt)).reshape(m, SUBLANES, tn)
        return [functools.partial(piece, j) for j in range(nh // tn)]

    for k in range(SUBLANES):
        in_copy(step_id, slot, k).wait()
    u = ubuf[slot].reshape(rows, width)
    up = u.astype(BF16)

    def b_project(s):
        us = up[:, s * S5_SLAB_CH:(s + 1) * S5_SLAB_CH]
        cols = slice(s * slab_states, (s + 1) * slab_states)
        xre_ref[:, cols] = jnp.dot(us, bre_ref[s], preferred_element_type=F32)
        xim_ref[:, cols] = jnp.dot(us, bim_ref[s], preferred_element_type=F32)

    def c_project(s):
        cols = slice(s * slab_states, (s + 1) * slab_states)
        return (jnp.dot(xre_ref[:, cols].astype(BF16), cre_ref[s], preferred_element_type=F32)
                - jnp.dot(xim_ref[:, cols].astype(BF16), cim_ref[s], preferred_element_type=F32))

    n_cb = n_state // cw
    cb_per_slab = slab_states // cw
    glu_prev = glu_pieces(zbuf[...], 1 - slot)
    glu_at = {(i * n_cb) // len(glu_prev): [] for i in range(len(glu_prev))}
    for i, piece in enumerate(glu_prev):
        glu_at[(i * n_cb) // len(glu_prev)].append(piece)
    ys = []
    b_project(0)
    row_id = lax.broadcasted_iota(jnp.int32, (SUBLANES, cw), 0)
    for cb in range(n_cb):
        cols = slice(cb * cw, (cb + 1) * cw)
        slab = cb // cb_per_slab
        if cb % cb_per_slab == 0:
            if slab + 1 < n_slab:
                b_project(slab + 1)
            if slab >= 1:
                ys.append(c_project(slab - 1))
        for piece in glu_at.get(cb, []):
            piece()
        lr = jnp.broadcast_to(lre_ref[:, cols], (SUBLANES, cw))
        li = jnp.broadcast_to(lim_ref[:, cols], (SUBLANES, cw))

        def step(tau, carry, store, cols=cols, lr=lr, li=li):
            sr, si = carry
            r0 = pl.multiple_of(tau * SUBLANES, SUBLANES)
            nr = sr * lr - si * li + xre_ref[pl.ds(r0, SUBLANES), cols]
            ni = sr * li + si * lr + xim_ref[pl.ds(r0, SUBLANES), cols]
            if store:
                xre_ref[pl.ds(r0, SUBLANES), cols] = nr
                xim_ref[pl.ds(r0, SUBLANES), cols] = ni
            return nr, ni

        zero = jnp.zeros((SUBLANES, cw), F32)
        er, ei = lax.fori_loop(0, m, functools.partial(step, store=False), (zero, zero), unroll=True)

        lmr = jnp.broadcast_to(pre_ref[:, cols], (SUBLANES, cw))
        lmi = jnp.broadcast_to(pim_ref[:, cols], (SUBLANES, cw))
        ir = jnp.where(row_id == 0, jnp.broadcast_to(car_re_ref[:, cols], (SUBLANES, cw)), 0.0)
        ii = jnp.where(row_id == 0, jnp.broadcast_to(car_im_ref[:, cols], (SUBLANES, cw)), 0.0)
        for k in range(1, SUBLANES):
            nr = ir * lmr - ii * lmi + er
            ni = ir * lmi + ii * lmr + ei
            ir = jnp.where(row_id == k, pltpu.roll(nr, 1, axis=0), ir)
            ii = jnp.where(row_id == k, pltpu.roll(ni, 1, axis=0), ii)

        fr, fi = lax.fori_loop(0, m, functools.partial(step, store=True), (ir, ii), unroll=True)
        car_re_ref[:, cols] = fr[SUBLANES - 1:SUBLANES, :]
        car_im_ref[:, cols] = fi[SUBLANES - 1:SUBLANES, :]

    ys.append(c_project(n_slab - 1))
    y = jnp.concatenate(ys, axis=1) + d_ref[...] * u
    zbuf[...] = _gelu_tanh(y).astype(BF16)

    @pl.when(step_id >= 1)
    def _():
        for k in range(SUBLANES):
            out_copy(step_id - 1, 1 - slot, k).start()

    @pl.when(step_id == n_steps - 1)
    def _():
        @pl.when(step_id >= 2)
        def _():
            for k in range(SUBLANES):
                out_copy(step_id - 2, slot, k).wait()
        for piece in glu_pieces(zbuf[...], slot):
            piece()
        for k in range(SUBLANES):
            out_copy(step_id, slot, k).start()
        for k in range(SUBLANES):
            out_copy(step_id, slot, k).wait()

        @pl.when(step_id >= 1)
        def _():
            for k in range(SUBLANES):
                out_copy(step_id - 1, 1 - slot, k).wait()


def s5_branch(proj, bre, bim, cre, cim, lre, lim, d_skip, w_glu, *, batch, seq, width, m, cw, tn):
    rows = SUBLANES * m
    n_t = seq // rows
    n_state = lre.shape[1]
    nh = w_glu.shape[1] // 2
    once = pl.Buffered(1)
    const3 = lambda b, t: (0, 0, 0)
    const2 = lambda b, t: (0, 0)
    return pl.pallas_call(
        functools.partial(_s5_kernel, m=m, cw=cw, tn=tn),
        grid=(batch, n_t),
        in_specs=[pl.BlockSpec(memory_space=pl.ANY),
                  pl.BlockSpec(bre.shape, const3, pipeline_mode=once),
                  pl.BlockSpec(bim.shape, const3, pipeline_mode=once),
                  pl.BlockSpec(cre.shape, const3, pipeline_mode=once),
                  pl.BlockSpec(cim.shape, const3, pipeline_mode=once),
                  pl.BlockSpec((1, n_state), const2),
                  pl.BlockSpec((1, n_state), const2),
                  pl.BlockSpec((1, width), const2),
                  pl.BlockSpec(w_glu.shape, const2, pipeline_mode=once)],
        out_specs=pl.BlockSpec(memory_space=pl.ANY),
        out_shape=jax.ShapeDtypeStruct((batch * seq, nh), F32),
        scratch_shapes=[pltpu.VMEM((2, m, SUBLANES, width), F32), pltpu.VMEM((2, m, SUBLANES, nh), F32),
                        pltpu.VMEM((rows, width), BF16),
                        pltpu.VMEM((rows, n_state), F32), pltpu.VMEM((rows, n_state), F32),
                        pltpu.VMEM((1, n_state), F32), pltpu.VMEM((1, n_state), F32),
                        pltpu.VMEM((1, n_state), F32), pltpu.VMEM((1, n_state), F32),
                        pltpu.SemaphoreType.DMA((2,)), pltpu.SemaphoreType.DMA((2,))],
        compiler_params=_cparams(("arbitrary", "arbitrary")),
        name="s5_branch",
    )(proj, bre, bim, cre, cim, lre, lim, d_skip.reshape(1, width), w_glu)


def _conv_kernel(a_ref, b_ref, ah_ref, bh_ref, dww_ref, dwb_ref, lng_ref, lnb_ref, w_ref, o_ref,
                 zs_ref, wb_ref, yc_ref, *, tm, taps):
    i = pl.program_id(1)
    off0 = CONV_HALO - (taps - 1)
    width = a_ref.shape[1]
    n_slab = width // LANES
    zh = jnp.where(i == 0, 0.0, ah_ref[...] * _sigmoid(bh_ref[...]))
    z = a_ref[...] * _sigmoid(b_ref[...])
    for s in range(n_slab):
        zs_ref[s, 0:CONV_HALO, :] = zh[:, s * LANES:(s + 1) * LANES]
        zs_ref[s, CONV_HALO:, :] = z[:, s * LANES:(s + 1) * LANES]

    @pl.when(i == 0)
    def _():
        for j in range(taps):
            wb_ref[j] = jnp.broadcast_to(dww_ref[j:j + 1, :], (SUBLANES, width))
        wb_ref[taps] = jnp.broadcast_to(dwb_ref[...], (SUBLANES, width))

    rg = CONV_ROW_CHUNK // SUBLANES

    def chunk(c, carry):
        r0 = pl.multiple_of(c * CONV_ROW_CHUNK, CONV_ROW_CHUNK)
        for s in range(n_slab):
            lanes = slice(s * LANES, (s + 1) * LANES)
            bias = wb_ref[taps, :, lanes]
            acc = [bias] * rg
            for j in range(taps):
                w = wb_ref[j, :, lanes]
                for g in range(rg):
                    acc[g] = acc[g] + w * zs_ref[s, pl.ds(r0 + g * SUBLANES + off0 + j, SUBLANES), :]
            for g in range(rg):
                yc_ref[pl.ds(pl.multiple_of(r0 + g * SUBLANES, SUBLANES), SUBLANES), lanes] = acc[g]
        return carry

    lax.fori_loop(0, tm // CONV_ROW_CHUNK, chunk, 0)
    y = yc_ref[...]
    mu = jnp.mean(y, axis=-1, keepdims=True)
    var = jnp.mean(jnp.square(y - mu), axis=-1, keepdims=True)
    y = (y - mu) * lax.rsqrt(var + LN_EPS) * lng_ref[...] + lnb_ref[...]
    y = y * _sigmoid(y)
    o_ref[...] = jnp.dot(y.astype(BF16), w_ref[...], preferred_element_type=F32)


def conv_branch(proj, dw_w, dw_b, ln_g, ln_b, w_out, *, batch, seq, width, col0, tm):
    n_t = seq // tm
    ca, cb = col0 // width, col0 // width + 1
    hb = tm // CONV_HALO
    taps = dw_w.shape[0]
    d_out = w_out.shape[1]
    const2 = lambda b, i: (0, 0)
    halo = lambda b, i: jnp.maximum((b * n_t + i) * hb - 1, 0)
    return pl.pallas_call(
        functools.partial(_conv_kernel, tm=tm, taps=taps),
        grid=(batch, n_t),
        in_specs=[pl.BlockSpec((tm, width), lambda b, i: (b * n_t + i, ca)),
                  pl.BlockSpec((tm, width), lambda b, i: (b * n_t + i, cb)),
                  pl.BlockSpec((CONV_HALO, width), lambda b, i: (halo(b, i), ca)),
                  pl.BlockSpec((CONV_HALO, width), lambda b, i: (halo(b, i), cb)),
                  pl.BlockSpec((taps, width), const2),
                  pl.BlockSpec((1, width), const2),
                  pl.BlockSpec((1, width), const2),
                  pl.BlockSpec((1, width), const2),
                  pl.BlockSpec((width, d_out), const2)],
        out_specs=pl.BlockSpec((tm, d_out), lambda b, i: (b * n_t + i, 0)),
        out_shape=jax.ShapeDtypeStruct((batch * seq, d_out), F32),
        scratch_shapes=[pltpu.VMEM((width // LANES, tm + CONV_HALO, LANES), F32),
                        pltpu.VMEM((taps + 1, SUBLANES, width), F32),
                        pltpu.VMEM((tm, width), F32)],
        compiler_params=_cparams(("parallel", "arbitrary")),
        name="conv_branch",
    )(proj, proj, proj, proj, dw_w, dw_b.reshape(1, width), ln_g.reshape(1, width),
      ln_b.reshape(1, width), w_out)


def _combine_kernel(*refs, parts):
    x_ref, g_ref = refs[0], refs[1]
    ya_ref, yb_ref, wo_ref, o_ref = refs[2 + 2 * parts], refs[3 + 2 * parts], refs[-2], refs[-1]
    gw = ya_ref.shape[1] // parts
    x = x_ref[...]
    h = _rms(x, g_ref[...]).astype(BF16)
    o_ref[...] = x
    for p in range(parts):
        cols = slice(p * gw, (p + 1) * gw)
        ga = _sigmoid(jnp.dot(h, refs[2 + p][...], preferred_element_type=F32))
        gb = _sigmoid(jnp.dot(h, refs[2 + parts + p][...], preferred_element_type=F32))
        mix = ga * ya_ref[:, cols] + gb * yb_ref[:, cols]
        o_ref[...] += jnp.dot(mix.astype(BF16), wo_ref[cols, :], preferred_element_type=F32)


def combine(x, g, w_in, y_a, y_b, w_out, *, gate_col0, tm):
    m, d = x.shape
    gw = math.gcd(gate_col0, d)
    parts = d // gw
    row = lambda i: (i, 0)
    once = pl.Buffered(1)
    gate_specs = [pl.BlockSpec((d, gw), functools.partial(lambda i, c: (0, c), c=(gate_col0 + s * d) // gw + p),
                               pipeline_mode=once) for s in range(2) for p in range(parts)]
    return pl.pallas_call(
        functools.partial(_combine_kernel, parts=parts),
        grid=(m // tm,),
        in_specs=[pl.BlockSpec((tm, d), row), pl.BlockSpec((1, d), lambda i: (0, 0))] + gate_specs +
                 [pl.BlockSpec((tm, d), row),
                  pl.BlockSpec((tm, d), row),
                  pl.BlockSpec((d, d), lambda i: (0, 0), pipeline_mode=once)],
        out_specs=pl.BlockSpec((tm, d), row),
        out_shape=jax.ShapeDtypeStruct((m, d), F32),
        compiler_params=_cparams(("parallel",)),
        name="gated_combine",
    )(x, g.reshape(1, d), *([w_in] * (2 * parts)), y_a, y_b, w_out)


def _route_tile(x, g_ref, w_ref, b_ref, tri_ref, h_ref, info_ref, infot_ref, cnt_ref, run_ref, *,
                n_groups, epg):
    tm = x.shape[0]
    h = _rms(x, g_ref[...])
    half = h.shape[1] // 2
    bits = pltpu.bitcast(h.astype(BF16).astype(F32), jnp.uint32)
    h_ref[...] = (bits[:, :half] >> 16) | (bits[:, half:] & jnp.uint32(0xFFFF0000))
    h_hi = h.astype(BF16)
    h_lo = (h - h_hi.astype(F32)).astype(BF16)
    p = jnp.dot(h_hi, w_ref[...], preferred_element_type=F32)
    logits = (p[:, :LANES] + p[:, LANES:] + jnp.dot(h_lo, w_ref[:, :LANES], preferred_element_type=F32)
              + b_ref[...])
    lane = lax.broadcasted_iota(jnp.int32, (tm, LANES), 1).astype(F32)
    neg = jnp.float32(-jnp.inf)
    big = jnp.float32(LANES)
    is_grp = lane < n_groups
    cl = jnp.where(is_grp, logits, neg)
    cmax = jnp.max(cl, axis=-1, keepdims=True)
    g_idx = jnp.min(jnp.where(cl == cmax, lane, big), axis=-1, keepdims=True)
    p_sel = 1.0 / jnp.sum(jnp.where(is_grp, jnp.exp(cl - cmax), 0.0), axis=-1, keepdims=True)
    lo = n_groups + g_idx * epg
    in_grp = (lane >= lo) & (lane < lo + epg)
    fl = jnp.where(in_grp, logits, neg)
    v1 = jnp.max(fl, axis=-1, keepdims=True)
    i1 = jnp.min(jnp.where(fl == v1, lane, big), axis=-1, keepdims=True)
    fl2 = jnp.where(lane == i1, neg, fl)
    v2 = jnp.max(fl2, axis=-1, keepdims=True)
    i2 = jnp.min(jnp.where(fl2 == v2, lane, big), axis=-1, keepdims=True)
    e2x = jnp.exp(v2 - v1)
    w1 = p_sel / (1.0 + e2x)
    w2 = p_sel * e2x / (1.0 + e2x)
    e1 = i1 - n_groups
    e2 = i2 - n_groups
    oh1 = lane == e1
    oh2 = lane == e2
    both = jnp.where(oh1 | oh2, 1.0, 0.0)
    before = jnp.dot(tri_ref[...], both.astype(BF16), preferred_element_type=F32) + run_ref[...]
    r1 = jnp.sum(jnp.where(oh1, before, 0.0), axis=-1, keepdims=True)
    r2 = jnp.sum(jnp.where(oh2, before, 0.0), axis=-1, keepdims=True)
    run_ref[...] += jnp.sum(both, axis=0, keepdims=True)
    cnt_ref[...] = run_ref[...]
    info = jnp.where(lane == 0, e1, 0.0)
    info = jnp.where(lane == 1, e2, info)
    info = jnp.where(lane == 2, r1, info)
    info = jnp.where(lane == 3, r2, info)
    info = jnp.where(lane == 4, w1, info)
    info = jnp.where(lane == 5, w2, info)
    info_ref[...] = info
    infot_ref[...] = jnp.transpose(info)[0:SUBLANES, :]


def _xattn_router_kernel(x_ref, q_ref, kv_ref, wo_ref, g_ref, w_ref, b_ref, tri_ref,
                         o_ref, h_ref, info_ref, infot_ref, cnt_ref, run_ref, *, heads, scale, n_groups, epg):
    @pl.when((pl.program_id(0) == 0) & (pl.program_id(1) == 0))
    def _():
        run_ref[...] = jnp.zeros_like(run_ref)

    d = x_ref.shape[1]
    hd = d // heads
    acc = x_ref[...]
    for h in range(heads):
        q = q_ref[:, h * hd:(h + 1) * hd]
        k = kv_ref[:, h * hd:(h + 1) * hd]
        v = kv_ref[:, d + h * hd:d + (h + 1) * hd]
        s = lax.dot_general(q, k, (((1,), (1,)), ((), ())), preferred_element_type=F32) * scale
        s = s - jnp.max(s, axis=-1, keepdims=True)
        p = jnp.exp(s)
        p = p / jnp.sum(p, axis=-1, keepdims=True)
        o = jnp.dot(p.astype(BF16), v, preferred_element_type=F32)
        acc = acc + jnp.dot(o.astype(BF16), wo_ref[h * hd:(h + 1) * hd, :], preferred_element_type=F32)
    o_ref[...] = acc
    _route_tile(acc, g_ref, w_ref, b_ref, tri_ref, h_ref, info_ref, infot_ref, cnt_ref, run_ref,
                n_groups=n_groups, epg=epg)


def xattn_router(x, q, kv, wo, g, w_pad, b_pad, *, batch, seq, n_mem, heads, n_groups, epg, tm):
    d = x.shape[1]
    m = batch * seq
    n_t = seq // tm
    scale = 1.0 / math.sqrt(d // heads)
    tri = (jnp.arange(tm)[:, None] > jnp.arange(tm)[None, :]).astype(BF16)
    row = lambda b, i: (b * n_t + i, 0)
    const = lambda b, i: (0, 0)
    once = pl.Buffered(1)
    return pl.pallas_call(
        functools.partial(_xattn_router_kernel, heads=heads, scale=scale, n_groups=n_groups, epg=epg),
        grid=(batch, n_t),
        in_specs=[pl.BlockSpec((tm, d), row),
                  pl.BlockSpec((tm, d), row),
                  pl.BlockSpec((n_mem, 2 * d), lambda b, i: (b, 0)),
                  pl.BlockSpec((d, d), const, pipeline_mode=once),
                  pl.BlockSpec((1, d), const),
                  pl.BlockSpec((d, 2 * LANES), const, pipeline_mode=once),
                  pl.BlockSpec((1, LANES), const),
                  pl.BlockSpec((tm, tm), const, pipeline_mode=once)],
        out_specs=[pl.BlockSpec((tm, d), row),
                   pl.BlockSpec((tm, d // 2), row),
                   pl.BlockSpec((tm, LANES), row),
                   pl.BlockSpec((SUBLANES, tm), lambda b, i: (0, b * n_t + i)),
                   pl.BlockSpec((1, LANES), const)],
        out_shape=(jax.ShapeDtypeStruct((m, d), F32),
                   jax.ShapeDtypeStruct((m, d // 2), jnp.uint32),
                   jax.ShapeDtypeStruct((m, LANES), F32),
                   jax.ShapeDtypeStruct((SUBLANES, m), F32),
                   jax.ShapeDtypeStruct((1, LANES), F32)),
        scratch_shapes=[pltpu.VMEM((1, LANES), F32)],
        compiler_params=_cparams(("arbitrary", "arbitrary")),
        name="xattn_router",
    )(x, q, kv, wo, g.reshape(1, d), w_pad, b_pad, tri)


def moe_dispatch_plan(infot, counts, n_tokens, n_experts):
    e = infot[0:2].astype(jnp.int32)
    rank = infot[2:4].astype(jnp.int32)
    counts = counts[0, :n_experts].astype(jnp.int32)
    pcounts = ((counts + MOE_BLOCK - 1) // MOE_BLOCK) * MOE_BLOCK
    pends = jnp.cumsum(pcounts)
    pstarts = pends - pcounts
    ids = jnp.arange(n_experts, dtype=jnp.int32)
    dest = rank + jnp.sum(jnp.where(e[:, :, None] == ids, pstarts, 0), axis=-1)
    n_rows = (-(-2 * n_tokens // MOE_BLOCK) + n_experts) * MOE_BLOCK
    n_used = (pends[-1] // MOE_BLOCK).reshape(1)
    return dest, n_rows, n_used, pcounts, pends


def _dispatch_kernel(pc_ref, pe_ref, dest_ref, h_ref, xs_hbm, zbuf, sem, tsem, *, n_experts):
    i = pl.program_id(0)
    tm = h_ref.shape[0]

    @pl.when(i == 0)
    def _():
        zbuf[...] = jnp.zeros_like(zbuf)
        for wait in (False, True):
            for e in range(n_experts):
                @pl.when(pc_ref[e] > 0)
                def _():
                    row0 = pl.multiple_of(pe_ref[e] - MOE_BLOCK, MOE_BLOCK)
                    fill = pltpu.make_async_copy(zbuf, xs_hbm.at[pl.ds(row0, MOE_BLOCK)], sem)
                    fill.wait() if wait else fill.start()

    n_blocks = xs_hbm.shape[0] // MOE_BLOCK
    n_used = pe_ref[n_experts - 1] // MOE_BLOCK

    def tail_fill(b, wait):
        row0 = pl.multiple_of(b * MOE_BLOCK, MOE_BLOCK)
        fill = pltpu.make_async_copy(zbuf, xs_hbm.at[pl.ds(row0, MOE_BLOCK)], tsem)
        fill.wait() if wait else fill.start()

    @pl.when(i == 0)
    def _():
        lax.fori_loop(n_used, n_blocks, lambda b, c: (tail_fill(b, False), c)[1], 0)

    for k in range(2):
        for r in range(tm):
            pltpu.make_async_copy(h_ref.at[pl.ds(r, 1)], xs_hbm.at[pl.ds(dest_ref[k, r], 1)],
                                  sem).start(priority=r % 2)
    for k in range(2):
        pltpu.make_async_copy(h_ref, xs_hbm.at[pl.ds(0, tm)], sem).wait()

    @pl.when(i == pl.num_programs(0) - 1)
    def _():
        lax.fori_loop(n_used, n_blocks, lambda b, c: (tail_fill(b, True), c)[1], 0)


def moe_dispatch(h, dest, pcounts, pends, *, n_rows, tm):
    t, d = h.shape
    grid_spec = pltpu.PrefetchScalarGridSpec(
        num_scalar_prefetch=2,
        grid=(t // tm,),
        in_specs=[pl.BlockSpec((2, tm), lambda i, pc, pe: (0, i), memory_space=pltpu.SMEM),
                  pl.BlockSpec((tm, d), lambda i, pc, pe: (i, 0))],
        out_specs=pl.BlockSpec(memory_space=pl.ANY),
        scratch_shapes=[pltpu.VMEM((MOE_BLOCK, d), h.dtype), pltpu.SemaphoreType.DMA(()),
                        pltpu.SemaphoreType.DMA(())],
    )
    return pl.pallas_call(
        functools.partial(_dispatch_kernel, n_experts=pcounts.shape[0]),
        grid_spec=grid_spec,
        out_shape=jax.ShapeDtypeStruct((n_rows, d), h.dtype),
        compiler_params=_cparams(("arbitrary",)),
        name="moe_dispatch",
    )(pcounts, pends, dest, h)


def _experts_kernel(first_ref, cnt_ref, nb_ref, xs_hbm, wg_hbm, wu_hbm, wd_hbm, ys_hbm,
                    xbuf, ybuf, wgf, wuf, wdf, wgb, wub, wdb, xsem, ysem, wsem):
    e = pl.program_id(0)
    n_e = pl.num_programs(0)
    nb = nb_ref[0]
    first, cnt = first_ref[e], cnt_ref[e]
    n_blocks = xs_hbm.shape[0] // MOE_BLOCK
    wslot = lax.rem(e, 2)

    def rows(g):
        return pl.ds(pl.multiple_of(g * MOE_BLOCK, MOE_BLOCK), MOE_BLOCK)

    def x_copy(g, s):
        return pltpu.make_async_copy(xs_hbm.at[rows(g)], xbuf.at[s], xsem.at[s])

    def y_copy(g, s):
        return pltpu.make_async_copy(ybuf.at[s], ys_hbm.at[rows(g)], ysem.at[s])

    def w_copies(ex, s):
        half = wd_hbm.shape[1] // 2
        lo, hi = pl.ds(0, half), pl.ds(half, half)
        return ((pltpu.make_async_copy(wg_hbm.at[ex], wgf.at[s], wsem.at[s]), 0),
                (pltpu.make_async_copy(wu_hbm.at[ex], wuf.at[s], wsem.at[s]), 1),
                (pltpu.make_async_copy(wd_hbm.at[ex, lo], wdf.at[s, lo], wsem.at[s]), 0),
                (pltpu.make_async_copy(wd_hbm.at[ex, hi], wdf.at[s, hi], wsem.at[s]), 1))

    @pl.when(e == 0)
    def _():
        for g0 in range(EXPERT_X_RING - 2):
            @pl.when(g0 < nb)
            def _():
                x_copy(g0, g0).start(priority=BLOCK_DMA_QUEUE)

    @pl.when((e == 0) & (cnt > 0))
    def _():
        for cp, queue in w_copies(0, 0):
            cp.start(priority=queue)

    e_next = jnp.minimum(e + 1, n_e - 1)

    @pl.when((e + 1 < n_e) & (cnt_ref[e_next] > 0))
    def _():
        for cp, queue in w_copies(e_next, 1 - wslot):
            cp.start(priority=queue)

    @pl.when(cnt > 0)
    def _():
        for cp, _ in w_copies(e, wslot):
            cp.wait()
        wgb[...] = wgf[wslot].astype(BF16)
        wub[...] = wuf[wslot].astype(BF16)
        wdb[...] = wdf[wslot].astype(BF16)

    def run_blocks(g, nblk):
        s = lax.rem(g, EXPERT_X_RING)
        sy = lax.rem(g, EXPERT_Y_RING)
        for j in range(nblk):
            ahead = g + j + EXPERT_X_RING - 2

            @pl.when(ahead < nb)
            def _():
                x_copy(ahead, lax.rem(ahead, EXPERT_X_RING)).start(priority=BLOCK_DMA_QUEUE)

        for j in range(nblk):
            x_copy(g + j, s + j).wait()
        half = xbuf.shape[2]
        packed = xbuf[pl.ds(s, nblk)].reshape(nblk * MOE_BLOCK, half)
        x_lo = pltpu.bitcast(packed << 16, F32).astype(BF16)
        x_hi = pltpu.bitcast(packed & jnp.uint32(0xFFFF0000), F32).astype(BF16)
        gte = (jnp.dot(x_lo, wgb[:half, :], preferred_element_type=F32)
               + jnp.dot(x_hi, wgb[half:, :], preferred_element_type=F32))
        up = (jnp.dot(x_lo, wub[:half, :], preferred_element_type=F32)
              + jnp.dot(x_hi, wub[half:, :], preferred_element_type=F32))
        act = (gte * _sigmoid(gte) * up).astype(BF16)
        y = jnp.dot(act, wdb[...], preferred_element_type=F32)

        for j in range(nblk):
            @pl.when(g + j >= EXPERT_Y_RING)
            def _():
                y_copy(g + j - EXPERT_Y_RING, sy + j).wait()

        ybuf[pl.ds(sy, nblk)] = y.reshape(nblk, MOE_BLOCK, y.shape[1])
        for j in range(nblk):
            y_copy(g + j, sy + j).start()

    lead = jnp.where(cnt > 0, first & 1, 0)
    rest = cnt - lead

    @pl.when(lead == 1)
    def _():
        run_blocks(first, 1)

    lax.fori_loop(0, rest // 2, lambda b, c: (run_blocks(first + lead + 2 * b, 2), c)[1], 0)

    @pl.when((rest & 1) == 1)
    def _():
        run_blocks(first + cnt - 1, 1)

    @pl.when(e == pl.num_programs(0) - 1)
    def _():
        for back in range(1, EXPERT_Y_RING + 1):
            @pl.when(nb >= back)
            def _():
                y_copy(nb - back, lax.rem(nb - back, EXPERT_Y_RING)).wait()

        ybuf[0] = jnp.zeros(ybuf.shape[1:], F32)
        lax.fori_loop(nb, n_blocks, lambda g, c: (y_copy(g, 0).start(), c)[1], 0)
        lax.fori_loop(nb, n_blocks, lambda g, c: (y_copy(g, 0).wait(), c)[1], 0)


def experts(xs, w_gate, w_up, w_down, first_blk, n_blk, n_used):
    n_rows, d_packed = xs.shape
    n_experts, d, de = w_gate.shape
    grid_spec = pltpu.PrefetchScalarGridSpec(
        num_scalar_prefetch=3,
        grid=(n_experts,),
        in_specs=[pl.BlockSpec(memory_space=pl.ANY)] * 4,
        out_specs=pl.BlockSpec(memory_space=pl.ANY),
        scratch_shapes=[pltpu.VMEM((EXPERT_X_RING, MOE_BLOCK, d_packed), xs.dtype),
                        pltpu.VMEM((EXPERT_Y_RING, MOE_BLOCK, d), F32),
                        pltpu.VMEM((2, d, de), F32), pltpu.VMEM((2, d, de), F32), pltpu.VMEM((2, de, d), F32),
                        pltpu.VMEM((d, de), BF16), pltpu.VMEM((d, de), BF16), pltpu.VMEM((de, d), BF16),
                        pltpu.SemaphoreType.DMA((EXPERT_X_RING,)), pltpu.SemaphoreType.DMA((EXPERT_Y_RING,)),
                        pltpu.SemaphoreType.DMA((2,))],
    )
    return pl.pallas_call(
        _experts_kernel,
        grid_spec=grid_spec,
        out_shape=jax.ShapeDtypeStruct((n_rows, d), F32),
        compiler_params=_cparams(("arbitrary",)),
        name="moe_experts",
    )(first_blk, n_blk, n_used, xs, w_gate, w_up, w_down)


def _moe_combine_kernel(dest_ref, destn_ref, x_ref, info_ref, g_ref, ys_hbm, o_ref, ybuf, sem, *, norm):
    i = pl.program_id(0)
    n = pl.num_programs(0)
    tm = x_ref.shape[0]
    slot = lax.rem(i, 2)

    def gather_start(dref, s):
        for k in range(2):
            for r in range(tm):
                pltpu.make_async_copy(ys_hbm.at[pl.ds(dref[k, r], 1)], ybuf.at[s, k, pl.ds(r, 1)],
                                      sem.at[s]).start(priority=r % 2)

    @pl.when(i == 0)
    def _():
        gather_start(dest_ref, 0)

    @pl.when(i + 1 < n)
    def _():
        gather_start(destn_ref, 1 - slot)

    for k in range(2):
        pltpu.make_async_copy(ys_hbm.at[pl.ds(0, tm)], ybuf.at[slot, k], sem.at[slot]).wait()
    w1 = info_ref[:, 4:5]
    w2 = info_ref[:, 5:6]
    s = x_ref[...] + (w1 * ybuf[slot, 0] + w2 * ybuf[slot, 1])
    o_ref[...] = _rms(s, g_ref[...]) if norm else s


def moe_combine(x, ys, dest, info, g, *, norm, tm):
    t, d = x.shape
    n = t // tm
    return pl.pallas_call(
        functools.partial(_moe_combine_kernel, norm=norm),
        grid=(n,),
        in_specs=[pl.BlockSpec((2, tm), lambda i: (0, i), memory_space=pltpu.SMEM),
                  pl.BlockSpec((2, tm), lambda i: (0, jnp.minimum(i + 1, n - 1)), memory_space=pltpu.SMEM),
                  pl.BlockSpec((tm, d), lambda i: (i, 0)),
                  pl.BlockSpec((tm, LANES), lambda i: (i, 0)),
                  pl.BlockSpec((1, d), lambda i: (0, 0)),
                  pl.BlockSpec(memory_space=pl.ANY)],
        out_specs=pl.BlockSpec((tm, d), lambda i: (i, 0)),
        out_shape=jax.ShapeDtypeStruct((t, d), F32),
        scratch_shapes=[pltpu.VMEM((2, 2, tm, d), F32), pltpu.SemaphoreType.DMA((2,))],
        compiler_params=_cparams(("arbitrary",)),
        name="moe_combine",
    )(dest, dest, x, info, g.reshape(1, d), ys)


def forward(x, mem, norm_mix_g, w_in, s5_lambda_re, s5_lambda_im, s5_log_dt, s5_b_re, s5_b_im, s5_c_re,
            s5_c_im, s5_d, s5_w_glu, conv_dw_w, conv_dw_b, conv_ln_g, conv_ln_b, conv_w_out, w_out,
            norm_xattn_g, norm_mem_g, xattn_wq, xattn_wk, xattn_wv, xattn_wo, norm_moe_g, router_w_group,
            router_b_group, router_w_expert, router_b_expert, exp_w_gate, exp_w_up, exp_w_down, norm_final_g,
            *, tiles):
    batch, seq, d = x.shape
    depth = w_in.shape[0]
    n_mem = mem.shape[1]
    s5_width = s5_d.shape[1]
    conv_width = conv_dw_b.shape[1]
    n_groups, epg = router_w_expert.shape[1], router_w_expert.shape[3]
    t = batch * seq
    xf = x.reshape(t, d)
    memf = mem.reshape(batch * n_mem, d)
    gate_col0 = s5_width + 2 * conv_width
    for l in range(depth):
        w_in_b = w_in[l].astype(BF16)
        proj = norm_mm_rows(xf, norm_mix_g[l], w_in_b, tm=tiles["proj_tm"], tn=tiles["proj_tn"],
                            out_dtype=F32, n_cols=gate_col0)
        lbr, lbi, bbr, bbi = s5_discretise(s5_lambda_re[l], s5_lambda_im[l], s5_log_dt[l],
                                           s5_b_re[l], s5_b_im[l])
        bre, bim, cre, cim = s5_block_diag(bbr, bbi, s5_c_re[l], s5_c_im[l])
        y_a = s5_branch(proj, bre, bim, cre, cim, lbr.reshape(1, -1), lbi.reshape(1, -1), s5_d[l],
                        s5_w_glu[l].astype(BF16), batch=batch, seq=seq, width=s5_width, m=tiles["s5_m"],
                        cw=tiles["s5_cw"], tn=tiles["glu_tn"])
        y_b = conv_branch(proj, conv_dw_w[l], conv_dw_b[l], conv_ln_g[l], conv_ln_b[l],
                          conv_w_out[l].astype(BF16), batch=batch, seq=seq, width=conv_width,
                          col0=s5_width, tm=tiles["conv_tm"])
        xf = combine(xf, norm_mix_g[l], w_in_b, y_a, y_b, w_out[l].astype(BF16), gate_col0=gate_col0,
                     tm=tiles["comb_tm"])

        q = norm_mm_rows(xf, norm_xattn_g[l], xattn_wq[l].astype(BF16), tm=tiles["q_tm"], tn=tiles["q_tn"],
                         out_dtype=BF16)
        wkv = jnp.concatenate([xattn_wk[l], xattn_wv[l]], axis=1).astype(BF16)
        kv = norm_mm(memf, norm_mem_g[l], wkv, tm=batch * n_mem, tn=tiles["kv_tn"], out_dtype=BF16)
        w_r = jnp.concatenate([router_w_group[l],
                               jnp.transpose(router_w_expert[l], (1, 0, 2)).reshape(d, n_groups * epg)], axis=1)
        b_r = jnp.concatenate([router_b_group[l], router_b_expert[l].reshape(-1)])
        n_r = w_r.shape[1]
        w_pad = jnp.pad(w_r, ((0, 0), (0, LANES - n_r)))
        w_hi = w_pad.astype(BF16)
        w_pad = jnp.concatenate([w_hi, (w_pad - w_hi.astype(F32)).astype(BF16)], axis=1)
        b_pad = jnp.pad(b_r, (0, LANES - n_r)).reshape(1, LANES)
        xf, h, info, infot, counts = xattn_router(
            xf, q, kv, xattn_wo[l].astype(BF16), norm_moe_g[l], w_pad, b_pad, batch=batch, seq=seq,
            n_mem=n_mem, heads=XATTN_HEADS, n_groups=n_groups, epg=epg, tm=tiles["xattn_tm"])
        dest, n_rows, n_used, pcounts, pends = moe_dispatch_plan(infot, counts, t, n_groups * epg)
        xs = moe_dispatch(h, dest, pcounts, pends, n_rows=n_rows, tm=tiles["moe_tm"])
        ys = experts(xs, exp_w_gate[l], exp_w_up[l], exp_w_down[l], (pends - pcounts) // MOE_BLOCK,
                     pcounts // MOE_BLOCK, n_used)
        last = l + 1 == depth
        xf = moe_combine(xf, ys, dest, info, norm_final_g if last else jnp.ones((d,), F32), norm=last,
                         tm=tiles["moe_tm"])
    return xf.reshape(batch, seq, d)


def kernel(x, mem, norm_mix_g, w_in, s5_lambda_re, s5_lambda_im, s5_log_dt, s5_b_re, s5_b_im, s5_c_re, s5_c_im, s5_d, s5_w_glu, conv_dw_w, conv_dw_b, conv_ln_g, conv_ln_b, conv_w_out, w_out, norm_xattn_g, norm_mem_g, xattn_wq, xattn_wk, xattn_wv, xattn_wo, norm_moe_g, router_w_group, router_b_group, router_w_expert, router_b_expert, exp_w_gate, exp_w_up, exp_w_down, norm_final_g):
    return forward(x, mem, norm_mix_g, w_in, s5_lambda_re, s5_lambda_im, s5_log_dt, s5_b_re, s5_b_im,
                   s5_c_re, s5_c_im, s5_d, s5_w_glu, conv_dw_w, conv_dw_b, conv_ln_g, conv_ln_b, conv_w_out,
                   w_out, norm_xattn_g, norm_mem_g, xattn_wq, xattn_wk, xattn_wv, xattn_wo, norm_moe_g,
                   router_w_group, router_b_group, router_w_expert, router_b_expert, exp_w_gate, exp_w_up,
                   exp_w_down, norm_final_g, tiles=TILES)
```

```python
import functools
import math

import jax
import jax.numpy as jnp
from jax import lax
from jax.experimental import pallas as pl
from jax.experimental.pallas import tpu as pltpu

F32 = jnp.float32
BF16 = jnp.bfloat16

RMS_EPS = 1e-6
LN_EPS = 1e-5

S5_SLAB_CH = 256
CONV_HALO = 32
CONV_ROW_CHUNK = 64
XATTN_HEADS = 4
MOE_BLOCK = 128
BLOCK_DMA_QUEUE = 1
EXPERT_X_RING = 8
EXPERT_Y_RING = 4
SUBLANES = 8
LANES = 128
VMEM_LIMIT = 56 * 1024 * 1024

TILES = dict(proj_tm=512, proj_tn=512, s5_m=64, s5_cw=512, glu_tn=512, conv_tm=256,
             comb_tm=256, q_tm=512, q_tn=512, kv_tn=512, xattn_tm=512, moe_tm=256)


def _cparams(sem):
    return pltpu.CompilerParams(dimension_semantics=sem, vmem_limit_bytes=VMEM_LIMIT)


def _rms(x, g):
    return x * lax.rsqrt(jnp.mean(x * x, axis=-1, keepdims=True) + RMS_EPS) * g


def _sigmoid(x):
    return 0.5 * jnp.tanh(0.5 * x) + 0.5


def _gelu_tanh(x):
    c = math.sqrt(2.0 / math.pi)
    return 0.5 * x * (1.0 + jnp.tanh(c * (x + 0.044715 * (x * x * x))))


def _norm_mm_kernel(x_ref, g_ref, w_ref, o_ref, h_ref, *, act_from):
    j = pl.program_id(1)

    @pl.when(j == 0)
    def _():
        h_ref[...] = _rms(x_ref[...], g_ref[...]).astype(BF16)

    y = jnp.dot(h_ref[...], w_ref[...], preferred_element_type=F32)
    if act_from is None:
        o_ref[...] = y.astype(o_ref.dtype)
    else:
        @pl.when(j < act_from)
        def _():
            o_ref[...] = y.astype(o_ref.dtype)

        @pl.when(j >= act_from)
        def _():
            o_ref[...] = _sigmoid(y).astype(o_ref.dtype)


def norm_mm(x, g, w, *, tm, tn, out_dtype=F32, act_from=None):
    m, k = x.shape
    n = w.shape[1]
    return pl.pallas_call(
        functools.partial(_norm_mm_kernel, act_from=act_from),
        grid=(m // tm, n // tn),
        in_specs=[pl.BlockSpec((tm, k), lambda i, j: (i, 0)),
                  pl.BlockSpec((1, k), lambda i, j: (0, 0)),
                  pl.BlockSpec((k, tn), lambda i, j: (0, j))],
        out_specs=pl.BlockSpec((tm, tn), lambda i, j: (i, j)),
        out_shape=jax.ShapeDtypeStruct((m, n), out_dtype),
        scratch_shapes=[pltpu.VMEM((tm, k), BF16)],
        compiler_params=_cparams(("parallel", "arbitrary")),
        name="norm_mm",
    )(x, g.reshape(1, k), w)


def _norm_mm_rows_kernel(x_ref, g_ref, w_ref, o_ref, *, tn):
    h = _rms(x_ref[...], g_ref[...]).astype(BF16)
    for j in range(o_ref.shape[1] // tn):
        cols = slice(j * tn, (j + 1) * tn)
        o_ref[:, cols] = jnp.dot(h, w_ref[:, cols], preferred_element_type=F32).astype(o_ref.dtype)


def norm_mm_rows(x, g, w, *, tm, tn, out_dtype, n_cols=None):
    m, k = x.shape
    n = w.shape[1] if n_cols is None else n_cols
    return pl.pallas_call(
        functools.partial(_norm_mm_rows_kernel, tn=tn),
        grid=(m // tm,),
        in_specs=[pl.BlockSpec((tm, k), lambda i: (i, 0)),
                  pl.BlockSpec((1, k), lambda i: (0, 0)),
                  pl.BlockSpec((k, n), lambda i: (0, 0), pipeline_mode=pl.Buffered(1))],
        out_specs=pl.BlockSpec((tm, n), lambda i: (i, 0)),
        out_shape=jax.ShapeDtypeStruct((m, n), out_dtype),
        compiler_params=_cparams(("parallel",)),
        name="norm_mm_rows",
    )(x, g.reshape(1, k), w)


def _s5_disc_kernel(lr_ref, li_ref, ldt_ref, br_ref, bi_ref, lbr_ref, lbi_ref, bbr_ref, bbi_ref):
    lr, li = lr_ref[...], li_ref[...]
    dt = jnp.exp(ldt_ref[...])
    mag = jnp.exp(lr * dt)
    ang = li * dt
    lb_re, lb_im = mag * jnp.cos(ang), mag * jnp.sin(ang)
    den = lr * lr + li * li
    nr, ni = lb_re - 1.0, lb_im
    coef_re = (nr * lr + ni * li) / den
    coef_im = (ni * lr - nr * li) / den
    lbr_ref[...] = lb_re
    lbi_ref[...] = lb_im
    br, bi = br_ref[...], bi_ref[...]
    bbr_ref[...] = coef_re[None] * br - coef_im[None] * bi
    bbi_ref[...] = coef_re[None] * bi + coef_im[None] * br


def s5_discretise(lam_re, lam_im, log_dt, b_re, b_im):
    g, n = lam_re.shape
    c = b_re.shape[-1]
    b_re_t = jnp.transpose(b_re, (2, 0, 1))
    b_im_t = jnp.transpose(b_im, (2, 0, 1))
    gn = jax.ShapeDtypeStruct((g, n), F32)
    cgn = jax.ShapeDtypeStruct((c, g, n), F32)
    return pl.pallas_call(
        _s5_disc_kernel,
        out_shape=(gn, gn, cgn, cgn),
        name="s5_discretise",
    )(lam_re, lam_im, log_dt.reshape(g, 1), b_re_t, b_im_t)


def s5_block_diag(bb_re, bb_im, c_re, c_im):
    c, g, n = bb_re.shape
    gs = S5_SLAB_CH // c
    n_slab = g // gs
    row_grp_b = jnp.arange(gs * c) // c
    col_grp_b = jnp.arange(gs * n) // n
    tile_n = (jnp.arange(n)[:, None] == jnp.arange(gs * n)[None, :] % n).astype(F32)
    tile_c = (jnp.arange(c)[:, None] == jnp.arange(gs * c)[None, :] % c).astype(F32)
    exact = lax.Precision.HIGHEST

    def bmat(bb):
        b = jnp.transpose(bb, (1, 0, 2)).reshape(n_slab, gs * c, n)
        full = jnp.einsum('srn,nq->srq', b, tile_n, precision=exact)
        return jnp.where(row_grp_b[:, None] == col_grp_b[None, :], full, 0.0).astype(BF16)

    def cmat(cc):
        cm = jnp.transpose(cc.reshape(n_slab, gs, c, n), (0, 1, 3, 2)).reshape(n_slab, gs * n, c)
        full = jnp.einsum('spc,cr->spr', cm, tile_c, precision=exact)
        return jnp.where(col_grp_b[:, None] == row_grp_b[None, :], full, 0.0).astype(BF16)

    return bmat(bb_re), bmat(bb_im), cmat(c_re), cmat(c_im)


def _s5_kernel(proj_hbm, bre_ref, bim_ref, cre_ref, cim_ref, lre_ref, lim_ref, d_ref, wglu_ref, ya_hbm,
               ubuf, yabuf, zbuf, xre_ref, xim_ref, pre_ref, pim_ref, car_re_ref, car_im_ref, isem, osem,
               *, m, cw, tn):
    t = pl.program_id(1)
    n_t = pl.num_programs(1)
    step_id = pl.program_id(0) * n_t + t
    n_steps = pl.num_programs(0) * n_t
    slot = lax.rem(step_id, 2)
    rows = SUBLANES * m
    width = d_ref.shape[1]
    n_state = lre_ref.shape[1]
    n_slab = bre_ref.shape[0]
    slab_states = n_state // n_slab

    def in_copy(step, s, k):
        r0 = pl.multiple_of(step * rows + k * m, SUBLANES)
        return pltpu.make_async_copy(proj_hbm.at[pl.ds(r0, m), pl.ds(0, width)], ubuf.at[s, :, k, :], isem.at[s])

    def out_copy(step, s, k):
        r0 = pl.multiple_of(step * rows + k * m, SUBLANES)
        return pltpu.make_async_copy(yabuf.at[s, :, k, :], ya_hbm.at[pl.ds(r0, m)], osem.at[s])

    @pl.when(step_id == 0)
    def _():
        for k in range(SUBLANES):
            in_copy(0, 0, k).start()

    @pl.when(step_id + 1 < n_steps)
    def _():
        for k in range(SUBLANES):
            in_copy(step_id + 1, 1 - slot, k).start()

    @pl.when(t == 0)
    def _():
        car_re_ref[...] = jnp.zeros_like(car_re_ref)
        car_im_ref[...] = jnp.zeros_like(car_im_ref)
        lr, li = lre_ref[...], lim_ref[...]

        def pw(_, carry):
            pr, pi = carry
            return pr * lr - pi * li, pr * li + pi * lr

        pr, pi = lax.fori_loop(1, m, pw, (lr, li))
        pre_ref[...] = pr
        pim_ref[...] = pi

    @pl.when(step_id == 0)
    def _():
        zbuf[...] = jnp.zeros_like(zbuf)

    @pl.when(step_id >= 3)
    def _():
        for k in range(SUBLANES):
            out_copy(step_id - 3, 1 - slot, k).wait()

    nh = wglu_ref.shape[1] // 2

    def glu_pieces(z, s):
        def piece(j):
            val = jnp.dot(z, wglu_ref[:, j * tn:(j + 1) * tn], preferred_element_type=F32)
            gt = jnp.dot(z, wglu_ref[:, nh + j * tn:nh + (j + 1) * tn], preferred_element_type=F32)
            yabuf[s, :, :, j * tn:(j + 1) * tn] = (val * _sigmoid(gt)).reshape(m, SUBLANES, tn)
        return [functools.partial(piece, j) for j in range(nh // tn)]

    for k in range(SUBLANES):
        in_copy(step_id, slot, k).wait()
    u = ubuf[slot].reshape(rows, width)
    up = u.astype(BF16)

    def b_project(s):
        us = up[:, s * S5_SLAB_CH:(s + 1) * S5_SLAB_CH]
        cols = slice(s * slab_states, (s + 1) * slab_states)
        xre_ref[:, cols] = jnp.dot(us, bre_ref[s], preferred_element_type=F32)
        xim_ref[:, cols] = jnp.dot(us, bim_ref[s], preferred_element_type=F32)

    def c_project(s):
        cols = slice(s * slab_states, (s + 1) * slab_states)
        return (jnp.dot(xre_ref[:, cols].astype(BF16), cre_ref[s], preferred_element_type=F32)
                - jnp.dot(xim_ref[:, cols].astype(BF16), cim_ref[s], preferred_element_type=F32))

    n_cb = n_state // cw
    cb_per_slab = slab_states // cw
    glu_prev = glu_pieces(zbuf[...], 1 - slot)
    glu_at = {(i * n_cb) // len(glu_prev): [] for i in range(len(glu_prev))}
    for i, piece in enumerate(glu_prev):
        glu_at[(i * n_cb) // len(glu_prev)].append(piece)
    ys = []
    b_project(0)
    row_id = lax.broadcasted_iota(jnp.int32, (SUBLANES, cw), 0)
    for cb in range(n_cb):
        cols = slice(cb * cw, (cb + 1) * cw)
        slab = cb // cb_per_slab
        if cb % cb_per_slab == 0:
            if slab + 1 < n_slab:
                b_project(slab + 1)
            if slab >= 1:
                ys.append(c_project(slab - 1))
        for piece in glu_at.get(cb, []):
            piece()
        lr = jnp.broadcast_to(lre_ref[:, cols], (SUBLANES, cw))
        li = jnp.broadcast_to(lim_ref[:, cols], (SUBLANES, cw))

        def step(tau, carry, store, cols=cols, lr=lr, li=li):
            sr, si = carry
            r0 = pl.multiple_of(tau * SUBLANES, SUBLANES)
            nr = sr * lr - si * li + xre_ref[pl.ds(r0, SUBLANES), cols]
            ni = sr * li + si * lr + xim_ref[pl.ds(r0, SUBLANES), cols]
            if store:
                xre_ref[pl.ds(r0, SUBLANES), cols] = nr
                xim_ref[pl.ds(r0, SUBLANES), cols] = ni
            return nr, ni

        zero = jnp.zeros((SUBLANES, cw), F32)
        er, ei = lax.fori_loop(0, m, functools.partial(step, store=False), (zero, zero), unroll=True)

        lmr = jnp.broadcast_to(pre_ref[:, cols], (SUBLANES, cw))
        lmi = jnp.broadcast_to(pim_ref[:, cols], (SUBLANES, cw))
        ir = jnp.where(row_id == 0, jnp.broadcast_to(car_re_ref[:, cols], (SUBLANES, cw)), 0.0)
        ii = jnp.where(row_id == 0, jnp.broadcast_to(car_im_ref[:, cols], (SUBLANES, cw)), 0.0)
        for k in range(1, SUBLANES):
            nr = ir * lmr - ii * lmi + er
            ni = ir * lmi + ii * lmr + ei
            ir = jnp.where(row_id == k, pltpu.roll(nr, 1, axis=0), ir)
            ii = jnp.where(row_id == k, pltpu.roll(ni, 1, axis=0), ii)

        fr, fi = lax.fori_loop(0, m, functools.partial(step, store=True), (ir, ii), unroll=True)
        car_re_ref[:, cols] = fr[SUBLANES - 1:SUBLANES, :]
        car_im_ref[:, cols] = fi[SUBLANES - 1:SUBLANES, :]

    ys.append(c_project(n_slab - 1))
    y = jnp.concatenate(ys, axis=1) + d_ref[...] * u
    zbuf[...] = _gelu_tanh(y).astype(BF16)

    @pl.when(step_id >= 1)
    def _():
        for k in range(SUBLANES):
            out_copy(step_id - 1, 1 - slot, k).start()

    @pl.when(step_id == n_steps - 1)
    def _():
        @pl.when(step_id >= 2)
        def _():
            for k in range(SUBLANES):
                out_copy(step_id - 2, slot, k).wait()
        for piece in glu_pieces(zbuf[...], slot):
            piece()
        for k in range(SUBLANES):
            out_copy(step_id, slot, k).start()
        for k in range(SUBLANES):
            out_copy(step_id, slot, k).wait()

        @pl.when(step_id >= 1)
        def _():
            for k in range(SUBLANES):
                out_copy(step_id - 1, 1 - slot, k).wait()


def s5_branch(proj, bre, bim, cre, cim, lre, lim, d_skip, w_glu, *, batch, seq, width, m, cw, tn):
    rows = SUBLANES * m
    n_t = seq // rows
    n_state = lre.shape[1]
    nh = w_glu.shape[1] // 2
    once = pl.Buffered(1)
    const3 = lambda b, t: (0, 0, 0)
    const2 = lambda b, t: (0, 0)
    return pl.pallas_call(
        functools.partial(_s5_kernel, m=m, cw=cw, tn=tn),
        grid=(batch, n_t),
        in_specs=[pl.BlockSpec(memory_space=pl.ANY),
                  pl.BlockSpec(bre.shape, const3, pipeline_mode=once),
                  pl.BlockSpec(bim.shape, const3, pipeline_mode=once),
                  pl.BlockSpec(cre.shape, const3, pipeline_mode=once),
                  pl.BlockSpec(cim.shape, const3, pipeline_mode=once),
                  pl.BlockSpec((1, n_state), const2),
                  pl.BlockSpec((1, n_state), const2),
                  pl.BlockSpec((1, width), const2),
                  pl.BlockSpec(w_glu.shape, const2, pipeline_mode=once)],
        out_specs=pl.BlockSpec(memory_space=pl.ANY),
        out_shape=jax.ShapeDtypeStruct((batch * seq, nh), F32),
        scratch_shapes=[pltpu.VMEM((2, m, SUBLANES, width), F32), pltpu.VMEM((2, m, SUBLANES, nh), F32),
                        pltpu.VMEM((rows, width), BF16),
                        pltpu.VMEM((rows, n_state), F32), pltpu.VMEM((rows, n_state), F32),
                        pltpu.VMEM((1, n_state), F32), pltpu.VMEM((1, n_state), F32),
                        pltpu.VMEM((1, n_state), F32), pltpu.VMEM((1, n_state), F32),
                        pltpu.SemaphoreType.DMA((2,)), pltpu.SemaphoreType.DMA((2,))],
        compiler_params=_cparams(("arbitrary", "arbitrary")),
        name="s5_branch",
    )(proj, bre, bim, cre, cim, lre, lim, d_skip.reshape(1, width), w_glu)


def _conv_kernel(a_ref, b_ref, ah_ref, bh_ref, dww_ref, dwb_ref, lng_ref, lnb_ref, w_ref, o_ref,
                 zs_ref, wb_ref, yc_ref, *, tm, taps):
    i = pl.program_id(1)
    off0 = CONV_HALO - (taps - 1)
    width = a_ref.shape[1]
    n_slab = width // LANES
    zh = jnp.where(i == 0, 0.0, ah_ref[...] * _sigmoid(bh_ref[...]))
    z = a_ref[...] * _sigmoid(b_ref[...])
    for s in range(n_slab):
        zs_ref[s, 0:CONV_HALO, :] = zh[:, s * LANES:(s + 1) * LANES]
        zs_ref[s, CONV_HALO:, :] = z[:, s * LANES:(s + 1) * LANES]

    @pl.when(i == 0)
    def _():
        for j in range(taps):
            wb_ref[j] = jnp.broadcast_to(dww_ref[j:j + 1, :], (SUBLANES, width))
        wb_ref[taps] = jnp.broadcast_to(dwb_ref[...], (SUBLANES, width))

    rg = CONV_ROW_CHUNK // SUBLANES

    def chunk(c, carry):
        r0 = pl.multiple_of(c * CONV_ROW_CHUNK, CONV_ROW_CHUNK)
        for s in range(n_slab):
            lanes = slice(s * LANES, (s + 1) * LANES)
            bias = wb_ref[taps, :, lanes]
            acc = [bias] * rg
            for j in range(taps):
                w = wb_ref[j, :, lanes]
                for g in range(rg):
                    acc[g] = acc[g] + w * zs_ref[s, pl.ds(r0 + g * SUBLANES + off0 + j, SUBLANES), :]
            for g in range(rg):
                yc_ref[pl.ds(pl.multiple_of(r0 + g * SUBLANES, SUBLANES), SUBLANES), lanes] = acc[g]
        return carry

    lax.fori_loop(0, tm // CONV_ROW_CHUNK, chunk, 0)
    y = yc_ref[...]
    mu = jnp.mean(y, axis=-1, keepdims=True)
    var = jnp.mean(jnp.square(y - mu), axis=-1, keepdims=True)
    y = (y - mu) * lax.rsqrt(var + LN_EPS) * lng_ref[...] + lnb_ref[...]
    y = y * _sigmoid(y)
    o_ref[...] = jnp.dot(y.astype(BF16), w_ref[...], preferred_element_type=F32)


def conv_branch(proj, dw_w, dw_b, ln_g, ln_b, w_out, *, batch, seq, width, col0, tm):
    n_t = seq // tm
    ca, cb = col0 // width, col0 // width + 1
    hb = tm // CONV_HALO
    taps = dw_w.shape[0]
    d_out = w_out.shape[1]
    const2 = lambda b, i: (0, 0)
    halo = lambda b, i: jnp.maximum((b * n_t + i) * hb - 1, 0)
    return pl.pallas_call(
        functools.partial(_conv_kernel, tm=tm, taps=taps),
        grid=(batch, n_t),
        in_specs=[pl.BlockSpec((tm, width), lambda b, i: (b * n_t + i, ca)),
                  pl.BlockSpec((tm, width), lambda b, i: (b * n_t + i, cb)),
                  pl.BlockSpec((CONV_HALO, width), lambda b, i: (halo(b, i), ca)),
                  pl.BlockSpec((CONV_HALO, width), lambda b, i: (halo(b, i), cb)),
                  pl.BlockSpec((taps, width), const2),
                  pl.BlockSpec((1, width), const2),
                  pl.BlockSpec((1, width), const2),
                  pl.BlockSpec((1, width), const2),
                  pl.BlockSpec((width, d_out), const2)],
        out_specs=pl.BlockSpec((tm, d_out), lambda b, i: (b * n_t + i, 0)),
        out_shape=jax.ShapeDtypeStruct((batch * seq, d_out), F32),
        scratch_shapes=[pltpu.VMEM((width // LANES, tm + CONV_HALO, LANES), F32),
                        pltpu.VMEM((taps + 1, SUBLANES, width), F32),
                        pltpu.VMEM((tm, width), F32)],
        compiler_params=_cparams(("parallel", "arbitrary")),
        name="conv_branch",
    )(proj, proj, proj, proj, dw_w, dw_b.reshape(1, width), ln_g.reshape(1, width),
      ln_b.reshape(1, width), w_out)


def _combine_kernel(*refs, parts):
    x_ref, g_ref = refs[0], refs[1]
    ya_ref, yb_ref, wo_ref, o_ref = refs[2 + 2 * parts], refs[3 + 2 * parts], refs[-2], refs[-1]
    gw = ya_ref.shape[1] // parts
    x = x_ref[...]
    h = _rms(x, g_ref[...]).astype(BF16)
    o_ref[...] = x
    for p in range(parts):
        cols = slice(p * gw, (p + 1) * gw)
        ga = _sigmoid(jnp.dot(h, refs[2 + p][...], preferred_element_type=F32))
        gb = _sigmoid(jnp.dot(h, refs[2 + parts + p][...], preferred_element_type=F32))
        mix = ga * ya_ref[:, cols] + gb * yb_ref[:, cols]
        o_ref[...] += jnp.dot(mix.astype(BF16), wo_ref[cols, :], preferred_element_type=F32)


def combine(x, g, w_in, y_a, y_b, w_out, *, gate_col0, tm):
    m, d = x.shape
    gw = math.gcd(gate_col0, d)
    parts = d // gw
    row = lambda i: (i, 0)
    once = pl.Buffered(1)
    gate_specs = [pl.BlockSpec((d, gw), functools.partial(lambda i, c: (0, c), c=(gate_col0 + s * d) // gw + p),
                               pipeline_mode=once) for s in range(2) for p in range(parts)]
    return pl.pallas_call(
        functools.partial(_combine_kernel, parts=parts),
        grid=(m // tm,),
        in_specs=[pl.BlockSpec((tm, d), row), pl.BlockSpec((1, d), lambda i: (0, 0))] + gate_specs +
                 [pl.BlockSpec((tm, d), row),
                  pl.BlockSpec((tm, d), row),
                  pl.BlockSpec((d, d), lambda i: (0, 0), pipeline_mode=once)],
        out_specs=pl.BlockSpec((tm, d), row),
        out_shape=jax.ShapeDtypeStruct((m, d), F32),
        compiler_params=_cparams(("parallel",)),
        name="gated_combine",
    )(x, g.reshape(1, d), *([w_in] * (2 * parts)), y_a, y_b, w_out)


def _route_tile(x, g_ref, w_ref, b_ref, tri_ref, h_ref, info_ref, infot_ref, cnt_ref, run_ref, *,
                n_groups, epg):
    tm = x.shape[0]
    h = _rms(x, g_ref[...])
    half = h.shape[1] // 2
    bits = pltpu.bitcast(h.astype(BF16).astype(F32), jnp.uint32)
    h_ref[...] = (bits[:, :half] >> 16) | (bits[:, half:] & jnp.uint32(0xFFFF0000))
    h_hi = h.astype(BF16)
    h_lo = (h - h_hi.astype(F32)).astype(BF16)
    p = jnp.dot(h_hi, w_ref[...], preferred_element_type=F32)
    logits = (p[:, :LANES] + p[:, LANES:] + jnp.dot(h_lo, w_ref[:, :LANES], preferred_element_type=F32)
              + b_ref[...])
    lane = lax.broadcasted_iota(jnp.int32, (tm, LANES), 1).astype(F32)
    neg = jnp.float32(-jnp.inf)
    big = jnp.float32(LANES)
    is_grp = lane < n_groups
    cl = jnp.where(is_grp, logits, neg)
    cmax = jnp.max(cl, axis=-1, keepdims=True)
    g_idx = jnp.min(jnp.where(cl == cmax, lane, big), axis=-1, keepdims=True)
    p_sel = 1.0 / jnp.sum(jnp.where(is_grp, jnp.exp(cl - cmax), 0.0), axis=-1, keepdims=True)
    lo = n_groups + g_idx * epg
    in_grp = (lane >= lo) & (lane < lo + epg)
    fl = jnp.where(in_grp, logits, neg)
    v1 = jnp.max(fl, axis=-1, keepdims=True)
    i1 = jnp.min(jnp.where(fl == v1, lane, big), axis=-1, keepdims=True)
    fl2 = jnp.where(lane == i1, neg, fl)
    v2 = jnp.max(fl2, axis=-1, keepdims=True)
    i2 = jnp.min(jnp.where(fl2 == v2, lane, big), axis=-1, keepdims=True)
    e2x = jnp.exp(v2 - v1)
    w1 = p_sel / (1.0 + e2x)
    w2 = p_sel * e2x / (1.0 + e2x)
    e1 = i1 - n_groups
    e2 = i2 - n_groups
    oh1 = lane == e1
    oh2 = lane == e2
    both = jnp.where(oh1 | oh2, 1.0, 0.0)
    before = jnp.dot(tri_ref[...], both.astype(BF16), preferred_element_type=F32) + run_ref[...]
    r1 = jnp.sum(jnp.where(oh1, before, 0.0), axis=-1, keepdims=True)
    r2 = jnp.sum(jnp.where(oh2, before, 0.0), axis=-1, keepdims=True)
    run_ref[...] += jnp.sum(both, axis=0, keepdims=True)
    cnt_ref[...] = run_ref[...]
    info = jnp.where(lane == 0, e1, 0.0)
    info = jnp.where(lane == 1, e2, info)
    info = jnp.where(lane == 2, r1, info)
    info = jnp.where(lane == 3, r2, info)
    info = jnp.where(lane == 4, w1, info)
    info = jnp.where(lane == 5, w2, info)
    info_ref[...] = info
    infot_ref[...] = jnp.transpose(info)[0:SUBLANES, :]


def _xattn_router_kernel(x_ref, q_ref, kv_ref, wo_ref, g_ref, w_ref, b_ref, tri_ref,
                         o_ref, h_ref, info_ref, infot_ref, cnt_ref, run_ref, *, heads, scale, n_groups, epg):
    @pl.when((pl.program_id(0) == 0) & (pl.program_id(1) == 0))
    def _():
        run_ref[...] = jnp.zeros_like(run_ref)

    d = x_ref.shape[1]
    hd = d // heads
    acc = x_ref[...]
    for h in range(heads):
        q = q_ref[:, h * hd:(h + 1) * hd]
        k = kv_ref[:, h * hd:(h + 1) * hd]
        v = kv_ref[:, d + h * hd:d + (h + 1) * hd]
        s = lax.dot_general(q, k, (((1,), (1,)), ((), ())), preferred_element_type=F32) * scale
        s = s - jnp.max(s, axis=-1, keepdims=True)
        p = jnp.exp(s)
        p = p / jnp.sum(p, axis=-1, keepdims=True)
        o = jnp.dot(p.astype(BF16), v, preferred_element_type=F32)
        acc = acc + jnp.dot(o.astype(BF16), wo_ref[h * hd:(h + 1) * hd, :], preferred_element_type=F32)
    o_ref[...] = acc
    _route_tile(acc, g_ref, w_ref, b_ref, tri_ref, h_ref, info_ref, infot_ref, cnt_ref, run_ref,
                n_groups=n_groups, epg=epg)


def xattn_router(x, q, kv, wo, g, w_pad, b_pad, *, batch, seq, n_mem, heads, n_groups, epg, tm):
    d = x.shape[1]
    m = batch * seq
    n_t = seq // tm
    scale = 1.0 / math.sqrt(d // heads)
    tri = (jnp.arange(tm)[:, None] > jnp.arange(tm)[None, :]).astype(BF16)
    row = lambda b, i: (b * n_t + i, 0)
    const = lambda b, i: (0, 0)
    once = pl.Buffered(1)
    return pl.pallas_call(
        functools.partial(_xattn_router_kernel, heads=heads, scale=scale, n_groups=n_groups, epg=epg),
        grid=(batch, n_t),
        in_specs=[pl.BlockSpec((tm, d), row),
                  pl.BlockSpec((tm, d), row),
                  pl.BlockSpec((n_mem, 2 * d), lambda b, i: (b, 0)),
                  pl.BlockSpec((d, d), const, pipeline_mode=once),
                  pl.BlockSpec((1, d), const),
                  pl.BlockSpec((d, 2 * LANES), const, pipeline_mode=once),
                  pl.BlockSpec((1, LANES), const),
                  pl.BlockSpec((tm, tm), const, pipeline_mode=once)],
        out_specs=[pl.BlockSpec((tm, d), row),
                   pl.BlockSpec((tm, d // 2), row),
                   pl.BlockSpec((tm, LANES), row),
                   pl.BlockSpec((SUBLANES, tm), lambda b, i: (0, b * n_t + i)),
                   pl.BlockSpec((1, LANES), const)],
        out_shape=(jax.ShapeDtypeStruct((m, d), F32),
                   jax.ShapeDtypeStruct((m, d // 2), jnp.uint32),
                   jax.ShapeDtypeStruct((m, LANES), F32),
                   jax.ShapeDtypeStruct((SUBLANES, m), F32),
                   jax.ShapeDtypeStruct((1, LANES), F32)),
        scratch_shapes=[pltpu.VMEM((1, LANES), F32)],
        compiler_params=_cparams(("arbitrary", "arbitrary")),
        name="xattn_router",
    )(x, q, kv, wo, g.reshape(1, d), w_pad, b_pad, tri)


def moe_dispatch_plan(infot, counts, n_tokens, n_experts):
    e = infot[0:2].astype(jnp.int32)
    rank = infot[2:4].astype(jnp.int32)
    counts = counts[0, :n_experts].astype(jnp.int32)
    pcounts = ((counts + MOE_BLOCK - 1) // MOE_BLOCK) * MOE_BLOCK
    pends = jnp.cumsum(pcounts)
    pstarts = pends - pcounts
    ids = jnp.arange(n_experts, dtype=jnp.int32)
    dest = rank + jnp.sum(jnp.where(e[:, :, None] == ids, pstarts, 0), axis=-1)
    n_rows = (-(-2 * n_tokens // MOE_BLOCK) + n_experts) * MOE_BLOCK
    n_used = (pends[-1] // MOE_BLOCK).reshape(1)
    return dest, n_rows, n_used, pcounts, pends


def _dispatch_kernel(pc_ref, pe_ref, dest_ref, h_ref, xs_hbm, zbuf, sem, tsem, *, n_experts):
    i = pl.program_id(0)
    tm = h_ref.shape[0]

    @pl.when(i == 0)
    def _():
        zbuf[...] = jnp.zeros_like(zbuf)
        for wait in (False, True):
            for e in range(n_experts):
                @pl.when(pc_ref[e] > 0)
                def _():
                    row0 = pl.multiple_of(pe_ref[e] - MOE_BLOCK, MOE_BLOCK)
                    fill = pltpu.make_async_copy(zbuf, xs_hbm.at[pl.ds(row0, MOE_BLOCK)], sem)
                    fill.wait() if wait else fill.start()

    n_blocks = xs_hbm.shape[0] // MOE_BLOCK
    n_used = pe_ref[n_experts - 1] // MOE_BLOCK

    def tail_fill(b, wait):
        row0 = pl.multiple_of(b * MOE_BLOCK, MOE_BLOCK)
        fill = pltpu.make_async_copy(zbuf, xs_hbm.at[pl.ds(row0, MOE_BLOCK)], tsem)
        fill.wait() if wait else fill.start()

    @pl.when(i == 0)
    def _():
        lax.fori_loop(n_used, n_blocks, lambda b, c: (tail_fill(b, False), c)[1], 0)

    for k in range(2):
        for r in range(tm):
            pltpu.make_async_copy(h_ref.at[pl.ds(r, 1)], xs_hbm.at[pl.ds(dest_ref[k, r], 1)],
                                  sem).start(priority=r % 2)
    for k in range(2):
        pltpu.make_async_copy(h_ref, xs_hbm.at[pl.ds(0, tm)], sem).wait()

    @pl.when(i == pl.num_programs(0) - 1)
    def _():
        lax.fori_loop(n_used, n_blocks, lambda b, c: (tail_fill(b, True), c)[1], 0)


def moe_dispatch(h, dest, pcounts, pends, *, n_rows, tm):
    t, d = h.shape
    grid_spec = pltpu.PrefetchScalarGridSpec(
        num_scalar_prefetch=2,
        grid=(t // tm,),
        in_specs=[pl.BlockSpec((2, tm), lambda i, pc, pe: (0, i), memory_space=pltpu.SMEM),
                  pl.BlockSpec((tm, d), lambda i, pc, pe: (i, 0))],
        out_specs=pl.BlockSpec(memory_space=pl.ANY),
        scratch_shapes=[pltpu.VMEM((MOE_BLOCK, d), h.dtype), pltpu.SemaphoreType.DMA(()),
                        pltpu.SemaphoreType.DMA(())],
    )
    return pl.pallas_call(
        functools.partial(_dispatch_kernel, n_experts=pcounts.shape[0]),
        grid_spec=grid_spec,
        out_shape=jax.ShapeDtypeStruct((n_rows, d), h.dtype),
        compiler_params=_cparams(("arbitrary",)),
        name="moe_dispatch",
    )(pcounts, pends, dest, h)


def _experts_kernel(first_ref, cnt_ref, nb_ref, xs_hbm, wg_hbm, wu_hbm, wd_hbm, ys_hbm,
                    xbuf, ybuf, wgf, wuf, wdf, wgb, wub, wdb, xsem, ysem, wsem):
    e = pl.program_id(0)
    n_e = pl.num_programs(0)
    nb = nb_ref[0]
    first, cnt = first_ref[e], cnt_ref[e]
    n_blocks = xs_hbm.shape[0] // MOE_BLOCK
    wslot = lax.rem(e, 2)

    def rows(g):
        return pl.ds(pl.multiple_of(g * MOE_BLOCK, MOE_BLOCK), MOE_BLOCK)

    def x_copy(g, s):
        return pltpu.make_async_copy(xs_hbm.at[rows(g)], xbuf.at[s], xsem.at[s])

    def y_copy(g, s):
        return pltpu.make_async_copy(ybuf.at[s], ys_hbm.at[rows(g)], ysem.at[s])

    def w_copies(ex, s):
        half = wd_hbm.shape[1] // 2
        lo, hi = pl.ds(0, half), pl.ds(half, half)
        return ((pltpu.make_async_copy(wg_hbm.at[ex], wgf.at[s], wsem.at[s]), 0),
                (pltpu.make_async_copy(wu_hbm.at[ex], wuf.at[s], wsem.at[s]), 1),
                (pltpu.make_async_copy(wd_hbm.at[ex, lo], wdf.at[s, lo], wsem.at[s]), 0),
                (pltpu.make_async_copy(wd_hbm.at[ex, hi], wdf.at[s, hi], wsem.at[s]), 1))

    @pl.when(e == 0)
    def _():
        for g0 in range(EXPERT_X_RING - 2):
            @pl.when(g0 < nb)
            def _():
                x_copy(g0, g0).start(priority=BLOCK_DMA_QUEUE)

    @pl.when((e == 0) & (cnt > 0))
    def _():
        for cp, queue in w_copies(0, 0):
            cp.start(priority=queue)

    e_next = jnp.minimum(e + 1, n_e - 1)

    @pl.when((e + 1 < n_e) & (cnt_ref[e_next] > 0))
    def _():
        for cp, queue in w_copies(e_next, 1 - wslot):
            cp.start(priority=queue)

    @pl.when(cnt > 0)
    def _():
        for cp, _ in w_copies(e, wslot):
            cp.wait()
        wgb[...] = wgf[wslot].astype(BF16)
        wub[...] = wuf[wslot].astype(BF16)
        wdb[...] = wdf[wslot].astype(BF16)

    def run_blocks(g, nblk):
        s = lax.rem(g, EXPERT_X_RING)
        sy = lax.rem(g, EXPERT_Y_RING)
        for j in range(nblk):
            ahead = g + j + EXPERT_X_RING - 2

            @pl.when(ahead < nb)
            def _():
                x_copy(ahead, lax.rem(ahead, EXPERT_X_RING)).start(priority=BLOCK_DMA_QUEUE)

        for j in range(nblk):
            x_copy(g + j, s + j).wait()
        half = xbuf.shape[2]
        packed = xbuf[pl.ds(s, nblk)].reshape(nblk * MOE_BLOCK, half)
        x_lo = pltpu.bitcast(packed << 16, F32).astype(BF16)
        x_hi = pltpu.bitcast(packed & jnp.uint32(0xFFFF0000), F32).astype(BF16)
        gte = (jnp.dot(x_lo, wgb[:half, :], preferred_element_type=F32)
               + jnp.dot(x_hi, wgb[half:, :], preferred_element_type=F32))
        up = (jnp.dot(x_lo, wub[:half, :], preferred_element_type=F32)
              + jnp.dot(x_hi, wub[half:, :], preferred_element_type=F32))
        act = (gte * _sigmoid(gte) * up).astype(BF16)
        y = jnp.dot(act, wdb[...], preferred_element_type=F32)

        for j in range(nblk):
            @pl.when(g + j >= EXPERT_Y_RING)
            def _():
                y_copy(g + j - EXPERT_Y_RING, sy + j).wait()

        ybuf[pl.ds(sy, nblk)] = y.reshape(nblk, MOE_BLOCK, y.shape[1])
        for j in range(nblk):
            y_copy(g + j, sy + j).start()

    lead = jnp.where(cnt > 0, first & 1, 0)
    rest = cnt - lead

    @pl.when(lead == 1)
    def _():
        run_blocks(first, 1)

    lax.fori_loop(0, rest // 2, lambda b, c: (run_blocks(first + lead + 2 * b, 2), c)[1], 0)

    @pl.when((rest & 1) == 1)
    def _():
        run_blocks(first + cnt - 1, 1)

    @pl.when(e == pl.num_programs(0) - 1)
    def _():
        for back in range(1, EXPERT_Y_RING + 1):
            @pl.when(nb >= back)
            def _():
                y_copy(nb - back, lax.rem(nb - back, EXPERT_Y_RING)).wait()

        ybuf[0] = jnp.zeros(ybuf.shape[1:], F32)
        lax.fori_loop(nb, n_blocks, lambda g, c: (y_copy(g, 0).start(), c)[1], 0)
        lax.fori_loop(nb, n_blocks, lambda g, c: (y_copy(g, 0).wait(), c)[1], 0)


def experts(xs, w_gate, w_up, w_down, first_blk, n_blk, n_used):
    n_rows, d_packed = xs.shape
    n_experts, d, de = w_gate.shape
    grid_spec = pltpu.PrefetchScalarGridSpec(
        num_scalar_prefetch=3,
        grid=(n_experts,),
        in_specs=[pl.BlockSpec(memory_space=pl.ANY)] * 4,
        out_specs=pl.BlockSpec(memory_space=pl.ANY),
        scratch_shapes=[pltpu.VMEM((EXPERT_X_RING, MOE_BLOCK, d_packed), xs.dtype),
                        pltpu.VMEM((EXPERT_Y_RING, MOE_BLOCK, d), F32),
                        pltpu.VMEM((2, d, de), F32), pltpu.VMEM((2, d, de), F32), pltpu.VMEM((2, de, d), F32),
                        pltpu.VMEM((d, de), BF16), pltpu.VMEM((d, de), BF16), pltpu.VMEM((de, d), BF16),
                        pltpu.SemaphoreType.DMA((EXPERT_X_RING,)), pltpu.SemaphoreType.DMA((EXPERT_Y_RING,)),
                        pltpu.SemaphoreType.DMA((2,))],
    )
    return pl.pallas_call(
        _experts_kernel,
        grid_spec=grid_spec,
        out_shape=jax.ShapeDtypeStruct((n_rows, d), F32),
        compiler_params=_cparams(("arbitrary",)),
        name="moe_experts",
    )(first_blk, n_blk, n_used, xs, w_gate, w_up, w_down)


def _moe_combine_kernel(dest_ref, destn_ref, x_ref, info_ref, g_ref, ys_hbm, o_ref, ybuf, sem, *, norm):
    i = pl.program_id(0)
    n = pl.num_programs(0)
    tm = x_ref.shape[0]
    slot = lax.rem(i, 2)

    def gather_start(dref, s):
        for k in range(2):
            for r in range(tm):
                pltpu.make_async_copy(ys_hbm.at[pl.ds(dref[k, r], 1)], ybuf.at[s, k, pl.ds(r, 1)],
                                      sem.at[s]).start(priority=r % 2)

    @pl.when(i == 0)
    def _():
        gather_start(dest_ref, 0)

    @pl.when(i + 1 < n)
    def _():
        gather_start(destn_ref, 1 - slot)

    for k in range(2):
        pltpu.make_async_copy(ys_hbm.at[pl.ds(0, tm)], ybuf.at[slot, k], sem.at[slot]).wait()
    w1 = info_ref[:, 4:5]
    w2 = info_ref[:, 5:6]
    s = x_ref[...] + (w1 * ybuf[slot, 0] + w2 * ybuf[slot, 1])
    o_ref[...] = _rms(s, g_ref[...]) if norm else s


def moe_combine(x, ys, dest, info, g, *, norm, tm):
    t, d = x.shape
    n = t // tm
    return pl.pallas_call(
        functools.partial(_moe_combine_kernel, norm=norm),
        grid=(n,),
        in_specs=[pl.BlockSpec((2, tm), lambda i: (0, i), memory_space=pltpu.SMEM),
                  pl.BlockSpec((2, tm), lambda i: (0, jnp.minimum(i + 1, n - 1)), memory_space=pltpu.SMEM),
                  pl.BlockSpec((tm, d), lambda i: (i, 0)),
                  pl.BlockSpec((tm, LANES), lambda i: (i, 0)),
                  pl.BlockSpec((1, d), lambda i: (0, 0)),
                  pl.BlockSpec(memory_space=pl.ANY)],
        out_specs=pl.BlockSpec((tm, d), lambda i: (i, 0)),
        out_shape=jax.ShapeDtypeStruct((t, d), F32),
        scratch_shapes=[pltpu.VMEM((2, 2, tm, d), F32), pltpu.SemaphoreType.DMA((2,))],
        compiler_params=_cparams(("arbitrary",)),
        name="moe_combine",
    )(dest, dest, x, info, g.reshape(1, d), ys)


def forward(x, mem, norm_mix_g, w_in, s5_lambda_re, s5_lambda_im, s5_log_dt, s5_b_re, s5_b_im, s5_c_re,
            s5_c_im, s5_d, s5_w_glu, conv_dw_w, conv_dw_b, conv_ln_g, conv_ln_b, conv_w_out, w_out,
            norm_xattn_g, norm_mem_g, xattn_wq, xattn_wk, xattn_wv, xattn_wo, norm_moe_g, router_w_group,
            router_b_group, router_w_expert, router_b_expert, exp_w_gate, exp_w_up, exp_w_down, norm_final_g,
            *, tiles):
    batch, seq, d = x.shape
    depth = w_in.shape[0]
    n_mem = mem.shape[1]
    s5_width = s5_d.shape[1]
    conv_width = conv_dw_b.shape[1]
    n_groups, epg = router_w_expert.shape[1], router_w_expert.shape[3]
    t = batch * seq
    xf = x.reshape(t, d)
    memf = mem.reshape(batch * n_mem, d)
    gate_col0 = s5_width + 2 * conv_width
    for l in range(depth):
        w_in_b = w_in[l].astype(BF16)
        proj = norm_mm_rows(xf, norm_mix_g[l], w_in_b, tm=tiles["proj_tm"], tn=tiles["proj_tn"],
                            out_dtype=F32, n_cols=gate_col0)
        lbr, lbi, bbr, bbi = s5_discretise(s5_lambda_re[l], s5_lambda_im[l], s5_log_dt[l],
                                           s5_b_re[l], s5_b_im[l])
        bre, bim, cre, cim = s5_block_diag(bbr, bbi, s5_c_re[l], s5_c_im[l])
        y_a = s5_branch(proj, bre, bim, cre, cim, lbr.reshape(1, -1), lbi.reshape(1, -1), s5_d[l],
                        s5_w_glu[l].astype(BF16), batch=batch, seq=seq, width=s5_width, m=tiles["s5_m"],
                        cw=tiles["s5_cw"], tn=tiles["glu_tn"])
        y_b = conv_branch(proj, conv_dw_w[l], conv_dw_b[l], conv_ln_g[l], conv_ln_b[l],
                          conv_w_out[l].astype(BF16), batch=batch, seq=seq, width=conv_width,
                          col0=s5_width, tm=tiles["conv_tm"])
        xf = combine(xf, norm_mix_g[l], w_in_b, y_a, y_b, w_out[l].astype(BF16), gate_col0=gate_col0,
                     tm=tiles["comb_tm"])

        q = norm_mm_rows(xf, norm_xattn_g[l], xattn_wq[l].astype(BF16), tm=tiles["q_tm"], tn=tiles["q_tn"],
                         out_dtype=BF16)
        wkv = jnp.concatenate([xattn_wk[l], xattn_wv[l]], axis=1).astype(BF16)
        kv = norm_mm(memf, norm_mem_g[l], wkv, tm=batch * n_mem, tn=tiles["kv_tn"], out_dtype=BF16)
        w_r = jnp.concatenate([router_w_group[l],
                               jnp.transpose(router_w_expert[l], (1, 0, 2)).reshape(d, n_groups * epg)], axis=1)
        b_r = jnp.concatenate([router_b_group[l], router_b_expert[l].reshape(-1)])
        n_r = w_r.shape[1]
        w_pad = jnp.pad(w_r, ((0, 0), (0, LANES - n_r)))
        w_hi = w_pad.astype(BF16)
        w_pad = jnp.concatenate([w_hi, (w_pad - w_hi.astype(F32)).astype(BF16)], axis=1)
        b_pad = jnp.pad(b_r, (0, LANES - n_r)).reshape(1, LANES)
        xf, h, info, infot, counts = xattn_router(
            xf, q, kv, xattn_wo[l].astype(BF16), norm_moe_g[l], w_pad, b_pad, batch=batch, seq=seq,
            n_mem=n_mem, heads=XATTN_HEADS, n_groups=n_groups, epg=epg, tm=tiles["xattn_tm"])
        dest, n_rows, n_used, pcounts, pends = moe_dispatch_plan(infot, counts, t, n_groups * epg)
        xs = moe_dispatch(h, dest, pcounts, pends, n_rows=n_rows, tm=tiles["moe_tm"])
        ys = experts(xs, exp_w_gate[l], exp_w_up[l], exp_w_down[l], (pends - pcounts) // MOE_BLOCK,
                     pcounts // MOE_BLOCK, n_used)
        last = l + 1 == depth
        xf = moe_combine(xf, ys, dest, info, norm_final_g if last else jnp.ones((d,), F32), norm=last,
                         tm=tiles["moe_tm"])
    return xf.reshape(batch, seq, d)


def kernel(x, mem, norm_mix_g, w_in, s5_lambda_re, s5_lambda_im, s5_log_dt, s5_b_re, s5_b_im, s5_c_re, s5_c_im, s5_d, s5_w_glu, conv_dw_w, conv_dw_b, conv_ln_g, conv_ln_b, conv_w_out, w_out, norm_xattn_g, norm_mem_g, xattn_wq, xattn_wk, xattn_wv, xattn_wo, norm_moe_g, router_w_group, router_b_group, router_w_expert, router_b_expert, exp_w_gate, exp_w_up, exp_w_down, norm_final_g):
    return forward(x, mem, norm_mix_g, w_in, s5_lambda_re, s5_lambda_im, s5_log_dt, s5_b_re, s5_b_im,
                   s5_c_re, s5_c_im, s5_d, s5_w_glu, conv_dw_w, conv_dw_b, conv_ln_g, conv_ln_b, conv_w_out,
                   w_out, norm_xattn_g, norm_mem_g, xattn_wq, xattn_wk, xattn_wv, xattn_wo, norm_moe_g,
                   router_w_group, router_b_group, router_w_expert, router_b_expert, exp_w_gate, exp_w_up,
                   exp_w_down, norm_final_g, tiles=TILES)
```

```python
import functools
import math

import jax
import jax.numpy as jnp
from jax import lax
from jax.experimental import pallas as pl
from jax.experimental.pallas import tpu as pltpu

F32 = jnp.float32
BF16 = jnp.bfloat16

RMS_EPS = 1e-6
LN_EPS = 1e-5

S5_SLAB_CH = 256
CONV_HALO = 32
CONV_ROW_CHUNK = 64
XATTN_HEADS = 4
MOE_BLOCK = 128
BLOCK_DMA_QUEUE = 1
EXPERT_X_RING = 8
EXPERT_Y_RING = 4
SUBLANES = 8
LANES = 128
VMEM_LIMIT = 56 * 1024 * 1024

TILES = dict(proj_tm=512, proj_tn=512, s5_m=64, s5_cw=512, glu_tn=512, conv_tm=512,
             comb_tm=256, q_tm=1024, q_tn=512, kv_tn=512, xattn_tm=512, dispatch_tm=1024, moe_tm=256)


def _cparams(sem):
    return pltpu.CompilerParams(dimension_semantics=sem, vmem_limit_bytes=VMEM_LIMIT)


def _rms(x, g):
    return x * lax.rsqrt(jnp.mean(x * x, axis=-1, keepdims=True) + RMS_EPS) * g


def _sigmoid(x):
    return 0.5 * jnp.tanh(0.5 * x) + 0.5


def _gelu_tanh(x):
    c = math.sqrt(2.0 / math.pi)
    return 0.5 * x * (1.0 + jnp.tanh(c * (x + 0.044715 * (x * x * x))))


def _norm_mm_kernel(x_ref, g_ref, w_ref, o_ref, h_ref, *, act_from):
    j = pl.program_id(1)

    @pl.when(j == 0)
    def _():
        h_ref[...] = _rms(x_ref[...], g_ref[...]).astype(BF16)

    y = jnp.dot(h_ref[...], w_ref[...], preferred_element_type=F32)
    if act_from is None:
        o_ref[...] = y.astype(o_ref.dtype)
    else:
        @pl.when(j < act_from)
        def _():
            o_ref[...] = y.astype(o_ref.dtype)

        @pl.when(j >= act_from)
        def _():
            o_ref[...] = _sigmoid(y).astype(o_ref.dtype)


def norm_mm(x, g, w, *, tm, tn, out_dtype=F32, act_from=None):
    m, k = x.shape
    n = w.shape[1]
    return pl.pallas_call(
        functools.partial(_norm_mm_kernel, act_from=act_from),
        grid=(m // tm, n // tn),
        in_specs=[pl.BlockSpec((tm, k), lambda i, j: (i, 0)),
                  pl.BlockSpec((1, k), lambda i, j: (0, 0)),
                  pl.BlockSpec((k, tn), lambda i, j: (0, j))],
        out_specs=pl.BlockSpec((tm, tn), lambda i, j: (i, j)),
        out_shape=jax.ShapeDtypeStruct((m, n), out_dtype),
        scratch_shapes=[pltpu.VMEM((tm, k), BF16)],
        compiler_params=_cparams(("parallel", "arbitrary")),
        name="norm_mm",
    )(x, g.reshape(1, k), w)


def _norm_mm_rows_kernel(x_ref, g_ref, w_ref, o_ref, *, tn):
    h = _rms(x_ref[...], g_ref[...]).astype(BF16)
    for j in range(o_ref.shape[1] // tn):
        cols = slice(j * tn, (j + 1) * tn)
        o_ref[:, cols] = jnp.dot(h, w_ref[:, cols], preferred_element_type=F32).astype(o_ref.dtype)


def norm_mm_rows(x, g, w, *, tm, tn, out_dtype, n_cols=None):
    m, k = x.shape
    n = w.shape[1] if n_cols is None else n_cols
    return pl.pallas_call(
        functools.partial(_norm_mm_rows_kernel, tn=tn),
        grid=(m // tm,),
        in_specs=[pl.BlockSpec((tm, k), lambda i: (i, 0)),
                  pl.BlockSpec((1, k), lambda i: (0, 0)),
                  pl.BlockSpec((k, n), lambda i: (0, 0), pipeline_mode=pl.Buffered(1))],
        out_specs=pl.BlockSpec((tm, n), lambda i: (i, 0)),
        out_shape=jax.ShapeDtypeStruct((m, n), out_dtype),
        compiler_params=_cparams(("parallel",)),
        name="norm_mm_rows",
    )(x, g.reshape(1, k), w)


def _s5_disc_kernel(lr_ref, li_ref, ldt_ref, br_ref, bi_ref, lbr_ref, lbi_ref, bbr_ref, bbi_ref):
    lr, li = lr_ref[...], li_ref[...]
    dt = jnp.exp(ldt_ref[...])
    mag = jnp.exp(lr * dt)
    ang = li * dt
    lb_re, lb_im = mag * jnp.cos(ang), mag * jnp.sin(ang)
    den = lr * lr + li * li
    nr, ni = lb_re - 1.0, lb_im
    coef_re = (nr * lr + ni * li) / den
    coef_im = (ni * lr - nr * li) / den
    lbr_ref[...] = lb_re
    lbi_ref[...] = lb_im
    br, bi = br_ref[...], bi_ref[...]
    bbr_ref[...] = coef_re[None] * br - coef_im[None] * bi
    bbi_ref[...] = coef_re[None] * bi + coef_im[None] * br


def s5_discretise(lam_re, lam_im, log_dt, b_re, b_im):
    g, n = lam_re.shape
    c = b_re.shape[-1]
    b_re_t = jnp.transpose(b_re, (2, 0, 1))
    b_im_t = jnp.transpose(b_im, (2, 0, 1))
    gn = jax.ShapeDtypeStruct((g, n), F32)
    cgn = jax.ShapeDtypeStruct((c, g, n), F32)
    return pl.pallas_call(
        _s5_disc_kernel,
        out_shape=(gn, gn, cgn, cgn),
        name="s5_discretise",
    )(lam_re, lam_im, log_dt.reshape(g, 1), b_re_t, b_im_t)


def s5_block_diag(bb_re, bb_im, c_re, c_im):
    c, g, n = bb_re.shape
    gs = S5_SLAB_CH // c
    n_slab = g // gs
    row_grp_b = jnp.arange(gs * c) // c
    col_grp_b = jnp.arange(gs * n) // n
    tile_n = (jnp.arange(n)[:, None] == jnp.arange(gs * n)[None, :] % n).astype(F32)
    tile_c = (jnp.arange(c)[:, None] == jnp.arange(gs * c)[None, :] % c).astype(F32)
    exact = lax.Precision.HIGHEST

    def bmat(bb):
        b = jnp.transpose(bb, (1, 0, 2)).reshape(n_slab, gs * c, n)
        full = jnp.einsum('srn,nq->srq', b, tile_n, precision=exact)
        return jnp.where(row_grp_b[:, None] == col_grp_b[None, :], full, 0.0).astype(BF16)

    def cmat(cc):
        cm = jnp.transpose(cc.reshape(n_slab, gs, c, n), (0, 1, 3, 2)).reshape(n_slab, gs * n, c)
        full = jnp.einsum('spc,cr->spr', cm, tile_c, precision=exact)
        return jnp.where(col_grp_b[:, None] == row_grp_b[None, :], full, 0.0).astype(BF16)

    return bmat(bb_re), bmat(bb_im), cmat(c_re), cmat(c_im)


def _s5_kernel(proj_hbm, bre_ref, bim_ref, cre_ref, cim_ref, lre_ref, lim_ref, d_ref, wglu_ref, ya_hbm,
               ubuf, yabuf, zbuf, xre_ref, xim_ref, pre_ref, pim_ref, car_re_ref, car_im_ref, isem, osem,
               *, m, cw, tn):
    t = pl.program_id(1)
    n_t = pl.num_programs(1)
    step_id = pl.program_id(0) * n_t + t
    n_steps = pl.num_programs(0) * n_t
    slot = lax.rem(step_id, 2)
    rows = SUBLANES * m
    width = d_ref.shape[1]
    n_state = lre_ref.shape[1]
    n_slab = bre_ref.shape[0]
    slab_states = n_state // n_slab

    def in_copy(step, s, k):
        r0 = pl.multiple_of(step * rows + k * m, SUBLANES)
        return pltpu.make_async_copy(proj_hbm.at[pl.ds(r0, m), pl.ds(0, width)], ubuf.at[s, :, k, :], isem.at[s])

    def out_copy(step, s, k):
        r0 = pl.multiple_of(step * rows + k * m, SUBLANES)
        return pltpu.make_async_copy(yabuf.at[s, :, k, :], ya_hbm.at[pl.ds(r0, m)], osem.at[s])

    @pl.when(step_id == 0)
    def _():
        for k in range(SUBLANES):
            in_copy(0, 0, k).start()

    @pl.when(step_id + 1 < n_steps)
    def _():
        for k in range(SUBLANES):
            in_copy(step_id + 1, 1 - slot, k).start()

    @pl.when(t == 0)
    def _():
        car_re_ref[...] = jnp.zeros_like(car_re_ref)
        car_im_ref[...] = jnp.zeros_like(car_im_ref)
        lr, li = lre_ref[...], lim_ref[...]

        def pw(_, carry):
            pr, pi = carry
            return pr * lr - pi * li, pr * li + pi * lr

        pr, pi = lax.fori_loop(1, m, pw, (lr, li))
        pre_ref[...] = pr
        pim_ref[...] = pi

    @pl.when(step_id == 0)
    def _():
        zbuf[...] = jnp.zeros_like(zbuf)

    @pl.when(step_id >= 3)
    def _():
        for k in range(SUBLANES):
            out_copy(step_id - 3, 1 - slot, k).wait()

    nh = wglu_ref.shape[1] // 2

    def glu_pieces(z, s):
        def piece(j):
            val = jnp.dot(z, wglu_ref[:, j * tn:(j + 1) * tn], preferred_element_type=F32)
            gt = jnp.dot(z, wglu_ref[:, nh + j * tn:nh + (j + 1) * tn], preferred_element_type=F32)
            yabuf[s, :, :, j * tn:(j + 1) * tn] = (val * _sigmoid(gt)).reshape(m, SUBLANES, tn)
        return [functools.partial(piece, j) for j in range(nh // tn)]

    for k in range(SUBLANES):
        in_copy(step_id, slot, k).wait()
    u = ubuf[slot].reshape(rows, width)
    up = u.astype(BF16)

    def b_project(s):
        us = up[:, s * S5_SLAB_CH:(s + 1) * S5_SLAB_CH]
        cols = slice(s * slab_states, (s + 1) * slab_states)
        xre_ref[:, cols] = jnp.dot(us, bre_ref[s], preferred_element_type=F32)
        xim_ref[:, cols] = jnp.dot(us, bim_ref[s], preferred_element_type=F32)

    def c_project(s):
        cols = slice(s * slab_states, (s + 1) * slab_states)
        return (jnp.dot(xre_ref[:, cols].astype(BF16), cre_ref[s], preferred_element_type=F32)
                - jnp.dot(xim_ref[:, cols].astype(BF16), cim_ref[s], preferred_element_type=F32))

    n_cb = n_state // cw
    cb_per_slab = slab_states // cw
    glu_prev = glu_pieces(zbuf[...], 1 - slot)
    glu_at = {(i * n_cb) // len(glu_prev): [] for i in range(len(glu_prev))}
    for i, piece in enumerate(glu_prev):
        glu_at[(i * n_cb) // len(glu_prev)].append(piece)
    ys = []
    b_project(0)
    row_id = lax.broadcasted_iota(jnp.int32, (SUBLANES, cw), 0)
    for cb in range(n_cb):
        cols = slice(cb * cw, (cb + 1) * cw)
        slab = cb // cb_per_slab
        if cb % cb_per_slab == 0:
            if slab + 1 < n_slab:
                b_project(slab + 1)
            if slab >= 1:
                ys.append(c_project(slab - 1))
        for piece in glu_at.get(cb, []):
            piece()
        lr = jnp.broadcast_to(lre_ref[:, cols], (SUBLANES, cw))
        li = jnp.broadcast_to(lim_ref[:, cols], (SUBLANES, cw))

        def step(tau, carry, store, cols=cols, lr=lr, li=li):
            sr, si = carry
            r0 = pl.multiple_of(tau * SUBLANES, SUBLANES)
            nr = sr * lr - si * li + xre_ref[pl.ds(r0, SUBLANES), cols]
            ni = sr * li + si * lr + xim_ref[pl.ds(r0, SUBLANES), cols]
            if store:
                xre_ref[pl.ds(r0, SUBLANES), cols] = nr
                xim_ref[pl.ds(r0, SUBLANES), cols] = ni
            return nr, ni

        zero = jnp.zeros((SUBLANES, cw), F32)
        er, ei = lax.fori_loop(0, m, functools.partial(step, store=False), (zero, zero), unroll=True)

        lmr = jnp.broadcast_to(pre_ref[:, cols], (SUBLANES, cw))
        lmi = jnp.broadcast_to(pim_ref[:, cols], (SUBLANES, cw))
        ir = jnp.where(row_id == 0, jnp.broadcast_to(car_re_ref[:, cols], (SUBLANES, cw)), 0.0)
        ii = jnp.where(row_id == 0, jnp.broadcast_to(car_im_ref[:, cols], (SUBLANES, cw)), 0.0)
        for k in range(1, SUBLANES):
            nr = ir * lmr - ii * lmi + er
            ni = ir * lmi + ii * lmr + ei
            ir = jnp.where(row_id == k, pltpu.roll(nr, 1, axis=0), ir)
            ii = jnp.where(row_id == k, pltpu.roll(ni, 1, axis=0), ii)

        fr, fi = lax.fori_loop(0, m, functools.partial(step, store=True), (ir, ii), unroll=True)
        car_re_ref[:, cols] = fr[SUBLANES - 1:SUBLANES, :]
        car_im_ref[:, cols] = fi[SUBLANES - 1:SUBLANES, :]

    ys.append(c_project(n_slab - 1))
    y = jnp.concatenate(ys, axis=1) + d_ref[...] * u
    zbuf[...] = _gelu_tanh(y).astype(BF16)

    @pl.when(step_id >= 1)
    def _():
        for k in range(SUBLANES):
            out_copy(step_id - 1, 1 - slot, k).start()

    @pl.when(step_id == n_steps - 1)
    def _():
        @pl.when(step_id >= 2)
        def _():
            for k in range(SUBLANES):
                out_copy(step_id - 2, slot, k).wait()
        for piece in glu_pieces(zbuf[...], slot):
            piece()
        for k in range(SUBLANES):
            out_copy(step_id, slot, k).start()
        for k in range(SUBLANES):
            out_copy(step_id, slot, k).wait()

        @pl.when(step_id >= 1)
        def _():
            for k in range(SUBLANES):
                out_copy(step_id - 1, 1 - slot, k).wait()


def s5_branch(proj, bre, bim, cre, cim, lre, lim, d_skip, w_glu, *, batch, seq, width, m, cw, tn):
    rows = SUBLANES * m
    n_t = seq // rows
    n_state = lre.shape[1]
    nh = w_glu.shape[1] // 2
    once = pl.Buffered(1)
    const3 = lambda b, t: (0, 0, 0)
    const2 = lambda b, t: (0, 0)
    return pl.pallas_call(
        functools.partial(_s5_kernel, m=m, cw=cw, tn=tn),
        grid=(batch, n_t),
        in_specs=[pl.BlockSpec(memory_space=pl.ANY),
                  pl.BlockSpec(bre.shape, const3, pipeline_mode=once),
                  pl.BlockSpec(bim.shape, const3, pipeline_mode=once),
                  pl.BlockSpec(cre.shape, const3, pipeline_mode=once),
                  pl.BlockSpec(cim.shape, const3, pipeline_mode=once),
                  pl.BlockSpec((1, n_state), const2),
                  pl.BlockSpec((1, n_state), const2),
                  pl.BlockSpec((1, width), const2),
                  pl.BlockSpec(w_glu.shape, const2, pipeline_mode=once)],
        out_specs=pl.BlockSpec(memory_space=pl.ANY),
        out_shape=jax.ShapeDtypeStruct((batch * seq, nh), F32),
        scratch_shapes=[pltpu.VMEM((2, m, SUBLANES, width), F32), pltpu.VMEM((2, m, SUBLANES, nh), F32),
                        pltpu.VMEM((rows, width), BF16),
                        pltpu.VMEM((rows, n_state), F32), pltpu.VMEM((rows, n_state), F32),
                        pltpu.VMEM((1, n_state), F32), pltpu.VMEM((1, n_state), F32),
                        pltpu.VMEM((1, n_state), F32), pltpu.VMEM((1, n_state), F32),
                        pltpu.SemaphoreType.DMA((2,)), pltpu.SemaphoreType.DMA((2,))],
        compiler_params=_cparams(("arbitrary", "arbitrary")),
        name="s5_branch",
    )(proj, bre, bim, cre, cim, lre, lim, d_skip.reshape(1, width), w_glu)


def _conv_kernel(a_ref, b_ref, ah_ref, bh_ref, dww_ref, dwb_ref, lng_ref, lnb_ref, w_ref, o_ref,
                 zs_ref, wb_ref, yc_ref, *, tm, taps):
    i = pl.program_id(1)
    off0 = CONV_HALO - (taps - 1)
    width = a_ref.shape[1]
    n_slab = width // LANES
    zh = jnp.where(i == 0, 0.0, ah_ref[...] * _sigmoid(bh_ref[...]))
    z = a_ref[...] * _sigmoid(b_ref[...])
    for s in range(n_slab):
        zs_ref[s, 0:CONV_HALO, :] = zh[:, s * LANES:(s + 1) * LANES]
        zs_ref[s, CONV_HALO:, :] = z[:, s * LANES:(s + 1) * LANES]

    @pl.when(i == 0)
    def _():
        for j in range(taps):
            wb_ref[j] = jnp.broadcast_to(dww_ref[j:j + 1, :], (SUBLANES, width))
        wb_ref[taps] = jnp.broadcast_to(dwb_ref[...], (SUBLANES, width))

    rg = CONV_ROW_CHUNK // SUBLANES

    def chunk(c, carry):
        r0 = pl.multiple_of(c * CONV_ROW_CHUNK, CONV_ROW_CHUNK)
        for s in range(n_slab):
            lanes = slice(s * LANES, (s + 1) * LANES)
            bias = wb_ref[taps, :, lanes]
            acc = [bias] * rg
            for j in range(taps):
                w = wb_ref[j, :, lanes]
                for g in range(rg):
                    acc[g] = acc[g] + w * zs_ref[s, pl.ds(r0 + g * SUBLANES + off0 + j, SUBLANES), :]
            for g in range(rg):
                yc_ref[pl.ds(pl.multiple_of(r0 + g * SUBLANES, SUBLANES), SUBLANES), lanes] = acc[g]
        return carry

    lax.fori_loop(0, tm // CONV_ROW_CHUNK, chunk, 0)
    y = yc_ref[...]
    mu = jnp.mean(y, axis=-1, keepdims=True)
    var = jnp.mean(jnp.square(y - mu), axis=-1, keepdims=True)
    y = (y - mu) * lax.rsqrt(var + LN_EPS) * lng_ref[...] + lnb_ref[...]
    y = y * _sigmoid(y)
    o_ref[...] = jnp.dot(y.astype(BF16), w_ref[...], preferred_element_type=F32)


def conv_branch(proj, dw_w, dw_b, ln_g, ln_b, w_out, *, batch, seq, width, col0, tm):
    n_t = seq // tm
    ca, cb = col0 // width, col0 // width + 1
    hb = tm // CONV_HALO
    taps = dw_w.shape[0]
    d_out = w_out.shape[1]
    const2 = lambda b, i: (0, 0)
    halo = lambda b, i: jnp.maximum((b * n_t + i) * hb - 1, 0)
    return pl.pallas_call(
        functools.partial(_conv_kernel, tm=tm, taps=taps),
        grid=(batch, n_t),
        in_specs=[pl.BlockSpec((tm, width), lambda b, i: (b * n_t + i, ca)),
                  pl.BlockSpec((tm, width), lambda b, i: (b * n_t + i, cb)),
                  pl.BlockSpec((CONV_HALO, width), lambda b, i: (halo(b, i), ca)),
                  pl.BlockSpec((CONV_HALO, width), lambda b, i: (halo(b, i), cb)),
                  pl.BlockSpec((taps, width), const2),
                  pl.BlockSpec((1, width), const2),
                  pl.BlockSpec((1, width), const2),
                  pl.BlockSpec((1, width), const2),
                  pl.BlockSpec((width, d_out), const2)],
        out_specs=pl.BlockSpec((tm, d_out), lambda b, i: (b * n_t + i, 0)),
        out_shape=jax.ShapeDtypeStruct((batch * seq, d_out), F32),
        scratch_shapes=[pltpu.VMEM((width // LANES, tm + CONV_HALO, LANES), F32),
                        pltpu.VMEM((taps + 1, SUBLANES, width), F32),
                        pltpu.VMEM((tm, width), F32)],
        compiler_params=_cparams(("parallel", "arbitrary")),
        name="conv_branch",
    )(proj, proj, proj, proj, dw_w, dw_b.reshape(1, width), ln_g.reshape(1, width),
      ln_b.reshape(1, width), w_out)


def _combine_kernel(*refs, parts):
    x_ref, g_ref = refs[0], refs[1]
    ya_ref, yb_ref, wo_ref, o_ref = refs[2 + 2 * parts], refs[3 + 2 * parts], refs[-2], refs[-1]
    gw = ya_ref.shape[1] // parts
    x = x_ref[...]
    h = _rms(x, g_ref[...]).astype(BF16)
    o_ref[...] = x
    for p in range(parts):
        cols = slice(p * gw, (p + 1) * gw)
        ga = _sigmoid(jnp.dot(h, refs[2 + p][...], preferred_element_type=F32))
        gb = _sigmoid(jnp.dot(h, refs[2 + parts + p][...], preferred_element_type=F32))
        mix = ga * ya_ref[:, cols] + gb * yb_ref[:, cols]
        o_ref[...] += jnp.dot(mix.astype(BF16), wo_ref[cols, :], preferred_element_type=F32)


def combine(x, g, w_in, y_a, y_b, w_out, *, gate_col0, tm):
    m, d = x.shape
    gw = math.gcd(gate_col0, d)
    parts = d // gw
    row = lambda i: (i, 0)
    once = pl.Buffered(1)
    gate_specs = [pl.BlockSpec((d, gw), functools.partial(lambda i, c: (0, c), c=(gate_col0 + s * d) // gw + p),
                               pipeline_mode=once) for s in range(2) for p in range(parts)]
    return pl.pallas_call(
        functools.partial(_combine_kernel, parts=parts),
        grid=(m // tm,),
        in_specs=[pl.BlockSpec((tm, d), row), pl.BlockSpec((1, d), lambda i: (0, 0))] + gate_specs +
                 [pl.BlockSpec((tm, d), row),
                  pl.BlockSpec((tm, d), row),
                  pl.BlockSpec((d, d), lambda i: (0, 0), pipeline_mode=once)],
        out_specs=pl.BlockSpec((tm, d), row),
        out_shape=jax.ShapeDtypeStruct((m, d), F32),
        compiler_params=_cparams(("parallel",)),
        name="gated_combine",
    )(x, g.reshape(1, d), *([w_in] * (2 * parts)), y_a, y_b, w_out)


def _route_tile(x, g_ref, w_ref, b_ref, tri_ref, h_ref, info_ref, infot_ref, cnt_ref, run_ref, *,
                n_groups, epg):
    tm = x.shape[0]
    h = _rms(x, g_ref[...])
    half = h.shape[1] // 2
    bits = pltpu.bitcast(h.astype(BF16).astype(F32), jnp.uint32)
    h_ref[...] = (bits[:, :half] >> 16) | (bits[:, half:] & jnp.uint32(0xFFFF0000))
    h_hi = h.astype(BF16)
    h_lo = (h - h_hi.astype(F32)).astype(BF16)
    p = jnp.dot(h_hi, w_ref[...], preferred_element_type=F32)
    logits = (p[:, :LANES] + p[:, LANES:] + jnp.dot(h_lo, w_ref[:, :LANES], preferred_element_type=F32)
              + b_ref[...])
    lane = lax.broadcasted_iota(jnp.int32, (tm, LANES), 1).astype(F32)
    neg = jnp.float32(-jnp.inf)
    big = jnp.float32(LANES)
    is_grp = lane < n_groups
    cl = jnp.where(is_grp, logits, neg)
    cmax = jnp.max(cl, axis=-1, keepdims=True)
    g_idx = jnp.min(jnp.where(cl == cmax, lane, big), axis=-1, keepdims=True)
    p_sel = 1.0 / jnp.sum(jnp.where(is_grp, jnp.exp(cl - cmax), 0.0), axis=-1, keepdims=True)
    lo = n_groups + g_idx * epg
    in_grp = (lane >= lo) & (lane < lo + epg)
    fl = jnp.where(in_grp, logits, neg)
    v1 = jnp.max(fl, axis=-1, keepdims=True)
    i1 = jnp.min(jnp.where(fl == v1, lane, big), axis=-1, keepdims=True)
    fl2 = jnp.where(lane == i1, neg, fl)
    v2 = jnp.max(fl2, axis=-1, keepdims=True)
    i2 = jnp.min(jnp.where(fl2 == v2, lane, big), axis=-1, keepdims=True)
    e2x = jnp.exp(v2 - v1)
    w1 = p_sel / (1.0 + e2x)
    w2 = p_sel * e2x / (1.0 + e2x)
    e1 = i1 - n_groups
    e2 = i2 - n_groups
    oh1 = lane == e1
    oh2 = lane == e2
    both = jnp.where(oh1 | oh2, 1.0, 0.0)
    before = jnp.dot(tri_ref[...], both.astype(BF16), preferred_element_type=F32) + run_ref[...]
    r1 = jnp.sum(jnp.where(oh1, before, 0.0), axis=-1, keepdims=True)
    r2 = jnp.sum(jnp.where(oh2, before, 0.0), axis=-1, keepdims=True)
    run_ref[...] += jnp.sum(both, axis=0, keepdims=True)
    cnt_ref[...] = run_ref[...]
    info = jnp.where(lane == 0, e1, 0.0)
    info = jnp.where(lane == 1, e2, info)
    info = jnp.where(lane == 2, r1, info)
    info = jnp.where(lane == 3, r2, info)
    info = jnp.where(lane == 4, w1, info)
    info = jnp.where(lane == 5, w2, info)
    info_ref[...] = info
    infot_ref[...] = jnp.transpose(info)[0:SUBLANES, :]


def _xattn_router_kernel(x_ref, q_ref, kv_ref, wo_ref, g_ref, w_ref, b_ref, tri_ref,
                         o_ref, h_ref, info_ref, infot_ref, cnt_ref, run_ref, *, heads, scale, n_groups, epg):
    @pl.when((pl.program_id(0) == 0) & (pl.program_id(1) == 0))
    def _():
        run_ref[...] = jnp.zeros_like(run_ref)

    d = x_ref.shape[1]
    hd = d // heads
    acc = x_ref[...]
    for h in range(heads):
        q = q_ref[:, h * hd:(h + 1) * hd]
        k = kv_ref[:, h * hd:(h + 1) * hd]
        v = kv_ref[:, d + h * hd:d + (h + 1) * hd]
        s = lax.dot_general(q, k, (((1,), (1,)), ((), ())), preferred_element_type=F32) * scale
        s = s - jnp.max(s, axis=-1, keepdims=True)
        p = jnp.exp(s)
        p = p / jnp.sum(p, axis=-1, keepdims=True)
        o = jnp.dot(p.astype(BF16), v, preferred_element_type=F32)
        acc = acc + jnp.dot(o.astype(BF16), wo_ref[h * hd:(h + 1) * hd, :], preferred_element_type=F32)
    o_ref[...] = acc
    _route_tile(acc, g_ref, w_ref, b_ref, tri_ref, h_ref, info_ref, infot_ref, cnt_ref, run_ref,
                n_groups=n_groups, epg=epg)


def xattn_router(x, q, kv, wo, g, w_pad, b_pad, *, batch, seq, n_mem, heads, n_groups, epg, tm):
    d = x.shape[1]
    m = batch * seq
    n_t = seq // tm
    scale = 1.0 / math.sqrt(d // heads)
    tri = (jnp.arange(tm)[:, None] > jnp.arange(tm)[None, :]).astype(BF16)
    row = lambda b, i: (b * n_t + i, 0)
    const = lambda b, i: (0, 0)
    once = pl.Buffered(1)
    return pl.pallas_call(
        functools.partial(_xattn_router_kernel, heads=heads, scale=scale, n_groups=n_groups, epg=epg),
        grid=(batch, n_t),
        in_specs=[pl.BlockSpec((tm, d), row),
                  pl.BlockSpec((tm, d), row),
                  pl.BlockSpec((n_mem, 2 * d), lambda b, i: (b, 0)),
                  pl.BlockSpec((d, d), const, pipeline_mode=once),
                  pl.BlockSpec((1, d), const),
                  pl.BlockSpec((d, 2 * LANES), const, pipeline_mode=once),
                  pl.BlockSpec((1, LANES), const),
                  pl.BlockSpec((tm, tm), const, pipeline_mode=once)],
        out_specs=[pl.BlockSpec((tm, d), row),
                   pl.BlockSpec((tm, d // 2), row),
                   pl.BlockSpec((tm, LANES), row),
                   pl.BlockSpec((SUBLANES, tm), lambda b, i: (0, b * n_t + i)),
                   pl.BlockSpec((1, LANES), const)],
        out_shape=(jax.ShapeDtypeStruct((m, d), F32),
                   jax.ShapeDtypeStruct((m, d // 2), jnp.uint32),
                   jax.ShapeDtypeStruct((m, LANES), F32),
                   jax.ShapeDtypeStruct((SUBLANES, m), F32),
                   jax.ShapeDtypeStruct((1, LANES), F32)),
        scratch_shapes=[pltpu.VMEM((1, LANES), F32)],
        compiler_params=_cparams(("arbitrary", "arbitrary")),
        name="xattn_router",
    )(x, q, kv, wo, g.reshape(1, d), w_pad, b_pad, tri)


def moe_dispatch_plan(infot, counts, n_tokens, n_experts):
    e = infot[0:2].astype(jnp.int32)
    rank = infot[2:4].astype(jnp.int32)
    counts = counts[0, :n_experts].astype(jnp.int32)
    pcounts = ((counts + MOE_BLOCK - 1) // MOE_BLOCK) * MOE_BLOCK
    pends = jnp.cumsum(pcounts)
    pstarts = pends - pcounts
    ids = jnp.arange(n_experts, dtype=jnp.int32)
    dest = rank + jnp.sum(jnp.where(e[:, :, None] == ids, pstarts, 0), axis=-1)
    n_rows = (-(-2 * n_tokens // MOE_BLOCK) + n_experts) * MOE_BLOCK
    n_used = (pends[-1] // MOE_BLOCK).reshape(1)
    return dest, n_rows, n_used, pcounts, pends


def _dispatch_kernel(pc_ref, pe_ref, dest_ref, h_ref, xs_hbm, zbuf, sem, tsem, *, n_experts):
    i = pl.program_id(0)
    tm = h_ref.shape[0]

    @pl.when(i == 0)
    def _():
        zbuf[...] = jnp.zeros_like(zbuf)
        for wait in (False, True):
            for e in range(n_experts):
                @pl.when(pc_ref[e] > 0)
                def _():
                    row0 = pl.multiple_of(pe_ref[e] - MOE_BLOCK, MOE_BLOCK)
                    fill = pltpu.make_async_copy(zbuf, xs_hbm.at[pl.ds(row0, MOE_BLOCK)], sem)
                    fill.wait() if wait else fill.start()

    n_blocks = xs_hbm.shape[0] // MOE_BLOCK
    n_used = pe_ref[n_experts - 1] // MOE_BLOCK

    def tail_fill(b, wait):
        row0 = pl.multiple_of(b * MOE_BLOCK, MOE_BLOCK)
        fill = pltpu.make_async_copy(zbuf, xs_hbm.at[pl.ds(row0, MOE_BLOCK)], tsem)
        fill.wait() if wait else fill.start()

    @pl.when(i == 0)
    def _():
        lax.fori_loop(n_used, n_blocks, lambda b, c: (tail_fill(b, False), c)[1], 0)

    for k in range(2):
        for r in range(tm):
            pltpu.make_async_copy(h_ref.at[pl.ds(r, 1)], xs_hbm.at[pl.ds(dest_ref[k, r], 1)],
                                  sem).start(priority=r % 2)
    for k in range(2):
        pltpu.make_async_copy(h_ref, xs_hbm.at[pl.ds(0, tm)], sem).wait()

    @pl.when(i == pl.num_programs(0) - 1)
    def _():
        lax.fori_loop(n_used, n_blocks, lambda b, c: (tail_fill(b, True), c)[1], 0)


def moe_dispatch(h, dest, pcounts, pends, *, n_rows, tm):
    t, d = h.shape
    grid_spec = pltpu.PrefetchScalarGridSpec(
        num_scalar_prefetch=2,
        grid=(t // tm,),
        in_specs=[pl.BlockSpec((2, tm), lambda i, pc, pe: (0, i), memory_space=pltpu.SMEM),
                  pl.BlockSpec((tm, d), lambda i, pc, pe: (i, 0))],
        out_specs=pl.BlockSpec(memory_space=pl.ANY),
        scratch_shapes=[pltpu.VMEM((MOE_BLOCK, d), h.dtype), pltpu.SemaphoreType.DMA(()),
                        pltpu.SemaphoreType.DMA(())],
    )
    return pl.pallas_call(
        functools.partial(_dispatch_kernel, n_experts=pcounts.shape[0]),
        grid_spec=grid_spec,
        out_shape=jax.ShapeDtypeStruct((n_rows, d), h.dtype),
        compiler_params=_cparams(("arbitrary",)),
        name="moe_dispatch",
    )(pcounts, pends, dest, h)


def _experts_kernel(first_ref, cnt_ref, nb_ref, xs_hbm, wg_hbm, wu_hbm, wd_hbm, ys_hbm,
                    xbuf, ybuf, wgf, wuf, wdf, wgb, wub, wdb, xsem, ysem, wsem):
    e = pl.program_id(0)
    n_e = pl.num_programs(0)
    nb = nb_ref[0]
    first, cnt = first_ref[e], cnt_ref[e]
    n_blocks = xs_hbm.shape[0] // MOE_BLOCK
    wslot = lax.rem(e, 2)

    def rows(g):
        return pl.ds(pl.multiple_of(g * MOE_BLOCK, MOE_BLOCK), MOE_BLOCK)

    def x_copy(g, s):
        return pltpu.make_async_copy(xs_hbm.at[rows(g)], xbuf.at[s], xsem.at[s])

    def y_copy(g, s):
        return pltpu.make_async_copy(ybuf.at[s], ys_hbm.at[rows(g)], ysem.at[s])

    def w_copies(ex, s):
        half = wd_hbm.shape[1] // 2
        lo, hi = pl.ds(0, half), pl.ds(half, half)
        return ((pltpu.make_async_copy(wg_hbm.at[ex], wgf.at[s], wsem.at[s]), 0),
                (pltpu.make_async_copy(wu_hbm.at[ex], wuf.at[s], wsem.at[s]), 1),
                (pltpu.make_async_copy(wd_hbm.at[ex, lo], wdf.at[s, lo], wsem.at[s]), 0),
                (pltpu.make_async_copy(wd_hbm.at[ex, hi], wdf.at[s, hi], wsem.at[s]), 1))

    @pl.when(e == 0)
    def _():
        for g0 in range(EXPERT_X_RING - 2):
            @pl.when(g0 < nb)
            def _():
                x_copy(g0, g0).start(priority=BLOCK_DMA_QUEUE)

    @pl.when((e == 0) & (cnt > 0))
    def _():
        for cp, queue in w_copies(0, 0):
            cp.start(priority=queue)

    e_next = jnp.minimum(e + 1, n_e - 1)

    @pl.when((e + 1 < n_e) & (cnt_ref[e_next] > 0))
    def _():
        for cp, queue in w_copies(e_next, 1 - wslot):
            cp.start(priority=queue)

    @pl.when(cnt > 0)
    def _():
        for cp, _ in w_copies(e, wslot):
            cp.wait()
        wgb[...] = wgf[wslot].astype(BF16)
        wub[...] = wuf[wslot].astype(BF16)
        wdb[...] = wdf[wslot].astype(BF16)

    def run_blocks(g, nblk):
        s = lax.rem(g, EXPERT_X_RING)
        sy = lax.rem(g, EXPERT_Y_RING)
        for j in range(nblk):
            ahead = g + j + EXPERT_X_RING - 2

            @pl.when(ahead < nb)
            def _():
                x_copy(ahead, lax.rem(ahead, EXPERT_X_RING)).start(priority=BLOCK_DMA_QUEUE)

        for j in range(nblk):
            x_copy(g + j, s + j).wait()
        half = xbuf.shape[2]
        packed = xbuf[pl.ds(s, nblk)].reshape(nblk * MOE_BLOCK, half)
        x_lo = pltpu.bitcast(packed << 16, F32).astype(BF16)
        x_hi = pltpu.bitcast(packed & jnp.uint32(0xFFFF0000), F32).astype(BF16)
        gte = (jnp.dot(x_lo, wgb[:half, :], preferred_element_type=F32)
               + jnp.dot(x_hi, wgb[half:, :], preferred_element_type=F32))
        up = (jnp.dot(x_lo, wub[:half, :], preferred_element_type=F32)
              + jnp.dot(x_hi, wub[half:, :], preferred_element_type=F32))
        act = (gte * _sigmoid(gte) * up).astype(BF16)
        y = jnp.dot(act, wdb[...], preferred_element_type=F32)

        for j in range(nblk):
            @pl.when(g + j >= EXPERT_Y_RING)
            def _():
                y_copy(g + j - EXPERT_Y_RING, sy + j).wait()

        ybuf[pl.ds(sy, nblk)] = y.reshape(nblk, MOE_BLOCK, y.shape[1])
        for j in range(nblk):
            y_copy(g + j, sy + j).start()

    lead = jnp.where(cnt > 0, first & 1, 0)
    rest = cnt - lead

    @pl.when(lead == 1)
    def _():
        run_blocks(first, 1)

    lax.fori_loop(0, rest // 2, lambda b, c: (run_blocks(first + lead + 2 * b, 2), c)[1], 0)

    @pl.when((rest & 1) == 1)
    def _():
        run_blocks(first + cnt - 1, 1)

    @pl.when(e == pl.num_programs(0) - 1)
    def _():
        for back in range(1, EXPERT_Y_RING + 1):
            @pl.when(nb >= back)
            def _():
                y_copy(nb - back, lax.rem(nb - back, EXPERT_Y_RING)).wait()

        ybuf[0] = jnp.zeros(ybuf.shape[1:], F32)
        lax.fori_loop(nb, n_blocks, lambda g, c: (y_copy(g, 0).start(), c)[1], 0)
        lax.fori_loop(nb, n_blocks, lambda g, c: (y_copy(g, 0).wait(), c)[1], 0)


def experts(xs, w_gate, w_up, w_down, first_blk, n_blk, n_used):
    n_rows, d_packed = xs.shape
    n_experts, d, de = w_gate.shape
    grid_spec = pltpu.PrefetchScalarGridSpec(
        num_scalar_prefetch=3,
        grid=(n_experts,),
        in_specs=[pl.BlockSpec(memory_space=pl.ANY)] * 4,
        out_specs=pl.BlockSpec(memory_space=pl.ANY),
        scratch_shapes=[pltpu.VMEM((EXPERT_X_RING, MOE_BLOCK, d_packed), xs.dtype),
                        pltpu.VMEM((EXPERT_Y_RING, MOE_BLOCK, d), F32),
                        pltpu.VMEM((2, d, de), F32), pltpu.VMEM((2, d, de), F32), pltpu.VMEM((2, de, d), F32),
                        pltpu.VMEM((d, de), BF16), pltpu.VMEM((d, de), BF16), pltpu.VMEM((de, d), BF16),
                        pltpu.SemaphoreType.DMA((EXPERT_X_RING,)), pltpu.SemaphoreType.DMA((EXPERT_Y_RING,)),
                        pltpu.SemaphoreType.DMA((2,))],
    )
    return pl.pallas_call(
        _experts_kernel,
        grid_spec=grid_spec,
        out_shape=jax.ShapeDtypeStruct((n_rows, d), F32),
        compiler_params=_cparams(("arbitrary",)),
        name="moe_experts",
    )(first_blk, n_blk, n_used, xs, w_gate, w_up, w_down)


def _moe_combine_kernel(dest_ref, destn_ref, x_ref, info_ref, g_ref, ys_hbm, o_ref, ybuf, sem, *, norm):
    i = pl.program_id(0)
    n = pl.num_programs(0)
    tm = x_ref.shape[0]
    slot = lax.rem(i, 2)

    def gather_start(dref, s):
        for k in range(2):
            for r in range(tm):
                pltpu.make_async_copy(ys_hbm.at[pl.ds(dref[k, r], 1)], ybuf.at[s, k, pl.ds(r, 1)],
                                      sem.at[s]).start(priority=r % 2)

    @pl.when(i == 0)
    def _():
        gather_start(dest_ref, 0)

    @pl.when(i + 1 < n)
    def _():
        gather_start(destn_ref, 1 - slot)

    for k in range(2):
        pltpu.make_async_copy(ys_hbm.at[pl.ds(0, tm)], ybuf.at[slot, k], sem.at[slot]).wait()
    w1 = info_ref[:, 4:5]
    w2 = info_ref[:, 5:6]
    s = x_ref[...] + (w1 * ybuf[slot, 0] + w2 * ybuf[slot, 1])
    o_ref[...] = _rms(s, g_ref[...]) if norm else s


def moe_combine(x, ys, dest, info, g, *, norm, tm):
    t, d = x.shape
    n = t // tm
    return pl.pallas_call(
        functools.partial(_moe_combine_kernel, norm=norm),
        grid=(n,),
        in_specs=[pl.BlockSpec((2, tm), lambda i: (0, i), memory_space=pltpu.SMEM),
                  pl.BlockSpec((2, tm), lambda i: (0, jnp.minimum(i + 1, n - 1)), memory_space=pltpu.SMEM),
                  pl.BlockSpec((tm, d), lambda i: (i, 0)),
                  pl.BlockSpec((tm, LANES), lambda i: (i, 0)),
                  pl.BlockSpec((1, d), lambda i: (0, 0)),
                  pl.BlockSpec(memory_space=pl.ANY)],
        out_specs=pl.BlockSpec((tm, d), lambda i: (i, 0)),
        out_shape=jax.ShapeDtypeStruct((t, d), F32),
        scratch_shapes=[pltpu.VMEM((2, 2, tm, d), F32), pltpu.SemaphoreType.DMA((2,))],
        compiler_params=_cparams(("arbitrary",)),
        name="moe_combine",
    )(dest, dest, x, info, g.reshape(1, d), ys)


def forward(x, mem, norm_mix_g, w_in, s5_lambda_re, s5_lambda_im, s5_log_dt, s5_b_re, s5_b_im, s5_c_re,
            s5_c_im, s5_d, s5_w_glu, conv_dw_w, conv_dw_b, conv_ln_g, conv_ln_b, conv_w_out, w_out,
            norm_xattn_g, norm_mem_g, xattn_wq, xattn_wk, xattn_wv, xattn_wo, norm_moe_g, router_w_group,
            router_b_group, router_w_expert, router_b_expert, exp_w_gate, exp_w_up, exp_w_down, norm_final_g,
            *, tiles):
    batch, seq, d = x.shape
    depth = w_in.shape[0]
    n_mem = mem.shape[1]
    s5_width = s5_d.shape[1]
    conv_width = conv_dw_b.shape[1]
    n_groups, epg = router_w_expert.shape[1], router_w_expert.shape[3]
    t = batch * seq
    xf = x.reshape(t, d)
    memf = mem.reshape(batch * n_mem, d)
    gate_col0 = s5_width + 2 * conv_width
    for l in range(depth):
        w_in_b = w_in[l].astype(BF16)
        proj = norm_mm_rows(xf, norm_mix_g[l], w_in_b, tm=tiles["proj_tm"], tn=tiles["proj_tn"],
                            out_dtype=F32, n_cols=gate_col0)
        lbr, lbi, bbr, bbi = s5_discretise(s5_lambda_re[l], s5_lambda_im[l], s5_log_dt[l],
                                           s5_b_re[l], s5_b_im[l])
        bre, bim, cre, cim = s5_block_diag(bbr, bbi, s5_c_re[l], s5_c_im[l])
        y_a = s5_branch(proj, bre, bim, cre, cim, lbr.reshape(1, -1), lbi.reshape(1, -1), s5_d[l],
                        s5_w_glu[l].astype(BF16), batch=batch, seq=seq, width=s5_width, m=tiles["s5_m"],
                        cw=tiles["s5_cw"], tn=tiles["glu_tn"])
        y_b = conv_branch(proj, conv_dw_w[l], conv_dw_b[l], conv_ln_g[l], conv_ln_b[l],
                          conv_w_out[l].astype(BF16), batch=batch, seq=seq, width=conv_width,
                          col0=s5_width, tm=tiles["conv_tm"])
        xf = combine(xf, norm_mix_g[l], w_in_b, y_a, y_b, w_out[l].astype(BF16), gate_col0=gate_col0,
                     tm=tiles["comb_tm"])

        q = norm_mm_rows(xf, norm_xattn_g[l], xattn_wq[l].astype(BF16), tm=tiles["q_tm"], tn=tiles["q_tn"],
                         out_dtype=BF16)
        wkv = jnp.concatenate([xattn_wk[l], xattn_wv[l]], axis=1).astype(BF16)
        kv = norm_mm(memf, norm_mem_g[l], wkv, tm=batch * n_mem, tn=tiles["kv_tn"], out_dtype=BF16)
        w_r = jnp.concatenate([router_w_group[l],
                               jnp.transpose(router_w_expert[l], (1, 0, 2)).reshape(d, n_groups * epg)], axis=1)
        b_r = jnp.concatenate([router_b_group[l], router_b_expert[l].reshape(-1)])
        n_r = w_r.shape[1]
        w_pad = jnp.pad(w_r, ((0, 0), (0, LANES - n_r)))
        w_hi = w_pad.astype(BF16)
        w_pad = jnp.concatenate([w_hi, (w_pad - w_hi.astype(F32)).astype(BF16)], axis=1)
        b_pad = jnp.pad(b_r, (0, LANES - n_r)).reshape(1, LANES)
        xf, h, info, infot, counts = xattn_router(
            xf, q, kv, xattn_wo[l].astype(BF16), norm_moe_g[l], w_pad, b_pad, batch=batch, seq=seq,
            n_mem=n_mem, heads=XATTN_HEADS, n_groups=n_groups, epg=epg, tm=tiles["xattn_tm"])
        dest, n_rows, n_used, pcounts, pends = moe_dispatch_plan(infot, counts, t, n_groups * epg)
        xs = moe_dispatch(h, dest, pcounts, pends, n_rows=n_rows, tm=tiles["dispatch_tm"])
        ys = experts(xs, exp_w_gate[l], exp_w_up[l], exp_w_down[l], (pends - pcounts) // MOE_BLOCK,
                     pcounts // MOE_BLOCK, n_used)
        last = l + 1 == depth
        xf = moe_combine(xf, ys, dest, info, norm_final_g if last else jnp.ones((d,), F32), norm=last,
                         tm=tiles["moe_tm"])
    return xf.reshape(batch, seq, d)


def kernel(x, mem, norm_mix_g, w_in, s5_lambda_re, s5_lambda_im, s5_log_dt, s5_b_re, s5_b_im, s5_c_re, s5_c_im, s5_d, s5_w_glu, conv_dw_w, conv_dw_b, conv_ln_g, conv_ln_b, conv_w_out, w_out, norm_xattn_g, norm_mem_g, xattn_wq, xattn_wk, xattn_wv, xattn_wo, norm_moe_g, router_w_group, router_b_group, router_w_expert, router_b_expert, exp_w_gate, exp_w_up, exp_w_down, norm_final_g):
    return forward(x, mem, norm_mix_g, w_in, s5_lambda_re, s5_lambda_im, s5_log_dt, s5_b_re, s5_b_im,
                   s5_c_re, s5_c_im, s5_d, s5_w_glu, conv_dw_w, conv_dw_b, conv_ln_g, conv_ln_b, conv_w_out,
                   w_out, norm_xattn_g, norm_mem_g, xattn_wq, xattn_wk, xattn_wv, xattn_wo, norm_moe_g,
                   router_w_group, router_b_group, router_w_expert, router_b_expert, exp_w_gate, exp_w_up,
                   exp_w_down, norm_final_g, tiles=TILES)
```

```python
import functools
import math

import jax
import jax.numpy as jnp
from jax import lax
from jax.experimental import pallas as pl
from jax.experimental.pallas import tpu as pltpu

F32 = jnp.float32
BF16 = jnp.bfloat16

RMS_EPS = 1e-6
LN_EPS = 1e-5

S5_SLAB_CH = 256
CONV_HALO = 32
CONV_ROW_CHUNK = 64
XATTN_HEADS = 4
MOE_BLOCK = 128
BLOCK_DMA_QUEUE = 1
EXPERT_X_RING = 8
EXPERT_Y_RING = 4
SUBLANES = 8
LANES = 128
VMEM_LIMIT = 56 * 1024 * 1024

TILES = dict(proj_tm=512, proj_tn=512, s5_m=64, s5_cw=512, glu_tn=512, conv_tm=512,
             comb_tm=256, kv_tn=512, xattn_tm=512, dispatch_tm=1024, moe_tm=256)


def _cparams(sem):
    return pltpu.CompilerParams(dimension_semantics=sem, vmem_limit_bytes=VMEM_LIMIT)


def _rms(x, g):
    return x * lax.rsqrt(jnp.mean(x * x, axis=-1, keepdims=True) + RMS_EPS) * g


def _sigmoid(x):
    return 0.5 * jnp.tanh(0.5 * x) + 0.5


def _gelu_tanh(x):
    c = math.sqrt(2.0 / math.pi)
    return 0.5 * x * (1.0 + jnp.tanh(c * (x + 0.044715 * (x * x * x))))


def _norm_mm_kernel(x_ref, g_ref, w_ref, o_ref, h_ref, *, act_from):
    j = pl.program_id(1)

    @pl.when(j == 0)
    def _():
        h_ref[...] = _rms(x_ref[...], g_ref[...]).astype(BF16)

    y = jnp.dot(h_ref[...], w_ref[...], preferred_element_type=F32)
    if act_from is None:
        o_ref[...] = y.astype(o_ref.dtype)
    else:
        @pl.when(j < act_from)
        def _():
            o_ref[...] = y.astype(o_ref.dtype)

        @pl.when(j >= act_from)
        def _():
            o_ref[...] = _sigmoid(y).astype(o_ref.dtype)


def norm_mm(x, g, w, *, tm, tn, out_dtype=F32, act_from=None):
    m, k = x.shape
    n = w.shape[1]
    return pl.pallas_call(
        functools.partial(_norm_mm_kernel, act_from=act_from),
        grid=(m // tm, n // tn),
        in_specs=[pl.BlockSpec((tm, k), lambda i, j: (i, 0)),
                  pl.BlockSpec((1, k), lambda i, j: (0, 0)),
                  pl.BlockSpec((k, tn), lambda i, j: (0, j))],
        out_specs=pl.BlockSpec((tm, tn), lambda i, j: (i, j)),
        out_shape=jax.ShapeDtypeStruct((m, n), out_dtype),
        scratch_shapes=[pltpu.VMEM((tm, k), BF16)],
        compiler_params=_cparams(("parallel", "arbitrary")),
        name="norm_mm",
    )(x, g.reshape(1, k), w)


def _norm_mm_rows_kernel(x_ref, g_ref, w_ref, o_ref, *, tn):
    h = _rms(x_ref[...], g_ref[...]).astype(BF16)
    for j in range(o_ref.shape[1] // tn):
        cols = slice(j * tn, (j + 1) * tn)
        o_ref[:, cols] = jnp.dot(h, w_ref[:, cols], preferred_element_type=F32).astype(o_ref.dtype)


def norm_mm_rows(x, g, w, *, tm, tn, out_dtype, n_cols=None):
    m, k = x.shape
    n = w.shape[1] if n_cols is None else n_cols
    return pl.pallas_call(
        functools.partial(_norm_mm_rows_kernel, tn=tn),
        grid=(m // tm,),
        in_specs=[pl.BlockSpec((tm, k), lambda i: (i, 0)),
                  pl.BlockSpec((1, k), lambda i: (0, 0)),
                  pl.BlockSpec((k, n), lambda i: (0, 0), pipeline_mode=pl.Buffered(1))],
        out_specs=pl.BlockSpec((tm, n), lambda i: (i, 0)),
        out_shape=jax.ShapeDtypeStruct((m, n), out_dtype),
        compiler_params=_cparams(("parallel",)),
        name="norm_mm_rows",
    )(x, g.reshape(1, k), w)


def _s5_disc_kernel(lr_ref, li_ref, ldt_ref, br_ref, bi_ref, lbr_ref, lbi_ref, bbr_ref, bbi_ref):
    lr, li = lr_ref[...], li_ref[...]
    dt = jnp.exp(ldt_ref[...])
    mag = jnp.exp(lr * dt)
    ang = li * dt
    lb_re, lb_im = mag * jnp.cos(ang), mag * jnp.sin(ang)
    den = lr * lr + li * li
    nr, ni = lb_re - 1.0, lb_im
    coef_re = (nr * lr + ni * li) / den
    coef_im = (ni * lr - nr * li) / den
    lbr_ref[...] = lb_re
    lbi_ref[...] = lb_im
    br, bi = br_ref[...], bi_ref[...]
    bbr_ref[...] = coef_re[None] * br - coef_im[None] * bi
    bbi_ref[...] = coef_re[None] * bi + coef_im[None] * br


def s5_discretise(lam_re, lam_im, log_dt, b_re, b_im):
    g, n = lam_re.shape
    c = b_re.shape[-1]
    b_re_t = jnp.transpose(b_re, (2, 0, 1))
    b_im_t = jnp.transpose(b_im, (2, 0, 1))
    gn = jax.ShapeDtypeStruct((g, n), F32)
    cgn = jax.ShapeDtypeStruct((c, g, n), F32)
    return pl.pallas_call(
        _s5_disc_kernel,
        out_shape=(gn, gn, cgn, cgn),
        name="s5_discretise",
    )(lam_re, lam_im, log_dt.reshape(g, 1), b_re_t, b_im_t)


def s5_block_diag(bb_re, bb_im, c_re, c_im):
    c, g, n = bb_re.shape
    gs = S5_SLAB_CH // c
    n_slab = g // gs
    row_grp_b = jnp.arange(gs * c) // c
    col_grp_b = jnp.arange(gs * n) // n
    tile_n = (jnp.arange(n)[:, None] == jnp.arange(gs * n)[None, :] % n).astype(F32)
    tile_c = (jnp.arange(c)[:, None] == jnp.arange(gs * c)[None, :] % c).astype(F32)
    exact = lax.Precision.HIGHEST

    def bmat(bb):
        b = jnp.transpose(bb, (1, 0, 2)).reshape(n_slab, gs * c, n)
        full = jnp.einsum('srn,nq->srq', b, tile_n, precision=exact)
        return jnp.where(row_grp_b[:, None] == col_grp_b[None, :], full, 0.0).astype(BF16)

    def cmat(cc):
        cm = jnp.transpose(cc.reshape(n_slab, gs, c, n), (0, 1, 3, 2)).reshape(n_slab, gs * n, c)
        full = jnp.einsum('spc,cr->spr', cm, tile_c, precision=exact)
        return jnp.where(col_grp_b[:, None] == row_grp_b[None, :], full, 0.0).astype(BF16)

    return bmat(bb_re), bmat(bb_im), cmat(c_re), cmat(c_im)


def _s5_kernel(proj_hbm, bre_ref, bim_ref, cre_ref, cim_ref, lre_ref, lim_ref, d_ref, wglu_ref, ya_hbm,
               ubuf, yabuf, zbuf, xre_ref, xim_ref, pre_ref, pim_ref, car_re_ref, car_im_ref, isem, osem,
               *, m, cw, tn):
    t = pl.program_id(1)
    n_t = pl.num_programs(1)
    step_id = pl.program_id(0) * n_t + t
    n_steps = pl.num_programs(0) * n_t
    slot = lax.rem(step_id, 2)
    rows = SUBLANES * m
    width = d_ref.shape[1]
    n_state = lre_ref.shape[1]
    n_slab = bre_ref.shape[0]
    slab_states = n_state // n_slab

    def in_copy(step, s, k):
        r0 = pl.multiple_of(step * rows + k * m, SUBLANES)
        return pltpu.make_async_copy(proj_hbm.at[pl.ds(r0, m), pl.ds(0, width)], ubuf.at[s, :, k, :], isem.at[s])

    def out_copy(step, s, k):
        r0 = pl.multiple_of(step * rows + k * m, SUBLANES)
        return pltpu.make_async_copy(yabuf.at[s, :, k, :], ya_hbm.at[pl.ds(r0, m)], osem.at[s])

    @pl.when(step_id == 0)
    def _():
        for k in range(SUBLANES):
            in_copy(0, 0, k).start()

    @pl.when(step_id + 1 < n_steps)
    def _():
        for k in range(SUBLANES):
            in_copy(step_id + 1, 1 - slot, k).start()

    @pl.when(t == 0)
    def _():
        car_re_ref[...] = jnp.zeros_like(car_re_ref)
        car_im_ref[...] = jnp.zeros_like(car_im_ref)
        lr, li = lre_ref[...], lim_ref[...]

        def pw(_, carry):
            pr, pi = carry
            return pr * lr - pi * li, pr * li + pi * lr

        pr, pi = lax.fori_loop(1, m, pw, (lr, li))
        pre_ref[...] = pr
        pim_ref[...] = pi

    @pl.when(step_id == 0)
    def _():
        zbuf[...] = jnp.zeros_like(zbuf)

    @pl.when(step_id >= 3)
    def _():
        for k in range(SUBLANES):
            out_copy(step_id - 3, 1 - slot, k).wait()

    nh = wglu_ref.shape[1] // 2

    def glu_pieces(z, s):
        def piece(j):
            val = jnp.dot(z, wglu_ref[:, j * tn:(j + 1) * tn], preferred_element_type=F32)
            gt = jnp.dot(z, wglu_ref[:, nh + j * tn:nh + (j + 1) * tn], preferred_element_type=F32)
            yabuf[s, :, :, j * tn:(j + 1) * tn] = (val * _sigmoid(gt)).reshape(m, SUBLANES, tn)
        return [functools.partial(piece, j) for j in range(nh // tn)]

    for k in range(SUBLANES):
        in_copy(step_id, slot, k).wait()
    u = ubuf[slot].reshape(rows, width)
    up = u.astype(BF16)

    def b_project(s):
        us = up[:, s * S5_SLAB_CH:(s + 1) * S5_SLAB_CH]
        cols = slice(s * slab_states, (s + 1) * slab_states)
        xre_ref[:, cols] = jnp.dot(us, bre_ref[s], preferred_element_type=F32)
        xim_ref[:, cols] = jnp.dot(us, bim_ref[s], preferred_element_type=F32)

    def c_project(s):
        cols = slice(s * slab_states, (s + 1) * slab_states)
        return (jnp.dot(xre_ref[:, cols].astype(BF16), cre_ref[s], preferred_element_type=F32)
                - jnp.dot(xim_ref[:, cols].astype(BF16), cim_ref[s], preferred_element_type=F32))

    n_cb = n_state // cw
    cb_per_slab = slab_states // cw
    glu_prev = glu_pieces(zbuf[...], 1 - slot)
    glu_at = {(i * n_cb) // len(glu_prev): [] for i in range(len(glu_prev))}
    for i, piece in enumerate(glu_prev):
        glu_at[(i * n_cb) // len(glu_prev)].append(piece)
    ys = []
    b_project(0)
    row_id = lax.broadcasted_iota(jnp.int32, (SUBLANES, cw), 0)
    for cb in range(n_cb):
        cols = slice(cb * cw, (cb + 1) * cw)
        slab = cb // cb_per_slab
        if cb % cb_per_slab == 0:
            if slab + 1 < n_slab:
                b_project(slab + 1)
            if slab >= 1:
                ys.append(c_project(slab - 1))
        for piece in glu_at.get(cb, []):
            piece()
        lr = jnp.broadcast_to(lre_ref[:, cols], (SUBLANES, cw))
        li = jnp.broadcast_to(lim_ref[:, cols], (SUBLANES, cw))

        def step(tau, carry, store, cols=cols, lr=lr, li=li):
            sr, si = carry
            r0 = pl.multiple_of(tau * SUBLANES, SUBLANES)
            nr = sr * lr - si * li + xre_ref[pl.ds(r0, SUBLANES), cols]
            ni = sr * li + si * lr + xim_ref[pl.ds(r0, SUBLANES), cols]
            if store:
                xre_ref[pl.ds(r0, SUBLANES), cols] = nr
                xim_ref[pl.ds(r0, SUBLANES), cols] = ni
            return nr, ni

        zero = jnp.zeros((SUBLANES, cw), F32)
        er, ei = lax.fori_loop(0, m, functools.partial(step, store=False), (zero, zero), unroll=True)

        lmr = jnp.broadcast_to(pre_ref[:, cols], (SUBLANES, cw))
        lmi = jnp.broadcast_to(pim_ref[:, cols], (SUBLANES, cw))
        ir = jnp.where(row_id == 0, jnp.broadcast_to(car_re_ref[:, cols], (SUBLANES, cw)), 0.0)
        ii = jnp.where(row_id == 0, jnp.broadcast_to(car_im_ref[:, cols], (SUBLANES, cw)), 0.0)
        for k in range(1, SUBLANES):
            nr = ir * lmr - ii * lmi + er
            ni = ir * lmi + ii * lmr + ei
            ir = jnp.where(row_id == k, pltpu.roll(nr, 1, axis=0), ir)
            ii = jnp.where(row_id == k, pltpu.roll(ni, 1, axis=0), ii)

        fr, fi = lax.fori_loop(0, m, functools.partial(step, store=True), (ir, ii), unroll=True)
        car_re_ref[:, cols] = fr[SUBLANES - 1:SUBLANES, :]
        car_im_ref[:, cols] = fi[SUBLANES - 1:SUBLANES, :]

    ys.append(c_project(n_slab - 1))
    y = jnp.concatenate(ys, axis=1) + d_ref[...] * u
    zbuf[...] = _gelu_tanh(y).astype(BF16)

    @pl.when(step_id >= 1)
    def _():
        for k in range(SUBLANES):
            out_copy(step_id - 1, 1 - slot, k).start()

    @pl.when(step_id == n_steps - 1)
    def _():
        @pl.when(step_id >= 2)
        def _():
            for k in range(SUBLANES):
                out_copy(step_id - 2, slot, k).wait()
        for piece in glu_pieces(zbuf[...], slot):
            piece()
        for k in range(SUBLANES):
            out_copy(step_id, slot, k).start()
        for k in range(SUBLANES):
            out_copy(step_id, slot, k).wait()

        @pl.when(step_id >= 1)
        def _():
            for k in range(SUBLANES):
                out_copy(step_id - 1, 1 - slot, k).wait()


def s5_branch(proj, bre, bim, cre, cim, lre, lim, d_skip, w_glu, *, batch, seq, width, m, cw, tn):
    rows = SUBLANES * m
    n_t = seq // rows
    n_state = lre.shape[1]
    nh = w_glu.shape[1] // 2
    once = pl.Buffered(1)
    const3 = lambda b, t: (0, 0, 0)
    const2 = lambda b, t: (0, 0)
    return pl.pallas_call(
        functools.partial(_s5_kernel, m=m, cw=cw, tn=tn),
        grid=(batch, n_t),
        in_specs=[pl.BlockSpec(memory_space=pl.ANY),
                  pl.BlockSpec(bre.shape, const3, pipeline_mode=once),
                  pl.BlockSpec(bim.shape, const3, pipeline_mode=once),
                  pl.BlockSpec(cre.shape, const3, pipeline_mode=once),
                  pl.BlockSpec(cim.shape, const3, pipeline_mode=once),
                  pl.BlockSpec((1, n_state), const2),
                  pl.BlockSpec((1, n_state), const2),
                  pl.BlockSpec((1, width), const2),
                  pl.BlockSpec(w_glu.shape, const2, pipeline_mode=once)],
        out_specs=pl.BlockSpec(memory_space=pl.ANY),
        out_shape=jax.ShapeDtypeStruct((batch * seq, nh), F32),
        scratch_shapes=[pltpu.VMEM((2, m, SUBLANES, width), F32), pltpu.VMEM((2, m, SUBLANES, nh), F32),
                        pltpu.VMEM((rows, width), BF16),
                        pltpu.VMEM((rows, n_state), F32), pltpu.VMEM((rows, n_state), F32),
                        pltpu.VMEM((1, n_state), F32), pltpu.VMEM((1, n_state), F32),
                        pltpu.VMEM((1, n_state), F32), pltpu.VMEM((1, n_state), F32),
                        pltpu.SemaphoreType.DMA((2,)), pltpu.SemaphoreType.DMA((2,))],
        compiler_params=_cparams(("arbitrary", "arbitrary")),
        name="s5_branch",
    )(proj, bre, bim, cre, cim, lre, lim, d_skip.reshape(1, width), w_glu)


def _conv_kernel(a_ref, b_ref, ah_ref, bh_ref, dww_ref, dwb_ref, lng_ref, lnb_ref, w_ref, o_ref,
                 zs_ref, wb_ref, yc_ref, *, tm, taps):
    i = pl.program_id(1)
    off0 = CONV_HALO - (taps - 1)
    width = a_ref.shape[1]
    n_slab = width // LANES
    zh = jnp.where(i == 0, 0.0, ah_ref[...] * _sigmoid(bh_ref[...]))
    z = a_ref[...] * _sigmoid(b_ref[...])
    for s in range(n_slab):
        zs_ref[s, 0:CONV_HALO, :] = zh[:, s * LANES:(s + 1) * LANES]
        zs_ref[s, CONV_HALO:, :] = z[:, s * LANES:(s + 1) * LANES]

    @pl.when(i == 0)
    def _():
        for j in range(taps):
            wb_ref[j] = jnp.broadcast_to(dww_ref[j:j + 1, :], (SUBLANES, width))
        wb_ref[taps] = jnp.broadcast_to(dwb_ref[...], (SUBLANES, width))

    rg = CONV_ROW_CHUNK // SUBLANES

    def chunk(c, carry):
        r0 = pl.multiple_of(c * CONV_ROW_CHUNK, CONV_ROW_CHUNK)
        for s in range(n_slab):
            lanes = slice(s * LANES, (s + 1) * LANES)
            bias = wb_ref[taps, :, lanes]
            acc = [bias] * rg
            for j in range(taps):
                w = wb_ref[j, :, lanes]
                for g in range(rg):
                    acc[g] = acc[g] + w * zs_ref[s, pl.ds(r0 + g * SUBLANES + off0 + j, SUBLANES), :]
            for g in range(rg):
                yc_ref[pl.ds(pl.multiple_of(r0 + g * SUBLANES, SUBLANES), SUBLANES), lanes] = acc[g]
        return carry

    lax.fori_loop(0, tm // CONV_ROW_CHUNK, chunk, 0)
    y = yc_ref[...]
    mu = jnp.mean(y, axis=-1, keepdims=True)
    var = jnp.mean(jnp.square(y - mu), axis=-1, keepdims=True)
    y = (y - mu) * lax.rsqrt(var + LN_EPS) * lng_ref[...] + lnb_ref[...]
    y = y * _sigmoid(y)
    o_ref[...] = jnp.dot(y.astype(BF16), w_ref[...], preferred_element_type=F32)


def conv_branch(proj, dw_w, dw_b, ln_g, ln_b, w_out, *, batch, seq, width, col0, tm):
    n_t = seq // tm
    ca, cb = col0 // width, col0 // width + 1
    hb = tm // CONV_HALO
    taps = dw_w.shape[0]
    d_out = w_out.shape[1]
    const2 = lambda b, i: (0, 0)
    halo = lambda b, i: jnp.maximum((b * n_t + i) * hb - 1, 0)
    return pl.pallas_call(
        functools.partial(_conv_kernel, tm=tm, taps=taps),
        grid=(batch, n_t),
        in_specs=[pl.BlockSpec((tm, width), lambda b, i: (b * n_t + i, ca)),
                  pl.BlockSpec((tm, width), lambda b, i: (b * n_t + i, cb)),
                  pl.BlockSpec((CONV_HALO, width), lambda b, i: (halo(b, i), ca)),
                  pl.BlockSpec((CONV_HALO, width), lambda b, i: (halo(b, i), cb)),
                  pl.BlockSpec((taps, width), const2),
                  pl.BlockSpec((1, width), const2),
                  pl.BlockSpec((1, width), const2),
                  pl.BlockSpec((1, width), const2),
                  pl.BlockSpec((width, d_out), const2)],
        out_specs=pl.BlockSpec((tm, d_out), lambda b, i: (b * n_t + i, 0)),
        out_shape=jax.ShapeDtypeStruct((batch * seq, d_out), F32),
        scratch_shapes=[pltpu.VMEM((width // LANES, tm + CONV_HALO, LANES), F32),
                        pltpu.VMEM((taps + 1, SUBLANES, width), F32),
                        pltpu.VMEM((tm, width), F32)],
        compiler_params=_cparams(("parallel", "arbitrary")),
        name="conv_branch",
    )(proj, proj, proj, proj, dw_w, dw_b.reshape(1, width), ln_g.reshape(1, width),
      ln_b.reshape(1, width), w_out)


def _combine_kernel(*refs, parts):
    x_ref, g_ref = refs[0], refs[1]
    ya_ref, yb_ref, wo_ref, o_ref = refs[2 + 2 * parts], refs[3 + 2 * parts], refs[-2], refs[-1]
    gw = ya_ref.shape[1] // parts
    x = x_ref[...]
    h = _rms(x, g_ref[...]).astype(BF16)
    o_ref[...] = x
    for p in range(parts):
        cols = slice(p * gw, (p + 1) * gw)
        ga = _sigmoid(jnp.dot(h, refs[2 + p][...], preferred_element_type=F32))
        gb = _sigmoid(jnp.dot(h, refs[2 + parts + p][...], preferred_element_type=F32))
        mix = ga * ya_ref[:, cols] + gb * yb_ref[:, cols]
        o_ref[...] += jnp.dot(mix.astype(BF16), wo_ref[cols, :], preferred_element_type=F32)


def combine(x, g, w_in, y_a, y_b, w_out, *, gate_col0, tm):
    m, d = x.shape
    gw = math.gcd(gate_col0, d)
    parts = d // gw
    row = lambda i: (i, 0)
    once = pl.Buffered(1)
    gate_specs = [pl.BlockSpec((d, gw), functools.partial(lambda i, c: (0, c), c=(gate_col0 + s * d) // gw + p),
                               pipeline_mode=once) for s in range(2) for p in range(parts)]
    return pl.pallas_call(
        functools.partial(_combine_kernel, parts=parts),
        grid=(m // tm,),
        in_specs=[pl.BlockSpec((tm, d), row), pl.BlockSpec((1, d), lambda i: (0, 0))] + gate_specs +
                 [pl.BlockSpec((tm, d), row),
                  pl.BlockSpec((tm, d), row),
                  pl.BlockSpec((d, d), lambda i: (0, 0), pipeline_mode=once)],
        out_specs=pl.BlockSpec((tm, d), row),
        out_shape=jax.ShapeDtypeStruct((m, d), F32),
        compiler_params=_cparams(("parallel",)),
        name="gated_combine",
    )(x, g.reshape(1, d), *([w_in] * (2 * parts)), y_a, y_b, w_out)


def _route_tile(x, g_ref, w_ref, b_ref, tri_ref, h_ref, info_ref, infot_ref, cnt_ref, run_ref, *,
                n_groups, epg):
    tm = x.shape[0]
    h = _rms(x, g_ref[...])
    half = h.shape[1] // 2
    bits = pltpu.bitcast(h.astype(BF16).astype(F32), jnp.uint32)
    h_ref[...] = (bits[:, :half] >> 16) | (bits[:, half:] & jnp.uint32(0xFFFF0000))
    h_hi = h.astype(BF16)
    h_lo = (h - h_hi.astype(F32)).astype(BF16)
    p = jnp.dot(h_hi, w_ref[...], preferred_element_type=F32)
    logits = (p[:, :LANES] + p[:, LANES:] + jnp.dot(h_lo, w_ref[:, :LANES], preferred_element_type=F32)
              + b_ref[...])
    lane = lax.broadcasted_iota(jnp.int32, (tm, LANES), 1).astype(F32)
    neg = jnp.float32(-jnp.inf)
    big = jnp.float32(LANES)
    is_grp = lane < n_groups
    cl = jnp.where(is_grp, logits, neg)
    cmax = jnp.max(cl, axis=-1, keepdims=True)
    g_idx = jnp.min(jnp.where(cl == cmax, lane, big), axis=-1, keepdims=True)
    p_sel = 1.0 / jnp.sum(jnp.where(is_grp, jnp.exp(cl - cmax), 0.0), axis=-1, keepdims=True)
    lo = n_groups + g_idx * epg
    in_grp = (lane >= lo) & (lane < lo + epg)
    fl = jnp.where(in_grp, logits, neg)
    v1 = jnp.max(fl, axis=-1, keepdims=True)
    i1 = jnp.min(jnp.where(fl == v1, lane, big), axis=-1, keepdims=True)
    fl2 = jnp.where(lane == i1, neg, fl)
    v2 = jnp.max(fl2, axis=-1, keepdims=True)
    i2 = jnp.min(jnp.where(fl2 == v2, lane, big), axis=-1, keepdims=True)
    e2x = jnp.exp(v2 - v1)
    w1 = p_sel / (1.0 + e2x)
    w2 = p_sel * e2x / (1.0 + e2x)
    e1 = i1 - n_groups
    e2 = i2 - n_groups
    oh1 = lane == e1
    oh2 = lane == e2
    both = jnp.where(oh1 | oh2, 1.0, 0.0)
    before = jnp.dot(tri_ref[...], both.astype(BF16), preferred_element_type=F32) + run_ref[...]
    r1 = jnp.sum(jnp.where(oh1, before, 0.0), axis=-1, keepdims=True)
    r2 = jnp.sum(jnp.where(oh2, before, 0.0), axis=-1, keepdims=True)
    run_ref[...] += jnp.sum(both, axis=0, keepdims=True)
    cnt_ref[...] = run_ref[...]
    info = jnp.where(lane == 0, e1, 0.0)
    info = jnp.where(lane == 1, e2, info)
    info = jnp.where(lane == 2, r1, info)
    info = jnp.where(lane == 3, r2, info)
    info = jnp.where(lane == 4, w1, info)
    info = jnp.where(lane == 5, w2, info)
    info_ref[...] = info
    infot_ref[...] = jnp.transpose(info)[0:SUBLANES, :]


def _xattn_router_kernel(x_ref, gx_ref, wq_ref, kv_ref, wo_ref, g_ref, w_ref, b_ref, tri_ref,
                         o_ref, h_ref, info_ref, infot_ref, cnt_ref, run_ref, *, heads, scale, n_groups, epg):
    @pl.when((pl.program_id(0) == 0) & (pl.program_id(1) == 0))
    def _():
        run_ref[...] = jnp.zeros_like(run_ref)

    d = x_ref.shape[1]
    hd = d // heads
    acc = x_ref[...]
    hq = _rms(acc, gx_ref[...]).astype(BF16)

    def q_head(h):
        return jnp.dot(hq, wq_ref[:, h * hd:(h + 1) * hd], preferred_element_type=F32).astype(BF16)

    q_next = q_head(0)
    for h in range(heads):
        q = q_next
        k = kv_ref[:, h * hd:(h + 1) * hd]
        v = kv_ref[:, d + h * hd:d + (h + 1) * hd]
        s = lax.dot_general(q, k, (((1,), (1,)), ((), ())), preferred_element_type=F32) * scale
        if h + 1 < heads:
            q_next = q_head(h + 1)
        s = s - jnp.max(s, axis=-1, keepdims=True)
        p = jnp.exp(s)
        p = p / jnp.sum(p, axis=-1, keepdims=True)
        o = jnp.dot(p.astype(BF16), v, preferred_element_type=F32)
        acc = acc + jnp.dot(o.astype(BF16), wo_ref[h * hd:(h + 1) * hd, :], preferred_element_type=F32)
    o_ref[...] = acc
    _route_tile(acc, g_ref, w_ref, b_ref, tri_ref, h_ref, info_ref, infot_ref, cnt_ref, run_ref,
                n_groups=n_groups, epg=epg)


def xattn_router(x, gx, wq, kv, wo, g, w_pad, b_pad, *, batch, seq, n_mem, heads, n_groups, epg, tm):
    d = x.shape[1]
    m = batch * seq
    n_t = seq // tm
    scale = 1.0 / math.sqrt(d // heads)
    tri = (jnp.arange(tm)[:, None] > jnp.arange(tm)[None, :]).astype(BF16)
    row = lambda b, i: (b * n_t + i, 0)
    const = lambda b, i: (0, 0)
    once = pl.Buffered(1)
    return pl.pallas_call(
        functools.partial(_xattn_router_kernel, heads=heads, scale=scale, n_groups=n_groups, epg=epg),
        grid=(batch, n_t),
        in_specs=[pl.BlockSpec((tm, d), row),
                  pl.BlockSpec((1, d), const),
                  pl.BlockSpec((d, d), const, pipeline_mode=once),
                  pl.BlockSpec((n_mem, 2 * d), lambda b, i: (b, 0)),
                  pl.BlockSpec((d, d), const, pipeline_mode=once),
                  pl.BlockSpec((1, d), const),
                  pl.BlockSpec((d, 2 * LANES), const, pipeline_mode=once),
                  pl.BlockSpec((1, LANES), const),
                  pl.BlockSpec((tm, tm), const, pipeline_mode=once)],
        out_specs=[pl.BlockSpec((tm, d), row),
                   pl.BlockSpec((tm, d // 2), row),
                   pl.BlockSpec((tm, LANES), row),
                   pl.BlockSpec((SUBLANES, tm), lambda b, i: (0, b * n_t + i)),
                   pl.BlockSpec((1, LANES), const)],
        out_shape=(jax.ShapeDtypeStruct((m, d), F32),
                   jax.ShapeDtypeStruct((m, d // 2), jnp.uint32),
                   jax.ShapeDtypeStruct((m, LANES), F32),
                   jax.ShapeDtypeStruct((SUBLANES, m), F32),
                   jax.ShapeDtypeStruct((1, LANES), F32)),
        scratch_shapes=[pltpu.VMEM((1, LANES), F32)],
        compiler_params=_cparams(("arbitrary", "arbitrary")),
        name="xattn_router",
    )(x, gx.reshape(1, d), wq, kv, wo, g.reshape(1, d), w_pad, b_pad, tri)


def moe_dispatch_plan(infot, counts, n_tokens, n_experts):
    e = infot[0:2].astype(jnp.int32)
    rank = infot[2:4].astype(jnp.int32)
    counts = counts[0, :n_experts].astype(jnp.int32)
    pcounts = ((counts + MOE_BLOCK - 1) // MOE_BLOCK) * MOE_BLOCK
    pends = jnp.cumsum(pcounts)
    pstarts = pends - pcounts
    ids = jnp.arange(n_experts, dtype=jnp.int32)
    dest = rank + jnp.sum(jnp.where(e[:, :, None] == ids, pstarts, 0), axis=-1)
    n_rows = (-(-2 * n_tokens // MOE_BLOCK) + n_experts) * MOE_BLOCK
    n_used = (pends[-1] // MOE_BLOCK).reshape(1)
    return dest, n_rows, n_used, pcounts, pends


def _dispatch_kernel(pc_ref, pe_ref, dest_ref, h_ref, xs_hbm, zbuf, sem, tsem, *, n_experts):
    i = pl.program_id(0)
    tm = h_ref.shape[0]

    @pl.when(i == 0)
    def _():
        zbuf[...] = jnp.zeros_like(zbuf)
        for wait in (False, True):
            for e in range(n_experts):
                @pl.when(pc_ref[e] > 0)
                def _():
                    row0 = pl.multiple_of(pe_ref[e] - MOE_BLOCK, MOE_BLOCK)
                    fill = pltpu.make_async_copy(zbuf, xs_hbm.at[pl.ds(row0, MOE_BLOCK)], sem)
                    fill.wait() if wait else fill.start()

    n_blocks = xs_hbm.shape[0] // MOE_BLOCK
    n_used = pe_ref[n_experts - 1] // MOE_BLOCK

    def tail_fill(b, wait):
        row0 = pl.multiple_of(b * MOE_BLOCK, MOE_BLOCK)
        fill = pltpu.make_async_copy(zbuf, xs_hbm.at[pl.ds(row0, MOE_BLOCK)], tsem)
        fill.wait() if wait else fill.start()

    @pl.when(i == 0)
    def _():
        lax.fori_loop(n_used, n_blocks, lambda b, c: (tail_fill(b, False), c)[1], 0)

    for k in range(2):
        for r in range(tm):
            pltpu.make_async_copy(h_ref.at[pl.ds(r, 1)], xs_hbm.at[pl.ds(dest_ref[k, r], 1)],
                                  sem).start(priority=r % 2)
    for k in range(2):
        pltpu.make_async_copy(h_ref, xs_hbm.at[pl.ds(0, tm)], sem).wait()

    @pl.when(i == pl.num_programs(0) - 1)
    def _():
        lax.fori_loop(n_used, n_blocks, lambda b, c: (tail_fill(b, True), c)[1], 0)


def moe_dispatch(h, dest, pcounts, pends, *, n_rows, tm):
    t, d = h.shape
    grid_spec = pltpu.PrefetchScalarGridSpec(
        num_scalar_prefetch=2,
        grid=(t // tm,),
        in_specs=[pl.BlockSpec((2, tm), lambda i, pc, pe: (0, i), memory_space=pltpu.SMEM),
                  pl.BlockSpec((tm, d), lambda i, pc, pe: (i, 0))],
        out_specs=pl.BlockSpec(memory_space=pl.ANY),
        scratch_shapes=[pltpu.VMEM((MOE_BLOCK, d), h.dtype), pltpu.SemaphoreType.DMA(()),
                        pltpu.SemaphoreType.DMA(())],
    )
    return pl.pallas_call(
        functools.partial(_dispatch_kernel, n_experts=pcounts.shape[0]),
        grid_spec=grid_spec,
        out_shape=jax.ShapeDtypeStruct((n_rows, d), h.dtype),
        compiler_params=_cparams(("arbitrary",)),
        name="moe_dispatch",
    )(pcounts, pends, dest, h)


def _experts_kernel(first_ref, cnt_ref, nb_ref, xs_hbm, wg_hbm, wu_hbm, wd_hbm, ys_hbm,
                    xbuf, ybuf, wgf, wuf, wdf, wgb, wub, wdb, xsem, ysem, wsem):
    e = pl.program_id(0)
    n_e = pl.num_programs(0)
    nb = nb_ref[0]
    first, cnt = first_ref[e], cnt_ref[e]
    n_blocks = xs_hbm.shape[0] // MOE_BLOCK
    wslot = lax.rem(e, 2)

    def rows(g):
        return pl.ds(pl.multiple_of(g * MOE_BLOCK, MOE_BLOCK), MOE_BLOCK)

    def x_copy(g, s):
        return pltpu.make_async_copy(xs_hbm.at[rows(g)], xbuf.at[s], xsem.at[s])

    def y_copy(g, s):
        return pltpu.make_async_copy(ybuf.at[s], ys_hbm.at[rows(g)], ysem.at[s])

    def w_copies(ex, s):
        half = wd_hbm.shape[1] // 2
        lo, hi = pl.ds(0, half), pl.ds(half, half)
        return ((pltpu.make_async_copy(wg_hbm.at[ex], wgf.at[s], wsem.at[s]), 0),
                (pltpu.make_async_copy(wu_hbm.at[ex], wuf.at[s], wsem.at[s]), 1),
                (pltpu.make_async_copy(wd_hbm.at[ex, lo], wdf.at[s, lo], wsem.at[s]), 0),
                (pltpu.make_async_copy(wd_hbm.at[ex, hi], wdf.at[s, hi], wsem.at[s]), 1))

    @pl.when(e == 0)
    def _():
        for g0 in range(EXPERT_X_RING - 2):
            @pl.when(g0 < nb)
            def _():
                x_copy(g0, g0).start(priority=BLOCK_DMA_QUEUE)

    @pl.when((e == 0) & (cnt > 0))
    def _():
        for cp, queue in w_copies(0, 0):
            cp.start(priority=queue)

    e_next = jnp.minimum(e + 1, n_e - 1)

    @pl.when((e + 1 < n_e) & (cnt_ref[e_next] > 0))
    def _():
        for cp, queue in w_copies(e_next, 1 - wslot):
            cp.start(priority=queue)

    @pl.when(cnt > 0)
    def _():
        for cp, _ in w_copies(e, wslot):
            cp.wait()
        wgb[...] = wgf[wslot].astype(BF16)
        wub[...] = wuf[wslot].astype(BF16)
        wdb[...] = wdf[wslot].astype(BF16)

    def run_blocks(g, nblk):
        s = lax.rem(g, EXPERT_X_RING)
        sy = lax.rem(g, EXPERT_Y_RING)
        for j in range(nblk):
            ahead = g + j + EXPERT_X_RING - 2

            @pl.when(ahead < nb)
            def _():
                x_copy(ahead, lax.rem(ahead, EXPERT_X_RING)).start(priority=BLOCK_DMA_QUEUE)

        for j in range(nblk):
            x_copy(g + j, s + j).wait()
        half = xbuf.shape[2]
        packed = xbuf[pl.ds(s, nblk)].reshape(nblk * MOE_BLOCK, half)
        x_lo = pltpu.bitcast(packed << 16, F32).astype(BF16)
        x_hi = pltpu.bitcast(packed & jnp.uint32(0xFFFF0000), F32).astype(BF16)
        gte = (jnp.dot(x_lo, wgb[:half, :], preferred_element_type=F32)
               + jnp.dot(x_hi, wgb[half:, :], preferred_element_type=F32))
        up = (jnp.dot(x_lo, wub[:half, :], preferred_element_type=F32)
              + jnp.dot(x_hi, wub[half:, :], preferred_element_type=F32))
        act = (gte * _sigmoid(gte) * up).astype(BF16)
        y = jnp.dot(act, wdb[...], preferred_element_type=F32)

        for j in range(nblk):
            @pl.when(g + j >= EXPERT_Y_RING)
            def _():
                y_copy(g + j - EXPERT_Y_RING, sy + j).wait()

        ybuf[pl.ds(sy, nblk)] = y.reshape(nblk, MOE_BLOCK, y.shape[1])
        for j in range(nblk):
            y_copy(g + j, sy + j).start()

    lead = jnp.where(cnt > 0, first & 1, 0)
    rest = cnt - lead

    @pl.when(lead == 1)
    def _():
        run_blocks(first, 1)

    lax.fori_loop(0, rest // 2, lambda b, c: (run_blocks(first + lead + 2 * b, 2), c)[1], 0)

    @pl.when((rest & 1) == 1)
    def _():
        run_blocks(first + cnt - 1, 1)

    @pl.when(e == pl.num_programs(0) - 1)
    def _():
        for back in range(1, EXPERT_Y_RING + 1):
            @pl.when(nb >= back)
            def _():
                y_copy(nb - back, lax.rem(nb - back, EXPERT_Y_RING)).wait()

        ybuf[0] = jnp.zeros(ybuf.shape[1:], F32)
        lax.fori_loop(nb, n_blocks, lambda g, c: (y_copy(g, 0).start(), c)[1], 0)
        lax.fori_loop(nb, n_blocks, lambda g, c: (y_copy(g, 0).wait(), c)[1], 0)


def experts(xs, w_gate, w_up, w_down, first_blk, n_blk, n_used):
    n_rows, d_packed = xs.shape
    n_experts, d, de = w_gate.shape
    grid_spec = pltpu.PrefetchScalarGridSpec(
        num_scalar_prefetch=3,
        grid=(n_experts,),
        in_specs=[pl.BlockSpec(memory_space=pl.ANY)] * 4,
        out_specs=pl.BlockSpec(memory_space=pl.ANY),
        scratch_shapes=[pltpu.VMEM((EXPERT_X_RING, MOE_BLOCK, d_packed), xs.dtype),
                        pltpu.VMEM((EXPERT_Y_RING, MOE_BLOCK, d), F32),
                        pltpu.VMEM((2, d, de), F32), pltpu.VMEM((2, d, de), F32), pltpu.VMEM((2, de, d), F32),
                        pltpu.VMEM((d, de), BF16), pltpu.VMEM((d, de), BF16), pltpu.VMEM((de, d), BF16),
                        pltpu.SemaphoreType.DMA((EXPERT_X_RING,)), pltpu.SemaphoreType.DMA((EXPERT_Y_RING,)),
                        pltpu.SemaphoreType.DMA((2,))],
    )
    return pl.pallas_call(
        _experts_kernel,
        grid_spec=grid_spec,
        out_shape=jax.ShapeDtypeStruct((n_rows, d), F32),
        compiler_params=_cparams(("arbitrary",)),
        name="moe_experts",
    )(first_blk, n_blk, n_used, xs, w_gate, w_up, w_down)


def _moe_combine_kernel(dest_ref, destn_ref, x_ref, info_ref, g_ref, ys_hbm, o_ref, ybuf, sem, *, norm):
    i = pl.program_id(0)
    n = pl.num_programs(0)
    tm = x_ref.shape[0]
    slot = lax.rem(i, 2)

    def gather_start(dref, s):
        for k in range(2):
            for r in range(tm):
                pltpu.make_async_copy(ys_hbm.at[pl.ds(dref[k, r], 1)], ybuf.at[s, k, pl.ds(r, 1)],
                                      sem.at[s]).start(priority=r % 2)

    @pl.when(i == 0)
    def _():
        gather_start(dest_ref, 0)

    @pl.when(i + 1 < n)
    def _():
        gather_start(destn_ref, 1 - slot)

    for k in range(2):
        pltpu.make_async_copy(ys_hbm.at[pl.ds(0, tm)], ybuf.at[slot, k], sem.at[slot]).wait()
    w1 = info_ref[:, 4:5]
    w2 = info_ref[:, 5:6]
    s = x_ref[...] + (w1 * ybuf[slot, 0] + w2 * ybuf[slot, 1])
    o_ref[...] = _rms(s, g_ref[...]) if norm else s


def moe_combine(x, ys, dest, info, g, *, norm, tm):
    t, d = x.shape
    n = t // tm
    return pl.pallas_call(
        functools.partial(_moe_combine_kernel, norm=norm),
        grid=(n,),
        in_specs=[pl.BlockSpec((2, tm), lambda i: (0, i), memory_space=pltpu.SMEM),
                  pl.BlockSpec((2, tm), lambda i: (0, jnp.minimum(i + 1, n - 1)), memory_space=pltpu.SMEM),
                  pl.BlockSpec((tm, d), lambda i: (i, 0)),
                  pl.BlockSpec((tm, LANES), lambda i: (i, 0)),
                  pl.BlockSpec((1, d), lambda i: (0, 0)),
                  pl.BlockSpec(memory_space=pl.ANY)],
        out_specs=pl.BlockSpec((tm, d), lambda i: (i, 0)),
        out_shape=jax.ShapeDtypeStruct((t, d), F32),
        scratch_shapes=[pltpu.VMEM((2, 2, tm, d), F32), pltpu.SemaphoreType.DMA((2,))],
        compiler_params=_cparams(("arbitrary",)),
        name="moe_combine",
    )(dest, dest, x, info, g.reshape(1, d), ys)


def forward(x, mem, norm_mix_g, w_in, s5_lambda_re, s5_lambda_im, s5_log_dt, s5_b_re, s5_b_im, s5_c_re,
            s5_c_im, s5_d, s5_w_glu, conv_dw_w, conv_dw_b, conv_ln_g, conv_ln_b, conv_w_out, w_out,
            norm_xattn_g, norm_mem_g, xattn_wq, xattn_wk, xattn_wv, xattn_wo, norm_moe_g, router_w_group,
            router_b_group, router_w_expert, router_b_expert, exp_w_gate, exp_w_up, exp_w_down, norm_final_g,
            *, tiles):
    batch, seq, d = x.shape
    depth = w_in.shape[0]
    n_mem = mem.shape[1]
    s5_width = s5_d.shape[1]
    conv_width = conv_dw_b.shape[1]
    n_groups, epg = router_w_expert.shape[1], router_w_expert.shape[3]
    t = batch * seq
    xf = x.reshape(t, d)
    memf = mem.reshape(batch * n_mem, d)
    gate_col0 = s5_width + 2 * conv_width
    for l in range(depth):
        w_in_b = w_in[l].astype(BF16)
        proj = norm_mm_rows(xf, norm_mix_g[l], w_in_b, tm=tiles["proj_tm"], tn=tiles["proj_tn"],
                            out_dtype=F32, n_cols=gate_col0)
        lbr, lbi, bbr, bbi = s5_discretise(s5_lambda_re[l], s5_lambda_im[l], s5_log_dt[l],
                                           s5_b_re[l], s5_b_im[l])
        bre, bim, cre, cim = s5_block_diag(bbr, bbi, s5_c_re[l], s5_c_im[l])
        y_a = s5_branch(proj, bre, bim, cre, cim, lbr.reshape(1, -1), lbi.reshape(1, -1), s5_d[l],
                        s5_w_glu[l].astype(BF16), batch=batch, seq=seq, width=s5_width, m=tiles["s5_m"],
                        cw=tiles["s5_cw"], tn=tiles["glu_tn"])
        y_b = conv_branch(proj, conv_dw_w[l], conv_dw_b[l], conv_ln_g[l], conv_ln_b[l],
                          conv_w_out[l].astype(BF16), batch=batch, seq=seq, width=conv_width,
                          col0=s5_width, tm=tiles["conv_tm"])
        xf = combine(xf, norm_mix_g[l], w_in_b, y_a, y_b, w_out[l].astype(BF16), gate_col0=gate_col0,
                     tm=tiles["comb_tm"])

        wkv = jnp.concatenate([xattn_wk[l], xattn_wv[l]], axis=1).astype(BF16)
        kv = norm_mm(memf, norm_mem_g[l], wkv, tm=batch * n_mem, tn=tiles["kv_tn"], out_dtype=BF16)
        w_r = jnp.concatenate([router_w_group[l],
                               jnp.transpose(router_w_expert[l], (1, 0, 2)).reshape(d, n_groups * epg)], axis=1)
        b_r = jnp.concatenate([router_b_group[l], router_b_expert[l].reshape(-1)])
        n_r = w_r.shape[1]
        w_pad = jnp.pad(w_r, ((0, 0), (0, LANES - n_r)))
        w_hi = w_pad.astype(BF16)
        w_pad = jnp.concatenate([w_hi, (w_pad - w_hi.astype(F32)).astype(BF16)], axis=1)
        b_pad = jnp.pad(b_r, (0, LANES - n_r)).reshape(1, LANES)
        xf, h, info, infot, counts = xattn_router(
            xf, norm_xattn_g[l], xattn_wq[l].astype(BF16), kv, xattn_wo[l].astype(BF16), norm_moe_g[l],
            w_pad, b_pad, batch=batch, seq=seq,
            n_mem=n_mem, heads=XATTN_HEADS, n_groups=n_groups, epg=epg, tm=tiles["xattn_tm"])
        dest, n_rows, n_used, pcounts, pends = moe_dispatch_plan(infot, counts, t, n_groups * epg)
        xs = moe_dispatch(h, dest, pcounts, pends, n_rows=n_rows, tm=tiles["dispatch_tm"])
        ys = experts(xs, exp_w_gate[l], exp_w_up[l], exp_w_down[l], (pends - pcounts) // MOE_BLOCK,
                     pcounts // MOE_BLOCK, n_used)
        last = l + 1 == depth
        xf = moe_combine(xf, ys, dest, info, norm_final_g if last else jnp.ones((d,), F32), norm=last,
                         tm=tiles["moe_tm"])
    return xf.reshape(batch, seq, d)


def kernel(x, mem, norm_mix_g, w_in, s5_lambda_re, s5_lambda_im, s5_log_dt, s5_b_re, s5_b_im, s5_c_re, s5_c_im, s5_d, s5_w_glu, conv_dw_w, conv_dw_b, conv_ln_g, conv_ln_b, conv_w_out, w_out, norm_xattn_g, norm_mem_g, xattn_wq, xattn_wk, xattn_wv, xattn_wo, norm_moe_g, router_w_group, router_b_group, router_w_expert, router_b_expert, exp_w_gate, exp_w_up, exp_w_down, norm_final_g):
    return forward(x, mem, norm_mix_g, w_in, s5_lambda_re, s5_lambda_im, s5_log_dt, s5_b_re, s5_b_im,
                   s5_c_re, s5_c_im, s5_d, s5_w_glu, conv_dw_w, conv_dw_b, conv_ln_g, conv_ln_b, conv_w_out,
                   w_out, norm_xattn_g, norm_mem_g, xattn_wq, xattn_wk, xattn_wv, xattn_wo, norm_moe_g,
                   router_w_group, router_b_group, router_w_expert, router_b_expert, exp_w_gate, exp_w_up,
                   exp_w_down, norm_final_g, tiles=TILES)
```

```python
import functools
import math

import jax
import jax.numpy as jnp
from jax import lax
from jax.experimental import pallas as pl
from jax.experimental.pallas import tpu as pltpu

F32 = jnp.float32
BF16 = jnp.bfloat16

RMS_EPS = 1e-6
LN_EPS = 1e-5

S5_SLAB_CH = 256
CONV_HALO = 32
CONV_ROW_CHUNK = 64
XATTN_HEADS = 4
MOE_BLOCK = 128
BLOCK_DMA_QUEUE = 1
EXPERT_X_RING = 8
EXPERT_Y_RING = 4
SUBLANES = 8
LANES = 128
VMEM_LIMIT = 56 * 1024 * 1024

TILES = dict(proj_tm=512, proj_tn=512, s5_m=64, s5_cw=512, glu_tn=256, conv_tm=512,
             comb_tm=256, kv_tn=512, xattn_tm=512, dispatch_tm=1024, moe_tm=256)


def _cparams(sem):
    return pltpu.CompilerParams(dimension_semantics=sem, vmem_limit_bytes=VMEM_LIMIT)


def _rms(x, g):
    return x * lax.rsqrt(jnp.mean(x * x, axis=-1, keepdims=True) + RMS_EPS) * g


def _sigmoid(x):
    return 0.5 * jnp.tanh(0.5 * x) + 0.5


def _gelu_tanh(x):
    c = math.sqrt(2.0 / math.pi)
    return 0.5 * x * (1.0 + jnp.tanh(c * (x + 0.044715 * (x * x * x))))


def _norm_mm_kernel(x_ref, g_ref, w_ref, o_ref, h_ref):
    @pl.when(pl.program_id(1) == 0)
    def _():
        h_ref[...] = _rms(x_ref[...], g_ref[...]).astype(BF16)

    o_ref[...] = jnp.dot(h_ref[...], w_ref[...], preferred_element_type=F32).astype(o_ref.dtype)


def norm_mm(x, g, w, *, tm, tn, out_dtype):
    m, k = x.shape
    n = w.shape[1]
    return pl.pallas_call(
        _norm_mm_kernel,
        grid=(m // tm, n // tn),
        in_specs=[pl.BlockSpec((tm, k), lambda i, j: (i, 0)),
                  pl.BlockSpec((1, k), lambda i, j: (0, 0)),
                  pl.BlockSpec((k, tn), lambda i, j: (0, j))],
        out_specs=pl.BlockSpec((tm, tn), lambda i, j: (i, j)),
        out_shape=jax.ShapeDtypeStruct((m, n), out_dtype),
        scratch_shapes=[pltpu.VMEM((tm, k), BF16)],
        compiler_params=_cparams(("parallel", "arbitrary")),
        name="norm_mm",
    )(x, g.reshape(1, k), w)


def _norm_mm_rows_kernel(x_ref, g_ref, w_ref, o_ref, *, tn):
    h = _rms(x_ref[...], g_ref[...]).astype(BF16)
    for j in range(o_ref.shape[1] // tn):
        cols = slice(j * tn, (j + 1) * tn)
        o_ref[:, cols] = jnp.dot(h, w_ref[:, cols], preferred_element_type=F32).astype(o_ref.dtype)


def norm_mm_rows(x, g, w, *, tm, tn, out_dtype, n_cols=None):
    m, k = x.shape
    n = w.shape[1] if n_cols is None else n_cols
    return pl.pallas_call(
        functools.partial(_norm_mm_rows_kernel, tn=tn),
        grid=(m // tm,),
        in_specs=[pl.BlockSpec((tm, k), lambda i: (i, 0)),
                  pl.BlockSpec((1, k), lambda i: (0, 0)),
                  pl.BlockSpec((k, n), lambda i: (0, 0), pipeline_mode=pl.Buffered(1))],
        out_specs=pl.BlockSpec((tm, n), lambda i: (i, 0)),
        out_shape=jax.ShapeDtypeStruct((m, n), out_dtype),
        compiler_params=_cparams(("parallel",)),
        name="norm_mm_rows",
    )(x, g.reshape(1, k), w)


def _s5_disc_kernel(lr_ref, li_ref, ldt_ref, br_ref, bi_ref, lbr_ref, lbi_ref, bbr_ref, bbi_ref):
    lr, li = lr_ref[...], li_ref[...]
    dt = jnp.exp(ldt_ref[...])
    mag = jnp.exp(lr * dt)
    ang = li * dt
    lb_re, lb_im = mag * jnp.cos(ang), mag * jnp.sin(ang)
    den = lr * lr + li * li
    nr, ni = lb_re - 1.0, lb_im
    coef_re = (nr * lr + ni * li) / den
    coef_im = (ni * lr - nr * li) / den
    lbr_ref[...] = lb_re
    lbi_ref[...] = lb_im
    br, bi = br_ref[...], bi_ref[...]
    bbr_ref[...] = coef_re[None] * br - coef_im[None] * bi
    bbi_ref[...] = coef_re[None] * bi + coef_im[None] * br


def s5_discretise(lam_re, lam_im, log_dt, b_re, b_im):
    g, n = lam_re.shape
    c = b_re.shape[-1]
    b_re_t = jnp.transpose(b_re, (2, 0, 1))
    b_im_t = jnp.transpose(b_im, (2, 0, 1))
    gn = jax.ShapeDtypeStruct((g, n), F32)
    cgn = jax.ShapeDtypeStruct((c, g, n), F32)
    return pl.pallas_call(
        _s5_disc_kernel,
        out_shape=(gn, gn, cgn, cgn),
        name="s5_discretise",
    )(lam_re, lam_im, log_dt.reshape(g, 1), b_re_t, b_im_t)


def s5_block_diag(bb_re, bb_im, c_re, c_im):
    c, g, n = bb_re.shape
    gs = S5_SLAB_CH // c
    n_slab = g // gs
    row_grp_b = jnp.arange(gs * c) // c
    col_grp_b = jnp.arange(gs * n) // n
    tile_n = (jnp.arange(n)[:, None] == jnp.arange(gs * n)[None, :] % n).astype(F32)
    tile_c = (jnp.arange(c)[:, None] == jnp.arange(gs * c)[None, :] % c).astype(F32)
    exact = lax.Precision.HIGHEST

    def bmat(bb):
        b = jnp.transpose(bb, (1, 0, 2)).reshape(n_slab, gs * c, n)
        full = jnp.einsum('srn,nq->srq', b, tile_n, precision=exact)
        return jnp.where(row_grp_b[:, None] == col_grp_b[None, :], full, 0.0).astype(BF16)

    def cmat(cc):
        cm = jnp.transpose(cc.reshape(n_slab, gs, c, n), (0, 1, 3, 2)).reshape(n_slab, gs * n, c)
        full = jnp.einsum('spc,cr->spr', cm, tile_c, precision=exact)
        return jnp.where(col_grp_b[:, None] == row_grp_b[None, :], full, 0.0).astype(BF16)

    return bmat(bb_re), bmat(bb_im), cmat(c_re), cmat(c_im)


def _s5_kernel(proj_hbm, bre_ref, bim_ref, cre_ref, cim_ref, lre_ref, lim_ref, d_ref, wglu_ref, ya_hbm,
               ubuf, yabuf, zbuf, xre_ref, xim_ref, pre_ref, pim_ref, car_re_ref, car_im_ref, isem, osem,
               *, m, cw, tn):
    t = pl.program_id(1)
    n_t = pl.num_programs(1)
    step_id = pl.program_id(0) * n_t + t
    n_steps = pl.num_programs(0) * n_t
    slot = lax.rem(step_id, 2)
    rows = SUBLANES * m
    width = d_ref.shape[1]
    n_state = lre_ref.shape[1]
    n_slab = bre_ref.shape[0]
    slab_states = n_state // n_slab

    def in_copy(step, s, k):
        r0 = pl.multiple_of(step * rows + k * m, SUBLANES)
        return pltpu.make_async_copy(proj_hbm.at[pl.ds(r0, m), pl.ds(0, width)], ubuf.at[s, :, k, :], isem.at[s])

    def out_copy(step, s, k):
        r0 = pl.multiple_of(step * rows + k * m, SUBLANES)
        return pltpu.make_async_copy(yabuf.at[s, :, k, :], ya_hbm.at[pl.ds(r0, m)], osem.at[s])

    @pl.when(step_id == 0)
    def _():
        for k in range(SUBLANES):
            in_copy(0, 0, k).start()

    @pl.when(step_id + 1 < n_steps)
    def _():
        for k in range(SUBLANES):
            in_copy(step_id + 1, 1 - slot, k).start()

    @pl.when(t == 0)
    def _():
        car_re_ref[...] = jnp.zeros_like(car_re_ref)
        car_im_ref[...] = jnp.zeros_like(car_im_ref)
        lr, li = lre_ref[...], lim_ref[...]

        def pw(_, carry):
            pr, pi = carry
            return pr * lr - pi * li, pr * li + pi * lr

        pr, pi = lax.fori_loop(1, m, pw, (lr, li))
        pre_ref[...] = pr
        pim_ref[...] = pi

    @pl.when(step_id == 0)
    def _():
        zbuf[...] = jnp.zeros_like(zbuf)

    @pl.when(step_id >= 3)
    def _():
        for k in range(SUBLANES):
            out_copy(step_id - 3, 1 - slot, k).wait()

    nh = wglu_ref.shape[1] // 2

    def glu_pieces(z, s):
        def piece(j):
            val = jnp.dot(z, wglu_ref[:, j * tn:(j + 1) * tn], preferred_element_type=F32)
            gt = jnp.dot(z, wglu_ref[:, nh + j * tn:nh + (j + 1) * tn], preferred_element_type=F32)
            yabuf[s, :, :, j * tn:(j + 1) * tn] = (val * _sigmoid(gt)).reshape(m, SUBLANES, tn)
        return [functools.partial(piece, j) for j in range(nh // tn)]

    for k in range(SUBLANES):
        in_copy(step_id, slot, k).wait()
    u = ubuf[slot].reshape(rows, width)
    up = u.astype(BF16)

    def b_project(s):
        us = up[:, s * S5_SLAB_CH:(s + 1) * S5_SLAB_CH]
        cols = slice(s * slab_states, (s + 1) * slab_states)
        xre_ref[:, cols] = jnp.dot(us, bre_ref[s], preferred_element_type=F32)
        xim_ref[:, cols] = jnp.dot(us, bim_ref[s], preferred_element_type=F32)

    def c_project(s):
        cols = slice(s * slab_states, (s + 1) * slab_states)
        return (jnp.dot(xre_ref[:, cols].astype(BF16), cre_ref[s], preferred_element_type=F32)
                - jnp.dot(xim_ref[:, cols].astype(BF16), cim_ref[s], preferred_element_type=F32))

    n_cb = n_state // cw
    cb_per_slab = slab_states // cw
    glu_prev = glu_pieces(zbuf[...], 1 - slot)
    glu_at = {(i * n_cb) // len(glu_prev): [] for i in range(len(glu_prev))}
    for i, piece in enumerate(glu_prev):
        glu_at[(i * n_cb) // len(glu_prev)].append(piece)
    ys = []
    b_project(0)
    row_id = lax.broadcasted_iota(jnp.int32, (SUBLANES, cw), 0)
    for cb in range(n_cb):
        cols = slice(cb * cw, (cb + 1) * cw)
        slab = cb // cb_per_slab
        if cb % cb_per_slab == 0:
            if slab + 1 < n_slab:
                b_project(slab + 1)
            if slab >= 1:
                ys.append(c_project(slab - 1))
        for piece in glu_at.get(cb, []):
            piece()
        lr = jnp.broadcast_to(lre_ref[:, cols], (SUBLANES, cw))
        li = jnp.broadcast_to(lim_ref[:, cols], (SUBLANES, cw))

        def step(tau, carry, store, cols=cols, lr=lr, li=li):
            sr, si = carry
            r0 = pl.multiple_of(tau * SUBLANES, SUBLANES)
            nr = sr * lr - si * li + xre_ref[pl.ds(r0, SUBLANES), cols]
            ni = sr * li + si * lr + xim_ref[pl.ds(r0, SUBLANES), cols]
            if store:
                xre_ref[pl.ds(r0, SUBLANES), cols] = nr
                xim_ref[pl.ds(r0, SUBLANES), cols] = ni
            return nr, ni

        zero = jnp.zeros((SUBLANES, cw), F32)
        er, ei = lax.fori_loop(0, m, functools.partial(step, store=False), (zero, zero), unroll=True)

        lmr = jnp.broadcast_to(pre_ref[:, cols], (SUBLANES, cw))
        lmi = jnp.broadcast_to(pim_ref[:, cols], (SUBLANES, cw))
        ir = jnp.where(row_id == 0, jnp.broadcast_to(car_re_ref[:, cols], (SUBLANES, cw)), 0.0)
        ii = jnp.where(row_id == 0, jnp.broadcast_to(car_im_ref[:, cols], (SUBLANES, cw)), 0.0)
        for k in range(1, SUBLANES):
            nr = ir * lmr - ii * lmi + er
            ni = ir * lmi + ii * lmr + ei
            ir = jnp.where(row_id == k, pltpu.roll(nr, 1, axis=0), ir)
            ii = jnp.where(row_id == k, pltpu.roll(ni, 1, axis=0), ii)

        fr, fi = lax.fori_loop(0, m, functools.partial(step, store=True), (ir, ii), unroll=True)
        car_re_ref[:, cols] = fr[SUBLANES - 1:SUBLANES, :]
        car_im_ref[:, cols] = fi[SUBLANES - 1:SUBLANES, :]

    ys.append(c_project(n_slab - 1))
    y = jnp.concatenate(ys, axis=1) + d_ref[...] * u
    zbuf[...] = _gelu_tanh(y).astype(BF16)

    @pl.when(step_id >= 1)
    def _():
        for k in range(SUBLANES):
            out_copy(step_id - 1, 1 - slot, k).start()

    @pl.when(step_id == n_steps - 1)
    def _():
        @pl.when(step_id >= 2)
        def _():
            for k in range(SUBLANES):
                out_copy(step_id - 2, slot, k).wait()
        for piece in glu_pieces(zbuf[...], slot):
            piece()
        for k in range(SUBLANES):
            out_copy(step_id, slot, k).start()
        for k in range(SUBLANES):
            out_copy(step_id, slot, k).wait()

        @pl.when(step_id >= 1)
        def _():
            for k in range(SUBLANES):
                out_copy(step_id - 1, 1 - slot, k).wait()


def s5_branch(proj, bre, bim, cre, cim, lre, lim, d_skip, w_glu, *, batch, seq, width, m, cw, tn):
    rows = SUBLANES * m
    n_t = seq // rows
    n_state = lre.shape[1]
    nh = w_glu.shape[1] // 2
    once = pl.Buffered(1)
    const3 = lambda b, t: (0, 0, 0)
    const2 = lambda b, t: (0, 0)
    return pl.pallas_call(
        functools.partial(_s5_kernel, m=m, cw=cw, tn=tn),
        grid=(batch, n_t),
        in_specs=[pl.BlockSpec(memory_space=pl.ANY),
                  pl.BlockSpec(bre.shape, const3, pipeline_mode=once),
                  pl.BlockSpec(bim.shape, const3, pipeline_mode=once),
                  pl.BlockSpec(cre.shape, const3, pipeline_mode=once),
                  pl.BlockSpec(cim.shape, const3, pipeline_mode=once),
                  pl.BlockSpec((1, n_state), const2),
                  pl.BlockSpec((1, n_state), const2),
                  pl.BlockSpec((1, width), const2),
                  pl.BlockSpec(w_glu.shape, const2, pipeline_mode=once)],
        out_specs=pl.BlockSpec(memory_space=pl.ANY),
        out_shape=jax.ShapeDtypeStruct((batch * seq, nh), F32),
        scratch_shapes=[pltpu.VMEM((2, m, SUBLANES, width), F32), pltpu.VMEM((2, m, SUBLANES, nh), F32),
                        pltpu.VMEM((rows, width), BF16),
                        pltpu.VMEM((rows, n_state), F32), pltpu.VMEM((rows, n_state), F32),
                        pltpu.VMEM((1, n_state), F32), pltpu.VMEM((1, n_state), F32),
                        pltpu.VMEM((1, n_state), F32), pltpu.VMEM((1, n_state), F32),
                        pltpu.SemaphoreType.DMA((2,)), pltpu.SemaphoreType.DMA((2,))],
        compiler_params=_cparams(("arbitrary", "arbitrary")),
        name="s5_branch",
    )(proj, bre, bim, cre, cim, lre, lim, d_skip.reshape(1, width), w_glu)


def _conv_kernel(a_ref, b_ref, ah_ref, bh_ref, dww_ref, dwb_ref, lng_ref, lnb_ref, w_ref, o_ref,
                 zs_ref, wb_ref, yc_ref, *, tm, taps):
    i = pl.program_id(1)
    off0 = CONV_HALO - (taps - 1)
    width = a_ref.shape[1]
    n_slab = width // LANES
    zh = jnp.where(i == 0, 0.0, ah_ref[...] * _sigmoid(bh_ref[...]))
    z = a_ref[...] * _sigmoid(b_ref[...])
    for s in range(n_slab):
        zs_ref[s, 0:CONV_HALO, :] = zh[:, s * LANES:(s + 1) * LANES]
        zs_ref[s, CONV_HALO:, :] = z[:, s * LANES:(s + 1) * LANES]

    @pl.when(i == 0)
    def _():
        for j in range(taps):
            wb_ref[j] = jnp.broadcast_to(dww_ref[j:j + 1, :], (SUBLANES, width))
        wb_ref[taps] = jnp.broadcast_to(dwb_ref[...], (SUBLANES, width))

    rg = CONV_ROW_CHUNK // SUBLANES

    def chunk(c, carry):
        r0 = pl.multiple_of(c * CONV_ROW_CHUNK, CONV_ROW_CHUNK)
        for s in range(n_slab):
            lanes = slice(s * LANES, (s + 1) * LANES)
            bias = wb_ref[taps, :, lanes]
            acc = [bias] * rg
            for j in range(taps):
                w = wb_ref[j, :, lanes]
                for g in range(rg):
                    acc[g] = acc[g] + w * zs_ref[s, pl.ds(r0 + g * SUBLANES + off0 + j, SUBLANES), :]
            for g in range(rg):
                yc_ref[pl.ds(pl.multiple_of(r0 + g * SUBLANES, SUBLANES), SUBLANES), lanes] = acc[g]
        return carry

    lax.fori_loop(0, tm // CONV_ROW_CHUNK, chunk, 0)
    y = yc_ref[...]
    mu = jnp.mean(y, axis=-1, keepdims=True)
    var = jnp.mean(jnp.square(y - mu), axis=-1, keepdims=True)
    y = (y - mu) * lax.rsqrt(var + LN_EPS) * lng_ref[...] + lnb_ref[...]
    y = y * _sigmoid(y)
    o_ref[...] = jnp.dot(y.astype(BF16), w_ref[...], preferred_element_type=F32)


def conv_branch(proj, dw_w, dw_b, ln_g, ln_b, w_out, *, batch, seq, width, col0, tm):
    n_t = seq // tm
    ca, cb = col0 // width, col0 // width + 1
    hb = tm // CONV_HALO
    taps = dw_w.shape[0]
    d_out = w_out.shape[1]
    const2 = lambda b, i: (0, 0)
    halo = lambda b, i: jnp.maximum((b * n_t + i) * hb - 1, 0)
    return pl.pallas_call(
        functools.partial(_conv_kernel, tm=tm, taps=taps),
        grid=(batch, n_t),
        in_specs=[pl.BlockSpec((tm, width), lambda b, i: (b * n_t + i, ca)),
                  pl.BlockSpec((tm, width), lambda b, i: (b * n_t + i, cb)),
                  pl.BlockSpec((CONV_HALO, width), lambda b, i: (halo(b, i), ca)),
                  pl.BlockSpec((CONV_HALO, width), lambda b, i: (halo(b, i), cb)),
                  pl.BlockSpec((taps, width), const2),
                  pl.BlockSpec((1, width), const2),
                  pl.BlockSpec((1, width), const2),
                  pl.BlockSpec((1, width), const2),
                  pl.BlockSpec((width, d_out), const2)],
        out_specs=pl.BlockSpec((tm, d_out), lambda b, i: (b * n_t + i, 0)),
        out_shape=jax.ShapeDtypeStruct((batch * seq, d_out), F32),
        scratch_shapes=[pltpu.VMEM((width // LANES, tm + CONV_HALO, LANES), F32),
                        pltpu.VMEM((taps + 1, SUBLANES, width), F32),
                        pltpu.VMEM((tm, width), F32)],
        compiler_params=_cparams(("parallel", "arbitrary")),
        name="conv_branch",
    )(proj, proj, proj, proj, dw_w, dw_b.reshape(1, width), ln_g.reshape(1, width),
      ln_b.reshape(1, width), w_out)


def _combine_kernel(*refs, parts):
    x_ref, g_ref = refs[0], refs[1]
    ya_ref, yb_ref, wo_ref, o_ref = refs[2 + 2 * parts], refs[3 + 2 * parts], refs[-2], refs[-1]
    gw = ya_ref.shape[1] // parts
    x = x_ref[...]
    h = _rms(x, g_ref[...]).astype(BF16)
    o_ref[...] = x
    for p in range(parts):
        cols = slice(p * gw, (p + 1) * gw)
        ga = _sigmoid(jnp.dot(h, refs[2 + p][...], preferred_element_type=F32))
        gb = _sigmoid(jnp.dot(h, refs[2 + parts + p][...], preferred_element_type=F32))
        mix = ga * ya_ref[:, cols] + gb * yb_ref[:, cols]
        o_ref[...] += jnp.dot(mix.astype(BF16), wo_ref[cols, :], preferred_element_type=F32)


def combine(x, g, w_in, y_a, y_b, w_out, *, gate_col0, tm):
    m, d = x.shape
    gw = math.gcd(gate_col0, d)
    parts = d // gw
    row = lambda i: (i, 0)
    once = pl.Buffered(1)
    gate_specs = [pl.BlockSpec((d, gw), functools.partial(lambda i, c: (0, c), c=(gate_col0 + s * d) // gw + p),
                               pipeline_mode=once) for s in range(2) for p in range(parts)]
    return pl.pallas_call(
        functools.partial(_combine_kernel, parts=parts),
        grid=(m // tm,),
        in_specs=[pl.BlockSpec((tm, d), row), pl.BlockSpec((1, d), lambda i: (0, 0))] + gate_specs +
                 [pl.BlockSpec((tm, d), row),
                  pl.BlockSpec((tm, d), row),
                  pl.BlockSpec((d, d), lambda i: (0, 0), pipeline_mode=once)],
        out_specs=pl.BlockSpec((tm, d), row),
        out_shape=jax.ShapeDtypeStruct((m, d), F32),
        compiler_params=_cparams(("parallel",)),
        name="gated_combine",
    )(x, g.reshape(1, d), *([w_in] * (2 * parts)), y_a, y_b, w_out)


def _route_tile(x, g_ref, w_ref, b_ref, tri_ref, h_ref, info_ref, infot_ref, cnt_ref, run_ref, *,
                n_groups, epg):
    tm = x.shape[0]
    h = _rms(x, g_ref[...])
    half = h.shape[1] // 2
    bits = pltpu.bitcast(h.astype(BF16).astype(F32), jnp.uint32)
    h_ref[...] = (bits[:, :half] >> 16) | (bits[:, half:] & jnp.uint32(0xFFFF0000))
    h_hi = h.astype(BF16)
    h_lo = (h - h_hi.astype(F32)).astype(BF16)
    p = jnp.dot(h_hi, w_ref[...], preferred_element_type=F32)
    logits = (p[:, :LANES] + p[:, LANES:] + jnp.dot(h_lo, w_ref[:, :LANES], preferred_element_type=F32)
              + b_ref[...])
    lane = lax.broadcasted_iota(jnp.int32, (tm, LANES), 1).astype(F32)
    neg = jnp.float32(-jnp.inf)
    big = jnp.float32(LANES)
    is_grp = lane < n_groups
    cl = jnp.where(is_grp, logits, neg)
    cmax = jnp.max(cl, axis=-1, keepdims=True)
    g_idx = jnp.min(jnp.where(cl == cmax, lane, big), axis=-1, keepdims=True)
    p_sel = 1.0 / jnp.sum(jnp.where(is_grp, jnp.exp(cl - cmax), 0.0), axis=-1, keepdims=True)
    lo = n_groups + g_idx * epg
    in_grp = (lane >= lo) & (lane < lo + epg)
    fl = jnp.where(in_grp, logits, neg)
    v1 = jnp.max(fl, axis=-1, keepdims=True)
    i1 = jnp.min(jnp.where(fl == v1, lane, big), axis=-1, keepdims=True)
    fl2 = jnp.where(lane == i1, neg, fl)
    v2 = jnp.max(fl2, axis=-1, keepdims=True)
    i2 = jnp.min(jnp.where(fl2 == v2, lane, big), axis=-1, keepdims=True)
    e2x = jnp.exp(v2 - v1)
    w1 = p_sel / (1.0 + e2x)
    w2 = p_sel * e2x / (1.0 + e2x)
    e1 = i1 - n_groups
    e2 = i2 - n_groups
    oh1 = lane == e1
    oh2 = lane == e2
    both = jnp.where(oh1 | oh2, 1.0, 0.0)
    before = jnp.dot(tri_ref[...], both.astype(BF16), preferred_element_type=F32) + run_ref[...]
    r1 = jnp.sum(jnp.where(oh1, before, 0.0), axis=-1, keepdims=True)
    r2 = jnp.sum(jnp.where(oh2, before, 0.0), axis=-1, keepdims=True)
    run_ref[...] += jnp.sum(both, axis=0, keepdims=True)
    cnt_ref[...] = run_ref[...]
    info = jnp.where(lane == 0, e1, 0.0)
    info = jnp.where(lane == 1, e2, info)
    info = jnp.where(lane == 2, r1, info)
    info = jnp.where(lane == 3, r2, info)
    info = jnp.where(lane == 4, w1, info)
    info = jnp.where(lane == 5, w2, info)
    info_ref[...] = info
    infot_ref[...] = jnp.transpose(info)[0:SUBLANES, :]


def _xattn_router_kernel(x_ref, gx_ref, wq_ref, kv_ref, wo_ref, g_ref, w_ref, b_ref, tri_ref,
                         o_ref, h_ref, info_ref, infot_ref, cnt_ref, run_ref, *, heads, scale, n_groups, epg):
    @pl.when((pl.program_id(0) == 0) & (pl.program_id(1) == 0))
    def _():
        run_ref[...] = jnp.zeros_like(run_ref)

    d = x_ref.shape[1]
    hd = d // heads
    acc = x_ref[...]
    hq = _rms(acc, gx_ref[...]).astype(BF16)

    def q_head(h):
        return jnp.dot(hq, wq_ref[:, h * hd:(h + 1) * hd], preferred_element_type=F32).astype(BF16)

    q_next = q_head(0)
    for h in range(heads):
        q = q_next
        k = kv_ref[:, h * hd:(h + 1) * hd]
        v = kv_ref[:, d + h * hd:d + (h + 1) * hd]
        s = lax.dot_general(q, k, (((1,), (1,)), ((), ())), preferred_element_type=F32) * scale
        if h + 1 < heads:
            q_next = q_head(h + 1)
        s = s - jnp.max(s, axis=-1, keepdims=True)
        p = jnp.exp(s)
        p = p / jnp.sum(p, axis=-1, keepdims=True)
        o = jnp.dot(p.astype(BF16), v, preferred_element_type=F32)
        acc = acc + jnp.dot(o.astype(BF16), wo_ref[h * hd:(h + 1) * hd, :], preferred_element_type=F32)
    o_ref[...] = acc
    _route_tile(acc, g_ref, w_ref, b_ref, tri_ref, h_ref, info_ref, infot_ref, cnt_ref, run_ref,
                n_groups=n_groups, epg=epg)


def xattn_router(x, gx, wq, kv, wo, g, w_pad, b_pad, *, batch, seq, n_mem, heads, n_groups, epg, tm):
    d = x.shape[1]
    m = batch * seq
    n_t = seq // tm
    scale = 1.0 / math.sqrt(d // heads)
    tri = (jnp.arange(tm)[:, None] > jnp.arange(tm)[None, :]).astype(BF16)
    row = lambda b, i: (b * n_t + i, 0)
    const = lambda b, i: (0, 0)
    once = pl.Buffered(1)
    return pl.pallas_call(
        functools.partial(_xattn_router_kernel, heads=heads, scale=scale, n_groups=n_groups, epg=epg),
        grid=(batch, n_t),
        in_specs=[pl.BlockSpec((tm, d), row),
                  pl.BlockSpec((1, d), const),
                  pl.BlockSpec((d, d), const, pipeline_mode=once),
                  pl.BlockSpec((n_mem, 2 * d), lambda b, i: (b, 0)),
                  pl.BlockSpec((d, d), const, pipeline_mode=once),
                  pl.BlockSpec((1, d), const),
                  pl.BlockSpec((d, 2 * LANES), const, pipeline_mode=once),
                  pl.BlockSpec((1, LANES), const),
                  pl.BlockSpec((tm, tm), const, pipeline_mode=once)],
        out_specs=[pl.BlockSpec((tm, d), row),
                   pl.BlockSpec((tm, d // 2), row),
                   pl.BlockSpec((tm, LANES), row),
                   pl.BlockSpec((SUBLANES, tm), lambda b, i: (0, b * n_t + i)),
                   pl.BlockSpec((1, LANES), const)],
        out_shape=(jax.ShapeDtypeStruct((m, d), F32),
                   jax.ShapeDtypeStruct((m, d // 2), jnp.uint32),
                   jax.ShapeDtypeStruct((m, LANES), F32),
                   jax.ShapeDtypeStruct((SUBLANES, m), F32),
                   jax.ShapeDtypeStruct((1, LANES), F32)),
        scratch_shapes=[pltpu.VMEM((1, LANES), F32)],
        compiler_params=_cparams(("arbitrary", "arbitrary")),
        name="xattn_router",
    )(x, gx.reshape(1, d), wq, kv, wo, g.reshape(1, d), w_pad, b_pad, tri)


def moe_dispatch_plan(infot, counts, n_tokens, n_experts):
    e = infot[0:2].astype(jnp.int32)
    rank = infot[2:4].astype(jnp.int32)
    counts = counts[0, :n_experts].astype(jnp.int32)
    pcounts = ((counts + MOE_BLOCK - 1) // MOE_BLOCK) * MOE_BLOCK
    pends = jnp.cumsum(pcounts)
    pstarts = pends - pcounts
    ids = jnp.arange(n_experts, dtype=jnp.int32)
    dest = rank + jnp.sum(jnp.where(e[:, :, None] == ids, pstarts, 0), axis=-1)
    n_rows = (-(-2 * n_tokens // MOE_BLOCK) + n_experts) * MOE_BLOCK
    n_used = (pends[-1] // MOE_BLOCK).reshape(1)
    return dest, n_rows, n_used, pcounts, pends


def _dispatch_kernel(pc_ref, pe_ref, dest_ref, h_ref, xs_hbm, zbuf, sem, tsem, *, n_experts):
    i = pl.program_id(0)
    tm = h_ref.shape[0]

    @pl.when(i == 0)
    def _():
        zbuf[...] = jnp.zeros_like(zbuf)
        for wait in (False, True):
            for e in range(n_experts):
                @pl.when(pc_ref[e] > 0)
                def _():
                    row0 = pl.multiple_of(pe_ref[e] - MOE_BLOCK, MOE_BLOCK)
                    fill = pltpu.make_async_copy(zbuf, xs_hbm.at[pl.ds(row0, MOE_BLOCK)], sem)
                    fill.wait() if wait else fill.start()

    n_blocks = xs_hbm.shape[0] // MOE_BLOCK
    n_used = pe_ref[n_experts - 1] // MOE_BLOCK

    def tail_fill(b, wait):
        row0 = pl.multiple_of(b * MOE_BLOCK, MOE_BLOCK)
        fill = pltpu.make_async_copy(zbuf, xs_hbm.at[pl.ds(row0, MOE_BLOCK)], tsem)
        fill.wait() if wait else fill.start()

    @pl.when(i == 0)
    def _():
        lax.fori_loop(n_used, n_blocks, lambda b, c: (tail_fill(b, False), c)[1], 0)

    for k in range(2):
        for r in range(tm):
            pltpu.make_async_copy(h_ref.at[pl.ds(r, 1)], xs_hbm.at[pl.ds(dest_ref[k, r], 1)],
                                  sem).start(priority=r % 2)
    for k in range(2):
        pltpu.make_async_copy(h_ref, xs_hbm.at[pl.ds(0, tm)], sem).wait()

    @pl.when(i == pl.num_programs(0) - 1)
    def _():
        lax.fori_loop(n_used, n_blocks, lambda b, c: (tail_fill(b, True), c)[1], 0)


def moe_dispatch(h, dest, pcounts, pends, *, n_rows, tm):
    t, d = h.shape
    grid_spec = pltpu.PrefetchScalarGridSpec(
        num_scalar_prefetch=2,
        grid=(t // tm,),
        in_specs=[pl.BlockSpec((2, tm), lambda i, pc, pe: (0, i), memory_space=pltpu.SMEM),
                  pl.BlockSpec((tm, d), lambda i, pc, pe: (i, 0))],
        out_specs=pl.BlockSpec(memory_space=pl.ANY),
        scratch_shapes=[pltpu.VMEM((MOE_BLOCK, d), h.dtype), pltpu.SemaphoreType.DMA(()),
                        pltpu.SemaphoreType.DMA(())],
    )
    return pl.pallas_call(
        functools.partial(_dispatch_kernel, n_experts=pcounts.shape[0]),
        grid_spec=grid_spec,
        out_shape=jax.ShapeDtypeStruct((n_rows, d), h.dtype),
        compiler_params=_cparams(("arbitrary",)),
        name="moe_dispatch",
    )(pcounts, pends, dest, h)


def _experts_kernel(first_ref, cnt_ref, nb_ref, xs_hbm, wg_hbm, wu_hbm, wd_hbm, ys_hbm,
                    xbuf, ybuf, wgf, wuf, wdf, wgb, wub, wdb, xsem, ysem, wsem):
    e = pl.program_id(0)
    n_e = pl.num_programs(0)
    nb = nb_ref[0]
    first, cnt = first_ref[e], cnt_ref[e]
    n_blocks = xs_hbm.shape[0] // MOE_BLOCK
    wslot = lax.rem(e, 2)

    def rows(g):
        return pl.ds(pl.multiple_of(g * MOE_BLOCK, MOE_BLOCK), MOE_BLOCK)

    def x_copy(g, s):
        return pltpu.make_async_copy(xs_hbm.at[rows(g)], xbuf.at[s], xsem.at[s])

    def y_copy(g, s):
        return pltpu.make_async_copy(ybuf.at[s], ys_hbm.at[rows(g)], ysem.at[s])

    def w_copies(ex, s):
        half = wd_hbm.shape[1] // 2
        lo, hi = pl.ds(0, half), pl.ds(half, half)
        return ((pltpu.make_async_copy(wg_hbm.at[ex], wgf.at[s], wsem.at[s]), 0),
                (pltpu.make_async_copy(wu_hbm.at[ex], wuf.at[s], wsem.at[s]), 1),
                (pltpu.make_async_copy(wd_hbm.at[ex, lo], wdf.at[s, lo], wsem.at[s]), 0),
                (pltpu.make_async_copy(wd_hbm.at[ex, hi], wdf.at[s, hi], wsem.at[s]), 1))

    @pl.when(e == 0)
    def _():
        for g0 in range(EXPERT_X_RING - 2):
            @pl.when(g0 < nb)
            def _():
                x_copy(g0, g0).start(priority=BLOCK_DMA_QUEUE)

    @pl.when((e == 0) & (cnt > 0))
    def _():
        for cp, queue in w_copies(0, 0):
            cp.start(priority=queue)

    e_next = jnp.minimum(e + 1, n_e - 1)

    @pl.when((e + 1 < n_e) & (cnt_ref[e_next] > 0))
    def _():
        for cp, queue in w_copies(e_next, 1 - wslot):
            cp.start(priority=queue)

    @pl.when(cnt > 0)
    def _():
        for cp, _ in w_copies(e, wslot):
            cp.wait()
        wgb[...] = wgf[wslot].astype(BF16)
        wub[...] = wuf[wslot].astype(BF16)
        wdb[...] = wdf[wslot].astype(BF16)

    def run_blocks(g, nblk):
        s = lax.rem(g, EXPERT_X_RING)
        sy = lax.rem(g, EXPERT_Y_RING)
        for j in range(nblk):
            ahead = g + j + EXPERT_X_RING - 2

            @pl.when(ahead < nb)
            def _():
                x_copy(ahead, lax.rem(ahead, EXPERT_X_RING)).start(priority=BLOCK_DMA_QUEUE)

        for j in range(nblk):
            x_copy(g + j, s + j).wait()
        half = xbuf.shape[2]
        packed = xbuf[pl.ds(s, nblk)].reshape(nblk * MOE_BLOCK, half)
        x_lo = pltpu.bitcast(packed << 16, F32).astype(BF16)
        x_hi = pltpu.bitcast(packed & jnp.uint32(0xFFFF0000), F32).astype(BF16)
        gte = (jnp.dot(x_lo, wgb[:half, :], preferred_element_type=F32)
               + jnp.dot(x_hi, wgb[half:, :], preferred_element_type=F32))
        up = (jnp.dot(x_lo, wub[:half, :], preferred_element_type=F32)
              + jnp.dot(x_hi, wub[half:, :], preferred_element_type=F32))
        act = (gte * _sigmoid(gte) * up).astype(BF16)
        y = jnp.dot(act, wdb[...], preferred_element_type=F32)

        for j in range(nblk):
            @pl.when(g + j >= EXPERT_Y_RING)
            def _():
                y_copy(g + j - EXPERT_Y_RING, sy + j).wait()

        ybuf[pl.ds(sy, nblk)] = y.reshape(nblk, MOE_BLOCK, y.shape[1])
        for j in range(nblk):
            y_copy(g + j, sy + j).start()

    lead = jnp.where(cnt > 0, first & 1, 0)
    rest = cnt - lead

    @pl.when(lead == 1)
    def _():
        run_blocks(first, 1)

    lax.fori_loop(0, rest // 2, lambda b, c: (run_blocks(first + lead + 2 * b, 2), c)[1], 0)

    @pl.when((rest & 1) == 1)
    def _():
        run_blocks(first + cnt - 1, 1)

    @pl.when(e == pl.num_programs(0) - 1)
    def _():
        for back in range(1, EXPERT_Y_RING + 1):
            @pl.when(nb >= back)
            def _():
                y_copy(nb - back, lax.rem(nb - back, EXPERT_Y_RING)).wait()

        ybuf[0] = jnp.zeros(ybuf.shape[1:], F32)
        lax.fori_loop(nb, n_blocks, lambda g, c: (y_copy(g, 0).start(), c)[1], 0)
        lax.fori_loop(nb, n_blocks, lambda g, c: (y_copy(g, 0).wait(), c)[1], 0)


def experts(xs, w_gate, w_up, w_down, first_blk, n_blk, n_used):
    n_rows, d_packed = xs.shape
    n_experts, d, de = w_gate.shape
    grid_spec = pltpu.PrefetchScalarGridSpec(
        num_scalar_prefetch=3,
        grid=(n_experts,),
        in_specs=[pl.BlockSpec(memory_space=pl.ANY)] * 4,
        out_specs=pl.BlockSpec(memory_space=pl.ANY),
        scratch_shapes=[pltpu.VMEM((EXPERT_X_RING, MOE_BLOCK, d_packed), xs.dtype),
                        pltpu.VMEM((EXPERT_Y_RING, MOE_BLOCK, d), F32),
                        pltpu.VMEM((2, d, de), F32), pltpu.VMEM((2, d, de), F32), pltpu.VMEM((2, de, d), F32),
                        pltpu.VMEM((d, de), BF16), pltpu.VMEM((d, de), BF16), pltpu.VMEM((de, d), BF16),
                        pltpu.SemaphoreType.DMA((EXPERT_X_RING,)), pltpu.SemaphoreType.DMA((EXPERT_Y_RING,)),
                        pltpu.SemaphoreType.DMA((2,))],
    )
    return pl.pallas_call(
        _experts_kernel,
        grid_spec=grid_spec,
        out_shape=jax.ShapeDtypeStruct((n_rows, d), F32),
        compiler_params=_cparams(("arbitrary",)),
        name="moe_experts",
    )(first_blk, n_blk, n_used, xs, w_gate, w_up, w_down)


def _moe_combine_kernel(dest_ref, destn_ref, x_ref, info_ref, g_ref, ys_hbm, o_ref, ybuf, sem, *, norm):
    i = pl.program_id(0)
    n = pl.num_programs(0)
    tm = x_ref.shape[0]
    slot = lax.rem(i, 2)

    def gather_start(dref, s):
        for k in range(2):
            for r in range(tm):
                pltpu.make_async_copy(ys_hbm.at[pl.ds(dref[k, r], 1)], ybuf.at[s, k, pl.ds(r, 1)],
                                      sem.at[s]).start(priority=r % 2)

    @pl.when(i == 0)
    def _():
        gather_start(dest_ref, 0)

    @pl.when(i + 1 < n)
    def _():
        gather_start(destn_ref, 1 - slot)

    for k in range(2):
        pltpu.make_async_copy(ys_hbm.at[pl.ds(0, tm)], ybuf.at[slot, k], sem.at[slot]).wait()
    w1 = info_ref[:, 4:5]
    w2 = info_ref[:, 5:6]
    s = x_ref[...] + (w1 * ybuf[slot, 0] + w2 * ybuf[slot, 1])
    o_ref[...] = _rms(s, g_ref[...]) if norm else s


def moe_combine(x, ys, dest, info, g, *, norm, tm):
    t, d = x.shape
    n = t // tm
    return pl.pallas_call(
        functools.partial(_moe_combine_kernel, norm=norm),
        grid=(n,),
        in_specs=[pl.BlockSpec((2, tm), lambda i: (0, i), memory_space=pltpu.SMEM),
                  pl.BlockSpec((2, tm), lambda i: (0, jnp.minimum(i + 1, n - 1)), memory_space=pltpu.SMEM),
                  pl.BlockSpec((tm, d), lambda i: (i, 0)),
                  pl.BlockSpec((tm, LANES), lambda i: (i, 0)),
                  pl.BlockSpec((1, d), lambda i: (0, 0)),
                  pl.BlockSpec(memory_space=pl.ANY)],
        out_specs=pl.BlockSpec((tm, d), lambda i: (i, 0)),
        out_shape=jax.ShapeDtypeStruct((t, d), F32),
        scratch_shapes=[pltpu.VMEM((2, 2, tm, d), F32), pltpu.SemaphoreType.DMA((2,))],
        compiler_params=_cparams(("arbitrary",)),
        name="moe_combine",
    )(dest, dest, x, info, g.reshape(1, d), ys)


def forward(x, mem, norm_mix_g, w_in, s5_lambda_re, s5_lambda_im, s5_log_dt, s5_b_re, s5_b_im, s5_c_re,
            s5_c_im, s5_d, s5_w_glu, conv_dw_w, conv_dw_b, conv_ln_g, conv_ln_b, conv_w_out, w_out,
            norm_xattn_g, norm_mem_g, xattn_wq, xattn_wk, xattn_wv, xattn_wo, norm_moe_g, router_w_group,
            router_b_group, router_w_expert, router_b_expert, exp_w_gate, exp_w_up, exp_w_down, norm_final_g,
            *, tiles):
    batch, seq, d = x.shape
    depth = w_in.shape[0]
    n_mem = mem.shape[1]
    s5_width = s5_d.shape[1]
    conv_width = conv_dw_b.shape[1]
    n_groups, epg = router_w_expert.shape[1], router_w_expert.shape[3]
    t = batch * seq
    xf = x.reshape(t, d)
    memf = mem.reshape(batch * n_mem, d)
    gate_col0 = s5_width + 2 * conv_width
    for l in range(depth):
        w_in_b = w_in[l].astype(BF16)
        proj = norm_mm_rows(xf, norm_mix_g[l], w_in_b, tm=tiles["proj_tm"], tn=tiles["proj_tn"],
                            out_dtype=F32, n_cols=gate_col0)
        lbr, lbi, bbr, bbi = s5_discretise(s5_lambda_re[l], s5_lambda_im[l], s5_log_dt[l],
                                           s5_b_re[l], s5_b_im[l])
        bre, bim, cre, cim = s5_block_diag(bbr, bbi, s5_c_re[l], s5_c_im[l])
        y_a = s5_branch(proj, bre, bim, cre, cim, lbr.reshape(1, -1), lbi.reshape(1, -1), s5_d[l],
                        s5_w_glu[l].astype(BF16), batch=batch, seq=seq, width=s5_width, m=tiles["s5_m"],
                        cw=tiles["s5_cw"], tn=tiles["glu_tn"])
        y_b = conv_branch(proj, conv_dw_w[l], conv_dw_b[l], conv_ln_g[l], conv_ln_b[l],
                          conv_w_out[l].astype(BF16), batch=batch, seq=seq, width=conv_width,
                          col0=s5_width, tm=tiles["conv_tm"])
        xf = combine(xf, norm_mix_g[l], w_in_b, y_a, y_b, w_out[l].astype(BF16), gate_col0=gate_col0,
                     tm=tiles["comb_tm"])

        wkv = jnp.concatenate([xattn_wk[l], xattn_wv[l]], axis=1).astype(BF16)
        kv = norm_mm(memf, norm_mem_g[l], wkv, tm=batch * n_mem, tn=tiles["kv_tn"], out_dtype=BF16)
        w_r = jnp.concatenate([router_w_group[l],
                               jnp.transpose(router_w_expert[l], (1, 0, 2)).reshape(d, n_groups * epg)], axis=1)
        b_r = jnp.concatenate([router_b_group[l], router_b_expert[l].reshape(-1)])
        n_r = w_r.shape[1]
        w_pad = jnp.pad(w_r, ((0, 0), (0, LANES - n_r)))
        w_hi = w_pad.astype(BF16)
        w_pad = jnp.concatenate([w_hi, (w_pad - w_hi.astype(F32)).astype(BF16)], axis=1)
        b_pad = jnp.pad(b_r, (0, LANES - n_r)).reshape(1, LANES)
        xf, h, info, infot, counts = xattn_router(
            xf, norm_xattn_g[l], xattn_wq[l].astype(BF16), kv, xattn_wo[l].astype(BF16), norm_moe_g[l],
            w_pad, b_pad, batch=batch, seq=seq,
            n_mem=n_mem, heads=XATTN_HEADS, n_groups=n_groups, epg=epg, tm=tiles["xattn_tm"])
        dest, n_rows, n_used, pcounts, pends = moe_dispatch_plan(infot, counts, t, n_groups * epg)
        xs = moe_dispatch(h, dest, pcounts, pends, n_rows=n_rows, tm=tiles["dispatch_tm"])
        ys = experts(xs, exp_w_gate[l], exp_w_up[l], exp_w_down[l], (pends - pcounts) // MOE_BLOCK,
                     pcounts // MOE_BLOCK, n_used)
        last = l + 1 == depth
        xf = moe_combine(xf, ys, dest, info, norm_final_g if last else jnp.ones((d,), F32), norm=last,
                         tm=tiles["moe_tm"])
    return xf.reshape(batch, seq, d)


def kernel(x, mem, norm_mix_g, w_in, s5_lambda_re, s5_lambda_im, s5_log_dt, s5_b_re, s5_b_im, s5_c_re, s5_c_im, s5_d, s5_w_glu, conv_dw_w, conv_dw_b, conv_ln_g, conv_ln_b, conv_w_out, w_out, norm_xattn_g, norm_mem_g, xattn_wq, xattn_wk, xattn_wv, xattn_wo, norm_moe_g, router_w_group, router_b_group, router_w_expert, router_b_expert, exp_w_gate, exp_w_up, exp_w_down, norm_final_g):
    return forward(x, mem, norm_mix_g, w_in, s5_lambda_re, s5_lambda_im, s5_log_dt, s5_b_re, s5_b_im,
                   s5_c_re, s5_c_im, s5_d, s5_w_glu, conv_dw_w, conv_dw_b, conv_ln_g, conv_ln_b, conv_w_out,
                   w_out, norm_xattn_g, norm_mem_g, xattn_wq, xattn_wk, xattn_wv, xattn_wo, norm_moe_g,
                   router_w_group, router_b_group, router_w_expert, router_b_expert, exp_w_gate, exp_w_up,
                   exp_w_down, norm_final_g, tiles=TILES)
```

```python
import functools
import math

import jax
import jax.numpy as jnp
from jax import lax
from jax.experimental import pallas as pl
from jax.experimental.pallas import tpu as pltpu

F32 = jnp.float32
BF16 = jnp.bfloat16

RMS_EPS = 1e-6
LN_EPS = 1e-5

S5_SLAB_CH = 256
CONV_HALO = 32
CONV_ROW_CHUNK = 64
XATTN_HEADS = 4
MOE_BLOCK = 128
BLOCK_DMA_QUEUE = 1
EXPERT_X_RING = 8
EXPERT_Y_RING = 4
SUBLANES = 8
LANES = 128
VMEM_LIMIT = 56 * 1024 * 1024

TILES = dict(proj_tm=512, proj_tn=512, s5_m=64, s5_cw=512, glu_tn=256, conv_tm=512,
             comb_tm=256, kv_tn=512, xattn_tm=512, dispatch_tm=1024, moe_tm=256)


def _cparams(sem):
    return pltpu.CompilerParams(dimension_semantics=sem, vmem_limit_bytes=VMEM_LIMIT)


def _rms(x, g):
    return x * lax.rsqrt(jnp.mean(x * x, axis=-1, keepdims=True) + RMS_EPS) * g


def _sigmoid(x):
    return 0.5 * jnp.tanh(0.5 * x) + 0.5


def _gelu_tanh(x):
    c = math.sqrt(2.0 / math.pi)
    return 0.5 * x * (1.0 + jnp.tanh(c * (x + 0.044715 * (x * x * x))))


def _norm_mm_kernel(x_ref, g_ref, w_ref, o_ref, h_ref):
    @pl.when(pl.program_id(1) == 0)
    def _():
        h_ref[...] = _rms(x_ref[...], g_ref[...]).astype(BF16)

    o_ref[...] = jnp.dot(h_ref[...], w_ref[...], preferred_element_type=F32).astype(o_ref.dtype)


def norm_mm(x, g, w, *, tm, tn, out_dtype):
    m, k = x.shape
    n = w.shape[1]
    return pl.pallas_call(
        _norm_mm_kernel,
        grid=(m // tm, n // tn),
        in_specs=[pl.BlockSpec((tm, k), lambda i, j: (i, 0)),
                  pl.BlockSpec((1, k), lambda i, j: (0, 0)),
                  pl.BlockSpec((k, tn), lambda i, j: (0, j))],
        out_specs=pl.BlockSpec((tm, tn), lambda i, j: (i, j)),
        out_shape=jax.ShapeDtypeStruct((m, n), out_dtype),
        scratch_shapes=[pltpu.VMEM((tm, k), BF16)],
        compiler_params=_cparams(("parallel", "arbitrary")),
        name="norm_mm",
    )(x, g.reshape(1, k), w)


def _norm_mm_rows_kernel(x_ref, g_ref, w_ref, o_ref, *, tn):
    h = _rms(x_ref[...], g_ref[...]).astype(BF16)
    for j in range(o_ref.shape[1] // tn):
        cols = slice(j * tn, (j + 1) * tn)
        o_ref[:, cols] = jnp.dot(h, w_ref[:, cols], preferred_element_type=F32).astype(o_ref.dtype)


def norm_mm_rows(x, g, w, *, tm, tn, out_dtype, n_cols=None):
    m, k = x.shape
    n = w.shape[1] if n_cols is None else n_cols
    return pl.pallas_call(
        functools.partial(_norm_mm_rows_kernel, tn=tn),
        grid=(m // tm,),
        in_specs=[pl.BlockSpec((tm, k), lambda i: (i, 0)),
                  pl.BlockSpec((1, k), lambda i: (0, 0)),
                  pl.BlockSpec((k, n), lambda i: (0, 0), pipeline_mode=pl.Buffered(1))],
        out_specs=pl.BlockSpec((tm, n), lambda i: (i, 0)),
        out_shape=jax.ShapeDtypeStruct((m, n), out_dtype),
        compiler_params=_cparams(("parallel",)),
        name="norm_mm_rows",
    )(x, g.reshape(1, k), w)


def _s5_disc_kernel(lr_ref, li_ref, ldt_ref, br_ref, bi_ref, lbr_ref, lbi_ref, bbr_ref, bbi_ref):
    lr, li = lr_ref[...], li_ref[...]
    dt = jnp.exp(ldt_ref[...])
    mag = jnp.exp(lr * dt)
    ang = li * dt
    lb_re, lb_im = mag * jnp.cos(ang), mag * jnp.sin(ang)
    den = lr * lr + li * li
    nr, ni = lb_re - 1.0, lb_im
    coef_re = (nr * lr + ni * li) / den
    coef_im = (ni * lr - nr * li) / den
    lbr_ref[...] = lb_re
    lbi_ref[...] = lb_im
    br, bi = br_ref[...], bi_ref[...]
    bbr_ref[...] = coef_re[None] * br - coef_im[None] * bi
    bbi_ref[...] = coef_re[None] * bi + coef_im[None] * br


def s5_discretise(lam_re, lam_im, log_dt, b_re, b_im):
    g, n = lam_re.shape
    c = b_re.shape[-1]
    b_re_t = jnp.transpose(b_re, (2, 0, 1))
    b_im_t = jnp.transpose(b_im, (2, 0, 1))
    gn = jax.ShapeDtypeStruct((g, n), F32)
    cgn = jax.ShapeDtypeStruct((c, g, n), F32)
    return pl.pallas_call(
        _s5_disc_kernel,
        out_shape=(gn, gn, cgn, cgn),
        name="s5_discretise",
    )(lam_re, lam_im, log_dt.reshape(g, 1), b_re_t, b_im_t)


def s5_block_diag(bb_re, bb_im, c_re, c_im):
    c, g, n = bb_re.shape
    gs = S5_SLAB_CH // c
    n_slab = g // gs
    row_grp_b = jnp.arange(gs * c) // c
    col_grp_b = jnp.arange(gs * n) // n
    tile_n = (jnp.arange(n)[:, None] == jnp.arange(gs * n)[None, :] % n).astype(F32)
    tile_c = (jnp.arange(c)[:, None] == jnp.arange(gs * c)[None, :] % c).astype(F32)
    exact = lax.Precision.HIGHEST

    def bmat(bb):
        b = jnp.transpose(bb, (1, 0, 2)).reshape(n_slab, gs * c, n)
        full = jnp.einsum('srn,nq->srq', b, tile_n, precision=exact)
        return jnp.where(row_grp_b[:, None] == col_grp_b[None, :], full, 0.0).astype(BF16)

    def cmat(cc):
        cm = jnp.transpose(cc.reshape(n_slab, gs, c, n), (0, 1, 3, 2)).reshape(n_slab, gs * n, c)
        full = jnp.einsum('spc,cr->spr', cm, tile_c, precision=exact)
        return jnp.where(col_grp_b[:, None] == row_grp_b[None, :], full, 0.0).astype(BF16)

    return bmat(bb_re), bmat(bb_im), cmat(c_re), cmat(c_im)


def _s5_kernel(proj_hbm, bre_ref, bim_ref, cre_ref, cim_ref, lre_ref, lim_ref, d_ref, wglu_ref, ya_hbm,
               ubuf, yabuf, zbuf, xre_ref, xim_ref, pre_ref, pim_ref, car_re_ref, car_im_ref, isem, osem,
               *, m, cw, tn):
    t = pl.program_id(1)
    n_t = pl.num_programs(1)
    step_id = pl.program_id(0) * n_t + t
    n_steps = pl.num_programs(0) * n_t
    slot = lax.rem(step_id, 2)
    rows = SUBLANES * m
    width = d_ref.shape[1]
    n_state = lre_ref.shape[1]
    n_slab = bre_ref.shape[0]
    slab_states = n_state // n_slab

    def in_copy(step, s, k):
        r0 = pl.multiple_of(step * rows + k * m, SUBLANES)
        return pltpu.make_async_copy(proj_hbm.at[pl.ds(r0, m), pl.ds(0, width)], ubuf.at[s, :, k, :], isem.at[s])

    def out_copy(step, s, k):
        r0 = pl.multiple_of(step * rows + k * m, SUBLANES)
        return pltpu.make_async_copy(yabuf.at[s, :, k, :], ya_hbm.at[pl.ds(r0, m)], osem.at[s])

    @pl.when(step_id == 0)
    def _():
        for k in range(SUBLANES):
            in_copy(0, 0, k).start()

    @pl.when(step_id + 1 < n_steps)
    def _():
        for k in range(SUBLANES):
            in_copy(step_id + 1, 1 - slot, k).start()

    @pl.when(t == 0)
    def _():
        car_re_ref[...] = jnp.zeros_like(car_re_ref)
        car_im_ref[...] = jnp.zeros_like(car_im_ref)
        lr, li = lre_ref[...], lim_ref[...]

        def pw(_, carry):
            pr, pi = carry
            return pr * lr - pi * li, pr * li + pi * lr

        pr, pi = lax.fori_loop(1, m, pw, (lr, li))
        pre_ref[...] = pr
        pim_ref[...] = pi

    @pl.when(step_id == 0)
    def _():
        zbuf[...] = jnp.zeros_like(zbuf)

    @pl.when(step_id >= 3)
    def _():
        for k in range(SUBLANES):
            out_copy(step_id - 3, 1 - slot, k).wait()

    nh = wglu_ref.shape[1] // 2

    def glu_pieces(z, s):
        def piece(j):
            val = jnp.dot(z, wglu_ref[:, j * tn:(j + 1) * tn], preferred_element_type=F32)
            gt = jnp.dot(z, wglu_ref[:, nh + j * tn:nh + (j + 1) * tn], preferred_element_type=F32)
            yabuf[s, :, :, j * tn:(j + 1) * tn] = (val * _sigmoid(gt)).reshape(m, SUBLANES, tn)
        return [functools.partial(piece, j) for j in range(nh // tn)]

    for k in range(SUBLANES):
        in_copy(step_id, slot, k).wait()
    u = ubuf[slot].reshape(rows, width)
    up = u.astype(BF16)

    def b_project(s):
        us = up[:, s * S5_SLAB_CH:(s + 1) * S5_SLAB_CH]
        cols = slice(s * slab_states, (s + 1) * slab_states)
        xre_ref[:, cols] = jnp.dot(us, bre_ref[s], preferred_element_type=F32)
        xim_ref[:, cols] = jnp.dot(us, bim_ref[s], preferred_element_type=F32)

    def c_project(s):
        cols = slice(s * slab_states, (s + 1) * slab_states)
        return (jnp.dot(xre_ref[:, cols].astype(BF16), cre_ref[s], preferred_element_type=F32)
                - jnp.dot(xim_ref[:, cols].astype(BF16), cim_ref[s], preferred_element_type=F32))

    n_cb = n_state // cw
    cb_per_slab = slab_states // cw
    glu_prev = glu_pieces(zbuf[...], 1 - slot)
    glu_at = {(i * n_cb) // len(glu_prev): [] for i in range(len(glu_prev))}
    for i, piece in enumerate(glu_prev):
        glu_at[(i * n_cb) // len(glu_prev)].append(piece)
    ys = []
    b_project(0)
    row_id = lax.broadcasted_iota(jnp.int32, (SUBLANES, cw), 0)
    for cb in range(n_cb):
        cols = slice(cb * cw, (cb + 1) * cw)
        slab = cb // cb_per_slab
        if cb % cb_per_slab == 0:
            if slab + 1 < n_slab:
                b_project(slab + 1)
            if slab >= 1:
                ys.append(c_project(slab - 1))
        for piece in glu_at.get(cb, []):
            piece()
        lr = jnp.broadcast_to(lre_ref[:, cols], (SUBLANES, cw))
        li = jnp.broadcast_to(lim_ref[:, cols], (SUBLANES, cw))

        def step(tau, carry, store, cols=cols, lr=lr, li=li):
            sr, si = carry
            r0 = pl.multiple_of(tau * SUBLANES, SUBLANES)
            nr = sr * lr - si * li + xre_ref[pl.ds(r0, SUBLANES), cols]
            ni = sr * li + si * lr + xim_ref[pl.ds(r0, SUBLANES), cols]
            if store:
                xre_ref[pl.ds(r0, SUBLANES), cols] = nr
                xim_ref[pl.ds(r0, SUBLANES), cols] = ni
            return nr, ni

        zero = jnp.zeros((SUBLANES, cw), F32)
        er, ei = lax.fori_loop(0, m, functools.partial(step, store=False), (zero, zero), unroll=True)

        lmr = jnp.broadcast_to(pre_ref[:, cols], (SUBLANES, cw))
        lmi = jnp.broadcast_to(pim_ref[:, cols], (SUBLANES, cw))
        ir = jnp.where(row_id == 0, jnp.broadcast_to(car_re_ref[:, cols], (SUBLANES, cw)), 0.0)
        ii = jnp.where(row_id == 0, jnp.broadcast_to(car_im_ref[:, cols], (SUBLANES, cw)), 0.0)
        for k in range(1, SUBLANES):
            nr = ir * lmr - ii * lmi + er
            ni = ir * lmi + ii * lmr + ei
            ir = jnp.where(row_id == k, pltpu.roll(nr, 1, axis=0), ir)
            ii = jnp.where(row_id == k, pltpu.roll(ni, 1, axis=0), ii)

        fr, fi = lax.fori_loop(0, m, functools.partial(step, store=True), (ir, ii), unroll=True)
        car_re_ref[:, cols] = fr[SUBLANES - 1:SUBLANES, :]
        car_im_ref[:, cols] = fi[SUBLANES - 1:SUBLANES, :]

    ys.append(c_project(n_slab - 1))
    y = jnp.concatenate(ys, axis=1) + d_ref[...] * u
    zbuf[...] = _gelu_tanh(y).astype(BF16)

    @pl.when(step_id >= 1)
    def _():
        for k in range(SUBLANES):
            out_copy(step_id - 1, 1 - slot, k).start()

    @pl.when(step_id == n_steps - 1)
    def _():
        @pl.when(step_id >= 2)
        def _():
            for k in range(SUBLANES):
                out_copy(step_id - 2, slot, k).wait()
        for piece in glu_pieces(zbuf[...], slot):
            piece()
        for k in range(SUBLANES):
            out_copy(step_id, slot, k).start()
        for k in range(SUBLANES):
            out_copy(step_id, slot, k).wait()

        @pl.when(step_id >= 1)
        def _():
            for k in range(SUBLANES):
                out_copy(step_id - 1, 1 - slot, k).wait()


def s5_branch(proj, bre, bim, cre, cim, lre, lim, d_skip, w_glu, *, batch, seq, width, m, cw, tn):
    rows = SUBLANES * m
    n_t = seq // rows
    n_state = lre.shape[1]
    nh = w_glu.shape[1] // 2
    once = pl.Buffered(1)
    const3 = lambda b, t: (0, 0, 0)
    const2 = lambda b, t: (0, 0)
    return pl.pallas_call(
        functools.partial(_s5_kernel, m=m, cw=cw, tn=tn),
        grid=(batch, n_t),
        in_specs=[pl.BlockSpec(memory_space=pl.ANY),
                  pl.BlockSpec(bre.shape, const3, pipeline_mode=once),
                  pl.BlockSpec(bim.shape, const3, pipeline_mode=once),
                  pl.BlockSpec(cre.shape, const3, pipeline_mode=once),
                  pl.BlockSpec(cim.shape, const3, pipeline_mode=once),
                  pl.BlockSpec((1, n_state), const2),
                  pl.BlockSpec((1, n_state), const2),
                  pl.BlockSpec((1, width), const2),
                  pl.BlockSpec(w_glu.shape, const2, pipeline_mode=once)],
        out_specs=pl.BlockSpec(memory_space=pl.ANY),
        out_shape=jax.ShapeDtypeStruct((batch * seq, nh), F32),
        scratch_shapes=[pltpu.VMEM((2, m, SUBLANES, width), F32), pltpu.VMEM((2, m, SUBLANES, nh), F32),
                        pltpu.VMEM((rows, width), BF16),
                        pltpu.VMEM((rows, n_state), F32), pltpu.VMEM((rows, n_state), F32),
                        pltpu.VMEM((1, n_state), F32), pltpu.VMEM((1, n_state), F32),
                        pltpu.VMEM((1, n_state), F32), pltpu.VMEM((1, n_state), F32),
                        pltpu.SemaphoreType.DMA((2,)), pltpu.SemaphoreType.DMA((2,))],
        compiler_params=_cparams(("arbitrary", "arbitrary")),
        name="s5_branch",
    )(proj, bre, bim, cre, cim, lre, lim, d_skip.reshape(1, width), w_glu)


def _conv_kernel(a_ref, b_ref, ah_ref, bh_ref, dww_ref, dwb_ref, lng_ref, lnb_ref, w_ref, o_ref,
                 zs_ref, wb_ref, yc_ref, *, tm, taps):
    i = pl.program_id(1)
    off0 = CONV_HALO - (taps - 1)
    width = a_ref.shape[1]
    n_slab = width // LANES
    zh = jnp.where(i == 0, 0.0, ah_ref[...] * _sigmoid(bh_ref[...]))
    z = a_ref[...] * _sigmoid(b_ref[...])
    for s in range(n_slab):
        zs_ref[s, 0:CONV_HALO, :] = zh[:, s * LANES:(s + 1) * LANES]
        zs_ref[s, CONV_HALO:, :] = z[:, s * LANES:(s + 1) * LANES]

    @pl.when(i == 0)
    def _():
        for j in range(taps):
            wb_ref[j] = jnp.broadcast_to(dww_ref[j:j + 1, :], (SUBLANES, width))
        wb_ref[taps] = jnp.broadcast_to(dwb_ref[...], (SUBLANES, width))

    rg = CONV_ROW_CHUNK // SUBLANES

    def chunk(c, carry):
        r0 = pl.multiple_of(c * CONV_ROW_CHUNK, CONV_ROW_CHUNK)
        for s in range(n_slab):
            lanes = slice(s * LANES, (s + 1) * LANES)
            bias = wb_ref[taps, :, lanes]
            acc = [bias] * rg
            for j in range(taps):
                w = wb_ref[j, :, lanes]
                for g in range(rg):
                    acc[g] = acc[g] + w * zs_ref[s, pl.ds(r0 + g * SUBLANES + off0 + j, SUBLANES), :]
            for g in range(rg):
                yc_ref[pl.ds(pl.multiple_of(r0 + g * SUBLANES, SUBLANES), SUBLANES), lanes] = acc[g]
        return carry

    lax.fori_loop(0, tm // CONV_ROW_CHUNK, chunk, 0)
    y = yc_ref[...]
    mu = jnp.mean(y, axis=-1, keepdims=True)
    var = jnp.mean(jnp.square(y - mu), axis=-1, keepdims=True)
    y = (y - mu) * lax.rsqrt(var + LN_EPS) * lng_ref[...] + lnb_ref[...]
    y = y * _sigmoid(y)
    o_ref[...] = jnp.dot(y.astype(BF16), w_ref[...], preferred_element_type=F32)


def conv_branch(proj, dw_w, dw_b, ln_g, ln_b, w_out, *, batch, seq, width, col0, tm):
    n_t = seq // tm
    ca, cb = col0 // width, col0 // width + 1
    hb = tm // CONV_HALO
    taps = dw_w.shape[0]
    d_out = w_out.shape[1]
    const2 = lambda b, i: (0, 0)
    halo = lambda b, i: jnp.maximum((b * n_t + i) * hb - 1, 0)
    return pl.pallas_call(
        functools.partial(_conv_kernel, tm=tm, taps=taps),
        grid=(batch, n_t),
        in_specs=[pl.BlockSpec((tm, width), lambda b, i: (b * n_t + i, ca)),
                  pl.BlockSpec((tm, width), lambda b, i: (b * n_t + i, cb)),
                  pl.BlockSpec((CONV_HALO, width), lambda b, i: (halo(b, i), ca)),
                  pl.BlockSpec((CONV_HALO, width), lambda b, i: (halo(b, i), cb)),
                  pl.BlockSpec((taps, width), const2),
                  pl.BlockSpec((1, width), const2),
                  pl.BlockSpec((1, width), const2),
                  pl.BlockSpec((1, width), const2),
                  pl.BlockSpec((width, d_out), const2)],
        out_specs=pl.BlockSpec((tm, d_out), lambda b, i: (b * n_t + i, 0)),
        out_shape=jax.ShapeDtypeStruct((batch * seq, d_out), F32),
        scratch_shapes=[pltpu.VMEM((width // LANES, tm + CONV_HALO, LANES), F32),
                        pltpu.VMEM((taps + 1, SUBLANES, width), F32),
                        pltpu.VMEM((tm, width), F32)],
        compiler_params=_cparams(("parallel", "arbitrary")),
        name="conv_branch",
    )(proj, proj, proj, proj, dw_w, dw_b.reshape(1, width), ln_g.reshape(1, width),
      ln_b.reshape(1, width), w_out)


def _combine_kernel(*refs, parts):
    x_ref, g_ref = refs[0], refs[1]
    ya_ref, yb_ref, wo_ref, o_ref = refs[2 + 2 * parts], refs[3 + 2 * parts], refs[-2], refs[-1]
    gw = ya_ref.shape[1] // parts
    x = x_ref[...]
    h = _rms(x, g_ref[...]).astype(BF16)
    o_ref[...] = x
    for p in range(parts):
        cols = slice(p * gw, (p + 1) * gw)
        ga = _sigmoid(jnp.dot(h, refs[2 + p][...], preferred_element_type=F32))
        gb = _sigmoid(jnp.dot(h, refs[2 + parts + p][...], preferred_element_type=F32))
        mix = ga * ya_ref[:, cols] + gb * yb_ref[:, cols]
        o_ref[...] += jnp.dot(mix.astype(BF16), wo_ref[cols, :], preferred_element_type=F32)


def combine(x, g, w_in, y_a, y_b, w_out, *, gate_col0, tm):
    m, d = x.shape
    gw = math.gcd(gate_col0, d)
    parts = d // gw
    row = lambda i: (i, 0)
    once = pl.Buffered(1)
    gate_specs = [pl.BlockSpec((d, gw), functools.partial(lambda i, c: (0, c), c=(gate_col0 + s * d) // gw + p),
                               pipeline_mode=once) for s in range(2) for p in range(parts)]
    return pl.pallas_call(
        functools.partial(_combine_kernel, parts=parts),
        grid=(m // tm,),
        in_specs=[pl.BlockSpec((tm, d), row), pl.BlockSpec((1, d), lambda i: (0, 0))] + gate_specs +
                 [pl.BlockSpec((tm, d), row),
                  pl.BlockSpec((tm, d), row),
                  pl.BlockSpec((d, d), lambda i: (0, 0), pipeline_mode=once)],
        out_specs=pl.BlockSpec((tm, d), row),
        out_shape=jax.ShapeDtypeStruct((m, d), F32),
        compiler_params=_cparams(("parallel",)),
        name="gated_combine",
    )(x, g.reshape(1, d), *([w_in] * (2 * parts)), y_a, y_b, w_out)


def _route_tile(x, g_ref, w_ref, b_ref, tri_ref, h_ref, info_ref, infot_ref, cnt_ref, run_ref, *,
                n_groups, epg):
    tm = x.shape[0]
    h = _rms(x, g_ref[...])
    half = h.shape[1] // 2
    bits = pltpu.bitcast(h.astype(BF16).astype(F32), jnp.uint32)
    packed = (bits[:, :half] >> 16) | (bits[:, half:] & jnp.uint32(0xFFFF0000))
    tr = half // LANES
    for s in range(tr):
        h_ref[pl.ds(s, tm, stride=tr), :] = packed[:, s * LANES:(s + 1) * LANES]
    h_hi = h.astype(BF16)
    h_lo = (h - h_hi.astype(F32)).astype(BF16)
    p = jnp.dot(h_hi, w_ref[...], preferred_element_type=F32)
    logits = (p[:, :LANES] + p[:, LANES:] + jnp.dot(h_lo, w_ref[:, :LANES], preferred_element_type=F32)
              + b_ref[...])
    lane = lax.broadcasted_iota(jnp.int32, (tm, LANES), 1).astype(F32)
    neg = jnp.float32(-jnp.inf)
    big = jnp.float32(LANES)
    is_grp = lane < n_groups
    cl = jnp.where(is_grp, logits, neg)
    cmax = jnp.max(cl, axis=-1, keepdims=True)
    g_idx = jnp.min(jnp.where(cl == cmax, lane, big), axis=-1, keepdims=True)
    p_sel = 1.0 / jnp.sum(jnp.where(is_grp, jnp.exp(cl - cmax), 0.0), axis=-1, keepdims=True)
    lo = n_groups + g_idx * epg
    in_grp = (lane >= lo) & (lane < lo + epg)
    fl = jnp.where(in_grp, logits, neg)
    v1 = jnp.max(fl, axis=-1, keepdims=True)
    i1 = jnp.min(jnp.where(fl == v1, lane, big), axis=-1, keepdims=True)
    fl2 = jnp.where(lane == i1, neg, fl)
    v2 = jnp.max(fl2, axis=-1, keepdims=True)
    i2 = jnp.min(jnp.where(fl2 == v2, lane, big), axis=-1, keepdims=True)
    e2x = jnp.exp(v2 - v1)
    w1 = p_sel / (1.0 + e2x)
    w2 = p_sel * e2x / (1.0 + e2x)
    e1 = i1 - n_groups
    e2 = i2 - n_groups
    oh1 = lane == e1
    oh2 = lane == e2
    both = jnp.where(oh1 | oh2, 1.0, 0.0)
    before = jnp.dot(tri_ref[...], both.astype(BF16), preferred_element_type=F32) + run_ref[...]
    r1 = jnp.sum(jnp.where(oh1, before, 0.0), axis=-1, keepdims=True)
    r2 = jnp.sum(jnp.where(oh2, before, 0.0), axis=-1, keepdims=True)
    run_ref[...] += jnp.sum(both, axis=0, keepdims=True)
    cnt_ref[...] = run_ref[...]
    info = jnp.where(lane == 0, e1, 0.0)
    info = jnp.where(lane == 1, e2, info)
    info = jnp.where(lane == 2, r1, info)
    info = jnp.where(lane == 3, r2, info)
    info = jnp.where(lane == 4, w1, info)
    info = jnp.where(lane == 5, w2, info)
    info_ref[...] = info
    infot_ref[...] = jnp.transpose(info)[0:SUBLANES, :]


def _xattn_router_kernel(x_ref, gx_ref, wq_ref, kv_ref, wo_ref, g_ref, w_ref, b_ref, tri_ref,
                         o_ref, h_ref, info_ref, infot_ref, cnt_ref, run_ref, *, heads, scale, n_groups, epg):
    @pl.when((pl.program_id(0) == 0) & (pl.program_id(1) == 0))
    def _():
        run_ref[...] = jnp.zeros_like(run_ref)

    d = x_ref.shape[1]
    hd = d // heads
    acc = x_ref[...]
    hq = _rms(acc, gx_ref[...]).astype(BF16)

    def q_head(h):
        return jnp.dot(hq, wq_ref[:, h * hd:(h + 1) * hd], preferred_element_type=F32).astype(BF16)

    q_next = q_head(0)
    for h in range(heads):
        q = q_next
        k = kv_ref[:, h * hd:(h + 1) * hd]
        v = kv_ref[:, d + h * hd:d + (h + 1) * hd]
        s = lax.dot_general(q, k, (((1,), (1,)), ((), ())), preferred_element_type=F32) * scale
        if h + 1 < heads:
            q_next = q_head(h + 1)
        s = s - jnp.max(s, axis=-1, keepdims=True)
        p = jnp.exp(s)
        p = p / jnp.sum(p, axis=-1, keepdims=True)
        o = jnp.dot(p.astype(BF16), v, preferred_element_type=F32)
        acc = acc + jnp.dot(o.astype(BF16), wo_ref[h * hd:(h + 1) * hd, :], preferred_element_type=F32)
    o_ref[...] = acc
    _route_tile(acc, g_ref, w_ref, b_ref, tri_ref, h_ref, info_ref, infot_ref, cnt_ref, run_ref,
                n_groups=n_groups, epg=epg)


def xattn_router(x, gx, wq, kv, wo, g, w_pad, b_pad, *, batch, seq, n_mem, heads, n_groups, epg, tm):
    d = x.shape[1]
    m = batch * seq
    n_t = seq // tm
    scale = 1.0 / math.sqrt(d // heads)
    tri = (jnp.arange(tm)[:, None] > jnp.arange(tm)[None, :]).astype(BF16)
    row = lambda b, i: (b * n_t + i, 0)
    const = lambda b, i: (0, 0)
    once = pl.Buffered(1)
    return pl.pallas_call(
        functools.partial(_xattn_router_kernel, heads=heads, scale=scale, n_groups=n_groups, epg=epg),
        grid=(batch, n_t),
        in_specs=[pl.BlockSpec((tm, d), row),
                  pl.BlockSpec((1, d), const),
                  pl.BlockSpec((d, d), const, pipeline_mode=once),
                  pl.BlockSpec((n_mem, 2 * d), lambda b, i: (b, 0)),
                  pl.BlockSpec((d, d), const, pipeline_mode=once),
                  pl.BlockSpec((1, d), const),
                  pl.BlockSpec((d, 2 * LANES), const, pipeline_mode=once),
                  pl.BlockSpec((1, LANES), const),
                  pl.BlockSpec((tm, tm), const, pipeline_mode=once)],
        out_specs=[pl.BlockSpec((tm, d), row),
                   pl.BlockSpec((tm * (d // 2 // LANES), LANES), row),
                   pl.BlockSpec((tm, LANES), row),
                   pl.BlockSpec((SUBLANES, tm), lambda b, i: (0, b * n_t + i)),
                   pl.BlockSpec((1, LANES), const)],
        out_shape=(jax.ShapeDtypeStruct((m, d), F32),
                   jax.ShapeDtypeStruct((m * (d // 2 // LANES), LANES), jnp.uint32),
                   jax.ShapeDtypeStruct((m, LANES), F32),
                   jax.ShapeDtypeStruct((SUBLANES, m), F32),
                   jax.ShapeDtypeStruct((1, LANES), F32)),
        scratch_shapes=[pltpu.VMEM((1, LANES), F32)],
        compiler_params=_cparams(("arbitrary", "arbitrary")),
        name="xattn_router",
    )(x, gx.reshape(1, d), wq, kv, wo, g.reshape(1, d), w_pad, b_pad, tri)


def moe_dispatch_plan(infot, counts, n_tokens, n_experts):
    e = infot[0:2].astype(jnp.int32)
    rank = infot[2:4].astype(jnp.int32)
    counts = counts[0, :n_experts].astype(jnp.int32)
    pcounts = ((counts + MOE_BLOCK - 1) // MOE_BLOCK) * MOE_BLOCK
    pends = jnp.cumsum(pcounts)
    pstarts = pends - pcounts
    ids = jnp.arange(n_experts, dtype=jnp.int32)
    dest = rank + jnp.sum(jnp.where(e[:, :, None] == ids, pstarts, 0), axis=-1)
    n_rows = (-(-2 * n_tokens // MOE_BLOCK) + n_experts) * MOE_BLOCK
    n_used = (pends[-1] // MOE_BLOCK).reshape(1)
    return dest, n_rows, n_used, pcounts, pends


def _dispatch_kernel(pc_ref, pe_ref, dest_ref, h_ref, xs_hbm, zbuf, sem, tsem, *, n_experts, tr):
    i = pl.program_id(0)
    tm = h_ref.shape[0] // tr
    blk = MOE_BLOCK * tr

    @pl.when(i == 0)
    def _():
        zbuf[...] = jnp.zeros_like(zbuf)
        for wait in (False, True):
            for e in range(n_experts):
                @pl.when(pc_ref[e] > 0)
                def _():
                    row0 = pl.multiple_of((pe_ref[e] - MOE_BLOCK) * tr, blk)
                    fill = pltpu.make_async_copy(zbuf, xs_hbm.at[pl.ds(row0, blk)], sem)
                    fill.wait() if wait else fill.start()

    n_blocks = xs_hbm.shape[0] // blk
    n_used = pe_ref[n_experts - 1] // MOE_BLOCK

    def tail_fill(b, wait):
        row0 = pl.multiple_of(b * blk, blk)
        fill = pltpu.make_async_copy(zbuf, xs_hbm.at[pl.ds(row0, blk)], tsem)
        fill.wait() if wait else fill.start()

    @pl.when(i == 0)
    def _():
        lax.fori_loop(n_used, n_blocks, lambda b, c: (tail_fill(b, False), c)[1], 0)

    for k in range(2):
        for r in range(tm):
            dst = pl.multiple_of(dest_ref[k, r] * tr, tr)
            pltpu.make_async_copy(h_ref.at[pl.ds(r * tr, tr)], xs_hbm.at[pl.ds(dst, tr)],
                                  sem).start(priority=r % 2)
    for k in range(2):
        pltpu.make_async_copy(h_ref, xs_hbm.at[pl.ds(0, tm * tr)], sem).wait()

    @pl.when(i == pl.num_programs(0) - 1)
    def _():
        lax.fori_loop(n_used, n_blocks, lambda b, c: (tail_fill(b, True), c)[1], 0)


def moe_dispatch(h, dest, pcounts, pends, *, n_rows, tm):
    t = dest.shape[1]
    tr = h.shape[0] // t
    grid_spec = pltpu.PrefetchScalarGridSpec(
        num_scalar_prefetch=2,
        grid=(t // tm,),
        in_specs=[pl.BlockSpec((2, tm), lambda i, pc, pe: (0, i), memory_space=pltpu.SMEM),
                  pl.BlockSpec((tm * tr, LANES), lambda i, pc, pe: (i, 0))],
        out_specs=pl.BlockSpec(memory_space=pl.ANY),
        scratch_shapes=[pltpu.VMEM((MOE_BLOCK * tr, LANES), h.dtype), pltpu.SemaphoreType.DMA(()),
                        pltpu.SemaphoreType.DMA(())],
    )
    return pl.pallas_call(
        functools.partial(_dispatch_kernel, n_experts=pcounts.shape[0], tr=tr),
        grid_spec=grid_spec,
        out_shape=jax.ShapeDtypeStruct((n_rows * tr, LANES), h.dtype),
        compiler_params=_cparams(("arbitrary",)),
        name="moe_dispatch",
    )(pcounts, pends, dest, h)


def _experts_kernel(first_ref, cnt_ref, nb_ref, xs_hbm, wg_hbm, wu_hbm, wd_hbm, ys_hbm,
                    xbuf, ybuf, wgf, wuf, wdf, wgb, wub, wdb, xsem, ysem, wsem):
    e = pl.program_id(0)
    n_e = pl.num_programs(0)
    nb = nb_ref[0]
    first, cnt = first_ref[e], cnt_ref[e]
    tr = xbuf.shape[1] // MOE_BLOCK
    n_blocks = xs_hbm.shape[0] // xbuf.shape[1]
    wslot = lax.rem(e, 2)

    def rows(g):
        return pl.ds(pl.multiple_of(g * MOE_BLOCK, MOE_BLOCK), MOE_BLOCK)

    def x_copy(g, s):
        xrows = pl.ds(pl.multiple_of(g * xbuf.shape[1], xbuf.shape[1]), xbuf.shape[1])
        return pltpu.make_async_copy(xs_hbm.at[xrows], xbuf.at[s], xsem.at[s])

    def y_copy(g, s):
        return pltpu.make_async_copy(ybuf.at[s], ys_hbm.at[rows(g)], ysem.at[s])

    def w_copies(ex, s):
        half = wd_hbm.shape[1] // 2
        lo, hi = pl.ds(0, half), pl.ds(half, half)
        return ((pltpu.make_async_copy(wg_hbm.at[ex], wgf.at[s], wsem.at[s]), 0),
                (pltpu.make_async_copy(wu_hbm.at[ex], wuf.at[s], wsem.at[s]), 1),
                (pltpu.make_async_copy(wd_hbm.at[ex, lo], wdf.at[s, lo], wsem.at[s]), 0),
                (pltpu.make_async_copy(wd_hbm.at[ex, hi], wdf.at[s, hi], wsem.at[s]), 1))

    @pl.when(e == 0)
    def _():
        for g0 in range(EXPERT_X_RING - 2):
            @pl.when(g0 < nb)
            def _():
                x_copy(g0, g0).start(priority=BLOCK_DMA_QUEUE)

    @pl.when((e == 0) & (cnt > 0))
    def _():
        for cp, queue in w_copies(0, 0):
            cp.start(priority=queue)

    e_next = jnp.minimum(e + 1, n_e - 1)

    @pl.when((e + 1 < n_e) & (cnt_ref[e_next] > 0))
    def _():
        for cp, queue in w_copies(e_next, 1 - wslot):
            cp.start(priority=queue)

    @pl.when(cnt > 0)
    def _():
        for cp, _ in w_copies(e, wslot):
            cp.wait()
        wgb[...] = wgf[wslot].astype(BF16)
        wub[...] = wuf[wslot].astype(BF16)
        wdb[...] = wdf[wslot].astype(BF16)

    def run_blocks(g, nblk):
        s = lax.rem(g, EXPERT_X_RING)
        sy = lax.rem(g, EXPERT_Y_RING)
        for j in range(nblk):
            ahead = g + j + EXPERT_X_RING - 2

            @pl.when(ahead < nb)
            def _():
                x_copy(ahead, lax.rem(ahead, EXPERT_X_RING)).start(priority=BLOCK_DMA_QUEUE)

        for j in range(nblk):
            x_copy(g + j, s + j).wait()
        half = tr * LANES
        packed = jnp.concatenate(
            [jnp.concatenate([xbuf[s + j, pl.ds(w, MOE_BLOCK, stride=tr), :] for w in range(tr)], axis=1)
             for j in range(nblk)], axis=0)
        x_lo = pltpu.bitcast(packed << 16, F32).astype(BF16)
        x_hi = pltpu.bitcast(packed & jnp.uint32(0xFFFF0000), F32).astype(BF16)
        gte = (jnp.dot(x_lo, wgb[:half, :], preferred_element_type=F32)
               + jnp.dot(x_hi, wgb[half:, :], preferred_element_type=F32))
        up = (jnp.dot(x_lo, wub[:half, :], preferred_element_type=F32)
              + jnp.dot(x_hi, wub[half:, :], preferred_element_type=F32))
        act = (gte * _sigmoid(gte) * up).astype(BF16)
        y = jnp.dot(act, wdb[...], preferred_element_type=F32)

        for j in range(nblk):
            @pl.when(g + j >= EXPERT_Y_RING)
            def _():
                y_copy(g + j - EXPERT_Y_RING, sy + j).wait()

        ybuf[pl.ds(sy, nblk)] = y.reshape(nblk, MOE_BLOCK, y.shape[1])
        for j in range(nblk):
            y_copy(g + j, sy + j).start()

    lead = jnp.where(cnt > 0, first & 1, 0)
    rest = cnt - lead

    @pl.when(lead == 1)
    def _():
        run_blocks(first, 1)

    lax.fori_loop(0, rest // 2, lambda b, c: (run_blocks(first + lead + 2 * b, 2), c)[1], 0)

    @pl.when((rest & 1) == 1)
    def _():
        run_blocks(first + cnt - 1, 1)

    @pl.when(e == pl.num_programs(0) - 1)
    def _():
        for back in range(1, EXPERT_Y_RING + 1):
            @pl.when(nb >= back)
            def _():
                y_copy(nb - back, lax.rem(nb - back, EXPERT_Y_RING)).wait()

        ybuf[0] = jnp.zeros(ybuf.shape[1:], F32)
        lax.fori_loop(nb, n_blocks, lambda g, c: (y_copy(g, 0).start(), c)[1], 0)
        lax.fori_loop(nb, n_blocks, lambda g, c: (y_copy(g, 0).wait(), c)[1], 0)


def experts(xs, w_gate, w_up, w_down, first_blk, n_blk, n_used):
    n_experts, d, de = w_gate.shape
    tr = d // 2 // LANES
    n_rows = xs.shape[0] // tr
    grid_spec = pltpu.PrefetchScalarGridSpec(
        num_scalar_prefetch=3,
        grid=(n_experts,),
        in_specs=[pl.BlockSpec(memory_space=pl.ANY)] * 4,
        out_specs=pl.BlockSpec(memory_space=pl.ANY),
        scratch_shapes=[pltpu.VMEM((EXPERT_X_RING, MOE_BLOCK * tr, LANES), xs.dtype),
                        pltpu.VMEM((EXPERT_Y_RING, MOE_BLOCK, d), F32),
                        pltpu.VMEM((2, d, de), F32), pltpu.VMEM((2, d, de), F32), pltpu.VMEM((2, de, d), F32),
                        pltpu.VMEM((d, de), BF16), pltpu.VMEM((d, de), BF16), pltpu.VMEM((de, d), BF16),
                        pltpu.SemaphoreType.DMA((EXPERT_X_RING,)), pltpu.SemaphoreType.DMA((EXPERT_Y_RING,)),
                        pltpu.SemaphoreType.DMA((2,))],
    )
    return pl.pallas_call(
        _experts_kernel,
        grid_spec=grid_spec,
        out_shape=jax.ShapeDtypeStruct((n_rows, d), F32),
        compiler_params=_cparams(("arbitrary",)),
        name="moe_experts",
    )(first_blk, n_blk, n_used, xs, w_gate, w_up, w_down)


def _moe_combine_kernel(dest_ref, destn_ref, x_ref, info_ref, g_ref, ys_hbm, o_ref, ybuf, sem, *, norm):
    i = pl.program_id(0)
    n = pl.num_programs(0)
    tm = x_ref.shape[0]
    slot = lax.rem(i, 2)

    def gather_start(dref, s):
        for k in range(2):
            for r in range(tm):
                pltpu.make_async_copy(ys_hbm.at[pl.ds(dref[k, r], 1)], ybuf.at[s, k, pl.ds(r, 1)],
                                      sem.at[s]).start(priority=r % 2)

    @pl.when(i == 0)
    def _():
        gather_start(dest_ref, 0)

    @pl.when(i + 1 < n)
    def _():
        gather_start(destn_ref, 1 - slot)

    for k in range(2):
        pltpu.make_async_copy(ys_hbm.at[pl.ds(0, tm)], ybuf.at[slot, k], sem.at[slot]).wait()
    w1 = info_ref[:, 4:5]
    w2 = info_ref[:, 5:6]
    s = x_ref[...] + (w1 * ybuf[slot, 0] + w2 * ybuf[slot, 1])
    o_ref[...] = _rms(s, g_ref[...]) if norm else s


def moe_combine(x, ys, dest, info, g, *, norm, tm):
    t, d = x.shape
    n = t // tm
    return pl.pallas_call(
        functools.partial(_moe_combine_kernel, norm=norm),
        grid=(n,),
        in_specs=[pl.BlockSpec((2, tm), lambda i: (0, i), memory_space=pltpu.SMEM),
                  pl.BlockSpec((2, tm), lambda i: (0, jnp.minimum(i + 1, n - 1)), memory_space=pltpu.SMEM),
                  pl.BlockSpec((tm, d), lambda i: (i, 0)),
                  pl.BlockSpec((tm, LANES), lambda i: (i, 0)),
                  pl.BlockSpec((1, d), lambda i: (0, 0)),
                  pl.BlockSpec(memory_space=pl.ANY)],
        out_specs=pl.BlockSpec((tm, d), lambda i: (i, 0)),
        out_shape=jax.ShapeDtypeStruct((t, d), F32),
        scratch_shapes=[pltpu.VMEM((2, 2, tm, d), F32), pltpu.SemaphoreType.DMA((2,))],
        compiler_params=_cparams(("arbitrary",)),
        name="moe_combine",
    )(dest, dest, x, info, g.reshape(1, d), ys)


def forward(x, mem, norm_mix_g, w_in, s5_lambda_re, s5_lambda_im, s5_log_dt, s5_b_re, s5_b_im, s5_c_re,
            s5_c_im, s5_d, s5_w_glu, conv_dw_w, conv_dw_b, conv_ln_g, conv_ln_b, conv_w_out, w_out,
            norm_xattn_g, norm_mem_g, xattn_wq, xattn_wk, xattn_wv, xattn_wo, norm_moe_g, router_w_group,
            router_b_group, router_w_expert, router_b_expert, exp_w_gate, exp_w_up, exp_w_down, norm_final_g,
            *, tiles):
    batch, seq, d = x.shape
    depth = w_in.shape[0]
    n_mem = mem.shape[1]
    s5_width = s5_d.shape[1]
    conv_width = conv_dw_b.shape[1]
    n_groups, epg = router_w_expert.shape[1], router_w_expert.shape[3]
    t = batch * seq
    xf = x.reshape(t, d)
    memf = mem.reshape(batch * n_mem, d)
    gate_col0 = s5_width + 2 * conv_width
    for l in range(depth):
        w_in_b = w_in[l].astype(BF16)
        proj = norm_mm_rows(xf, norm_mix_g[l], w_in_b, tm=tiles["proj_tm"], tn=tiles["proj_tn"],
                            out_dtype=F32, n_cols=gate_col0)
        lbr, lbi, bbr, bbi = s5_discretise(s5_lambda_re[l], s5_lambda_im[l], s5_log_dt[l],
                                           s5_b_re[l], s5_b_im[l])
        bre, bim, cre, cim = s5_block_diag(bbr, bbi, s5_c_re[l], s5_c_im[l])
        y_a = s5_branch(proj, bre, bim, cre, cim, lbr.reshape(1, -1), lbi.reshape(1, -1), s5_d[l],
                        s5_w_glu[l].astype(BF16), batch=batch, seq=seq, width=s5_width, m=tiles["s5_m"],
                        cw=tiles["s5_cw"], tn=tiles["glu_tn"])
        y_b = conv_branch(proj, conv_dw_w[l], conv_dw_b[l], conv_ln_g[l], conv_ln_b[l],
                          conv_w_out[l].astype(BF16), batch=batch, seq=seq, width=conv_width,
                          col0=s5_width, tm=tiles["conv_tm"])
        xf = combine(xf, norm_mix_g[l], w_in_b, y_a, y_b, w_out[l].astype(BF16), gate_col0=gate_col0,
                     tm=tiles["comb_tm"])

        wkv = jnp.concatenate([xattn_wk[l], xattn_wv[l]], axis=1).astype(BF16)
        kv = norm_mm(memf, norm_mem_g[l], wkv, tm=batch * n_mem, tn=tiles["kv_tn"], out_dtype=BF16)
        w_r = jnp.concatenate([router_w_group[l],
                               jnp.transpose(router_w_expert[l], (1, 0, 2)).reshape(d, n_groups * epg)], axis=1)
        b_r = jnp.concatenate([router_b_group[l], router_b_expert[l].reshape(-1)])
        n_r = w_r.shape[1]
        w_pad = jnp.pad(w_r, ((0, 0), (0, LANES - n_r)))
        w_hi = w_pad.astype(BF16)
        w_pad = jnp.concatenate([w_hi, (w_pad - w_hi.astype(F32)).astype(BF16)], axis=1)
        b_pad = jnp.pad(b_r, (0, LANES - n_r)).reshape(1, LANES)
        xf, h, info, infot, counts = xattn_router(
            xf, norm_xattn_g[l], xattn_wq[l].astype(BF16), kv, xattn_wo[l].astype(BF16), norm_moe_g[l],
            w_pad, b_pad, batch=batch, seq=seq,
            n_mem=n_mem, heads=XATTN_HEADS, n_groups=n_groups, epg=epg, tm=tiles["xattn_tm"])
        dest, n_rows, n_used, pcounts, pends = moe_dispatch_plan(infot, counts, t, n_groups * epg)
        xs = moe_dispatch(h, dest, pcounts, pends, n_rows=n_rows, tm=tiles["dispatch_tm"])
        ys = experts(xs, exp_w_gate[l], exp_w_up[l], exp_w_down[l], (pends - pcounts) // MOE_BLOCK,
                     pcounts // MOE_BLOCK, n_used)
        last = l + 1 == depth
        xf = moe_combine(xf, ys, dest, info, norm_final_g if last else jnp.ones((d,), F32), norm=last,
                         tm=tiles["moe_tm"])
    return xf.reshape(batch, seq, d)


def kernel(x, mem, norm_mix_g, w_in, s5_lambda_re, s5_lambda_im, s5_log_dt, s5_b_re, s5_b_im, s5_c_re, s5_c_im, s5_d, s5_w_glu, conv_dw_w, conv_dw_b, conv_ln_g, conv_ln_b, conv_w_out, w_out, norm_xattn_g, norm_mem_g, xattn_wq, xattn_wk, xattn_wv, xattn_wo, norm_moe_g, router_w_group, router_b_group, router_w_expert, router_b_expert, exp_w_gate, exp_w_up, exp_w_down, norm_final_g):
    return forward(x, mem, norm_mix_g, w_in, s5_lambda_re, s5_lambda_im, s5_log_dt, s5_b_re, s5_b_im,
                   s5_c_re, s5_c_im, s5_d, s5_w_glu, conv_dw_w, conv_dw_b, conv_ln_g, conv_ln_b, conv_w_out,
                   w_out, norm_xattn_g, norm_mem_g, xattn_wq, xattn_wk, xattn_wv, xattn_wo, norm_moe_g,
                   router_w_group, router_b_group, router_w_expert, router_b_expert, exp_w_gate, exp_w_up,
                   exp_w_down, norm_final_g, tiles=TILES)
```

```python
import functools
import math

import jax
import jax.numpy as jnp
from jax import lax
from jax.experimental import pallas as pl
from jax.experimental.pallas import tpu as pltpu

F32 = jnp.float32
BF16 = jnp.bfloat16

RMS_EPS = 1e-6
LN_EPS = 1e-5

S5_SLAB_CH = 256
CONV_HALO = 32
CONV_ROW_CHUNK = 64
XATTN_HEADS = 4
MOE_BLOCK = 128
BLOCK_DMA_QUEUE = 1
EXPERT_X_RING = 8
EXPERT_Y_RING = 4
SUBLANES = 8
LANES = 128
VMEM_LIMIT = 56 * 1024 * 1024

TILES = dict(proj_tm=512, proj_tn=512, s5_m=64, s5_cw=512, glu_tn=256, conv_tm=512,
             comb_tm=256, kv_tn=512, xattn_tm=512, dispatch_tm=1024, moe_tm=256)


def _cparams(sem):
    return pltpu.CompilerParams(dimension_semantics=sem, vmem_limit_bytes=VMEM_LIMIT)


def _rms(x, g):
    return x * lax.rsqrt(jnp.mean(x * x, axis=-1, keepdims=True) + RMS_EPS) * g


def _sigmoid(x):
    return 0.5 * jnp.tanh(0.5 * x) + 0.5


def _gelu_tanh(x):
    c = math.sqrt(2.0 / math.pi)
    return 0.5 * x * (1.0 + jnp.tanh(c * (x + 0.044715 * (x * x * x))))


def _norm_mm_kernel(x_ref, g_ref, w_ref, o_ref, h_ref):
    @pl.when(pl.program_id(1) == 0)
    def _():
        h_ref[...] = _rms(x_ref[...], g_ref[...]).astype(BF16)

    o_ref[...] = jnp.dot(h_ref[...], w_ref[...], preferred_element_type=F32).astype(o_ref.dtype)


def norm_mm(x, g, w, *, tm, tn, out_dtype):
    m, k = x.shape
    n = w.shape[1]
    return pl.pallas_call(
        _norm_mm_kernel,
        grid=(m // tm, n // tn),
        in_specs=[pl.BlockSpec((tm, k), lambda i, j: (i, 0)),
                  pl.BlockSpec((1, k), lambda i, j: (0, 0)),
                  pl.BlockSpec((k, tn), lambda i, j: (0, j))],
        out_specs=pl.BlockSpec((tm, tn), lambda i, j: (i, j)),
        out_shape=jax.ShapeDtypeStruct((m, n), out_dtype),
        scratch_shapes=[pltpu.VMEM((tm, k), BF16)],
        compiler_params=_cparams(("parallel", "arbitrary")),
        name="norm_mm",
    )(x, g.reshape(1, k), w)


def _norm_mm_rows_kernel(x_ref, g_ref, w_ref, o_ref, *, tn):
    h = _rms(x_ref[...], g_ref[...]).astype(BF16)
    for j in range(o_ref.shape[1] // tn):
        cols = slice(j * tn, (j + 1) * tn)
        o_ref[:, cols] = jnp.dot(h, w_ref[:, cols], preferred_element_type=F32).astype(o_ref.dtype)


def norm_mm_rows(x, g, w, *, tm, tn, out_dtype, n_cols=None):
    m, k = x.shape
    n = w.shape[1] if n_cols is None else n_cols
    return pl.pallas_call(
        functools.partial(_norm_mm_rows_kernel, tn=tn),
        grid=(m // tm,),
        in_specs=[pl.BlockSpec((tm, k), lambda i: (i, 0)),
                  pl.BlockSpec((1, k), lambda i: (0, 0)),
                  pl.BlockSpec((k, n), lambda i: (0, 0), pipeline_mode=pl.Buffered(1))],
        out_specs=pl.BlockSpec((tm, n), lambda i: (i, 0)),
        out_shape=jax.ShapeDtypeStruct((m, n), out_dtype),
        compiler_params=_cparams(("parallel",)),
        name="norm_mm_rows",
    )(x, g.reshape(1, k), w)


def _s5_disc_kernel(lr_ref, li_ref, ldt_ref, br_ref, bi_ref, lbr_ref, lbi_ref, bbr_ref, bbi_ref):
    lr, li = lr_ref[...], li_ref[...]
    dt = jnp.exp(ldt_ref[...])
    mag = jnp.exp(lr * dt)
    ang = li * dt
    lb_re, lb_im = mag * jnp.cos(ang), mag * jnp.sin(ang)
    den = lr * lr + li * li
    nr, ni = lb_re - 1.0, lb_im
    coef_re = (nr * lr + ni * li) / den
    coef_im = (ni * lr - nr * li) / den
    lbr_ref[...] = lb_re
    lbi_ref[...] = lb_im
    br, bi = br_ref[...], bi_ref[...]
    bbr_ref[...] = coef_re[None] * br - coef_im[None] * bi
    bbi_ref[...] = coef_re[None] * bi + coef_im[None] * br


def s5_discretise(lam_re, lam_im, log_dt, b_re, b_im):
    g, n = lam_re.shape
    c = b_re.shape[-1]
    b_re_t = jnp.transpose(b_re, (2, 0, 1))
    b_im_t = jnp.transpose(b_im, (2, 0, 1))
    gn = jax.ShapeDtypeStruct((g, n), F32)
    cgn = jax.ShapeDtypeStruct((c, g, n), F32)
    return pl.pallas_call(
        _s5_disc_kernel,
        out_shape=(gn, gn, cgn, cgn),
        name="s5_discretise",
    )(lam_re, lam_im, log_dt.reshape(g, 1), b_re_t, b_im_t)


def s5_block_diag(bb_re, bb_im, c_re, c_im):
    c, g, n = bb_re.shape
    gs = S5_SLAB_CH // c
    n_slab = g // gs
    row_grp_b = jnp.arange(gs * c) // c
    col_grp_b = jnp.arange(gs * n) // n
    tile_n = (jnp.arange(n)[:, None] == jnp.arange(gs * n)[None, :] % n).astype(F32)
    tile_c = (jnp.arange(c)[:, None] == jnp.arange(gs * c)[None, :] % c).astype(F32)
    exact = lax.Precision.HIGHEST

    def bmat(bb):
        b = jnp.transpose(bb, (1, 0, 2)).reshape(n_slab, gs * c, n)
        full = jnp.einsum('srn,nq->srq', b, tile_n, precision=exact)
        return jnp.where(row_grp_b[:, None] == col_grp_b[None, :], full, 0.0).astype(BF16)

    def cmat(cc):
        cm = jnp.transpose(cc.reshape(n_slab, gs, c, n), (0, 1, 3, 2)).reshape(n_slab, gs * n, c)
        full = jnp.einsum('spc,cr->spr', cm, tile_c, precision=exact)
        return jnp.where(col_grp_b[:, None] == row_grp_b[None, :], full, 0.0).astype(BF16)

    return bmat(bb_re), bmat(bb_im), cmat(c_re), cmat(c_im)


def _s5_kernel(proj_hbm, bre_ref, bim_ref, cre_ref, cim_ref, lre_ref, lim_ref, d_ref, wglu_ref, ya_hbm,
               ubuf, yabuf, zbuf, xre_ref, xim_ref, pre_ref, pim_ref, car_re_ref, car_im_ref, isem, osem,
               *, m, cw, tn):
    t = pl.program_id(1)
    n_t = pl.num_programs(1)
    step_id = pl.program_id(0) * n_t + t
    n_steps = pl.num_programs(0) * n_t
    slot = lax.rem(step_id, 2)
    rows = SUBLANES * m
    width = d_ref.shape[1]
    n_state = lre_ref.shape[1]
    n_slab = bre_ref.shape[0]
    slab_states = n_state // n_slab

    def in_copy(step, s, k):
        r0 = pl.multiple_of(step * rows + k * m, SUBLANES)
        return pltpu.make_async_copy(proj_hbm.at[pl.ds(r0, m), pl.ds(0, width)], ubuf.at[s, :, k, :], isem.at[s])

    def out_copy(step, s, k):
        r0 = pl.multiple_of(step * rows + k * m, SUBLANES)
        return pltpu.make_async_copy(yabuf.at[s, :, k, :], ya_hbm.at[pl.ds(r0, m)], osem.at[s])

    @pl.when(step_id == 0)
    def _():
        for k in range(SUBLANES):
            in_copy(0, 0, k).start()

    @pl.when(step_id + 1 < n_steps)
    def _():
        for k in range(SUBLANES):
            in_copy(step_id + 1, 1 - slot, k).start()

    @pl.when(t == 0)
    def _():
        car_re_ref[...] = jnp.zeros_like(car_re_ref)
        car_im_ref[...] = jnp.zeros_like(car_im_ref)
        lr, li = lre_ref[...], lim_ref[...]

        def pw(_, carry):
            pr, pi = carry
            return pr * lr - pi * li, pr * li + pi * lr

        pr, pi = lax.fori_loop(1, m, pw, (lr, li))
        pre_ref[...] = pr
        pim_ref[...] = pi

    @pl.when(step_id == 0)
    def _():
        zbuf[...] = jnp.zeros_like(zbuf)

    @pl.when(step_id >= 3)
    def _():
        for k in range(SUBLANES):
            out_copy(step_id - 3, 1 - slot, k).wait()

    nh = wglu_ref.shape[1] // 2

    def glu_pieces(z, s):
        def piece(j):
            val = jnp.dot(z, wglu_ref[:, j * tn:(j + 1) * tn], preferred_element_type=F32)
            gt = jnp.dot(z, wglu_ref[:, nh + j * tn:nh + (j + 1) * tn], preferred_element_type=F32)
            yabuf[s, :, :, j * tn:(j + 1) * tn] = (val * _sigmoid(gt)).reshape(m, SUBLANES, tn)
        return [functools.partial(piece, j) for j in range(nh // tn)]

    for k in range(SUBLANES):
        in_copy(step_id, slot, k).wait()
    u = ubuf[slot].reshape(rows, width)
    up = u.astype(BF16)

    def b_project(s):
        us = up[:, s * S5_SLAB_CH:(s + 1) * S5_SLAB_CH]
        cols = slice(s * slab_states, (s + 1) * slab_states)
        xre_ref[:, cols] = jnp.dot(us, bre_ref[s], preferred_element_type=F32)
        xim_ref[:, cols] = jnp.dot(us, bim_ref[s], preferred_element_type=F32)

    def c_project(s):
        cols = slice(s * slab_states, (s + 1) * slab_states)
        return (jnp.dot(xre_ref[:, cols].astype(BF16), cre_ref[s], preferred_element_type=F32)
                - jnp.dot(xim_ref[:, cols].astype(BF16), cim_ref[s], preferred_element_type=F32))

    n_cb = n_state // cw
    cb_per_slab = slab_states // cw
    glu_prev = glu_pieces(zbuf[...], 1 - slot)
    glu_at = {(i * n_cb) // len(glu_prev): [] for i in range(len(glu_prev))}
    for i, piece in enumerate(glu_prev):
        glu_at[(i * n_cb) // len(glu_prev)].append(piece)
    ys = []
    b_project(0)
    row_id = lax.broadcasted_iota(jnp.int32, (SUBLANES, cw), 0)
    for cb in range(n_cb):
        cols = slice(cb * cw, (cb + 1) * cw)
        slab = cb // cb_per_slab
        if cb % cb_per_slab == 0:
            if slab + 1 < n_slab:
                b_project(slab + 1)
            if slab >= 1:
                ys.append(c_project(slab - 1))
        for piece in glu_at.get(cb, []):
            piece()
        lr = jnp.broadcast_to(lre_ref[:, cols], (SUBLANES, cw))
        li = jnp.broadcast_to(lim_ref[:, cols], (SUBLANES, cw))

        def step(tau, carry, store, cols=cols, lr=lr, li=li):
            sr, si = carry
            r0 = pl.multiple_of(tau * SUBLANES, SUBLANES)
            nr = sr * lr - si * li + xre_ref[pl.ds(r0, SUBLANES), cols]
            ni = sr * li + si * lr + xim_ref[pl.ds(r0, SUBLANES), cols]
            if store:
                xre_ref[pl.ds(r0, SUBLANES), cols] = nr
                xim_ref[pl.ds(r0, SUBLANES), cols] = ni
            return nr, ni

        zero = jnp.zeros((SUBLANES, cw), F32)
        er, ei = lax.fori_loop(0, m, functools.partial(step, store=False), (zero, zero), unroll=True)

        lmr = jnp.broadcast_to(pre_ref[:, cols], (SUBLANES, cw))
        lmi = jnp.broadcast_to(pim_ref[:, cols], (SUBLANES, cw))
        ir = jnp.where(row_id == 0, jnp.broadcast_to(car_re_ref[:, cols], (SUBLANES, cw)), 0.0)
        ii = jnp.where(row_id == 0, jnp.broadcast_to(car_im_ref[:, cols], (SUBLANES, cw)), 0.0)
        for k in range(1, SUBLANES):
            nr = ir * lmr - ii * lmi + er
            ni = ir * lmi + ii * lmr + ei
            ir = jnp.where(row_id == k, pltpu.roll(nr, 1, axis=0), ir)
            ii = jnp.where(row_id == k, pltpu.roll(ni, 1, axis=0), ii)

        fr, fi = lax.fori_loop(0, m, functools.partial(step, store=True), (ir, ii), unroll=True)
        car_re_ref[:, cols] = fr[SUBLANES - 1:SUBLANES, :]
        car_im_ref[:, cols] = fi[SUBLANES - 1:SUBLANES, :]

    ys.append(c_project(n_slab - 1))
    y = jnp.concatenate(ys, axis=1) + d_ref[...] * u
    zbuf[...] = _gelu_tanh(y).astype(BF16)

    @pl.when(step_id >= 1)
    def _():
        for k in range(SUBLANES):
            out_copy(step_id - 1, 1 - slot, k).start()

    @pl.when(step_id == n_steps - 1)
    def _():
        @pl.when(step_id >= 2)
        def _():
            for k in range(SUBLANES):
                out_copy(step_id - 2, slot, k).wait()
        for piece in glu_pieces(zbuf[...], slot):
            piece()
        for k in range(SUBLANES):
            out_copy(step_id, slot, k).start()
        for k in range(SUBLANES):
            out_copy(step_id, slot, k).wait()

        @pl.when(step_id >= 1)
        def _():
            for k in range(SUBLANES):
                out_copy(step_id - 1, 1 - slot, k).wait()


def s5_branch(proj, bre, bim, cre, cim, lre, lim, d_skip, w_glu, *, batch, seq, width, m, cw, tn):
    rows = SUBLANES * m
    n_t = seq // rows
    n_state = lre.shape[1]
    nh = w_glu.shape[1] // 2
    once = pl.Buffered(1)
    const3 = lambda b, t: (0, 0, 0)
    const2 = lambda b, t: (0, 0)
    return pl.pallas_call(
        functools.partial(_s5_kernel, m=m, cw=cw, tn=tn),
        grid=(batch, n_t),
        in_specs=[pl.BlockSpec(memory_space=pl.ANY),
                  pl.BlockSpec(bre.shape, const3, pipeline_mode=once),
                  pl.BlockSpec(bim.shape, const3, pipeline_mode=once),
                  pl.BlockSpec(cre.shape, const3, pipeline_mode=once),
                  pl.BlockSpec(cim.shape, const3, pipeline_mode=once),
                  pl.BlockSpec((1, n_state), const2),
                  pl.BlockSpec((1, n_state), const2),
                  pl.BlockSpec((1, width), const2),
                  pl.BlockSpec(w_glu.shape, const2, pipeline_mode=once)],
        out_specs=pl.BlockSpec(memory_space=pl.ANY),
        out_shape=jax.ShapeDtypeStruct((batch * seq, nh), F32),
        scratch_shapes=[pltpu.VMEM((2, m, SUBLANES, width), F32), pltpu.VMEM((2, m, SUBLANES, nh), F32),
                        pltpu.VMEM((rows, width), BF16),
                        pltpu.VMEM((rows, n_state), F32), pltpu.VMEM((rows, n_state), F32),
                        pltpu.VMEM((1, n_state), F32), pltpu.VMEM((1, n_state), F32),
                        pltpu.VMEM((1, n_state), F32), pltpu.VMEM((1, n_state), F32),
                        pltpu.SemaphoreType.DMA((2,)), pltpu.SemaphoreType.DMA((2,))],
        compiler_params=_cparams(("arbitrary", "arbitrary")),
        name="s5_branch",
    )(proj, bre, bim, cre, cim, lre, lim, d_skip.reshape(1, width), w_glu)


def _conv_kernel(a_ref, b_ref, ah_ref, bh_ref, dww_ref, dwb_ref, lng_ref, lnb_ref, w_ref, o_ref,
                 zs_ref, wb_ref, yc_ref, *, tm, taps):
    i = pl.program_id(1)
    off0 = CONV_HALO - (taps - 1)
    width = a_ref.shape[1]
    n_slab = width // LANES
    zh = jnp.where(i == 0, 0.0, ah_ref[...] * _sigmoid(bh_ref[...]))
    z = a_ref[...] * _sigmoid(b_ref[...])
    for s in range(n_slab):
        zs_ref[s, 0:CONV_HALO, :] = zh[:, s * LANES:(s + 1) * LANES]
        zs_ref[s, CONV_HALO:, :] = z[:, s * LANES:(s + 1) * LANES]

    @pl.when(i == 0)
    def _():
        for j in range(taps):
            wb_ref[j] = jnp.broadcast_to(dww_ref[j:j + 1, :], (SUBLANES, width))
        wb_ref[taps] = jnp.broadcast_to(dwb_ref[...], (SUBLANES, width))

    rg = CONV_ROW_CHUNK // SUBLANES

    def chunk(c, carry):
        r0 = pl.multiple_of(c * CONV_ROW_CHUNK, CONV_ROW_CHUNK)
        for s in range(n_slab):
            lanes = slice(s * LANES, (s + 1) * LANES)
            bias = wb_ref[taps, :, lanes]
            acc = [bias] * rg
            for j in range(taps):
                w = wb_ref[j, :, lanes]
                for g in range(rg):
                    acc[g] = acc[g] + w * zs_ref[s, pl.ds(r0 + g * SUBLANES + off0 + j, SUBLANES), :]
            for g in range(rg):
                yc_ref[pl.ds(pl.multiple_of(r0 + g * SUBLANES, SUBLANES), SUBLANES), lanes] = acc[g]
        return carry

    lax.fori_loop(0, tm // CONV_ROW_CHUNK, chunk, 0)
    y = yc_ref[...]
    mu = jnp.mean(y, axis=-1, keepdims=True)
    var = jnp.mean(jnp.square(y - mu), axis=-1, keepdims=True)
    y = (y - mu) * lax.rsqrt(var + LN_EPS) * lng_ref[...] + lnb_ref[...]
    y = y * _sigmoid(y)
    o_ref[...] = jnp.dot(y.astype(BF16), w_ref[...], preferred_element_type=F32)


def conv_branch(proj, dw_w, dw_b, ln_g, ln_b, w_out, *, batch, seq, width, col0, tm):
    n_t = seq // tm
    ca, cb = col0 // width, col0 // width + 1
    hb = tm // CONV_HALO
    taps = dw_w.shape[0]
    d_out = w_out.shape[1]
    const2 = lambda b, i: (0, 0)
    halo = lambda b, i: jnp.maximum((b * n_t + i) * hb - 1, 0)
    return pl.pallas_call(
        functools.partial(_conv_kernel, tm=tm, taps=taps),
        grid=(batch, n_t),
        in_specs=[pl.BlockSpec((tm, width), lambda b, i: (b * n_t + i, ca)),
                  pl.BlockSpec((tm, width), lambda b, i: (b * n_t + i, cb)),
                  pl.BlockSpec((CONV_HALO, width), lambda b, i: (halo(b, i), ca)),
                  pl.BlockSpec((CONV_HALO, width), lambda b, i: (halo(b, i), cb)),
                  pl.BlockSpec((taps, width), const2),
                  pl.BlockSpec((1, width), const2),
                  pl.BlockSpec((1, width), const2),
                  pl.BlockSpec((1, width), const2),
                  pl.BlockSpec((width, d_out), const2)],
        out_specs=pl.BlockSpec((tm, d_out), lambda b, i: (b * n_t + i, 0)),
        out_shape=jax.ShapeDtypeStruct((batch * seq, d_out), F32),
        scratch_shapes=[pltpu.VMEM((width // LANES, tm + CONV_HALO, LANES), F32),
                        pltpu.VMEM((taps + 1, SUBLANES, width), F32),
                        pltpu.VMEM((tm, width), F32)],
        compiler_params=_cparams(("parallel", "arbitrary")),
        name="conv_branch",
    )(proj, proj, proj, proj, dw_w, dw_b.reshape(1, width), ln_g.reshape(1, width),
      ln_b.reshape(1, width), w_out)


def _combine_kernel(*refs, parts):
    x_ref, g_ref = refs[0], refs[1]
    ya_ref, yb_ref, wo_ref, o_ref = refs[2 + 2 * parts], refs[3 + 2 * parts], refs[-2], refs[-1]
    gw = ya_ref.shape[1] // parts
    x = x_ref[...]
    h = _rms(x, g_ref[...]).astype(BF16)
    o_ref[...] = x
    for p in range(parts):
        cols = slice(p * gw, (p + 1) * gw)
        ga = _sigmoid(jnp.dot(h, refs[2 + p][...], preferred_element_type=F32))
        gb = _sigmoid(jnp.dot(h, refs[2 + parts + p][...], preferred_element_type=F32))
        mix = ga * ya_ref[:, cols] + gb * yb_ref[:, cols]
        o_ref[...] += jnp.dot(mix.astype(BF16), wo_ref[cols, :], preferred_element_type=F32)


def combine(x, g, w_in, y_a, y_b, w_out, *, gate_col0, tm):
    m, d = x.shape
    gw = math.gcd(gate_col0, d)
    parts = d // gw
    row = lambda i: (i, 0)
    once = pl.Buffered(1)
    gate_specs = [pl.BlockSpec((d, gw), functools.partial(lambda i, c: (0, c), c=(gate_col0 + s * d) // gw + p),
                               pipeline_mode=once) for s in range(2) for p in range(parts)]
    return pl.pallas_call(
        functools.partial(_combine_kernel, parts=parts),
        grid=(m // tm,),
        in_specs=[pl.BlockSpec((tm, d), row), pl.BlockSpec((1, d), lambda i: (0, 0))] + gate_specs +
                 [pl.BlockSpec((tm, d), row),
                  pl.BlockSpec((tm, d), row),
                  pl.BlockSpec((d, d), lambda i: (0, 0), pipeline_mode=once)],
        out_specs=pl.BlockSpec((tm, d), row),
        out_shape=jax.ShapeDtypeStruct((m, d), F32),
        compiler_params=_cparams(("parallel",)),
        name="gated_combine",
    )(x, g.reshape(1, d), *([w_in] * (2 * parts)), y_a, y_b, w_out)


def _route_tile(x, g_ref, w_ref, b_ref, tri_ref, h_ref, info_ref, infot_ref, cnt_ref, run_ref, *,
                n_groups, epg):
    tm = x.shape[0]
    h = _rms(x, g_ref[...])
    half = h.shape[1] // 2
    bits = pltpu.bitcast(h.astype(BF16).astype(F32), jnp.uint32)
    packed = (bits[:, :half] >> 16) | (bits[:, half:] & jnp.uint32(0xFFFF0000))
    tr = half // LANES
    for s in range(tr):
        h_ref[pl.ds(s, tm, stride=tr), :] = packed[:, s * LANES:(s + 1) * LANES]
    h_hi = h.astype(BF16)
    h_lo = (h - h_hi.astype(F32)).astype(BF16)
    p = jnp.dot(h_hi, w_ref[...], preferred_element_type=F32)
    logits = (p[:, :LANES] + p[:, LANES:] + jnp.dot(h_lo, w_ref[:, :LANES], preferred_element_type=F32)
              + b_ref[...])
    lane = lax.broadcasted_iota(jnp.int32, (tm, LANES), 1).astype(F32)
    neg = jnp.float32(-jnp.inf)
    big = jnp.float32(LANES)
    is_grp = lane < n_groups
    cl = jnp.where(is_grp, logits, neg)
    cmax = jnp.max(cl, axis=-1, keepdims=True)
    g_idx = jnp.min(jnp.where(cl == cmax, lane, big), axis=-1, keepdims=True)
    p_sel = 1.0 / jnp.sum(jnp.where(is_grp, jnp.exp(cl - cmax), 0.0), axis=-1, keepdims=True)
    lo = n_groups + g_idx * epg
    in_grp = (lane >= lo) & (lane < lo + epg)
    fl = jnp.where(in_grp, logits, neg)
    v1 = jnp.max(fl, axis=-1, keepdims=True)
    i1 = jnp.min(jnp.where(fl == v1, lane, big), axis=-1, keepdims=True)
    fl2 = jnp.where(lane == i1, neg, fl)
    v2 = jnp.max(fl2, axis=-1, keepdims=True)
    i2 = jnp.min(jnp.where(fl2 == v2, lane, big), axis=-1, keepdims=True)
    e2x = jnp.exp(v2 - v1)
    w1 = p_sel / (1.0 + e2x)
    w2 = p_sel * e2x / (1.0 + e2x)
    e1 = i1 - n_groups
    e2 = i2 - n_groups
    oh1 = lane == e1
    oh2 = lane == e2
    both = jnp.where(oh1 | oh2, 1.0, 0.0)
    before = jnp.dot(tri_ref[...], both.astype(BF16), preferred_element_type=F32) + run_ref[...]
    r1 = jnp.sum(jnp.where(oh1, before, 0.0), axis=-1, keepdims=True)
    r2 = jnp.sum(jnp.where(oh2, before, 0.0), axis=-1, keepdims=True)
    run_ref[...] += jnp.sum(both, axis=0, keepdims=True)
    cnt_ref[...] = run_ref[...]
    info = jnp.where(lane == 0, e1, 0.0)
    info = jnp.where(lane == 1, e2, info)
    info = jnp.where(lane == 2, r1, info)
    info = jnp.where(lane == 3, r2, info)
    info = jnp.where(lane == 4, w1, info)
    info = jnp.where(lane == 5, w2, info)
    info_ref[...] = info
    infot_ref[...] = jnp.transpose(info)[0:SUBLANES, :]


def _xattn_router_kernel(x_ref, gx_ref, wq_ref, kv_ref, wo_ref, g_ref, w_ref, b_ref, tri_ref,
                         o_ref, h_ref, info_ref, infot_ref, cnt_ref, run_ref, *, heads, scale, n_groups, epg):
    @pl.when((pl.program_id(0) == 0) & (pl.program_id(1) == 0))
    def _():
        run_ref[...] = jnp.zeros_like(run_ref)

    d = x_ref.shape[1]
    hd = d // heads
    acc = x_ref[...]
    hq = _rms(acc, gx_ref[...]).astype(BF16)

    def q_head(h):
        return jnp.dot(hq, wq_ref[:, h * hd:(h + 1) * hd], preferred_element_type=F32).astype(BF16)

    q_next = q_head(0)
    for h in range(heads):
        q = q_next
        k = kv_ref[:, h * hd:(h + 1) * hd]
        v = kv_ref[:, d + h * hd:d + (h + 1) * hd]
        s = lax.dot_general(q, k, (((1,), (1,)), ((), ())), preferred_element_type=F32) * scale
        if h + 1 < heads:
            q_next = q_head(h + 1)
        s = s - jnp.max(s, axis=-1, keepdims=True)
        p = jnp.exp(s)
        p = p / jnp.sum(p, axis=-1, keepdims=True)
        o = jnp.dot(p.astype(BF16), v, preferred_element_type=F32)
        acc = acc + jnp.dot(o.astype(BF16), wo_ref[h * hd:(h + 1) * hd, :], preferred_element_type=F32)
    o_ref[...] = acc
    _route_tile(acc, g_ref, w_ref, b_ref, tri_ref, h_ref, info_ref, infot_ref, cnt_ref, run_ref,
                n_groups=n_groups, epg=epg)


def xattn_router(x, gx, wq, kv, wo, g, w_pad, b_pad, *, batch, seq, n_mem, heads, n_groups, epg, tm):
    d = x.shape[1]
    m = batch * seq
    n_t = seq // tm
    scale = 1.0 / math.sqrt(d // heads)
    tri = (jnp.arange(tm)[:, None] > jnp.arange(tm)[None, :]).astype(BF16)
    row = lambda b, i: (b * n_t + i, 0)
    const = lambda b, i: (0, 0)
    once = pl.Buffered(1)
    return pl.pallas_call(
        functools.partial(_xattn_router_kernel, heads=heads, scale=scale, n_groups=n_groups, epg=epg),
        grid=(batch, n_t),
        in_specs=[pl.BlockSpec((tm, d), row),
                  pl.BlockSpec((1, d), const),
                  pl.BlockSpec((d, d), const, pipeline_mode=once),
                  pl.BlockSpec((n_mem, 2 * d), lambda b, i: (b, 0)),
                  pl.BlockSpec((d, d), const, pipeline_mode=once),
                  pl.BlockSpec((1, d), const),
                  pl.BlockSpec((d, 2 * LANES), const, pipeline_mode=once),
                  pl.BlockSpec((1, LANES), const),
                  pl.BlockSpec((tm, tm), const, pipeline_mode=once)],
        out_specs=[pl.BlockSpec((tm, d), row),
                   pl.BlockSpec((tm * (d // 2 // LANES), LANES), row),
                   pl.BlockSpec((tm, LANES), row),
                   pl.BlockSpec((SUBLANES, tm), lambda b, i: (0, b * n_t + i)),
                   pl.BlockSpec((1, LANES), const)],
        out_shape=(jax.ShapeDtypeStruct((m, d), F32),
                   jax.ShapeDtypeStruct((m * (d // 2 // LANES), LANES), jnp.uint32),
                   jax.ShapeDtypeStruct((m, LANES), F32),
                   jax.ShapeDtypeStruct((SUBLANES, m), F32),
                   jax.ShapeDtypeStruct((1, LANES), F32)),
        scratch_shapes=[pltpu.VMEM((1, LANES), F32)],
        compiler_params=_cparams(("arbitrary", "arbitrary")),
        name="xattn_router",
    )(x, gx.reshape(1, d), wq, kv, wo, g.reshape(1, d), w_pad, b_pad, tri)


def moe_dispatch_plan(infot, counts, n_tokens, n_experts):
    e = infot[0:2].astype(jnp.int32)
    rank = infot[2:4].astype(jnp.int32)
    counts = counts[0, :n_experts].astype(jnp.int32)
    pcounts = ((counts + MOE_BLOCK - 1) // MOE_BLOCK) * MOE_BLOCK
    pends = jnp.cumsum(pcounts)
    pstarts = pends - pcounts
    ids = jnp.arange(n_experts, dtype=jnp.int32)
    dest = rank + jnp.sum(jnp.where(e[:, :, None] == ids, pstarts, 0), axis=-1)
    n_rows = (-(-2 * n_tokens // MOE_BLOCK) + n_experts) * MOE_BLOCK
    n_used = (pends[-1] // MOE_BLOCK).reshape(1)
    return dest, n_rows, n_used, pcounts, pends


def _dispatch_kernel(pc_ref, pe_ref, dest_ref, h_hbm, xs_hbm, hbuf, zbuf, isem, csem, sem, tsem, *, n_experts, tr):
    i = pl.program_id(0)
    n = pl.num_programs(0)
    tile_rows = hbuf.shape[1]
    tm = tile_rows // tr
    blk = MOE_BLOCK * tr
    slot = lax.rem(i, 3)
    par = lax.rem(i, 2)

    def fetch(step, s):
        src = h_hbm.at[pl.ds(pl.multiple_of(step * tile_rows, tile_rows), tile_rows)]
        return pltpu.make_async_copy(src, hbuf.at[s], isem.at[s])

    def drain(s, p):
        for _ in range(2):
            pltpu.make_async_copy(hbuf.at[s], xs_hbm.at[pl.ds(0, tile_rows)], csem.at[p]).wait()

    @pl.when(i == 0)
    def _():
        fetch(0, 0).start()
        zbuf[...] = jnp.zeros_like(zbuf)
        for wait in (False, True):
            for e in range(n_experts):
                @pl.when(pc_ref[e] > 0)
                def _():
                    row0 = pl.multiple_of((pe_ref[e] - MOE_BLOCK) * tr, blk)
                    fill = pltpu.make_async_copy(zbuf, xs_hbm.at[pl.ds(row0, blk)], sem)
                    fill.wait() if wait else fill.start()

    n_blocks = xs_hbm.shape[0] // blk
    n_used = pe_ref[n_experts - 1] // MOE_BLOCK

    def tail_fill(b, wait):
        row0 = pl.multiple_of(b * blk, blk)
        fill = pltpu.make_async_copy(zbuf, xs_hbm.at[pl.ds(row0, blk)], tsem)
        fill.wait() if wait else fill.start()

    @pl.when(i == 0)
    def _():
        lax.fori_loop(n_used, n_blocks, lambda b, c: (tail_fill(b, False), c)[1], 0)

    @pl.when(i + 1 < n)
    def _():
        fetch(i + 1, lax.rem(i + 1, 3)).start()

    fetch(i, slot).wait()
    for k in range(2):
        for r in range(tm):
            dst = pl.multiple_of(dest_ref[k, r] * tr, tr)
            pltpu.make_async_copy(hbuf.at[slot, pl.ds(r * tr, tr)], xs_hbm.at[pl.ds(dst, tr)],
                                  csem.at[par]).start(priority=r % 2)

    @pl.when(i >= 1)
    def _():
        drain(lax.rem(i + 2, 3), 1 - par)

    @pl.when(i == n - 1)
    def _():
        drain(slot, par)
        lax.fori_loop(n_used, n_blocks, lambda b, c: (tail_fill(b, True), c)[1], 0)


def moe_dispatch(h, dest, pcounts, pends, *, n_rows, tm):
    t = dest.shape[1]
    tr = h.shape[0] // t
    grid_spec = pltpu.PrefetchScalarGridSpec(
        num_scalar_prefetch=2,
        grid=(t // tm,),
        in_specs=[pl.BlockSpec((2, tm), lambda i, pc, pe: (0, i), memory_space=pltpu.SMEM),
                  pl.BlockSpec(memory_space=pl.ANY)],
        out_specs=pl.BlockSpec(memory_space=pl.ANY),
        scratch_shapes=[pltpu.VMEM((3, tm * tr, LANES), h.dtype), pltpu.VMEM((MOE_BLOCK * tr, LANES), h.dtype),
                        pltpu.SemaphoreType.DMA((3,)), pltpu.SemaphoreType.DMA((2,)),
                        pltpu.SemaphoreType.DMA(()), pltpu.SemaphoreType.DMA(())],
    )
    return pl.pallas_call(
        functools.partial(_dispatch_kernel, n_experts=pcounts.shape[0], tr=tr),
        grid_spec=grid_spec,
        out_shape=jax.ShapeDtypeStruct((n_rows * tr, LANES), h.dtype),
        compiler_params=_cparams(("arbitrary",)),
        name="moe_dispatch",
    )(pcounts, pends, dest, h)


def _experts_kernel(first_ref, cnt_ref, nb_ref, xs_hbm, wg_hbm, wu_hbm, wd_hbm, ys_hbm,
                    xbuf, ybuf, wgf, wuf, wdf, wgb, wub, wdb, xsem, ysem, wsem):
    e = pl.program_id(0)
    n_e = pl.num_programs(0)
    nb = nb_ref[0]
    first, cnt = first_ref[e], cnt_ref[e]
    tr = xbuf.shape[1] // MOE_BLOCK
    n_blocks = xs_hbm.shape[0] // xbuf.shape[1]
    wslot = lax.rem(e, 2)

    def rows(g):
        return pl.ds(pl.multiple_of(g * MOE_BLOCK, MOE_BLOCK), MOE_BLOCK)

    def x_copy(g, s):
        xrows = pl.ds(pl.multiple_of(g * xbuf.shape[1], xbuf.shape[1]), xbuf.shape[1])
        return pltpu.make_async_copy(xs_hbm.at[xrows], xbuf.at[s], xsem.at[s])

    def y_copy(g, s):
        return pltpu.make_async_copy(ybuf.at[s], ys_hbm.at[rows(g)], ysem.at[s])

    def w_copies(ex, s):
        half = wd_hbm.shape[1] // 2
        lo, hi = pl.ds(0, half), pl.ds(half, half)
        return ((pltpu.make_async_copy(wg_hbm.at[ex], wgf.at[s], wsem.at[s]), 0),
                (pltpu.make_async_copy(wu_hbm.at[ex], wuf.at[s], wsem.at[s]), 1),
                (pltpu.make_async_copy(wd_hbm.at[ex, lo], wdf.at[s, lo], wsem.at[s]), 0),
                (pltpu.make_async_copy(wd_hbm.at[ex, hi], wdf.at[s, hi], wsem.at[s]), 1))

    @pl.when(e == 0)
    def _():
        for g0 in range(EXPERT_X_RING - 2):
            @pl.when(g0 < nb)
            def _():
                x_copy(g0, g0).start(priority=BLOCK_DMA_QUEUE)

    @pl.when((e == 0) & (cnt > 0))
    def _():
        for cp, queue in w_copies(0, 0):
            cp.start(priority=queue)

    e_next = jnp.minimum(e + 1, n_e - 1)

    @pl.when((e + 1 < n_e) & (cnt_ref[e_next] > 0))
    def _():
        for cp, queue in w_copies(e_next, 1 - wslot):
            cp.start(priority=queue)

    @pl.when(cnt > 0)
    def _():
        for cp, _ in w_copies(e, wslot):
            cp.wait()
        wgb[...] = wgf[wslot].astype(BF16)
        wub[...] = wuf[wslot].astype(BF16)
        wdb[...] = wdf[wslot].astype(BF16)

    def run_blocks(g, nblk):
        s = lax.rem(g, EXPERT_X_RING)
        sy = lax.rem(g, EXPERT_Y_RING)
        for j in range(nblk):
            ahead = g + j + EXPERT_X_RING - 2

            @pl.when(ahead < nb)
            def _():
                x_copy(ahead, lax.rem(ahead, EXPERT_X_RING)).start(priority=BLOCK_DMA_QUEUE)

        for j in range(nblk):
            x_copy(g + j, s + j).wait()
        half = tr * LANES
        packed = jnp.concatenate(
            [jnp.concatenate([xbuf[s + j, pl.ds(w, MOE_BLOCK, stride=tr), :] for w in range(tr)], axis=1)
             for j in range(nblk)], axis=0)
        x_lo = pltpu.bitcast(packed << 16, F32).astype(BF16)
        x_hi = pltpu.bitcast(packed & jnp.uint32(0xFFFF0000), F32).astype(BF16)
        gte = (jnp.dot(x_lo, wgb[:half, :], preferred_element_type=F32)
               + jnp.dot(x_hi, wgb[half:, :], preferred_element_type=F32))
        up = (jnp.dot(x_lo, wub[:half, :], preferred_element_type=F32)
              + jnp.dot(x_hi, wub[half:, :], preferred_element_type=F32))
        act = (gte * _sigmoid(gte) * up).astype(BF16)
        y = jnp.dot(act, wdb[...], preferred_element_type=F32)

        for j in range(nblk):
            @pl.when(g + j >= EXPERT_Y_RING)
            def _():
                y_copy(g + j - EXPERT_Y_RING, sy + j).wait()

        ybuf[pl.ds(sy, nblk)] = y.reshape(nblk, MOE_BLOCK, y.shape[1])
        for j in range(nblk):
            y_copy(g + j, sy + j).start()

    lead = jnp.where(cnt > 0, first & 1, 0)
    rest = cnt - lead

    @pl.when(lead == 1)
    def _():
        run_blocks(first, 1)

    lax.fori_loop(0, rest // 2, lambda b, c: (run_blocks(first + lead + 2 * b, 2), c)[1], 0)

    @pl.when((rest & 1) == 1)
    def _():
        run_blocks(first + cnt - 1, 1)

    @pl.when(e == pl.num_programs(0) - 1)
    def _():
        for back in range(1, EXPERT_Y_RING + 1):
            @pl.when(nb >= back)
            def _():
                y_copy(nb - back, lax.rem(nb - back, EXPERT_Y_RING)).wait()

        ybuf[0] = jnp.zeros(ybuf.shape[1:], F32)
        lax.fori_loop(nb, n_blocks, lambda g, c: (y_copy(g, 0).start(), c)[1], 0)
        lax.fori_loop(nb, n_blocks, lambda g, c: (y_copy(g, 0).wait(), c)[1], 0)


def experts(xs, w_gate, w_up, w_down, first_blk, n_blk, n_used):
    n_experts, d, de = w_gate.shape
    tr = d // 2 // LANES
    n_rows = xs.shape[0] // tr
    grid_spec = pltpu.PrefetchScalarGridSpec(
        num_scalar_prefetch=3,
        grid=(n_experts,),
        in_specs=[pl.BlockSpec(memory_space=pl.ANY)] * 4,
        out_specs=pl.BlockSpec(memory_space=pl.ANY),
        scratch_shapes=[pltpu.VMEM((EXPERT_X_RING, MOE_BLOCK * tr, LANES), xs.dtype),
                        pltpu.VMEM((EXPERT_Y_RING, MOE_BLOCK, d), F32),
                        pltpu.VMEM((2, d, de), F32), pltpu.VMEM((2, d, de), F32), pltpu.VMEM((2, de, d), F32),
                        pltpu.VMEM((d, de), BF16), pltpu.VMEM((d, de), BF16), pltpu.VMEM((de, d), BF16),
                        pltpu.SemaphoreType.DMA((EXPERT_X_RING,)), pltpu.SemaphoreType.DMA((EXPERT_Y_RING,)),
                        pltpu.SemaphoreType.DMA((2,))],
    )
    return pl.pallas_call(
        _experts_kernel,
        grid_spec=grid_spec,
        out_shape=jax.ShapeDtypeStruct((n_rows, d), F32),
        compiler_params=_cparams(("arbitrary",)),
        name="moe_experts",
    )(first_blk, n_blk, n_used, xs, w_gate, w_up, w_down)


def _moe_combine_kernel(dest_ref, destn_ref, x_ref, info_ref, g_ref, ys_hbm, o_ref, ybuf, sem, *, norm):
    i = pl.program_id(0)
    n = pl.num_programs(0)
    tm = x_ref.shape[0]
    slot = lax.rem(i, 2)

    def gather_start(dref, s):
        for k in range(2):
            for r in range(tm):
                pltpu.make_async_copy(ys_hbm.at[pl.ds(dref[k, r], 1)], ybuf.at[s, k, pl.ds(r, 1)],
                                      sem.at[s]).start(priority=r % 2)

    @pl.when(i == 0)
    def _():
        gather_start(dest_ref, 0)

    @pl.when(i + 1 < n)
    def _():
        gather_start(destn_ref, 1 - slot)

    for k in range(2):
        pltpu.make_async_copy(ys_hbm.at[pl.ds(0, tm)], ybuf.at[slot, k], sem.at[slot]).wait()
    w1 = info_ref[:, 4:5]
    w2 = info_ref[:, 5:6]
    s = x_ref[...] + (w1 * ybuf[slot, 0] + w2 * ybuf[slot, 1])
    o_ref[...] = _rms(s, g_ref[...]) if norm else s


def moe_combine(x, ys, dest, info, g, *, norm, tm):
    t, d = x.shape
    n = t // tm
    return pl.pallas_call(
        functools.partial(_moe_combine_kernel, norm=norm),
        grid=(n,),
        in_specs=[pl.BlockSpec((2, tm), lambda i: (0, i), memory_space=pltpu.SMEM),
                  pl.BlockSpec((2, tm), lambda i: (0, jnp.minimum(i + 1, n - 1)), memory_space=pltpu.SMEM),
                  pl.BlockSpec((tm, d), lambda i: (i, 0)),
                  pl.BlockSpec((tm, LANES), lambda i: (i, 0)),
                  pl.BlockSpec((1, d), lambda i: (0, 0)),
                  pl.BlockSpec(memory_space=pl.ANY)],
        out_specs=pl.BlockSpec((tm, d), lambda i: (i, 0)),
        out_shape=jax.ShapeDtypeStruct((t, d), F32),
        scratch_shapes=[pltpu.VMEM((2, 2, tm, d), F32), pltpu.SemaphoreType.DMA((2,))],
        compiler_params=_cparams(("arbitrary",)),
        name="moe_combine",
    )(dest, dest, x, info, g.reshape(1, d), ys)


def forward(x, mem, norm_mix_g, w_in, s5_lambda_re, s5_lambda_im, s5_log_dt, s5_b_re, s5_b_im, s5_c_re,
            s5_c_im, s5_d, s5_w_glu, conv_dw_w, conv_dw_b, conv_ln_g, conv_ln_b, conv_w_out, w_out,
            norm_xattn_g, norm_mem_g, xattn_wq, xattn_wk, xattn_wv, xattn_wo, norm_moe_g, router_w_group,
            router_b_group, router_w_expert, router_b_expert, exp_w_gate, exp_w_up, exp_w_down, norm_final_g,
            *, tiles):
    batch, seq, d = x.shape
    depth = w_in.shape[0]
    n_mem = mem.shape[1]
    s5_width = s5_d.shape[1]
    conv_width = conv_dw_b.shape[1]
    n_groups, epg = router_w_expert.shape[1], router_w_expert.shape[3]
    t = batch * seq
    xf = x.reshape(t, d)
    memf = mem.reshape(batch * n_mem, d)
    gate_col0 = s5_width + 2 * conv_width
    for l in range(depth):
        w_in_b = w_in[l].astype(BF16)
        proj = norm_mm_rows(xf, norm_mix_g[l], w_in_b, tm=tiles["proj_tm"], tn=tiles["proj_tn"],
                            out_dtype=F32, n_cols=gate_col0)
        lbr, lbi, bbr, bbi = s5_discretise(s5_lambda_re[l], s5_lambda_im[l], s5_log_dt[l],
                                           s5_b_re[l], s5_b_im[l])
        bre, bim, cre, cim = s5_block_diag(bbr, bbi, s5_c_re[l], s5_c_im[l])
        y_a = s5_branch(proj, bre, bim, cre, cim, lbr.reshape(1, -1), lbi.reshape(1, -1), s5_d[l],
                        s5_w_glu[l].astype(BF16), batch=batch, seq=seq, width=s5_width, m=tiles["s5_m"],
                        cw=tiles["s5_cw"], tn=tiles["glu_tn"])
        y_b = conv_branch(proj, conv_dw_w[l], conv_dw_b[l], conv_ln_g[l], conv_ln_b[l],
                          conv_w_out[l].astype(BF16), batch=batch, seq=seq, width=conv_width,
                          col0=s5_width, tm=tiles["conv_tm"])
        xf = combine(xf, norm_mix_g[l], w_in_b, y_a, y_b, w_out[l].astype(BF16), gate_col0=gate_col0,
                     tm=tiles["comb_tm"])

        wkv = jnp.concatenate([xattn_wk[l], xattn_wv[l]], axis=1).astype(BF16)
        kv = norm_mm(memf, norm_mem_g[l], wkv, tm=batch * n_mem, tn=tiles["kv_tn"], out_dtype=BF16)
        w_r = jnp.concatenate([router_w_group[l],
                               jnp.transpose(router_w_expert[l], (1, 0, 2)).reshape(d, n_groups * epg)], axis=1)
        b_r = jnp.concatenate([router_b_group[l], router_b_expert[l].reshape(-1)])
        n_r = w_r.shape[1]
        w_pad = jnp.pad(w_r, ((0, 0), (0, LANES - n_r)))
        w_hi = w_pad.astype(BF16)
        w_pad = jnp.concatenate([w_hi, (w_pad - w_hi.astype(F32)).astype(BF16)], axis=1)
        b_pad = jnp.pad(b_r, (0, LANES - n_r)).reshape(1, LANES)
        xf, h, info, infot, counts = xattn_router(
            xf, norm_xattn_g[l], xattn_wq[l].astype(BF16), kv, xattn_wo[l].astype(BF16), norm_moe_g[l],
            w_pad, b_pad, batch=batch, seq=seq,
            n_mem=n_mem, heads=XATTN_HEADS, n_groups=n_groups, epg=epg, tm=tiles["xattn_tm"])
        dest, n_rows, n_used, pcounts, pends = moe_dispatch_plan(infot, counts, t, n_groups * epg)
        xs = moe_dispatch(h, dest, pcounts, pends, n_rows=n_rows, tm=tiles["dispatch_tm"])
        ys = experts(xs, exp_w_gate[l], exp_w_up[l], exp_w_down[l], (pends - pcounts) // MOE_BLOCK,
                     pcounts // MOE_BLOCK, n_used)
        last = l + 1 == depth
        xf = moe_combine(xf, ys, dest, info, norm_final_g if last else jnp.ones((d,), F32), norm=last,
                         tm=tiles["moe_tm"])
    return xf.reshape(batch, seq, d)


def kernel(x, mem, norm_mix_g, w_in, s5_lambda_re, s5_lambda_im, s5_log_dt, s5_b_re, s5_b_im, s5_c_re, s5_c_im, s5_d, s5_w_glu, conv_dw_w, conv_dw_b, conv_ln_g, conv_ln_b, conv_w_out, w_out, norm_xattn_g, norm_mem_g, xattn_wq, xattn_wk, xattn_wv, xattn_wo, norm_moe_g, router_w_group, router_b_group, router_w_expert, router_b_expert, exp_w_gate, exp_w_up, exp_w_down, norm_final_g):
    return forward(x, mem, norm_mix_g, w_in, s5_lambda_re, s5_lambda_im, s5_log_dt, s5_b_re, s5_b_im,
                   s5_c_re, s5_c_im, s5_d, s5_w_glu, conv_dw_w, conv_dw_b, conv_ln_g, conv_ln_b, conv_w_out,
                   w_out, norm_xattn_g, norm_mem_g, xattn_wq, xattn_wk, xattn_wv, xattn_wo, norm_moe_g,
                   router_w_group, router_b_group, router_w_expert, router_b_expert, exp_w_gate, exp_w_up,
                   exp_w_down, norm_final_g, tiles=TILES)
```
